```python
import jax
import jax.numpy as jnp
from jax import lax
import numpy as np

D_MODEL = 1024
BATCH = 4
SEQ = 4096
DEPTH = 4

GRID_W = 64
CTX_LEN = 256
EPS = 1e-6

N_GROUPS = 4
GROUP_W = D_MODEL // N_GROUPS
HEAD_DIM = 64
N_HEADS = GROUP_W // HEAD_DIM

NA_ROWS = 8
NA_COLS = 16

MLA_Q_RANK = (3 * D_MODEL) // 16
MLA_KV_RANK = D_MODEL // 8
MLA_NOPE = HEAD_DIM
MLA_ROPE = HEAD_DIM // 2
MLA_V = HEAD_DIM
ROPE_THETA = 10000.0
Q_BLOCK = 128

RW_DECAY_RANK = 64
RW_AAA_RANK = 64
RW_GATE_RANK = 128
RW_GN_EPS = 64e-5

GLA_DK = HEAD_DIM // 2
GLA_DV = HEAD_DIM
GLA_GATE_RANK = 16
GLA_TAU = 16.0
GLA_CHUNK = 64

D_FF = 2816
CONV_W = 3

NA_SIZES = (GROUP_W, GROUP_W, GROUP_W)
MLA_SIZES = (MLA_Q_RANK, MLA_KV_RANK, MLA_ROPE)
RW_SIZES = (GROUP_W, GROUP_W, GROUP_W, RW_DECAY_RANK, RW_AAA_RANK, RW_GATE_RANK)
GLA_SIZES = (N_HEADS * GLA_DK, N_HEADS * GLA_DK, GROUP_W, GLA_GATE_RANK, GROUP_W)
GROUP_SIZES = (sum(NA_SIZES), sum(MLA_SIZES), sum(RW_SIZES), sum(GLA_SIZES))
D_IN = sum(GROUP_SIZES)

kernel_name = 'hybrid_natten_mla_rwkv7_gla_dit'

F32 = jnp.float32


def split_sizes(z, sizes):
    idx = [int(s) for s in np.cumsum(sizes)[:-1]]
    return jnp.split(z, idx, axis=-1)


def rmsnorm(x, g):
    x32 = x.astype(F32)
    y = x32 * lax.rsqrt(jnp.mean(x32 * x32, axis=-1, keepdims=True) + EPS)
    return (y * g.astype(F32)).astype(x.dtype)


def heads(t):
    B, T, F = t.shape
    return t.reshape(B, T, N_HEADS, F // N_HEADS).transpose(0, 2, 1, 3)


def merge_heads(t):
    B, H, T, d = t.shape
    return t.transpose(0, 2, 1, 3).reshape(B, T, H * d)


def centred_shift(z):
    zp = jnp.pad(z, ((0, 0), (1, 1), (0, 0)))
    return 0.5 * (zp[:, :-2] + zp[:, 2:])


def dwconv_centred(z, w, b):
    pad = CONV_W // 2
    zp = jnp.pad(z, ((0, 0), (pad, pad), (0, 0)))
    T = z.shape[1]
    out = b
    for tap in range(CONV_W):
        out = out + zp[:, tap:tap + T] * w[tap]
    return out


def rotate(x, pos):
    d = x.shape[-1]
    inv = ROPE_THETA ** (-jnp.arange(0, d, 2, dtype=F32) / d)
    ang = pos[:, None] * inv[None, :]
    cos = jnp.cos(ang)[None, :, None, :].astype(x.dtype)
    sin = jnp.sin(ang)[None, :, None, :].astype(x.dtype)
    x1, x2 = jnp.split(x, 2, axis=-1)
    return jnp.concatenate([x1 * cos - x2 * sin, x1 * sin + x2 * cos], axis=-1)


def axial_rope(x, row, col):
    xr, xc = jnp.split(x, 2, axis=-1)
    return jnp.concatenate([rotate(xr, row), rotate(xc, col)], axis=-1)


def softmax_attend(q, k, v):
    logits = jnp.einsum('bhqd,bhkd->bhqk', q, k, preferred_element_type=F32) * (q.shape[-1] ** -0.5)
    p = jax.nn.softmax(logits, axis=-1).astype(v.dtype)
    return jnp.einsum('bhqk,bhkd->bhqd', p, v)


def blockwise_attend(q, k, v):
    B, H, T, d = q.shape
    nb = T // Q_BLOCK
    qb = jnp.moveaxis(q.reshape(B, H, nb, Q_BLOCK, d), 2, 0)
    out = lax.map(lambda qq: softmax_attend(qq, k, v), qb)
    return jnp.moveaxis(out, 0, 2).reshape(B, H, T, v.shape[-1])


def natten_group(z_lat, z_ctx, rpb, need_ctx):
    B, T, _ = z_lat.shape
    rows = T // GRID_W
    kr = min(NA_ROWS, rows)
    q_c, k_c, v_c = [heads(t) for t in split_sizes(z_ctx, NA_SIZES)]

    def grid(t):
        return heads(t).reshape(B, N_HEADS, rows, GRID_W, HEAD_DIM)

    q_g, k_g, v_g = [grid(t) for t in split_sizes(z_lat, NA_SIZES)]
    r = jnp.arange(rows)
    row_idx = jnp.clip(r - kr // 2, 0, rows - kr)[:, None] + jnp.arange(kr)[None, :]
    j = jnp.arange(GRID_W)
    col_start = jnp.clip(j - NA_COLS // 2, 0, GRID_W - NA_COLS)
    col_in = (j[None, :] >= col_start[:, None]) & (j[None, :] < col_start[:, None] + NA_COLS)
    row_off = row_idx - r[:, None] + (NA_ROWS - 1)
    col_off = jnp.clip(j[None, :] - j[:, None], -(NA_COLS - 1), NA_COLS - 1) + (NA_COLS - 1)
    bias = rpb.astype(F32)[:, row_off][..., col_off]
    bias = jnp.where(col_in[:, None, :], bias.transpose(0, 1, 3, 2, 4), -jnp.inf)
    k_band = k_g[:, :, row_idx]
    v_band = v_g[:, :, row_idx]
    scale = HEAD_DIM ** -0.5
    l_win = jnp.einsum('bhrqd,bhrnkd->bhrqnk', q_g, k_band, preferred_element_type=F32) * scale + bias[None]
    l_ctx = jnp.einsum('bhrqd,bhld->bhrql', q_g, k_c, preferred_element_type=F32) * scale
    nwin = kr * GRID_W
    logits = jnp.concatenate([l_win.reshape(B, N_HEADS, rows, GRID_W, nwin), l_ctx], axis=-1)
    p = jax.nn.softmax(logits, axis=-1).astype(v_g.dtype)
    p_win = p[..., :nwin].reshape(B, N_HEADS, rows, GRID_W, kr, GRID_W)
    out = (jnp.einsum('bhrqnk,bhrnkd->bhrqd', p_win, v_band)
           + jnp.einsum('bhrql,bhld->bhrqd', p[..., nwin:], v_c))
    y_lat = merge_heads(out.reshape(B, N_HEADS, T, HEAD_DIM))
    y_ctx = merge_heads(softmax_attend(q_c, k_c, v_c)) if need_ctx else None
    return y_lat, y_ctx


def mla_group(z_lat, z_ctx, row, col, q_norm, w_uq, kv_norm, w_ukv, need_ctx):
    def project(z):
        B, T, _ = z.shape
        cq, ckv, k_rope = split_sizes(z, MLA_SIZES)
        q = (rmsnorm(cq, q_norm) @ w_uq).reshape(B, T, N_HEADS, MLA_NOPE + MLA_ROPE)
        kv = (rmsnorm(ckv, kv_norm) @ w_ukv).reshape(B, T, N_HEADS, MLA_NOPE + MLA_V)
        q_nope, q_rope = jnp.split(q, [MLA_NOPE], axis=-1)
        k_nope, v = jnp.split(kv, [MLA_NOPE], axis=-1)
        return q_nope, q_rope, k_nope, k_rope[:, :, None, :], v

    def assemble(q_nope, q_rope, k_nope, k_rope, v):
        k_rope = jnp.broadcast_to(k_rope, k_nope.shape[:-1] + (MLA_ROPE,))
        q = jnp.concatenate([q_nope, q_rope], axis=-1).transpose(0, 2, 1, 3)
        k = jnp.concatenate([k_nope, k_rope], axis=-1).transpose(0, 2, 1, 3)
        return q, k, v.transpose(0, 2, 1, 3)

    q_c, k_c, v_c = assemble(*project(z_ctx))
    qn, qr, kn, krp, vl = project(z_lat)
    q_l, k_l, v_l = assemble(qn, axial_rope(qr, row, col), kn, axial_rope(krp, row, col), vl)
    y_lat = blockwise_attend(q_l, jnp.concatenate([k_l, k_c], axis=2), jnp.concatenate([v_l, v_c], axis=2))
    y_ctx = merge_heads(softmax_attend(q_c, k_c, v_c)) if need_ctx else None
    return merge_heads(y_lat), y_ctx


def rwkv_scan(S0, r, w, k, v, a, b):
    def step(S, inp):
        rt, wt, kt, vt, at, bt = inp
        sa = jnp.einsum('bhvk,bhk->bhv', S, at)
        S = S * wt[:, :, None, :] + sa[..., None] * bt[:, :, None, :] + vt[..., None] * kt[:, :, None, :]
        return S, jnp.einsum('bhvk,bhk->bhv', S, rt)

    xs = tuple(jnp.moveaxis(t.astype(F32), 1, 0) for t in (r, w, k, v, a, b))
    S, y = lax.scan(step, S0, xs)
    return jnp.moveaxis(y, 0, 1), S


def rwkv_group(z_lat, z_ctx, mu, w0, w_up, a0, a_up, g_up, k_k, k_a, r_k, ln_w, ln_b, need_ctx):
    def prep(z):
        B, T, _ = z.shape
        z = z + mu * (centred_shift(z) - z)
        r, k, v, wd, ad, gd = split_sizes(z, RW_SIZES)

        def hd(t):
            return t.reshape(B, T, N_HEADS, HEAD_DIM)

        kk = hd(k * k_k).astype(F32)
        kk = kk * lax.rsqrt(jnp.sum(kk * kk, axis=-1, keepdims=True) + 1e-12)
        per_dir = []
        for d in range(2):
            w_raw = -jax.nn.softplus(-(w0[d] + jnp.tanh(wd) @ w_up[d]).astype(F32)) - 0.5
            decay = jnp.exp(-jnp.exp(w_raw))
            a = jax.nn.sigmoid((a0[d] + ad @ a_up[d]).astype(F32))
            k_d = k.astype(F32) * (1.0 + (a - 1.0) * k_a.astype(F32))
            per_dir.append((hd(decay), hd(k_d), hd(a)))
        g = jax.nn.sigmoid(gd) @ g_up
        return hd(r).astype(F32), hd(v).astype(F32), kk, per_dir, g

    def run(p, d, S0):
        r, v, kk, per_dir, _ = p
        decay, k_d, a = per_dir[d]
        seq = (r, decay, k_d, v, -kk, kk * a)
        if d == 1:
            seq = tuple(jnp.flip(t, 1) for t in seq)
        y, S = rwkv_scan(S0, *seq)
        return (jnp.flip(y, 1) if d == 1 else y), S

    def finish(p, ys, dtype):
        r, v, kk, per_dir, g = p
        y = ys[0] + ys[1]
        mean = jnp.mean(y, axis=-1, keepdims=True)
        var = jnp.mean(jnp.square(y - mean), axis=-1, keepdims=True)
        y = ((y - mean) * lax.rsqrt(var + RW_GN_EPS) * ln_w.astype(F32).reshape(N_HEADS, HEAD_DIM)
             + ln_b.astype(F32).reshape(N_HEADS, HEAD_DIM))
        for (_, k_d, _) in per_dir:
            y = y + jnp.sum(r * k_d * r_k.astype(F32), axis=-1, keepdims=True) * v
        B, T = y.shape[:2]
        return y.reshape(B, T, GROUP_W).astype(dtype) * g

    pc = prep(z_ctx)
    pl = prep(z_lat)
    S0 = jnp.zeros((z_lat.shape[0], N_HEADS, HEAD_DIM, HEAD_DIM), F32)
    yl, yc = [], []
    for d in range(2):
        y_c, S_c = run(pc, d, S0)
        y_l, _ = run(pl, d, S_c)
        yc.append(y_c)
        yl.append(y_l)
    y_lat = finish(pl, yl, z_lat.dtype)
    y_ctx = finish(pc, yc, z_ctx.dtype) if need_ctx else None
    return y_lat, y_ctx


def gla_chunked(S0, q, k, v, log_a):
    B, H, T, _ = q.shape
    dv = v.shape[-1]
    nc = T // GLA_CHUNK

    def ch(t):
        return t.astype(F32).reshape(B, H, nc, GLA_CHUNK, t.shape[-1])

    q, k, v, log_a = ch(q), ch(k), ch(v), ch(log_a)
    b = jnp.cumsum(log_a, axis=3)
    b_last = b[:, :, :, -1:, :]
    q_e = q * jnp.exp(b)
    k_e = k * jnp.exp(-b)
    k_s = k * jnp.exp(b_last - b)
    causal = jnp.tril(jnp.ones((GLA_CHUNK, GLA_CHUNK), dtype=bool))
    A = jnp.where(causal, jnp.einsum('bhnid,bhnjd->bhnij', q_e, k_e), 0.0)
    o_intra = jnp.einsum('bhnij,bhnjv->bhniv', A, v)
    chunk_kv = jnp.einsum('bhnjd,bhnjv->bhndv', k_s, v)
    chunk_decay = jnp.exp(b_last[:, :, :, 0, :])

    def step(S, inp):
        dec, kv = inp
        return dec[..., None] * S + kv, S

    S_fin, S_prev = lax.scan(step, S0, (jnp.moveaxis(chunk_decay, 2, 0), jnp.moveaxis(chunk_kv, 2, 0)))
    o_inter = jnp.einsum('bhnid,nbhdv->bhniv', q_e, S_prev)
    return (o_intra + o_inter).reshape(B, H, T, dv), S_fin


def gla_group(z_lat, z_ctx, gate_up, gate_b, norm_g, need_ctx):
    def prep(z):
        q, k, v, gd, og = split_sizes(z, GLA_SIZES)
        log_a = [heads(jax.nn.log_sigmoid((gd @ gate_up[d] + gate_b[d]).astype(F32)) / GLA_TAU)
                 for d in range(2)]
        return heads(q) * (GLA_DK ** -0.5), heads(k), heads(v), log_a, og

    def run(p, d, S0):
        q, k, v, log_a, _ = p
        seq = (q, k, v, log_a[d])
        if d == 1:
            seq = tuple(jnp.flip(t, 2) for t in seq)
        o, S = gla_chunked(S0, *seq)
        return (jnp.flip(o, 2) if d == 1 else o), S

    def finish(p, os, dtype):
        o = (os[0] + os[1]).transpose(0, 2, 1, 3)
        o = rmsnorm(o, norm_g.reshape(N_HEADS, GLA_DV))
        B, T = o.shape[:2]
        return o.reshape(B, T, GROUP_W).astype(dtype) * jax.nn.silu(p[4])

    pc = prep(z_ctx)
    pl = prep(z_lat)
    S0 = jnp.zeros((z_lat.shape[0], N_HEADS, GLA_DK, GLA_DV), F32)
    ol, oc = [], []
    for d in range(2):
        o_c, S_c = run(pc, d, S0)
        o_l, _ = run(pl, d, S_c)
        oc.append(o_c)
        ol.append(o_l)
    y_lat = finish(pl, ol, z_lat.dtype)
    y_ctx = finish(pc, oc, z_ctx.dtype) if need_ctx else None
    return y_lat, y_ctx


def token_mixers(hl, hc, row, col, need_ctx, w_in, w_out, na_rpb, mla_q_norm, mla_w_uq, mla_kv_norm,
                 mla_w_ukv, rw_mu, rw_w0, rw_w_up, rw_a0, rw_a_up, rw_g_up, rw_k_k, rw_k_a, rw_r_k,
                 rw_ln_w, rw_ln_b, gla_gate_up, gla_gate_b, gla_norm):
    zl = split_sizes(hl @ w_in, GROUP_SIZES)
    zc = split_sizes(hc @ w_in, GROUP_SIZES)
    outs = [
        natten_group(zl[0], zc[0], na_rpb, need_ctx),
        mla_group(zl[1], zc[1], row, col, mla_q_norm, mla_w_uq, mla_kv_norm, mla_w_ukv, need_ctx),
        rwkv_group(zl[2], zc[2], rw_mu, rw_w0, rw_w_up, rw_a0, rw_a_up, rw_g_up, rw_k_k, rw_k_a,
                   rw_r_k, rw_ln_w, rw_ln_b, need_ctx),
        gla_group(zl[3], zc[3], gla_gate_up, gla_gate_b, gla_norm, need_ctx),
    ]
    y_lat = jnp.concatenate([o[0] for o in outs], axis=-1) @ w_out
    y_ctx = (jnp.concatenate([o[1] for o in outs], axis=-1) @ w_out) if need_ctx else None
    return y_lat, y_ctx


def conv_ffn(h, w_up, conv_w, conv_b, w_down):
    u = dwconv_centred(h @ w_up, conv_w, conv_b)
    val, gate = jnp.split(u, 2, axis=-1)
    return (jax.nn.silu(gate) * val) @ w_down


def modulation(cvec, w_mod, b_mod):
    return jnp.split(jax.nn.silu(cvec) @ w_mod + b_mod, 6, axis=-1)


def setup_inputs(seed: int = 0) -> dict:
    key = jax.random.key(seed)
    ks = iter(jax.random.split(key, 48))
    L = DEPTH

    def nrm(shape, scale):
        return scale * jax.random.normal(next(ks), shape, F32)

    return {
        'x': nrm((BATCH, SEQ, D_MODEL), 1.0),
        'c': nrm((BATCH, D_MODEL), 1.0),
        'ctx': nrm((BATCH, CTX_LEN, D_MODEL), 1.0),
        'c_ctx': nrm((D_MODEL,), 1.0),
        'w_mod': nrm((L, D_MODEL, 6 * D_MODEL), 0.5 * D_MODEL ** -0.5),
        'b_mod': nrm((L, 6 * D_MODEL), 0.02),
        'g_mix_pre': 1.0 + nrm((L, D_MODEL), 0.05),
        'g_mix_post': 1.0 + nrm((L, D_MODEL), 0.05),
        'g_ffn_pre': 1.0 + nrm((L, D_MODEL), 0.05),
        'g_ffn_post': 1.0 + nrm((L, D_MODEL), 0.05),
        'w_in': nrm((L, D_MODEL, D_IN), D_MODEL ** -0.5),
        'w_out': nrm((L, D_MODEL, D_MODEL), D_MODEL ** -0.5),
        'na_rpb': nrm((L, N_HEADS, 2 * NA_ROWS - 1, 2 * NA_COLS - 1), 0.1),
        'mla_q_norm': 1.0 + nrm((L, MLA_Q_RANK), 0.05),
        'mla_w_uq': nrm((L, MLA_Q_RANK, N_HEADS * (MLA_NOPE + MLA_ROPE)), MLA_Q_RANK ** -0.5),
        'mla_kv_norm': 1.0 + nrm((L, MLA_KV_RANK), 0.05),
        'mla_w_ukv': nrm((L, MLA_KV_RANK, N_HEADS * (MLA_NOPE + MLA_V)), MLA_KV_RANK ** -0.5),
        'rw_mu': jax.random.uniform(next(ks), (L, sum(RW_SIZES)), F32),
        'rw_w0': nrm((L, 2, GROUP_W), 0.5),
        'rw_w_up': nrm((L, 2, RW_DECAY_RANK, GROUP_W), 0.5 * RW_DECAY_RANK ** -0.5),
        'rw_a0': nrm((L, 2, GROUP_W), 0.5),
        'rw_a_up': nrm((L, 2, RW_AAA_RANK, GROUP_W), RW_AAA_RANK ** -0.5),
        'rw_g_up': nrm((L, RW_GATE_RANK, GROUP_W), RW_GATE_RANK ** -0.5),
        'rw_k_k': 0.85 + nrm((L, GROUP_W), 0.05),
        'rw_k_a': 1.0 + nrm((L, GROUP_W), 0.05),
        'rw_r_k': nrm((L, N_HEADS, HEAD_DIM), 0.1),
        'rw_ln_w': 1.0 + nrm((L, GROUP_W), 0.05),
        'rw_ln_b': nrm((L, GROUP_W), 0.02),
        'gla_gate_up': nrm((L, 2, GLA_GATE_RANK, N_HEADS * GLA_DK), GLA_GATE_RANK ** -0.5),
        'gla_gate_b': nrm((L, 2, N_HEADS * GLA_DK), 0.5),
        'gla_norm': 1.0 + nrm((L, GROUP_W), 0.05),
        'ffn_w_up': nrm((L, D_MODEL, 2 * D_FF), D_MODEL ** -0.5),
        'ffn_conv_w': nrm((L, CONV_W, 2 * D_FF), 0.3) + jnp.array([0.0, 1.0, 0.0], F32)[None, :, None],
        'ffn_conv_b': nrm((L, 2 * D_FF), 0.02),
        'ffn_w_down': nrm((L, D_FF, D_MODEL), D_FF ** -0.5),
    }


def reference(x, c, ctx, c_ctx, w_mod, b_mod, g_mix_pre, g_mix_post, g_ffn_pre, g_ffn_post, w_in, w_out,
              na_rpb, mla_q_norm, mla_w_uq, mla_kv_norm, mla_w_ukv, rw_mu, rw_w0, rw_w_up, rw_a0, rw_a_up,
              rw_g_up, rw_k_k, rw_k_a, rw_r_k, rw_ln_w, rw_ln_b, gla_gate_up, gla_gate_b, gla_norm,
              ffn_w_up, ffn_conv_w, ffn_conv_b, ffn_w_down):
    T = x.shape[1]
    t = jnp.arange(T)
    row = (t // GRID_W).astype(F32)
    col = (t % GRID_W).astype(F32)
    xl, xc = x, ctx
    for i in range(DEPTH):
        need_ctx = i < DEPTH - 1
        ml = [m[:, None, :] for m in modulation(c, w_mod[i], b_mod[i])]
        mc = modulation(c_ctx, w_mod[i], b_mod[i])
        hl = rmsnorm(xl, g_mix_pre[i]) * (1.0 + ml[1]) + ml[0]
        hc = rmsnorm(xc, g_mix_pre[i]) * (1.0 + mc[1]) + mc[0]
        yl, yc = token_mixers(hl, hc, row, col, need_ctx, w_in[i], w_out[i], na_rpb[i], mla_q_norm[i],
                              mla_w_uq[i], mla_kv_norm[i], mla_w_ukv[i], rw_mu[i], rw_w0[i], rw_w_up[i],
                              rw_a0[i], rw_a_up[i], rw_g_up[i], rw_k_k[i], rw_k_a[i], rw_r_k[i], rw_ln_w[i],
                              rw_ln_b[i], gla_gate_up[i], gla_gate_b[i], gla_norm[i])
        xl = xl + ml[2] * rmsnorm(yl, g_mix_post[i])
        hl = rmsnorm(xl, g_ffn_pre[i]) * (1.0 + ml[4]) + ml[3]
        xl = xl + ml[5] * rmsnorm(conv_ffn(hl, ffn_w_up[i], ffn_conv_w[i], ffn_conv_b[i], ffn_w_down[i]),
                                  g_ffn_post[i])
        if need_ctx:
            xc = xc + mc[2] * rmsnorm(yc, g_mix_post[i])
            hc = rmsnorm(xc, g_ffn_pre[i]) * (1.0 + mc[4]) + mc[3]
            xc = xc + mc[5] * rmsnorm(conv_ffn(hc, ffn_w_up[i], ffn_conv_w[i], ffn_conv_b[i], ffn_w_down[i]),
                                      g_ffn_post[i])
    return xl
```

```python
import functools

import numpy as np
import jax
import jax.numpy as jnp
from jax import lax
from jax.experimental import pallas as pl
from jax.experimental.pallas import tpu as pltpu

F32 = jnp.float32
BF16 = jnp.bfloat16

D_MODEL = 1024
GRID_W = 64
EPS = 1e-6
N_HEADS = 4
HEAD_DIM = 64
GROUP_W = 256
NA_ROWS = 8
NA_COLS = 16
MLA_Q_RANK = 192
MLA_KV_RANK = 128
MLA_NOPE = 64
MLA_ROPE = 32
MLA_HEAD_PAD = 128
ROPE_THETA = 10000.0
RW_GN_EPS = 64e-5
GLA_DK = 32
GLA_TAU = 16.0
D_FF = 2816
CHUNK = 64
TM = 256
HALO = 8
FF_CHUNK = 256
Z_COLS = 3072

COL_NA = 0
COL_MLA = 768
COL_GLA_Q = 1152
COL_GLA_K = 1280
COL_GLA_G = 1408
COL_GLA_V = 1536
COL_GLA_O = 1792
COL_RW = 2048

VMEM_LIMIT_V7X = 56 * 1024 * 1024


def _cparams(n_axes):
    return pltpu.CompilerParams(dimension_semantics=("arbitrary",) * n_axes,
                                vmem_limit_bytes=VMEM_LIMIT_V7X)


def _const_spec(shape):
    nd = len(shape)
    return pl.BlockSpec(shape, lambda *_: (0,) * nd, pipeline_mode=pl.Buffered(1))


def _dot(a, b):
    return jnp.dot(a, b, preferred_element_type=F32)


def _dot_nt(a, b):
    return lax.dot_general(a, b, (((1,), (1,)), ((), ())), preferred_element_type=F32)


def _split3(x):
    hi = x.astype(BF16)
    r1 = x - hi.astype(F32)
    mid = r1.astype(BF16)
    lo = (r1 - mid.astype(F32)).astype(BF16)
    return hi, mid, lo


def _dot_sel_lhs(m, x):
    hi, mid, lo = _split3(x)
    return _dot(m, hi) + (_dot(m, mid) + _dot(m, lo))


def _dot_sel_rhs(x, m):
    hi, mid, lo = _split3(x)
    return _dot(hi, m) + (_dot(mid, m) + _dot(lo, m))


def _mm(a, b, passes):
    if passes == 1:
        return _dot(a.astype(BF16), b.astype(BF16))
    ah, am, al = _split3(a)
    bh, bm, bl = _split3(b)
    out = _dot(ah, bh) + (_dot(ah, bm) + _dot(am, bh))
    if passes == 6:
        out = out + (_dot(am, bm) + (_dot(ah, bl) + _dot(al, bh)))
    return out


def _sigmoid(x):
    return 1.0 / (1.0 + jnp.exp(-x))


def _softplus(x):
    return jnp.maximum(x, 0.0) + jnp.log1p(jnp.exp(-jnp.abs(x)))


def _rms(x):
    return x * lax.rsqrt(jnp.mean(x * x, axis=-1, keepdims=True) + EPS)


def _norm_mod(x, g, shift, scale):
    return (_rms(x) * g) * (1.0 + scale) + shift


def _lane_head(width, per_head):
    return lax.broadcasted_iota(jnp.int32, (1, width), 1) // per_head


def _block_ones(n, blk):
    i = np.arange(n) // blk
    return (i[:, None] == i[None, :]).astype(np.float32)


def _mod_kernel(c_ref, w_ref, b_ref, o_ref):
    cv = c_ref[...]
    s = cv * _sigmoid(cv)
    o_ref[0] = _dot(s.astype(BF16), w_ref[0].astype(BF16)) + b_ref[0]


def _modulation(cvecs, w_mod, b_mod):
    L, D, N = w_mod.shape
    tn = 1536
    return pl.pallas_call(
        _mod_kernel,
        grid=(L, N // tn),
        in_specs=[pl.BlockSpec((8, D), lambda l, n: (0, 0)),
                  pl.BlockSpec((1, D, tn), lambda l, n: (l, 0, n)),
                  pl.BlockSpec((1, 1, tn), lambda l, n: (l, 0, n))],
        out_specs=pl.BlockSpec((1, 8, tn), lambda l, n: (l, 0, n)),
        out_shape=jax.ShapeDtypeStruct((L, 8, N), F32),
        compiler_params=_cparams(2),
    )(cvecs, w_mod, b_mod.reshape(L, 1, N))


def _mod_spec(nlat):
    return pl.BlockSpec((1, 1, 8, D_MODEL), lambda b, i: (b, jnp.where(i < nlat, 1, 0), 0, 0))


def _in_proj_kernel(x_ref, mod_ref, g_ref, w_ref, z_ref):
    m = mod_ref[0, 0]
    h = _norm_mod(x_ref[0], g_ref[...], m[0:1], m[1:2])
    z_ref[0] = _dot(h.astype(BF16), w_ref[...])


def _in_proj(xs, modtab, g_pre, w_in_p, nlat):
    B, S, D = xs.shape
    return pl.pallas_call(
        _in_proj_kernel,
        grid=(B, S // TM),
        in_specs=[pl.BlockSpec((1, TM, D), lambda b, i: (b, i, 0)),
                  _mod_spec(nlat),
                  _const_spec((1, D)),
                  _const_spec((D, Z_COLS))],
        out_specs=pl.BlockSpec((1, TM, Z_COLS), lambda b, i: (b, i, 0)),
        out_shape=jax.ShapeDtypeStruct((B, S, Z_COLS), F32),
        compiler_params=_cparams(2),
    )(xs, modtab, g_pre, w_in_p)


def _out_proj_kernel(x_ref, y0_ref, y1_ref, y2_ref, y3_ref, mod_ref, g_ref, w_ref, o_ref):
    m = mod_ref[0, 0]
    y = jnp.concatenate([y0_ref[0], y1_ref[0], y2_ref[0], y3_ref[0]], axis=-1)
    y = _dot(y.astype(BF16), w_ref[...])
    o_ref[0] = x_ref[0] + m[2:3] * (_rms(y) * g_ref[...])


def _out_proj(xs, ys, modtab, g_post, w_out_b, nlat):
    B, S, D = xs.shape
    yspec = pl.BlockSpec((1, TM, GROUP_W), lambda b, i: (b, i, 0))
    return pl.pallas_call(
        _out_proj_kernel,
        grid=(B, S // TM),
        in_specs=[pl.BlockSpec((1, TM, D), lambda b, i: (b, i, 0)), yspec, yspec, yspec, yspec,
                  _mod_spec(nlat), _const_spec((1, D)), _const_spec((D, D))],
        out_specs=pl.BlockSpec((1, TM, D), lambda b, i: (b, i, 0)),
        out_shape=jax.ShapeDtypeStruct((B, S, D), F32),
        compiler_params=_cparams(2),
    )(xs, *ys, modtab, g_post, w_out_b)


def _halo_specs(width, col_block, nlat, ntiles):
    per = TM // HALO
    last = ntiles * per - 1
    prev = pl.BlockSpec((1, HALO, width), lambda b, i: (b, jnp.maximum(i * per - 1, 0), col_block))
    nxt = pl.BlockSpec((1, HALO, width), lambda b, i: (b, jnp.minimum((i + 1) * per, last), col_block))
    return prev, nxt


def _neighbour_ok(i, nlat):
    prev_ok = jnp.logical_and(i != 0, i != nlat)
    next_ok = jnp.logical_and(i != nlat - 1, i < nlat)
    return prev_ok, next_ok


def _ffn_kernel(xp_ref, x_ref, xn_ref, mod_ref, gpre_ref, gpost_ref, wup_ref, cw_ref, cb_ref,
                wdn_ref, o_ref, *, nlat):
    i = pl.program_id(1)
    prev_ok, next_ok = _neighbour_ok(i, nlat)
    m = mod_ref[0, 0]
    x = x_ref[0]
    xe = jnp.concatenate([xp_ref[0], x, xn_ref[0]], axis=0)
    n = TM + 2 * HALO
    h = _norm_mod(xe, gpre_ref[...], m[3:4], m[4:5])
    row = lax.broadcasted_iota(jnp.int32, (n, 1), 0)
    valid = jnp.logical_or(jnp.logical_and(row >= HALO, row < HALO + TM),
                           jnp.logical_or(jnp.logical_and(row < HALO, prev_ok),
                                          jnp.logical_and(row >= HALO + TM, next_ok)))
    hb = jnp.where(valid, h, 0.0).astype(BF16)
    acc = jnp.zeros((TM, D_MODEL), F32)
    for c in range(D_FF // FF_CHUNK):
        halves = []
        for base in (0, D_FF):
            lo = base + c * FF_CHUNK
            z = _dot(hb, wup_ref[:, lo:lo + FF_CHUNK])
            cw = cw_ref[:, lo:lo + FF_CHUNK]
            u = (cb_ref[:, lo:lo + FF_CHUNK]
                 + pltpu.roll(z, 1, 0)[HALO:HALO + TM] * cw[0:1]
                 + z[HALO:HALO + TM] * cw[1:2]
                 + pltpu.roll(z, n - 1, 0)[HALO:HALO + TM] * cw[2:3])
            halves.append(u)
        val, gate = halves
        a = (gate * _sigmoid(gate)) * val
        acc = acc + _dot(a.astype(BF16), wdn_ref[c * FF_CHUNK:(c + 1) * FF_CHUNK, :])
    o_ref[0] = x + m[5:6] * (_rms(acc) * gpost_ref[...])


def _ffn(xs, modtab, g_pre, g_post, w_up_b, conv_w, conv_b, w_dn_b, nlat):
    B, S, D = xs.shape
    ntiles = S // TM
    prev, nxt = _halo_specs(D, 0, nlat, ntiles)
    return pl.pallas_call(
        functools.partial(_ffn_kernel, nlat=nlat),
        grid=(B, ntiles),
        in_specs=[prev, pl.BlockSpec((1, TM, D), lambda b, i: (b, i, 0)), nxt,
                  _mod_spec(nlat), _const_spec((1, D)), _const_spec((1, D)),
                  _const_spec((D, 2 * D_FF)), _const_spec((3, 2 * D_FF)), _const_spec((1, 2 * D_FF)),
                  _const_spec((D_FF, D))],
        out_specs=pl.BlockSpec((1, TM, D), lambda b, i: (b, i, 0)),
        out_shape=jax.ShapeDtypeStruct((B, S, D), F32),
        compiler_params=_cparams(2),
    )(xs, xs, xs, modtab, g_pre, g_post, w_up_b, conv_w, conv_b, w_dn_b)


def _natten_bias(rpb):
    j = np.arange(GRID_W)
    col_start = np.clip(j - NA_COLS // 2, 0, GRID_W - NA_COLS)
    col_in = (j[None, :] >= col_start[:, None]) & (j[None, :] < col_start[:, None] + NA_COLS)
    col_off = np.clip(j[None, :] - j[:, None], -(NA_COLS - 1), NA_COLS - 1) + (NA_COLS - 1)
    bq = jnp.where(col_in[None, None], rpb.astype(F32)[:, :, col_off], -jnp.inf)
    win = np.arange(NA_ROWS)[:, None] + np.arange(NA_ROWS)[None, :]
    b2 = bq[:, win]
    return b2.transpose(0, 1, 3, 2, 4).reshape(N_HEADS, NA_ROWS, GRID_W, NA_ROWS * GRID_W)


def _natten_kernel(q_ref, k_ref, v_ref, qc_ref, kc_ref, vc_ref, bias_ref, o_ref, *, nlat, rows):
    j = pl.program_id(1)
    lane_h = _lane_head(GROUP_W, HEAD_DIM)
    kc = kc_ref[0].astype(BF16)
    vc = vc_ref[0].astype(BF16)
    scale = HEAD_DIM ** -0.5
    rows_per_tile = TM // GRID_W
    nwin = NA_ROWS * GRID_W

    @pl.when(j < nlat)
    def _():
        def body(rl, carry):
            r = j * rows_per_tile + rl
            rs = jnp.clip(r - NA_ROWS // 2, 0, rows - NA_ROWS)
            s0 = rs - r + (NA_ROWS - 1)
            q = q_ref[0, pl.ds(pl.multiple_of(rl * GRID_W, GRID_W), GRID_W), :] * scale
            kw = k_ref[0, pl.ds(pl.multiple_of(rs * GRID_W, GRID_W), nwin), :].astype(BF16)
            vw = v_ref[0, pl.ds(pl.multiple_of(rs * GRID_W, GRID_W), nwin), :].astype(BF16)
            acc = jnp.zeros((GRID_W, GROUP_W), F32)
            for h in range(N_HEADS):
                hm = lane_h == h
                qh = jnp.where(hm, q, 0.0).astype(BF16)
                s_w = _dot_nt(qh, kw) + bias_ref[h, s0]
                s_c = _dot_nt(qh, kc)
                mx = jnp.maximum(jnp.max(s_w, axis=-1, keepdims=True), jnp.max(s_c, axis=-1, keepdims=True))
                p_w = jnp.exp(s_w - mx)
                p_c = jnp.exp(s_c - mx)
                den = jnp.sum(p_w, axis=-1, keepdims=True) + jnp.sum(p_c, axis=-1, keepdims=True)
                o = _dot(p_w.astype(BF16), vw) + _dot(p_c.astype(BF16), vc)
                acc = acc + jnp.where(hm, o * (1.0 / den), 0.0)
            o_ref[0, pl.ds(pl.multiple_of(rl * GRID_W, GRID_W), GRID_W), :] = acc
            return carry
        lax.fori_loop(0, rows_per_tile, body, 0)

    @pl.when(j >= nlat)
    def _():
        q = qc_ref[0] * scale
        acc = jnp.zeros((TM, GROUP_W), F32)
        for h in range(N_HEADS):
            hm = lane_h == h
            s = _dot_nt(jnp.where(hm, q, 0.0).astype(BF16), kc)
            p = jnp.exp(s - jnp.max(s, axis=-1, keepdims=True))
            den = jnp.sum(p, axis=-1, keepdims=True)
            acc = acc + jnp.where(hm, _dot(p.astype(BF16), vc) * (1.0 / den), 0.0)
        o_ref[0] = acc


def _natten(z, bias, T):
    B, S, _ = z.shape
    nlat = T // TM
    ntiles = S // TM
    rows = T // GRID_W
    assert rows >= NA_ROWS and S - T == TM
    cb = COL_NA // GROUP_W
    lat = lambda c: pl.BlockSpec((1, T, GROUP_W), lambda b, j: (b, 0, cb + c))
    ctx = lambda c: pl.BlockSpec((1, TM, GROUP_W), lambda b, j: (b, nlat, cb + c))
    return pl.pallas_call(
        functools.partial(_natten_kernel, nlat=nlat, rows=rows),
        grid=(B, ntiles),
        in_specs=[pl.BlockSpec((1, TM, GROUP_W), lambda b, j: (b, jnp.minimum(j, nlat - 1), cb)),
                  lat(1), lat(2), ctx(0), ctx(1), ctx(2),
                  _const_spec((N_HEADS, NA_ROWS, GRID_W, NA_ROWS * GRID_W))],
        out_specs=pl.BlockSpec((1, TM, GROUP_W), lambda b, j: (b, j, 0)),
        out_shape=jax.ShapeDtypeStruct((B, S, GROUP_W), F32),
        compiler_params=_cparams(2),
    )(z, z, z, z, z, z, bias)


def _rope_tables(T, Tc):
    t = np.arange(T)
    row = (t // GRID_W).astype(np.float32)
    col = (t % GRID_W).astype(np.float32)
    d = MLA_ROPE // 2
    inv = ROPE_THETA ** (-jnp.arange(0, d, 2, dtype=F32) / d)
    cs, sn = [], []
    for pos in (row, col):
        ang = jnp.asarray(pos)[:, None] * inv[None, :]
        cs += [jnp.cos(ang), jnp.cos(ang)]
        sn += [-jnp.sin(ang), jnp.sin(ang)]
    cos = jnp.concatenate([jnp.ones((T, MLA_NOPE), F32)] + cs + [jnp.ones((T, MLA_HEAD_PAD - MLA_NOPE - MLA_ROPE), F32)], axis=1)
    sin = jnp.concatenate([jnp.zeros((T, MLA_NOPE), F32)] + sn + [jnp.zeros((T, MLA_HEAD_PAD - MLA_NOPE - MLA_ROPE), F32)], axis=1)
    cos = jnp.concatenate([cos, jnp.ones((Tc, MLA_HEAD_PAD), F32)], axis=0)
    sin = jnp.concatenate([sin, jnp.zeros((Tc, MLA_HEAD_PAD), F32)], axis=0)
    return cos, sin


def _rope_swap_perm():
    q = MLA_ROPE // 4
    return np.concatenate([np.arange(q, 2 * q), np.arange(0, q), np.arange(3 * q, 4 * q), np.arange(2 * q, 3 * q)])


def _mla_weights(w_uq, w_ukv):
    L = w_uq.shape[0]
    wq = w_uq.reshape(L, MLA_Q_RANK, N_HEADS, MLA_NOPE + MLA_ROPE)
    pad = MLA_HEAD_PAD - MLA_NOPE - MLA_ROPE
    zq = jnp.zeros((L, MLA_Q_RANK, N_HEADS, pad), F32)
    wq1 = jnp.concatenate([wq, zq], axis=-1).reshape(L, MLA_Q_RANK, N_HEADS * MLA_HEAD_PAD)
    rope_sw = wq[..., MLA_NOPE:][..., _rope_swap_perm()]
    wq2 = jnp.concatenate([jnp.zeros((L, MLA_Q_RANK, N_HEADS, MLA_NOPE), F32), rope_sw, zq], axis=-1)
    wq2 = wq2.reshape(L, MLA_Q_RANK, N_HEADS * MLA_HEAD_PAD)
    wkv = w_ukv.reshape(L, MLA_KV_RANK, N_HEADS, 2 * MLA_NOPE)
    wk = jnp.concatenate([wkv[..., :MLA_NOPE], jnp.zeros((L, MLA_KV_RANK, N_HEADS, MLA_HEAD_PAD - MLA_NOPE), F32)], axis=-1)
    wk = wk.reshape(L, MLA_KV_RANK, N_HEADS * MLA_HEAD_PAD)
    wv = wkv[..., MLA_NOPE:].reshape(L, MLA_KV_RANK, N_HEADS * MLA_NOPE)
    return wq1.astype(BF16), wq2.astype(BF16), wk.astype(BF16), wv.astype(BF16)


def _rope_place():
    e = np.zeros((MLA_ROPE, N_HEADS * MLA_HEAD_PAD), np.float32)
    for h in range(N_HEADS):
        e[np.arange(MLA_ROPE), h * MLA_HEAD_PAD + MLA_NOPE + np.arange(MLA_ROPE)] = 1.0
    return jnp.asarray(e, BF16)


def _mla_up_kernel(z_ref, cos_ref, sin_ref, qn_ref, kvn_ref, wq1_ref, wq2_ref, wk_ref, wv_ref, e_ref,
                   q_ref, kt_ref, v_ref):
    z = z_ref[0]
    ckv = z[:, :MLA_KV_RANK]
    cq = z[:, MLA_KV_RANK:MLA_KV_RANK + MLA_Q_RANK]
    kr = z[:, MLA_KV_RANK + MLA_Q_RANK:MLA_KV_RANK + MLA_Q_RANK + MLA_ROPE]
    krs = z[:, MLA_KV_RANK + MLA_Q_RANK + MLA_ROPE:]
    cos = jnp.concatenate([cos_ref[...]] * N_HEADS, axis=-1)
    sin = jnp.concatenate([sin_ref[...]] * N_HEADS, axis=-1)
    nq = (_rms(cq) * qn_ref[...]).astype(BF16)
    nkv = (_rms(ckv) * kvn_ref[...]).astype(BF16)
    q = _dot(nq, wq1_ref[...]) * cos + _dot(nq, wq2_ref[...]) * sin
    scale = (MLA_NOPE + MLA_ROPE) ** -0.5
    q_ref[0] = (q * scale).astype(BF16)
    k = _dot(nkv, wk_ref[...]) + _dot_sel_rhs(kr, e_ref[...]) * cos + _dot_sel_rhs(krs, e_ref[...]) * sin
    kt_ref[0] = k.T.astype(BF16)
    v_ref[0] = _dot(nkv, wv_ref[...]).astype(BF16)


def _mla_up(z, cos, sin, q_norm, kv_norm, wq1, wq2, wk, wv, place):
    B, S, _ = z.shape
    HP = N_HEADS * MLA_HEAD_PAD
    zw = MLA_KV_RANK + MLA_Q_RANK + 2 * MLA_ROPE
    tab = pl.BlockSpec((TM, MLA_HEAD_PAD), lambda b, i: (i, 0))
    return pl.pallas_call(
        _mla_up_kernel,
        grid=(B, S // TM),
        in_specs=[pl.BlockSpec((1, TM, zw), lambda b, i: (b, i, COL_MLA // zw)), tab, tab,
                  _const_spec((1, MLA_Q_RANK)), _const_spec((1, MLA_KV_RANK)),
                  _const_spec((MLA_Q_RANK, HP)), _const_spec((MLA_Q_RANK, HP)),
                  _const_spec((MLA_KV_RANK, HP)), _const_spec((MLA_KV_RANK, GROUP_W)),
                  _const_spec((MLA_ROPE, HP))],
        out_specs=[pl.BlockSpec((1, TM, HP), lambda b, i: (b, i, 0)),
                   pl.BlockSpec((1, HP, TM), lambda b, i: (b, 0, i)),
                   pl.BlockSpec((1, TM, GROUP_W), lambda b, i: (b, i, 0))],
        out_shape=[jax.ShapeDtypeStruct((B, S, HP), BF16),
                   jax.ShapeDtypeStruct((B, HP, S), BF16),
                   jax.ShapeDtypeStruct((B, S, GROUP_W), BF16)],
        compiler_params=_cparams(2),
    )(z, cos, sin, q_norm, kv_norm, wq1, wq2, wk, wv, place)


def _mla_attn_kernel(q_ref, kt_ref, v_ref, o_ref, *, nlat, T):
    j = pl.program_id(1)
    lane_h = _lane_head(GROUP_W, HEAD_DIM)

    def attend(lo, hi):
        acc = jnp.zeros((TM, GROUP_W), F32)
        v = v_ref[0, lo:hi, :]
        for h in range(N_HEADS):
            q = q_ref[0, :, h * MLA_HEAD_PAD:(h + 1) * MLA_HEAD_PAD]
            s = _dot(q, kt_ref[0, h * MLA_HEAD_PAD:(h + 1) * MLA_HEAD_PAD, lo:hi])
            p = jnp.exp(s - jnp.max(s, axis=-1, keepdims=True))
            den = jnp.sum(p, axis=-1, keepdims=True)
            acc = acc + jnp.where(lane_h == h, _dot(p.astype(BF16), v) * (1.0 / den), 0.0)
        o_ref[0] = acc

    @pl.when(j < nlat)
    def _():
        attend(0, T + TM)

    @pl.when(j >= nlat)
    def _():
        attend(T, T + TM)


def _mla_attn(q, kt, v, T):
    B, S, HP = q.shape
    nlat = T // TM
    return pl.pallas_call(
        functools.partial(_mla_attn_kernel, nlat=nlat, T=T),
        grid=(B, S // TM),
        in_specs=[pl.BlockSpec((1, TM, HP), lambda b, j: (b, j, 0)),
                  pl.BlockSpec((1, HP, S), lambda b, j: (b, 0, 0)),
                  pl.BlockSpec((1, S, GROUP_W), lambda b, j: (b, 0, 0))],
        out_specs=pl.BlockSpec((1, TM, GROUP_W), lambda b, j: (b, j, 0)),
        out_shape=jax.ShapeDtypeStruct((B, S, GROUP_W), F32),
        compiler_params=_cparams(2),
    )(q, kt, v)


def _scan_masks():
    t = np.arange(CHUNK)
    inc = np.stack([t[:, None] >= t[None, :], t[:, None] <= t[None, :]]).astype(np.float32)
    strict = np.stack([t[:, None] > t[None, :], t[:, None] < t[None, :]]).astype(np.float32)
    blk = _block_ones(N_HEADS * CHUNK, CHUNK)
    inc_bd = np.tile(inc, (1, N_HEADS, N_HEADS)) * blk
    strict_bd = np.tile(strict, (1, N_HEADS, N_HEADS)) * blk
    return jnp.asarray(inc, BF16), jnp.asarray(inc_bd), jnp.asarray(strict_bd)


def _scan_tile(d, p, nlat):
    return jnp.where(p == 0, nlat, jnp.where(d == 0, p - 1, nlat - p))


def _stack_heads(x, lane_h):
    return jnp.concatenate([jnp.where(lane_h == h, x, 0.0) for h in range(N_HEADS)], axis=0)


def _tile_rows(x):
    return jnp.concatenate([x] * N_HEADS, axis=0)


def _collapse_heads(x):
    c = x.shape[0] // N_HEADS
    return (x[0:c] + x[c:2 * c]) + (x[2 * c:3 * c] + x[3 * c:4 * c])


RW_PASSES_GRAM = 1
RW_PASSES_INV = 1
RW_PASSES_APPLY = 1


def _rw_prep_kernel(zp_ref, z_ref, zn_ref, mu_ref, kk_ref, ka_ref, rk_ref, w0_ref, a0_ref, wup_ref,
                    aup_ref, gup_ref, ones_ref,
                    r_ref, v_ref, ah_ref, g_ref, bon_ref, lw_ref, kd_ref, bd_ref, *, nlat):
    i = pl.program_id(1)
    prev_ok, next_ok = _neighbour_ok(i, nlat)
    z = z_ref[0]
    row = lax.broadcasted_iota(jnp.int32, (TM, 1), 0)
    before = jnp.where(prev_ok, zp_ref[0, HALO - 1:HALO, :], 0.0)
    after = jnp.where(next_ok, zn_ref[0, 0:1, :], 0.0)
    zprev = jnp.where(row == 0, before, pltpu.roll(z, 1, 0))
    znext = jnp.where(row == TM - 1, after, pltpu.roll(z, TM - 1, 0))
    zs = z + mu_ref[...] * (0.5 * (zprev + znext) - z)
    r = zs[:, 0:GROUP_W]
    k = zs[:, GROUP_W:2 * GROUP_W]
    v = zs[:, 2 * GROUP_W:3 * GROUP_W]
    low = zs[:, 3 * GROUP_W:3 * GROUP_W + 128]
    gd = zs[:, 3 * GROUP_W + 128:]
    ones = ones_ref[...]
    kk = k * kk_ref[...]
    kk = kk * lax.rsqrt(_dot_sel_rhs(kk * kk, ones) + 1e-12)
    wl = _dot(jnp.tanh(low).astype(BF16), wup_ref[...])
    al = _dot(low.astype(BF16), aup_ref[...])
    ksum = jnp.zeros((TM, GROUP_W), F32)
    for d in range(2):
        w_raw = -_softplus(-(w0_ref[d:d + 1, :] + wl[:, d * GROUP_W:(d + 1) * GROUP_W])) - 0.5
        lw_ref[d, 0] = -jnp.exp(w_raw)
        a = _sigmoid(a0_ref[d:d + 1, :] + al[:, d * GROUP_W:(d + 1) * GROUP_W])
        kd = k * (1.0 + (a - 1.0) * ka_ref[...])
        kd_ref[d, 0] = kd
        bd_ref[d, 0] = kk * a
        ksum = ksum + kd
    r_ref[0] = r
    v_ref[0] = v
    ah_ref[0] = -kk
    g_ref[0] = _dot(_sigmoid(gd).astype(BF16), gup_ref[...])
    bon_ref[0] = _dot_sel_rhs(r * ksum * rk_ref[...], ones) * v


def _rw_prep(z, mu, k_k, k_a, r_k, w0, a0, wup_p, aup_p, gup_b, ones_b, nlat):
    B, S, _ = z.shape
    ntiles = S // TM
    W = 4 * GROUP_W
    cb = COL_RW // W
    prev, nxt = _halo_specs(W, cb, nlat, ntiles)
    one = pl.BlockSpec((1, TM, GROUP_W), lambda b, i: (b, i, 0))
    two = pl.BlockSpec((2, 1, TM, GROUP_W), lambda b, i: (0, b, i, 0))
    s1 = jax.ShapeDtypeStruct((B, S, GROUP_W), F32)
    s2 = jax.ShapeDtypeStruct((2, B, S, GROUP_W), F32)
    return pl.pallas_call(
        functools.partial(_rw_prep_kernel, nlat=nlat),
        grid=(B, ntiles),
        in_specs=[prev, pl.BlockSpec((1, TM, W), lambda b, i: (b, i, cb)), nxt,
                  _const_spec((1, W)), _const_spec((1, GROUP_W)), _const_spec((1, GROUP_W)),
                  _const_spec((1, GROUP_W)), _const_spec((2, GROUP_W)), _const_spec((2, GROUP_W)),
                  _const_spec((128, 2 * GROUP_W)), _const_spec((128, 2 * GROUP_W)),
                  _const_spec((128, GROUP_W)), _const_spec((GROUP_W, GROUP_W))],
        out_specs=[one, one, one, one, one, two, two, two],
        out_shape=[s1, s1, s1, s1, s1, s2, s2, s2],
        compiler_params=_cparams(2),
    )(z, z, z, mu, k_k, k_a, r_k, w0, a0, wup_p, aup_p, gup_b, ones_b)


def _rw_scan_kernel(r_ref, v_ref, ah_ref, lw_ref, kd_ref, bd_ref, mi_ref, mibd_ref, msbd_ref, bm_ref,
                    y_ref, s_ref):
    d = pl.program_id(0)
    p = pl.program_id(2)

    @pl.when(p == 0)
    def _():
        s_ref[...] = jnp.zeros_like(s_ref)

    lane_h = _lane_head(GROUP_W, HEAD_DIM)
    mi = mi_ref[0]
    inc_bd = mibd_ref[0]
    strict_bd = msbd_ref[0]
    bm = bm_ref[...]
    ones_c = jnp.ones((CHUNK, CHUNK), BF16)
    n = N_HEADS * CHUNK
    eye = (lax.broadcasted_iota(jnp.int32, (n, n), 0) == lax.broadcasted_iota(jnp.int32, (n, n), 1)).astype(F32)
    nch = TM // CHUNK
    for j in range(nch):
        ci = jnp.where(d == 0, j, nch - 1 - j)
        sl = pl.ds(pl.multiple_of(ci * CHUNK, CHUNK), CHUNK)
        r = r_ref[0, sl, :]
        v = v_ref[0, sl, :]
        ah = ah_ref[0, sl, :]
        lw = lw_ref[0, 0, sl, :]
        kd = kd_ref[0, 0, sl, :]
        bd = bd_ref[0, 0, sl, :]
        cs = _dot_sel_lhs(mi, lw)
        tot = _dot_sel_lhs(ones_c, lw)
        e_neg = jnp.exp(-cs)
        e_hat = jnp.exp(tot - cs)
        a_t = ah * jnp.exp(cs - lw)
        r_t = r * jnp.exp(cs)
        b_t = bd * e_neg
        k_t = kd * e_neg
        a_s = _stack_heads(a_t, lane_h)
        r_s = _stack_heads(r_t, lane_h)
        v_s = _stack_heads(v, lane_h)
        bt4 = _tile_rows(b_t).T
        kt4 = _tile_rows(k_t).T
        n_bd = _mm(a_s, bt4, RW_PASSES_GRAM) * strict_bd
        ak_bd = _mm(a_s, kt4, RW_PASSES_GRAM) * strict_bd
        rb_bd = _mm(r_s, bt4, RW_PASSES_GRAM) * inc_bd
        rk_bd = _mm(r_s, kt4, RW_PASSES_GRAM) * inc_bd
        x = eye + n_bd
        pw = n_bd
        for _ in range(5):
            pw = _mm(pw, pw, RW_PASSES_INV)
            x = x + _mm(x, pw, RW_PASSES_INV)
        p1 = _mm(x, _mm(ak_bd, v_s, RW_PASSES_APPLY), RW_PASSES_APPLY)
        p2 = _mm(x, a_s, RW_PASSES_APPLY)
        y_loc = _mm(rk_bd, v_s, RW_PASSES_APPLY)
        bk_t = jnp.concatenate([bd * e_hat, kd * e_hat], axis=0).T
        decay = _tile_rows(jnp.exp(tot)).T
        s = s_ref[...]
        u_s = p1 + _mm(p2, s, RW_PASSES_APPLY)
        u = _collapse_heads(u_s)
        y_s = _mm(r_s, s, RW_PASSES_APPLY) + _mm(rb_bd, u_s, RW_PASSES_APPLY) + y_loc
        y_ref[0, 0, sl, :] = _collapse_heads(y_s)
        s_ref[...] = decay * s + _mm(bk_t, jnp.concatenate([u, v], axis=0), RW_PASSES_APPLY) * bm


def _rw_scan(r, v, ah, lw, kd, bd, masks, bm, nlat):
    B, S, _ = r.shape
    mi, mibd, msbd = masks
    n = N_HEADS * CHUNK
    one = pl.BlockSpec((1, TM, GROUP_W), lambda d, b, p: (b, _scan_tile(d, p, nlat), 0))
    two = pl.BlockSpec((1, 1, TM, GROUP_W), lambda d, b, p: (d, b, _scan_tile(d, p, nlat), 0))
    return pl.pallas_call(
        _rw_scan_kernel,
        grid=(2, B, S // TM),
        in_specs=[one, one, one, two, two, two,
                  pl.BlockSpec((1, CHUNK, CHUNK), lambda d, b, p: (d, 0, 0)),
                  pl.BlockSpec((1, n, n), lambda d, b, p: (d, 0, 0)),
                  pl.BlockSpec((1, n, n), lambda d, b, p: (d, 0, 0)),
                  _const_spec((n, n))],
        out_specs=two,
        out_shape=jax.ShapeDtypeStruct((2, B, S, GROUP_W), F32),
        scratch_shapes=[pltpu.VMEM((n, GROUP_W), F32)],
        compiler_params=_cparams(3),
    )(r, v, ah, lw, kd, bd, mi, mibd, msbd, bm)


def _rw_finish_kernel(y_ref, bon_ref, g_ref, lnw_ref, lnb_ref, ones_ref, o_ref):
    ones = ones_ref[...]
    y = y_ref[0, 0] + y_ref[1, 0]
    mean = _dot_sel_rhs(y, ones) * (1.0 / HEAD_DIM)
    yc = y - mean
    var = _dot_sel_rhs(yc * yc, ones) * (1.0 / HEAD_DIM)
    yn = yc * lax.rsqrt(var + RW_GN_EPS) * lnw_ref[...] + lnb_ref[...]
    o_ref[0] = (yn + bon_ref[0]) * g_ref[0]


def _rw_finish(y2, bon, g, ln_w, ln_b, ones_b):
    _, B, S, _ = y2.shape
    one = pl.BlockSpec((1, TM, GROUP_W), lambda b, i: (b, i, 0))
    return pl.pallas_call(
        _rw_finish_kernel,
        grid=(B, S // TM),
        in_specs=[pl.BlockSpec((2, 1, TM, GROUP_W), lambda b, i: (0, b, i, 0)), one, one,
                  _const_spec((1, GROUP_W)), _const_spec((1, GROUP_W)), _const_spec((GROUP_W, GROUP_W))],
        out_specs=one,
        out_shape=jax.ShapeDtypeStruct((B, S, GROUP_W), F32),
        compiler_params=_cparams(2),
    )(y2, bon, g, ln_w, ln_b, ones_b)


def _gla_scan_kernel(q_ref, k_ref, gd_ref, v_ref, gup_ref, gb_ref, mi_ref, mibd_ref, bm_ref, o_ref, s_ref):
    d = pl.program_id(0)
    p = pl.program_id(2)

    @pl.when(p == 0)
    def _():
        s_ref[...] = jnp.zeros_like(s_ref)

    lane_hk = _lane_head(N_HEADS * GLA_DK, GLA_DK)
    lane_hv = _lane_head(GROUP_W, HEAD_DIM)
    mi = mi_ref[0]
    inc_bd = mibd_ref[0]
    bm = bm_ref[...]
    ones_c = jnp.ones((CHUNK, CHUNK), BF16)
    gup = gup_ref[0]
    gb = gb_ref[0]
    nch = TM // CHUNK
    for j in range(nch):
        ci = jnp.where(d == 0, j, nch - 1 - j)
        sl = pl.ds(pl.multiple_of(ci * CHUNK, CHUNK), CHUNK)
        q = q_ref[0, sl, :] * (GLA_DK ** -0.5)
        k = k_ref[0, sl, :]
        v = v_ref[0, sl, :]
        la = -_softplus(-(_dot(gd_ref[0, sl, :].astype(BF16), gup) + gb)) * (1.0 / GLA_TAU)
        b = _dot_sel_lhs(mi, la)
        tot = _dot_sel_lhs(ones_c, la)
        q_s = _stack_heads(q * jnp.exp(b), lane_hk)
        ke4 = _tile_rows(k * jnp.exp(-b)).T
        ks4 = _tile_rows(k * jnp.exp(tot - b)).T
        v_s = _stack_heads(v, lane_hv)
        a_bd = _dot(q_s.astype(BF16), ke4.astype(BF16)) * inc_bd
        s = s_ref[...]
        o_s = _dot(a_bd.astype(BF16), v_s.astype(BF16)) + _dot(q_s.astype(BF16), s.astype(BF16))
        o_ref[0, 0, sl, :] = _collapse_heads(o_s)
        decay = _tile_rows(jnp.exp(tot)).T
        s_ref[...] = decay * s + _dot(ks4.astype(BF16), v_s.astype(BF16)) * bm


def _gla_scan(z, gup_p, gb, masks, bm, T):
    B, S, _ = z.shape
    nlat = T // TM
    mi, mibd, _ = masks
    n = N_HEADS * CHUNK
    wk = N_HEADS * GLA_DK
    zs = lambda w, col: pl.BlockSpec((1, TM, w), lambda d, b, p: (b, _scan_tile(d, p, nlat), col // w))
    return pl.pallas_call(
        _gla_scan_kernel,
        grid=(2, B, S // TM),
        in_specs=[zs(wk, COL_GLA_Q), zs(wk, COL_GLA_K), zs(wk, COL_GLA_G), zs(GROUP_W, COL_GLA_V),
                  pl.BlockSpec((1, wk, wk), lambda d, b, p: (d, 0, 0)),
                  pl.BlockSpec((1, 1, wk), lambda d, b, p: (d, 0, 0)),
                  pl.BlockSpec((1, CHUNK, CHUNK), lambda d, b, p: (d, 0, 0)),
                  pl.BlockSpec((1, n, n), lambda d, b, p: (d, 0, 0)),
                  _const_spec((wk, GROUP_W))],
        out_specs=pl.BlockSpec((1, 1, TM, GROUP_W), lambda d, b, p: (d, b, _scan_tile(d, p, nlat), 0)),
        out_shape=jax.ShapeDtypeStruct((2, B, S, GROUP_W), F32),
        scratch_shapes=[pltpu.VMEM((wk, GROUP_W), F32)],
        compiler_params=_cparams(3),
    )(z, z, z, z, gup_p, gb, mi, mibd, bm)


def _gla_finish_kernel(o_ref, og_ref, ng_ref, ones_ref, y_ref):
    o = o_ref[0, 0] + o_ref[1, 0]
    ms = _dot_sel_rhs(o * o, ones_ref[...]) * (1.0 / HEAD_DIM)
    og = og_ref[0]
    y_ref[0] = (o * lax.rsqrt(ms + EPS) * ng_ref[...]) * (og * _sigmoid(og))


def _gla_finish(o2, z, norm_g, ones_b):
    _, B, S, _ = o2.shape
    return pl.pallas_call(
        _gla_finish_kernel,
        grid=(B, S // TM),
        in_specs=[pl.BlockSpec((2, 1, TM, GROUP_W), lambda b, i: (0, b, i, 0)),
                  pl.BlockSpec((1, TM, GROUP_W), lambda b, i: (b, i, COL_GLA_O // GROUP_W)),
                  _const_spec((1, GROUP_W)), _const_spec((GROUP_W, GROUP_W))],
        out_specs=pl.BlockSpec((1, TM, GROUP_W), lambda b, i: (b, i, 0)),
        out_shape=jax.ShapeDtypeStruct((B, S, GROUP_W), F32),
        compiler_params=_cparams(2),
    )(o2, z, norm_g, ones_b)


def _pack_w_in(w_in):
    L, D, _ = w_in.shape
    na, mla, rw, gla = jnp.split(w_in, [768, 1120, 2144], axis=-1)
    cq, ckv, kr = jnp.split(mla, [MLA_Q_RANK, MLA_Q_RANK + MLA_KV_RANK], axis=-1)
    gq, gk, gv, gg, go = jnp.split(gla, [128, 256, 512, 528], axis=-1)
    pad = jnp.zeros((L, D, COL_GLA_V - COL_GLA_G - gg.shape[-1]), F32)
    packed = jnp.concatenate([na, ckv, cq, kr, kr[..., _rope_swap_perm()], gq, gk, gg, pad, gv, go, rw], axis=-1)
    assert packed.shape[-1] == Z_COLS
    return packed.astype(BF16)


def kernel(x, c, ctx, c_ctx, w_mod, b_mod, g_mix_pre, g_mix_post, g_ffn_pre, g_ffn_post, w_in, w_out, na_rpb, mla_q_norm, mla_w_uq, mla_kv_norm, mla_w_ukv, rw_mu, rw_w0, rw_w_up, rw_a0, rw_a_up, rw_g_up, rw_k_k, rw_k_a, rw_r_k, rw_ln_w, rw_ln_b, gla_gate_up, gla_gate_b, gla_norm, ffn_w_up, ffn_conv_w, ffn_conv_b, ffn_w_down):
    B, T, D = x.shape
    Tc = ctx.shape[1]
    L = w_in.shape[0]
    assert D == D_MODEL and Tc == TM and T % TM == 0 and B + 1 <= 8
    nlat = T // TM

    w_in_p = _pack_w_in(w_in)
    w_out_b = w_out.astype(BF16)
    wq1, wq2, wk, wv = _mla_weights(mla_w_uq, mla_w_ukv)
    place = _rope_place()
    cos, sin = _rope_tables(T, Tc)
    zero_lo = jnp.zeros((L, 64, 2 * GROUP_W), F32)
    rw_wup_p = jnp.concatenate([jnp.concatenate([rw_w_up[:, 0], rw_w_up[:, 1]], axis=-1), zero_lo], axis=1).astype(BF16)
    rw_aup_p = jnp.concatenate([zero_lo, jnp.concatenate([rw_a_up[:, 0], rw_a_up[:, 1]], axis=-1)], axis=1).astype(BF16)
    rw_gup_b = rw_g_up.astype(BF16)
    wk_gla = N_HEADS * GLA_DK
    gla_gup_p = jnp.concatenate([gla_gate_up, jnp.zeros((L, 2, wk_gla - gla_gate_up.shape[2], wk_gla), F32)], axis=2).astype(BF16)
    ffn_up_b = ffn_w_up.astype(BF16)
    ffn_dn_b = ffn_w_down.astype(BF16)
    ones_b = jnp.asarray(_block_ones(GROUP_W, HEAD_DIM), BF16)
    rw_bm = jnp.asarray(_block_ones(N_HEADS * CHUNK, CHUNK))
    gla_bm = jnp.asarray((np.arange(wk_gla)[:, None] // GLA_DK == np.arange(GROUP_W)[None, :] // HEAD_DIM).astype(np.float32))
    masks = _scan_masks()

    cvecs = jnp.zeros((8, D), F32).at[:B].set(c).at[B].set(c_ctx)
    mods = _modulation(cvecs, w_mod, b_mod).reshape(L, 8, 6, D)
    mods = jnp.pad(mods, ((0, 0), (0, 0), (0, 2), (0, 0)))
    modtabs = jnp.stack([jnp.broadcast_to(mods[:, B:B + 1], (L, B, 8, D)), mods[:, :B]], axis=2)

    xs = jnp.concatenate([x, ctx], axis=1)
    row = lambda a: a.reshape(1, -1)
    for i in range(L):
        modtab = modtabs[i]
        z = _in_proj(xs, modtab, row(g_mix_pre[i]), w_in_p[i], nlat)
        y_na = _natten(z, _natten_bias(na_rpb[i]), T)
        q, kt, v = _mla_up(z, cos, sin, row(mla_q_norm[i]), row(mla_kv_norm[i]), wq1[i], wq2[i], wk[i], wv[i], place)
        y_mla = _mla_attn(q, kt, v, T)
        r, vv, ah, g, bon, lw, kd, bd = _rw_prep(z, row(rw_mu[i]), row(rw_k_k[i]), row(rw_k_a[i]), row(rw_r_k[i]),
                                                 rw_w0[i], rw_a0[i], rw_wup_p[i], rw_aup_p[i], rw_gup_b[i], ones_b, nlat)
        y2 = _rw_scan(r, vv, ah, lw, kd, bd, masks, rw_bm, nlat)
        y_rw = _rw_finish(y2, bon, g, row(rw_ln_w[i]), row(rw_ln_b[i]), ones_b)
        o2 = _gla_scan(z, gla_gup_p[i], gla_gate_b[i][:, None, :], masks, gla_bm, T)
        y_gla = _gla_finish(o2, z, row(gla_norm[i]), ones_b)
        xs = _out_proj(xs, (y_na, y_mla, y_rw, y_gla), modtab, row(g_mix_post[i]), w_out_b[i], nlat)
        xs = _ffn(xs, modtab, row(g_ffn_pre[i]), row(g_ffn_post[i]), ffn_up_b[i], ffn_conv_w[i],
                  row(ffn_conv_b[i]), ffn_dn_b[i], nlat)
    return xs[:, :T]
```

```python
import functools

import numpy as np
import jax
import jax.numpy as jnp
from jax import lax
from jax.experimental import pallas as pl
from jax.experimental.pallas import tpu as pltpu

F32 = jnp.float32
BF16 = jnp.bfloat16

D_MODEL = 1024
GRID_W = 64
EPS = 1e-6
N_HEADS = 4
HEAD_DIM = 64
GROUP_W = 256
NA_ROWS = 8
NA_COLS = 16
NA_UNION = 12
MLA_Q_RANK = 192
MLA_KV_RANK = 128
MLA_NOPE = 64
MLA_ROPE = 32
MLA_HEAD_PAD = 128
ROPE_THETA = 10000.0
RW_GN_EPS = 64e-5
GLA_DK = 32
GLA_TAU = 16.0
D_FF = 2816
CHUNK = 64
TQ = 256
TMT = 512
HALO = 8
FF_CHUNK = 256
Z_COLS = 3072

COL_NA = 0
COL_MLA = 768
COL_GLA_Q = 1152
COL_GLA_K = 1280
COL_GLA_G = 1408
COL_GLA_V = 1536
COL_GLA_O = 1792
COL_RW = 2048

VMEM_LIMIT_V7X = 56 * 1024 * 1024


def _cparams(n_axes):
    return pltpu.CompilerParams(dimension_semantics=("arbitrary",) * n_axes,
                                vmem_limit_bytes=VMEM_LIMIT_V7X)


def _const_spec(shape):
    nd = len(shape)
    return pl.BlockSpec(shape, lambda *_: (0,) * nd, pipeline_mode=pl.Buffered(1))


def _dot(a, b):
    return jnp.dot(a, b, preferred_element_type=F32)


def _split3(x):
    hi = x.astype(BF16)
    r1 = x - hi.astype(F32)
    mid = r1.astype(BF16)
    lo = (r1 - mid.astype(F32)).astype(BF16)
    return hi, mid, lo


def _dot_sel_lhs(m, x):
    hi, mid, lo = _split3(x)
    return _dot(m, hi) + (_dot(m, mid) + _dot(m, lo))


def _dot_sel_rhs(x, m):
    hi, mid, lo = _split3(x)
    return _dot(hi, m) + (_dot(mid, m) + _dot(lo, m))


def _bdot(a, b):
    return _dot(a.astype(BF16), b.astype(BF16))


def _sigmoid(x):
    return 1.0 / (1.0 + jnp.exp(-x))


def _softplus(x):
    return jnp.maximum(x, 0.0) + jnp.log1p(jnp.exp(-jnp.abs(x)))


def _rms(x):
    return x * lax.rsqrt(jnp.mean(x * x, axis=-1, keepdims=True) + EPS)


def _norm_mod(x, g, shift, scale):
    return (_rms(x) * g) * (1.0 + scale) + shift


def _lane_head(width, per_head):
    return lax.broadcasted_iota(jnp.int32, (1, width), 1) // per_head


def _block_ones(n, blk):
    i = np.arange(n) // blk
    return (i[:, None] == i[None, :]).astype(np.float32)


def _mod_kernel(c_ref, w_ref, b_ref, o_ref):
    cv = c_ref[...]
    s = cv * _sigmoid(cv)
    o_ref[0] = _dot(s.astype(BF16), w_ref[0].astype(BF16)) + b_ref[0]


def _modulation(cvecs, w_mod, b_mod):
    L, D, N = w_mod.shape
    tn = 1536
    return pl.pallas_call(
        _mod_kernel,
        grid=(L, N // tn),
        in_specs=[pl.BlockSpec((8, D), lambda l, n: (0, 0)),
                  pl.BlockSpec((1, D, tn), lambda l, n: (l, 0, n)),
                  pl.BlockSpec((1, 1, tn), lambda l, n: (l, 0, n))],
        out_specs=pl.BlockSpec((1, 8, tn), lambda l, n: (l, 0, n)),
        out_shape=jax.ShapeDtypeStruct((L, 8, N), F32),
        compiler_params=_cparams(2),
    )(cvecs, w_mod, b_mod.reshape(L, 1, N))


def _tok_tiles(S, T):
    assert T % TMT == 0 and 0 < S - T <= TMT
    return T // TMT + 1, T // TMT


def _tok_spec(width, col_block=0):
    return pl.BlockSpec((1, TMT, width), lambda b, i: (b, i, col_block))


def _mod_spec(nlat):
    return pl.BlockSpec((1, 1, 8, D_MODEL), lambda b, i: (b, jnp.where(i < nlat, 1, 0), 0, 0))


def _halo_specs(width, col_block, S):
    per = TMT // HALO
    last = S // HALO - 1
    prev = pl.BlockSpec((1, HALO, width), lambda b, i: (b, jnp.maximum(i * per - 1, 0), col_block))
    nxt = pl.BlockSpec((1, HALO, width), lambda b, i: (b, jnp.minimum((i + 1) * per, last), col_block))
    return prev, nxt


def _neighbour_ok(i, nlat):
    prev_ok = jnp.logical_and(i != 0, i != nlat)
    next_ok = i < nlat - 1
    return prev_ok, next_ok


def _rows_in_tile(i, nlat, tc):
    return jnp.where(i < nlat, TMT, tc)


def _in_proj_kernel(x_ref, mod_ref, g_ref, w_ref, z_ref):
    m = mod_ref[0, 0]
    h = _norm_mod(x_ref[0], g_ref[...], m[0:1], m[1:2])
    z_ref[0] = _dot(h.astype(BF16), w_ref[...])


def _in_proj(xs, modtab, g_pre, w_in_p, T):
    B, S, D = xs.shape
    ntiles, nlat = _tok_tiles(S, T)
    return pl.pallas_call(
        _in_proj_kernel,
        grid=(B, ntiles),
        in_specs=[_tok_spec(D), _mod_spec(nlat), _const_spec((1, D)), _const_spec((D, Z_COLS))],
        out_specs=_tok_spec(Z_COLS),
        out_shape=jax.ShapeDtypeStruct((B, S, Z_COLS), F32),
        compiler_params=_cparams(2),
    )(xs, modtab, g_pre, w_in_p)


def _out_proj_kernel(x_ref, na_ref, mla_ref, yf_ref, yb_ref, bon_ref, g_ref, of_ref, ob_ref, og_ref,
                     mod_ref, gpost_ref, w_ref, lnw_ref, lnb_ref, gn_ref, ones_ref, o_ref):
    m = mod_ref[0, 0]
    ones = ones_ref[...]
    inv_n = 1.0 / HEAD_DIM
    y = yf_ref[0] + yb_ref[0]
    yc = y - _dot_sel_rhs(y, ones) * inv_n
    var = _dot_sel_rhs(yc * yc, ones) * inv_n
    y_rw = (yc * lax.rsqrt(var + RW_GN_EPS) * lnw_ref[...] + lnb_ref[...] + bon_ref[0]) * g_ref[0]
    o = of_ref[0] + ob_ref[0]
    ms = _dot_sel_rhs(o * o, ones) * inv_n
    og = og_ref[0]
    y_gla = (o * lax.rsqrt(ms + EPS) * gn_ref[...]) * (og * _sigmoid(og))
    y = jnp.concatenate([na_ref[0], mla_ref[0], y_rw, y_gla], axis=-1)
    y = _dot(y.astype(BF16), w_ref[...])
    o_ref[0] = x_ref[0] + m[2:3] * (_rms(y) * gpost_ref[...])


def _out_proj(xs, z, y_na, y_mla, yf, yb, bon, g, of, ob, modtab, g_post, w_out_b, ln_w, ln_b, gla_norm,
              ones_b, T):
    B, S, D = xs.shape
    ntiles, nlat = _tok_tiles(S, T)
    grp = _tok_spec(GROUP_W)
    vec = _const_spec((1, GROUP_W))
    return pl.pallas_call(
        _out_proj_kernel,
        grid=(B, ntiles),
        in_specs=[_tok_spec(D), grp, grp, grp, grp, grp, grp, grp, grp,
                  _tok_spec(GROUP_W, COL_GLA_O // GROUP_W),
                  _mod_spec(nlat), _const_spec((1, D)), _const_spec((D, D)), vec, vec, vec,
                  _const_spec((GROUP_W, GROUP_W))],
        out_specs=_tok_spec(D),
        out_shape=jax.ShapeDtypeStruct((B, S, D), F32),
        compiler_params=_cparams(2),
    )(xs, y_na, y_mla, yf, yb, bon, g, of, ob, z, modtab, g_post, w_out_b, ln_w, ln_b, gla_norm, ones_b)


def _ffn_kernel(xp_ref, x_ref, xn_ref, mod_ref, gpre_ref, gpost_ref, wup_ref, cw_ref, cb_ref,
                wdn_ref, o_ref, *, nlat, tc):
    i = pl.program_id(1)
    prev_ok, next_ok = _neighbour_ok(i, nlat)
    m = mod_ref[0, 0]
    x = x_ref[0]
    xe = jnp.concatenate([xp_ref[0], x, xn_ref[0]], axis=0)
    n = TMT + 2 * HALO
    h = _norm_mod(xe, gpre_ref[...], m[3:4], m[4:5])
    row = lax.broadcasted_iota(jnp.int32, (n, 1), 0)
    valid = jnp.logical_or(jnp.logical_and(row >= HALO, row < HALO + _rows_in_tile(i, nlat, tc)),
                           jnp.logical_or(jnp.logical_and(row < HALO, prev_ok),
                                          jnp.logical_and(row >= HALO + TMT, next_ok)))
    hb = jnp.where(valid, h, 0.0).astype(BF16)
    nchunks = D_FF // FF_CHUNK

    def up(c):
        return [_dot(hb, wup_ref[:, base + c * FF_CHUNK:base + (c + 1) * FF_CHUNK]) for base in (0, D_FF)]

    def conv(z, lo):
        cw = cw_ref[:, lo:lo + FF_CHUNK]
        return (cb_ref[:, lo:lo + FF_CHUNK]
                + pltpu.roll(z, 1, 0)[HALO:HALO + TMT] * cw[0:1]
                + z[HALO:HALO + TMT] * cw[1:2]
                + pltpu.roll(z, n - 1, 0)[HALO:HALO + TMT] * cw[2:3])

    acc = jnp.zeros((TMT, D_MODEL), F32)
    z_next = up(0)
    for c in range(nchunks):
        z_val, z_gate = z_next
        if c + 1 < nchunks:
            z_next = up(c + 1)
        val = conv(z_val, c * FF_CHUNK)
        gate = conv(z_gate, D_FF + c * FF_CHUNK)
        a = (gate * _sigmoid(gate)) * val
        acc = acc + _dot(a.astype(BF16), wdn_ref[c * FF_CHUNK:(c + 1) * FF_CHUNK, :])
    o_ref[0] = x + m[5:6] * (_rms(acc) * gpost_ref[...])


def _ffn(xs, modtab, g_pre, g_post, w_up_b, conv_w, conv_b, w_dn_b, T, latent_only):
    B, S, D = xs.shape
    ntiles, nlat = _tok_tiles(S, T)
    prev, nxt = _halo_specs(D, 0, S)
    return pl.pallas_call(
        functools.partial(_ffn_kernel, nlat=nlat, tc=S - T),
        grid=(B, nlat if latent_only else ntiles),
        in_specs=[prev, _tok_spec(D), nxt, _mod_spec(nlat), _const_spec((1, D)), _const_spec((1, D)),
                  _const_spec((D, 2 * D_FF)), _const_spec((3, 2 * D_FF)), _const_spec((1, 2 * D_FF)),
                  _const_spec((D_FF, D))],
        out_specs=_tok_spec(D),
        out_shape=jax.ShapeDtypeStruct((B, T if latent_only else S, D), F32),
        compiler_params=_cparams(2),
    )(xs, xs, xs, modtab, g_pre, g_post, w_up_b, conv_w, conv_b, w_dn_b)


def _natten_bias(rpb, rows):
    rt = TQ // GRID_W
    j = np.arange(GRID_W)
    col_start = np.clip(j - NA_COLS // 2, 0, GRID_W - NA_COLS)
    col_in = (j[None, :] >= col_start[:, None]) & (j[None, :] < col_start[:, None] + NA_COLS)
    col_off = np.clip(j[None, :] - j[:, None], -(NA_COLS - 1), NA_COLS - 1) + (NA_COLS - 1)
    bq = jnp.where(col_in[None, None], rpb.astype(F32)[:, :, col_off], -jnp.inf)
    cases = []
    for r0 in (0, rt, rows - rt):
        r = r0 + np.arange(rt)
        rs = np.clip(r - NA_ROWS // 2, 0, rows - NA_ROWS)
        us = min(max(r0 - NA_ROWS // 2, 0), rows - NA_UNION)
        krow = us + np.arange(NA_UNION)
        ok = (krow[None, :] >= rs[:, None]) & (krow[None, :] < rs[:, None] + NA_ROWS)
        ro = np.clip(krow[None, :] - r[:, None] + (NA_ROWS - 1), 0, 2 * NA_ROWS - 2)
        b = jnp.where(ok[None, :, :, None, None], bq[:, ro], -jnp.inf)
        cases.append(b.transpose(0, 1, 3, 2, 4).reshape(N_HEADS, rt * GRID_W, NA_UNION * GRID_W))
    return jnp.stack(cases)


def _natten_kernel(q_ref, k_ref, v_ref, qc_ref, kc_ref, vc_ref, bias_ref, o_ref, *, nlat, rows):
    j = pl.program_id(1)
    lane_h = _lane_head(GROUP_W, HEAD_DIM)
    kct = kc_ref[0].T.astype(BF16)
    vc = vc_ref[0].astype(BF16)
    scale = HEAD_DIM ** -0.5
    nwin = NA_UNION * GRID_W

    @pl.when(j < nlat)
    def _():
        us = jnp.clip(j * (TQ // GRID_W) - NA_ROWS // 2, 0, rows - NA_UNION)
        start = pl.multiple_of(us * GRID_W, GRID_W)
        q = q_ref[0] * scale
        kwt = k_ref[0, pl.ds(start, nwin), :].T.astype(BF16)
        vw = v_ref[0, pl.ds(start, nwin), :].astype(BF16)
        acc = jnp.zeros((TQ, GROUP_W), F32)

        def logits(h):
            qh = jnp.where(lane_h == h, q, 0.0).astype(BF16)
            return _dot(qh, kwt), _dot(qh, kct)

        s_next = logits(0)
        for h in range(N_HEADS):
            hm = lane_h == h
            s_w, s_c = s_next
            if h + 1 < N_HEADS:
                s_next = logits(h + 1)
            s_w = s_w + bias_ref[0, h]
            mx = jnp.maximum(jnp.max(s_w, axis=-1, keepdims=True), jnp.max(s_c, axis=-1, keepdims=True))
            p_w = jnp.exp(s_w - mx)
            p_c = jnp.exp(s_c - mx)
            den = jnp.sum(p_w, axis=-1, keepdims=True) + jnp.sum(p_c, axis=-1, keepdims=True)
            o = _dot(p_w.astype(BF16), vw) + _dot(p_c.astype(BF16), vc)
            acc = acc + jnp.where(hm, o * (1.0 / den), 0.0)
        o_ref[0] = acc

    @pl.when(j >= nlat)
    def _():
        q = qc_ref[0] * scale
        acc = jnp.zeros((TQ, GROUP_W), F32)
        for h in range(N_HEADS):
            hm = lane_h == h
            s = _dot(jnp.where(hm, q, 0.0).astype(BF16), kct)
            p = jnp.exp(s - jnp.max(s, axis=-1, keepdims=True))
            den = jnp.sum(p, axis=-1, keepdims=True)
            acc = acc + jnp.where(hm, _dot(p.astype(BF16), vc) * (1.0 / den), 0.0)
        o_ref[0] = acc


def _natten(z, bias, T):
    B, S, _ = z.shape
    nlat = T // TQ
    rows = T // GRID_W
    assert rows >= 16 and rows % (TQ // GRID_W) == 0 and S - T == TQ
    cb = COL_NA // GROUP_W
    lat = lambda c: pl.BlockSpec((1, T, GROUP_W), lambda b, j: (b, 0, cb + c))
    ctx = lambda c: pl.BlockSpec((1, TQ, GROUP_W), lambda b, j: (b, nlat, cb + c))
    case = lambda j: jnp.where(j == 0, 0, jnp.where(j >= nlat - 1, 2, 1))
    return pl.pallas_call(
        functools.partial(_natten_kernel, nlat=nlat, rows=rows),
        grid=(B, nlat + 1),
        in_specs=[pl.BlockSpec((1, TQ, GROUP_W), lambda b, j: (b, jnp.minimum(j, nlat - 1), cb)),
                  lat(1), lat(2), ctx(0), ctx(1), ctx(2),
                  pl.BlockSpec((1, N_HEADS, TQ, NA_UNION * GRID_W), lambda b, j: (case(j), 0, 0, 0))],
        out_specs=pl.BlockSpec((1, TQ, GROUP_W), lambda b, j: (b, j, 0)),
        out_shape=jax.ShapeDtypeStruct((B, S, GROUP_W), F32),
        compiler_params=_cparams(2),
    )(z, z, z, z, z, z, bias)


def _rope_tables(T, Tc):
    t = np.arange(T)
    row = (t // GRID_W).astype(np.float32)
    col = (t % GRID_W).astype(np.float32)
    d = MLA_ROPE // 2
    inv = ROPE_THETA ** (-jnp.arange(0, d, 2, dtype=F32) / d)
    cs, sn = [], []
    for pos in (row, col):
        ang = jnp.asarray(pos)[:, None] * inv[None, :]
        cs += [jnp.cos(ang), jnp.cos(ang)]
        sn += [-jnp.sin(ang), jnp.sin(ang)]
    pad = MLA_HEAD_PAD - MLA_NOPE - MLA_ROPE
    cos = jnp.concatenate([jnp.ones((T, MLA_NOPE), F32)] + cs + [jnp.ones((T, pad), F32)], axis=1)
    sin = jnp.concatenate([jnp.zeros((T, MLA_NOPE), F32)] + sn + [jnp.zeros((T, pad), F32)], axis=1)
    cos = jnp.concatenate([cos, jnp.ones((Tc, MLA_HEAD_PAD), F32)], axis=0)
    sin = jnp.concatenate([sin, jnp.zeros((Tc, MLA_HEAD_PAD), F32)], axis=0)
    return cos, sin


def _rope_swap_perm():
    q = MLA_ROPE // 4
    return np.concatenate([np.arange(q, 2 * q), np.arange(0, q), np.arange(3 * q, 4 * q), np.arange(2 * q, 3 * q)])


def _mla_weights(w_uq, w_ukv):
    L = w_uq.shape[0]
    wq = w_uq.reshape(L, MLA_Q_RANK, N_HEADS, MLA_NOPE + MLA_ROPE)
    pad = MLA_HEAD_PAD - MLA_NOPE - MLA_ROPE
    zq = jnp.zeros((L, MLA_Q_RANK, N_HEADS, pad), F32)
    wq1 = jnp.concatenate([wq, zq], axis=-1).reshape(L, MLA_Q_RANK, N_HEADS * MLA_HEAD_PAD)
    rope_sw = wq[..., MLA_NOPE:][..., _rope_swap_perm()]
    wq2 = jnp.concatenate([jnp.zeros((L, MLA_Q_RANK, N_HEADS, MLA_NOPE), F32), rope_sw, zq], axis=-1)
    wq2 = wq2.reshape(L, MLA_Q_RANK, N_HEADS * MLA_HEAD_PAD)
    wkv = w_ukv.reshape(L, MLA_KV_RANK, N_HEADS, 2 * MLA_NOPE)
    wk = jnp.concatenate([wkv[..., :MLA_NOPE], jnp.zeros((L, MLA_KV_RANK, N_HEADS, MLA_HEAD_PAD - MLA_NOPE), F32)], axis=-1)
    wk = wk.reshape(L, MLA_KV_RANK, N_HEADS * MLA_HEAD_PAD)
    wv = wkv[..., MLA_NOPE:].reshape(L, MLA_KV_RANK, N_HEADS * MLA_NOPE)
    return wq1.astype(BF16), wq2.astype(BF16), wk.astype(BF16), wv.astype(BF16)


def _rope_place():
    e = np.zeros((MLA_ROPE, N_HEADS * MLA_HEAD_PAD), np.float32)
    for h in range(N_HEADS):
        e[np.arange(MLA_ROPE), h * MLA_HEAD_PAD + MLA_NOPE + np.arange(MLA_ROPE)] = 1.0
    return jnp.asarray(e, BF16)


def _mla_up_kernel(z_ref, cos_ref, sin_ref, qn_ref, kvn_ref, wq1_ref, wq2_ref, wk_ref, wv_ref, e_ref,
                   q_ref, kt_ref, v_ref):
    z = z_ref[0]
    ckv = z[:, :MLA_KV_RANK]
    cq = z[:, MLA_KV_RANK:MLA_KV_RANK + MLA_Q_RANK]
    kr = z[:, MLA_KV_RANK + MLA_Q_RANK:MLA_KV_RANK + MLA_Q_RANK + MLA_ROPE]
    krs = z[:, MLA_KV_RANK + MLA_Q_RANK + MLA_ROPE:]
    cos = jnp.concatenate([cos_ref[...]] * N_HEADS, axis=-1)
    sin = jnp.concatenate([sin_ref[...]] * N_HEADS, axis=-1)
    nq = (_rms(cq) * qn_ref[...]).astype(BF16)
    nkv = (_rms(ckv) * kvn_ref[...]).astype(BF16)
    q = _dot(nq, wq1_ref[...]) * cos + _dot(nq, wq2_ref[...]) * sin
    scale = (MLA_NOPE + MLA_ROPE) ** -0.5
    q_ref[0] = (q * scale).astype(BF16)
    k = _dot(nkv, wk_ref[...]) + _dot_sel_rhs(kr, e_ref[...]) * cos + _dot_sel_rhs(krs, e_ref[...]) * sin
    kt_ref[0] = k.T.astype(BF16)
    v_ref[0] = _dot(nkv, wv_ref[...]).astype(BF16)


def _mla_up(z, cos, sin, q_norm, kv_norm, wq1, wq2, wk, wv, place, T):
    B, S, _ = z.shape
    ntiles, _ = _tok_tiles(S, T)
    HP = N_HEADS * MLA_HEAD_PAD
    zw = MLA_KV_RANK + MLA_Q_RANK + 2 * MLA_ROPE
    tab = pl.BlockSpec((TMT, MLA_HEAD_PAD), lambda b, i: (i, 0))
    return pl.pallas_call(
        _mla_up_kernel,
        grid=(B, ntiles),
        in_specs=[_tok_spec(zw, COL_MLA // zw), tab, tab,
                  _const_spec((1, MLA_Q_RANK)), _const_spec((1, MLA_KV_RANK)),
                  _const_spec((MLA_Q_RANK, HP)), _const_spec((MLA_Q_RANK, HP)),
                  _const_spec((MLA_KV_RANK, HP)), _const_spec((MLA_KV_RANK, GROUP_W)),
                  _const_spec((MLA_ROPE, HP))],
        out_specs=[_tok_spec(HP), pl.BlockSpec((1, HP, TMT), lambda b, i: (b, 0, i)), _tok_spec(GROUP_W)],
        out_shape=[jax.ShapeDtypeStruct((B, S, HP), BF16),
                   jax.ShapeDtypeStruct((B, HP, S), BF16),
                   jax.ShapeDtypeStruct((B, S, GROUP_W), BF16)],
        compiler_params=_cparams(2),
    )(z, cos, sin, q_norm, kv_norm, wq1, wq2, wk, wv, place)


def _mla_attn_kernel(q_ref, kt_ref, v_ref, o_ref, *, nlat, T):
    j = pl.program_id(1)
    lane_h = _lane_head(GROUP_W, HEAD_DIM)

    def attend(lo, hi):
        acc = jnp.zeros((TQ, GROUP_W), F32)
        v = v_ref[0, lo:hi, :]

        def logits(h):
            q = q_ref[0, :, h * MLA_HEAD_PAD:(h + 1) * MLA_HEAD_PAD]
            return _dot(q, kt_ref[0, h * MLA_HEAD_PAD:(h + 1) * MLA_HEAD_PAD, lo:hi])

        s_next = logits(0)
        for h in range(N_HEADS):
            s = s_next
            if h + 1 < N_HEADS:
                s_next = logits(h + 1)
            p = jnp.exp(s - jnp.max(s, axis=-1, keepdims=True))
            den = jnp.sum(p, axis=-1, keepdims=True)
            acc = acc + jnp.where(lane_h == h, _dot(p.astype(BF16), v) * (1.0 / den), 0.0)
        o_ref[0] = acc

    @pl.when(j < nlat)
    def _():
        attend(0, T + TQ)

    @pl.when(j >= nlat)
    def _():
        attend(T, T + TQ)


def _mla_attn(q, kt, v, T):
    B, S, HP = q.shape
    nlat = T // TQ
    return pl.pallas_call(
        functools.partial(_mla_attn_kernel, nlat=nlat, T=T),
        grid=(B, S // TQ),
        in_specs=[pl.BlockSpec((1, TQ, HP), lambda b, j: (b, j, 0)),
                  pl.BlockSpec((1, HP, S), lambda b, j: (b, 0, 0)),
                  pl.BlockSpec((1, S, GROUP_W), lambda b, j: (b, 0, 0))],
        out_specs=pl.BlockSpec((1, TQ, GROUP_W), lambda b, j: (b, j, 0)),
        out_shape=jax.ShapeDtypeStruct((B, S, GROUP_W), F32),
        compiler_params=_cparams(2),
    )(q, kt, v)


def _scan_masks():
    t = np.arange(CHUNK)
    inc = np.stack([t[:, None] >= t[None, :], t[:, None] <= t[None, :]]).astype(np.float32)
    strict = np.stack([t[:, None] > t[None, :], t[:, None] < t[None, :]]).astype(np.float32)
    blk = _block_ones(N_HEADS * CHUNK, CHUNK)
    inc_bd = np.tile(inc, (1, N_HEADS, N_HEADS)) * blk
    strict_bd = np.tile(strict, (1, N_HEADS, N_HEADS)) * blk
    return jnp.asarray(inc, BF16), jnp.asarray(inc_bd), jnp.asarray(strict_bd)


def _fwd_tile(p, nlat):
    return jnp.where(p == 0, nlat, p - 1)


def _bwd_tile(p, nlat):
    return jnp.where(p == 0, nlat, nlat - p)


def _chunk_order(d):
    nch = TQ // CHUNK
    return range(nch) if d == 0 else range(nch - 1, -1, -1)


def _stack_heads(x, lane_h):
    return jnp.concatenate([jnp.where(lane_h == h, x, 0.0) for h in range(N_HEADS)], axis=0)


def _tile_rows(x):
    return jnp.concatenate([x] * N_HEADS, axis=0)


def _collapse_heads(x):
    c = x.shape[0] // N_HEADS
    return (x[0:c] + x[c:2 * c]) + (x[2 * c:3 * c] + x[3 * c:4 * c])


def _rw_prep_kernel(zp_ref, z_ref, zn_ref, mu_ref, kk_ref, ka_ref, rk_ref, w0_ref, a0_ref, wup_ref,
                    aup_ref, gup_ref, ones_ref,
                    r_ref, v_ref, ah_ref, g_ref, bon_ref, lw_ref, kd_ref, bd_ref, *, nlat, tc):
    i = pl.program_id(1)
    prev_ok, next_ok = _neighbour_ok(i, nlat)
    z = z_ref[0]
    row = lax.broadcasted_iota(jnp.int32, (TMT, 1), 0)
    before = jnp.where(prev_ok, zp_ref[0, HALO - 1:HALO, :], 0.0)
    after = jnp.where(next_ok, zn_ref[0, 0:1, :], 0.0)
    zprev = jnp.where(row == 0, before, pltpu.roll(z, 1, 0))
    znext = jnp.where(row == _rows_in_tile(i, nlat, tc) - 1, after, pltpu.roll(z, TMT - 1, 0))
    zs = z + mu_ref[...] * (0.5 * (zprev + znext) - z)
    r = zs[:, 0:GROUP_W]
    k = zs[:, GROUP_W:2 * GROUP_W]
    v = zs[:, 2 * GROUP_W:3 * GROUP_W]
    low = zs[:, 3 * GROUP_W:3 * GROUP_W + 128]
    gd = zs[:, 3 * GROUP_W + 128:]
    ones = ones_ref[...]
    kk = k * kk_ref[...]
    kk = kk * lax.rsqrt(_dot_sel_rhs(kk * kk, ones) + 1e-12)
    wl = _dot(jnp.tanh(low).astype(BF16), wup_ref[...])
    al = _dot(low.astype(BF16), aup_ref[...])
    ksum = jnp.zeros((TMT, GROUP_W), F32)
    for d in range(2):
        w_raw = -_softplus(-(w0_ref[d:d + 1, :] + wl[:, d * GROUP_W:(d + 1) * GROUP_W])) - 0.5
        lw_ref[d, 0] = -jnp.exp(w_raw)
        a = _sigmoid(a0_ref[d:d + 1, :] + al[:, d * GROUP_W:(d + 1) * GROUP_W])
        kd = k * (1.0 + (a - 1.0) * ka_ref[...])
        kd_ref[d, 0] = kd
        bd_ref[d, 0] = kk * a
        ksum = ksum + kd
    r_ref[0] = r
    v_ref[0] = v
    ah_ref[0] = -kk
    g_ref[0] = _dot(_sigmoid(gd).astype(BF16), gup_ref[...])
    bon_ref[0] = _dot_sel_rhs(r * ksum * rk_ref[...], ones) * v


def _rw_prep(z, mu, k_k, k_a, r_k, w0, a0, wup_p, aup_p, gup_b, ones_b, T):
    B, S, _ = z.shape
    ntiles, nlat = _tok_tiles(S, T)
    W = 4 * GROUP_W
    cb = COL_RW // W
    prev, nxt = _halo_specs(W, cb, S)
    one = _tok_spec(GROUP_W)
    two = pl.BlockSpec((2, 1, TMT, GROUP_W), lambda b, i: (0, b, i, 0))
    s1 = jax.ShapeDtypeStruct((B, S, GROUP_W), F32)
    s2 = jax.ShapeDtypeStruct((2, B, S, GROUP_W), F32)
    return pl.pallas_call(
        functools.partial(_rw_prep_kernel, nlat=nlat, tc=S - T),
        grid=(B, ntiles),
        in_specs=[prev, _tok_spec(W, cb), nxt,
                  _const_spec((1, W)), _const_spec((1, GROUP_W)), _const_spec((1, GROUP_W)),
                  _const_spec((1, GROUP_W)), _const_spec((2, GROUP_W)), _const_spec((2, GROUP_W)),
                  _const_spec((128, 2 * GROUP_W)), _const_spec((128, 2 * GROUP_W)),
                  _const_spec((128, GROUP_W)), _const_spec((GROUP_W, GROUP_W))],
        out_specs=[one, one, one, one, one, two, two, two],
        out_shape=[s1, s1, s1, s1, s1, s2, s2, s2],
        compiler_params=_cparams(2),
    )(z, z, z, mu, k_k, k_a, r_k, w0, a0, wup_p, aup_p, gup_b, ones_b)


def _rw_scan_kernel(rf_ref, rb_ref, vf_ref, vb_ref, af_ref, ab_ref, lwf_ref, lwb_ref, kdf_ref, kdb_ref,
                    bdf_ref, bdb_ref, mi_ref, mibd_ref, msbd_ref, bm_ref, yf_ref, yb_ref, s_ref):
    p = pl.program_id(1)

    @pl.when(p == 0)
    def _():
        s_ref[...] = jnp.zeros_like(s_ref)

    lane_h = _lane_head(GROUP_W, HEAD_DIM)
    n = N_HEADS * CHUNK
    eye = (lax.broadcasted_iota(jnp.int32, (n, n), 0) == lax.broadcasted_iota(jnp.int32, (n, n), 1)).astype(F32)
    bm = bm_ref[...]
    ones_c = jnp.ones((CHUNK, CHUNK), BF16)
    refs = ((rf_ref, vf_ref, af_ref, lwf_ref, kdf_ref, bdf_ref, yf_ref),
            (rb_ref, vb_ref, ab_ref, lwb_ref, kdb_ref, bdb_ref, yb_ref))
    orders = [list(_chunk_order(d)) for d in range(2)]
    inst = [(step, d) for step in range(TQ // CHUNK) for d in range(2)]
    idx = range(len(inst))
    rows = [pl.ds(orders[d][step] * CHUNK, CHUNK) for step, d in inst]

    ar_s, v_s, bt4, kt4, bks_t, decay = [], [], [], [], [], []
    for (step, d), sl in zip(inst, rows):
        r_ref, v_ref, a_ref, lw_ref, kd_ref, bd_ref, _ = refs[d]
        lw = lw_ref[0, 0, sl, :]
        kd = kd_ref[0, 0, sl, :]
        bd = bd_ref[0, 0, sl, :]
        cs = _dot_sel_lhs(mi_ref[d], lw)
        tot = _dot_sel_lhs(ones_c, lw)
        e_neg = jnp.exp(-cs)
        e_hat = jnp.exp(tot - cs)
        a_t = a_ref[0, sl, :] * jnp.exp(cs - lw)
        r_t = r_ref[0, sl, :] * jnp.exp(cs)
        ar_s.append(jnp.concatenate([_stack_heads(a_t, lane_h), _stack_heads(r_t, lane_h)], axis=0).astype(BF16))
        v_s.append(_stack_heads(v_ref[0, sl, :], lane_h).astype(BF16))
        bt4.append(_tile_rows(bd * e_neg).T.astype(BF16))
        kt4.append(_tile_rows(kd * e_neg).T.astype(BF16))
        bks_t.append(jnp.concatenate([(_tile_rows(bd * e_hat).T * bm).astype(BF16),
                                      (_tile_rows(kd * e_hat).T * bm).astype(BF16)], axis=1))
        decay.append(_tile_rows(jnp.exp(tot)).T)
    g_b = [_dot(ar_s[i], bt4[i]) for i in idx]
    g_k = [_dot(ar_s[i], kt4[i]) for i in idx]
    pw = [g_b[i][:n] * msbd_ref[inst[i][1]] for i in idx]
    ak = [(g_k[i][:n] * msbd_ref[inst[i][1]]).astype(BF16) for i in idx]
    rb = [(g_b[i][n:] * mibd_ref[inst[i][1]]).astype(BF16) for i in idx]
    rk = [(g_k[i][n:] * mibd_ref[inst[i][1]]).astype(BF16) for i in idx]
    x = [eye + pw[i] for i in idx]
    pw = [_bdot(pw[i], pw[i]) for i in idx]
    for _ in range(4):
        px = [_bdot(jnp.concatenate([pw[i], x[i]], axis=0), pw[i]) for i in idx]
        x = [x[i] + px[i][n:] for i in idx]
        pw = [px[i][:n] for i in idx]
    x = [(x[i] + _bdot(x[i], pw[i])).astype(BF16) for i in idx]
    akv = [_dot(ak[i], v_s[i]).astype(BF16) for i in idx]
    p12 = [_dot(x[i], jnp.concatenate([akv[i], ar_s[i][:n]], axis=1)).astype(BF16) for i in idx]
    rbp = [_dot(rb[i], p12[i]) for i in idx]
    y_c = [_collapse_heads(rbp[i][:, :GROUP_W] + _dot(rk[i], v_s[i])) for i in idx]
    gq = [jnp.concatenate([_dot(bks_t[i][:, :n], p12[i][:, GROUP_W:]),
                           ar_s[i][n:].astype(F32) + rbp[i][:, GROUP_W:]], axis=0).astype(BF16) for i in idx]
    c_s = [_dot(bks_t[i], jnp.concatenate([p12[i][:, :GROUP_W], v_s[i]], axis=0)) for i in idx]

    s = [s_ref[0], s_ref[1]]
    for i, ((step, d), sl) in enumerate(zip(inst, rows)):
        m = _dot(gq[i], s[d].astype(BF16))
        refs[d][6][0, sl, :] = _collapse_heads(m[n:]) + y_c[i]
        s[d] = decay[i] * s[d] + m[:n] + c_s[i]
    s_ref[0] = s[0]
    s_ref[1] = s[1]


def _rw_scan(r, v, ah, lw, kd, bd, masks, bm, T):
    B, S, _ = r.shape
    nlat = T // TQ
    mi, mibd, msbd = masks
    n = N_HEADS * CHUNK
    fwd = pl.BlockSpec((1, TQ, GROUP_W), lambda b, p: (b, _fwd_tile(p, nlat), 0))
    bwd = pl.BlockSpec((1, TQ, GROUP_W), lambda b, p: (b, _bwd_tile(p, nlat), 0))
    fwd2 = pl.BlockSpec((1, 1, TQ, GROUP_W), lambda b, p: (0, b, _fwd_tile(p, nlat), 0))
    bwd2 = pl.BlockSpec((1, 1, TQ, GROUP_W), lambda b, p: (1, b, _bwd_tile(p, nlat), 0))
    out = jax.ShapeDtypeStruct((B, S, GROUP_W), F32)
    return pl.pallas_call(
        _rw_scan_kernel,
        grid=(B, nlat + 1),
        in_specs=[fwd, bwd, fwd, bwd, fwd, bwd, fwd2, bwd2, fwd2, bwd2, fwd2, bwd2,
                  _const_spec((2, CHUNK, CHUNK)), _const_spec((2, n, n)), _const_spec((2, n, n)),
                  _const_spec((n, n))],
        out_specs=[fwd, bwd],
        out_shape=[out, out],
        scratch_shapes=[pltpu.VMEM((2, n, GROUP_W), F32)],
        compiler_params=_cparams(2),
    )(r, r, v, v, ah, ah, lw, lw, kd, kd, bd, bd, mi, mibd, msbd, bm)


def _gla_scan_kernel(qf_ref, qb_ref, kf_ref, kb_ref, gf_ref, gb_ref, vf_ref, vb_ref, gup_ref, gbias_ref,
                     mi_ref, mibd_ref, bm_ref, of_ref, ob_ref, s_ref):
    p = pl.program_id(1)

    @pl.when(p == 0)
    def _():
        s_ref[...] = jnp.zeros_like(s_ref)

    lane_hk = _lane_head(N_HEADS * GLA_DK, GLA_DK)
    lane_hv = _lane_head(GROUP_W, HEAD_DIM)
    bm = bm_ref[...]
    ones_c = jnp.ones((CHUNK, CHUNK), BF16)
    refs = ((qf_ref, kf_ref, gf_ref, vf_ref, of_ref), (qb_ref, kb_ref, gb_ref, vb_ref, ob_ref))
    orders = [list(_chunk_order(d)) for d in range(2)]
    inst = [(step, d) for step in range(TQ // CHUNK) for d in range(2)]
    idx = range(len(inst))
    rows = [pl.ds(orders[d][step] * CHUNK, CHUNK) for step, d in inst]

    q_s, ke4, ks4, v_s, decay = [], [], [], [], []
    for (step, d), sl in zip(inst, rows):
        q_ref, k_ref, g_ref, v_ref, _ = refs[d]
        k = k_ref[0, sl, :]
        la = -_softplus(-(_dot(g_ref[0, sl, :].astype(BF16), gup_ref[d]) + gbias_ref[d])) * (1.0 / GLA_TAU)
        b = _dot_sel_lhs(mi_ref[d], la)
        tot = _dot_sel_lhs(ones_c, la)
        q_s.append(_stack_heads(q_ref[0, sl, :] * (GLA_DK ** -0.5) * jnp.exp(b), lane_hk).astype(BF16))
        ke4.append(_tile_rows(k * jnp.exp(-b)).T.astype(BF16))
        ks4.append(_tile_rows(k * jnp.exp(tot - b)).T.astype(BF16))
        v_s.append(_stack_heads(v_ref[0, sl, :], lane_hv).astype(BF16))
        decay.append(_tile_rows(jnp.exp(tot)).T)
    a_bd = [(_dot(q_s[i], ke4[i]) * mibd_ref[inst[i][1]]).astype(BF16) for i in idx]
    o_in = [_dot(a_bd[i], v_s[i]) for i in idx]
    kv = [_dot(ks4[i], v_s[i]) * bm for i in idx]

    s = [s_ref[0], s_ref[1]]
    for i, ((step, d), sl) in enumerate(zip(inst, rows)):
        refs[d][4][0, sl, :] = _collapse_heads(o_in[i] + _dot(q_s[i], s[d].astype(BF16)))
        s[d] = decay[i] * s[d] + kv[i]
    s_ref[0] = s[0]
    s_ref[1] = s[1]


def _gla_scan(z, gup_p, gb, masks, bm, T):
    B, S, _ = z.shape
    nlat = T // TQ
    mi, mibd, _ = masks
    n = N_HEADS * CHUNK
    wk = N_HEADS * GLA_DK
    fwd = lambda w, col: pl.BlockSpec((1, TQ, w), lambda b, p: (b, _fwd_tile(p, nlat), col // w))
    bwd = lambda w, col: pl.BlockSpec((1, TQ, w), lambda b, p: (b, _bwd_tile(p, nlat), col // w))
    out = jax.ShapeDtypeStruct((B, S, GROUP_W), F32)
    return pl.pallas_call(
        _gla_scan_kernel,
        grid=(B, nlat + 1),
        in_specs=[fwd(wk, COL_GLA_Q), bwd(wk, COL_GLA_Q), fwd(wk, COL_GLA_K), bwd(wk, COL_GLA_K),
                  fwd(wk, COL_GLA_G), bwd(wk, COL_GLA_G), fwd(GROUP_W, COL_GLA_V), bwd(GROUP_W, COL_GLA_V),
                  _const_spec((2, wk, wk)), _const_spec((2, 1, wk)),
                  _const_spec((2, CHUNK, CHUNK)), _const_spec((2, n, n)), _const_spec((wk, GROUP_W))],
        out_specs=[fwd(GROUP_W, 0), bwd(GROUP_W, 0)],
        out_shape=[out, out],
        scratch_shapes=[pltpu.VMEM((2, wk, GROUP_W), F32)],
        compiler_params=_cparams(2),
    )(z, z, z, z, z, z, z, z, gup_p, gb, mi, mibd, bm)


def _pack_w_in(w_in):
    L, D, _ = w_in.shape
    na, mla, rw, gla = jnp.split(w_in, [768, 1120, 2144], axis=-1)
    cq, ckv, kr = jnp.split(mla, [MLA_Q_RANK, MLA_Q_RANK + MLA_KV_RANK], axis=-1)
    gq, gk, gv, gg, go = jnp.split(gla, [128, 256, 512, 528], axis=-1)
    pad = jnp.zeros((L, D, COL_GLA_V - COL_GLA_G - gg.shape[-1]), F32)
    packed = jnp.concatenate([na, ckv, cq, kr, kr[..., _rope_swap_perm()], gq, gk, gg, pad, gv, go, rw], axis=-1)
    assert packed.shape[-1] == Z_COLS
    return packed.astype(BF16)


def kernel(x, c, ctx, c_ctx, w_mod, b_mod, g_mix_pre, g_mix_post, g_ffn_pre, g_ffn_post, w_in, w_out, na_rpb, mla_q_norm, mla_w_uq, mla_kv_norm, mla_w_ukv, rw_mu, rw_w0, rw_w_up, rw_a0, rw_a_up, rw_g_up, rw_k_k, rw_k_a, rw_r_k, rw_ln_w, rw_ln_b, gla_gate_up, gla_gate_b, gla_norm, ffn_w_up, ffn_conv_w, ffn_conv_b, ffn_w_down):
    B, T, D = x.shape
    Tc = ctx.shape[1]
    L = w_in.shape[0]
    assert D == D_MODEL and Tc == TQ and T % TMT == 0 and B + 1 <= 8

    w_in_p = _pack_w_in(w_in)
    w_out_b = w_out.astype(BF16)
    wq1, wq2, wk, wv = _mla_weights(mla_w_uq, mla_w_ukv)
    place = _rope_place()
    cos, sin = _rope_tables(T, Tc)
    zero_lo = jnp.zeros((L, 64, 2 * GROUP_W), F32)
    rw_wup_p = jnp.concatenate([jnp.concatenate([rw_w_up[:, 0], rw_w_up[:, 1]], axis=-1), zero_lo], axis=1).astype(BF16)
    rw_aup_p = jnp.concatenate([zero_lo, jnp.concatenate([rw_a_up[:, 0], rw_a_up[:, 1]], axis=-1)], axis=1).astype(BF16)
    rw_gup_b = rw_g_up.astype(BF16)
    wk_gla = N_HEADS * GLA_DK
    gla_gup_p = jnp.concatenate([gla_gate_up, jnp.zeros((L, 2, wk_gla - gla_gate_up.shape[2], wk_gla), F32)], axis=2).astype(BF16)
    ffn_up_b = ffn_w_up.astype(BF16)
    ffn_dn_b = ffn_w_down.astype(BF16)
    ones_b = jnp.asarray(_block_ones(GROUP_W, HEAD_DIM), BF16)
    rw_bm = jnp.asarray(_block_ones(N_HEADS * CHUNK, CHUNK))
    gla_bm = jnp.asarray((np.arange(wk_gla)[:, None] // GLA_DK == np.arange(GROUP_W)[None, :] // HEAD_DIM).astype(np.float32))
    masks = _scan_masks()

    cvecs = jnp.zeros((8, D), F32).at[:B].set(c).at[B].set(c_ctx)
    mods = _modulation(cvecs, w_mod, b_mod).reshape(L, 8, 6, D)
    mods = jnp.pad(mods, ((0, 0), (0, 0), (0, 2), (0, 0)))
    modtabs = jnp.stack([jnp.broadcast_to(mods[:, B:B + 1], (L, B, 8, D)), mods[:, :B]], axis=2)

    xs = jnp.concatenate([x, ctx], axis=1)
    row = lambda a: a.reshape(1, -1)
    for i in range(L):
        modtab = modtabs[i]
        z = _in_proj(xs, modtab, row(g_mix_pre[i]), w_in_p[i], T)
        y_na = _natten(z, _natten_bias(na_rpb[i], T // GRID_W), T)
        q, kt, v = _mla_up(z, cos, sin, row(mla_q_norm[i]), row(mla_kv_norm[i]), wq1[i], wq2[i], wk[i], wv[i], place, T)
        y_mla = _mla_attn(q, kt, v, T)
        r, vv, ah, g, bon, lw, kd, bd = _rw_prep(z, row(rw_mu[i]), row(rw_k_k[i]), row(rw_k_a[i]), row(rw_r_k[i]),
                                                 rw_w0[i], rw_a0[i], rw_wup_p[i], rw_aup_p[i], rw_gup_b[i], ones_b, T)
        yf, yb = _rw_scan(r, vv, ah, lw, kd, bd, masks, rw_bm, T)
        of, ob = _gla_scan(z, gla_gup_p[i], gla_gate_b[i][:, None, :], masks, gla_bm, T)
        xs = _out_proj(xs, z, y_na, y_mla, yf, yb, bon, g, of, ob, modtab, row(g_mix_post[i]), w_out_b[i],
                       row(rw_ln_w[i]), row(rw_ln_b[i]), row(gla_norm[i]), ones_b, T)
        xs = _ffn(xs, modtab, row(g_ffn_pre[i]), row(g_ffn_post[i]), ffn_up_b[i], ffn_conv_w[i],
                  row(ffn_conv_b[i]), ffn_dn_b[i], T, latent_only=(i == L - 1))
    return xs
```

```python
import functools

import numpy as np
import jax
import jax.numpy as jnp
from jax import lax
from jax.experimental import pallas as pl
from jax.experimental.pallas import tpu as pltpu

F32 = jnp.float32
BF16 = jnp.bfloat16

D_MODEL = 1024
GRID_W = 64
EPS = 1e-6
N_HEADS = 4
HEAD_DIM = 64
GROUP_W = 256
NA_ROWS = 8
NA_COLS = 16
NA_UNION = 12
MLA_Q_RANK = 192
MLA_KV_RANK = 128
MLA_NOPE = 64
MLA_ROPE = 32
MLA_HEAD_PAD = 128
ROPE_THETA = 10000.0
RW_GN_EPS = 64e-5
GLA_DK = 32
GLA_TAU = 16.0
D_FF = 2816
CHUNK = 64
TQ = 256
TMT = 512
HALO = 8
FF_CHUNK = 256
Z_COLS = 3072

COL_NA = 0
COL_MLA = 768
COL_GLA_Q = 1152
COL_GLA_K = 1280
COL_GLA_G = 1408
COL_GLA_V = 1536
COL_GLA_O = 1792
COL_RW = 2048

VMEM_LIMIT_V7X = 56 * 1024 * 1024


def _cparams(n_axes):
    return pltpu.CompilerParams(dimension_semantics=("arbitrary",) * n_axes,
                                vmem_limit_bytes=VMEM_LIMIT_V7X)


def _const_spec(shape, layer=None):
    nd = len(shape)
    if layer is None:
        return pl.BlockSpec(shape, lambda *_: (0,) * nd, pipeline_mode=pl.Buffered(1))
    return pl.BlockSpec((None,) + tuple(shape), lambda *_: (layer,) + (0,) * nd, pipeline_mode=pl.Buffered(1))


def _dot(a, b):
    return jnp.dot(a, b, preferred_element_type=F32)


def _split3(x):
    hi = x.astype(BF16)
    r1 = x - hi.astype(F32)
    mid = r1.astype(BF16)
    lo = (r1 - mid.astype(F32)).astype(BF16)
    return hi, mid, lo


def _dot_sel_lhs(m, x):
    hi, mid, lo = _split3(x)
    return _dot(m, hi) + (_dot(m, mid) + _dot(m, lo))


def _dot_sel_rhs(x, m):
    hi, mid, lo = _split3(x)
    return _dot(hi, m) + (_dot(mid, m) + _dot(lo, m))


def _bdot(a, b):
    return _dot(a.astype(BF16), b.astype(BF16))


def _sigmoid(x):
    return 1.0 / (1.0 + jnp.exp(-x))


def _softplus(x):
    return jnp.maximum(x, 0.0) + jnp.log1p(jnp.exp(-jnp.abs(x)))


def _rms(x):
    return x * lax.rsqrt(jnp.mean(x * x, axis=-1, keepdims=True) + EPS)


def _norm_mod(x, g, shift, scale):
    return (_rms(x) * g) * (1.0 + scale) + shift


def _lane_head(width, per_head):
    return lax.broadcasted_iota(jnp.int32, (1, width), 1) // per_head


def _block_ones(n, blk):
    i = np.arange(n) // blk
    return (i[:, None] == i[None, :]).astype(np.float32)


def _mod_kernel(c_ref, w_ref, b_ref, o_ref):
    cv = c_ref[...]
    s = cv * _sigmoid(cv)
    o_ref[0] = _dot(s.astype(BF16), w_ref[0].astype(BF16)) + b_ref[0]


def _modulation(cvecs, w_mod, b_mod):
    L, D, N = w_mod.shape
    tn = 1536
    return pl.pallas_call(
        _mod_kernel,
        grid=(L, N // tn),
        in_specs=[pl.BlockSpec((8, D), lambda l, n: (0, 0)),
                  pl.BlockSpec((1, D, tn), lambda l, n: (l, 0, n)),
                  pl.BlockSpec((1, 1, tn), lambda l, n: (l, 0, n))],
        out_specs=pl.BlockSpec((1, 8, tn), lambda l, n: (l, 0, n)),
        out_shape=jax.ShapeDtypeStruct((L, 8, N), F32),
        compiler_params=_cparams(2),
    )(cvecs, w_mod, b_mod.reshape(L, 1, N))


def _tok_tiles(S, T):
    assert T % TMT == 0 and 0 < S - T <= TMT
    return T // TMT + 1, T // TMT


def _tok_spec(width, col_block=0):
    return pl.BlockSpec((1, TMT, width), lambda b, i: (b, i, col_block))


def _mod_spec(nlat, layer):
    return pl.BlockSpec((None, 1, 1, 8, D_MODEL), lambda b, i: (layer, b, jnp.where(i < nlat, 1, 0), 0, 0))


def _halo_specs(width, col_block, S):
    per = TMT // HALO
    last = S // HALO - 1
    prev = pl.BlockSpec((1, HALO, width), lambda b, i: (b, jnp.maximum(i * per - 1, 0), col_block))
    nxt = pl.BlockSpec((1, HALO, width), lambda b, i: (b, jnp.minimum((i + 1) * per, last), col_block))
    return prev, nxt


def _neighbour_ok(i, nlat):
    prev_ok = jnp.logical_and(i != 0, i != nlat)
    next_ok = i < nlat - 1
    return prev_ok, next_ok


def _rows_in_tile(i, nlat, tc):
    return jnp.where(i < nlat, TMT, tc)


def _in_proj_kernel(x_ref, mod_ref, g_ref, w_ref, z_ref):
    m = mod_ref[0, 0]
    h = _norm_mod(x_ref[0], g_ref[...], m[0:1], m[1:2])
    z_ref[0] = _dot(h.astype(BF16), w_ref[...])


def _in_proj(xs, modtab, g_pre, w_in_p, T, layer):
    B, S, D = xs.shape
    ntiles, nlat = _tok_tiles(S, T)
    return pl.pallas_call(
        _in_proj_kernel,
        grid=(B, ntiles),
        in_specs=[_tok_spec(D), _mod_spec(nlat, layer), _const_spec((1, D), layer),
                  _const_spec((D, Z_COLS), layer)],
        out_specs=_tok_spec(Z_COLS),
        out_shape=jax.ShapeDtypeStruct((B, S, Z_COLS), F32),
        compiler_params=_cparams(2),
    )(xs, modtab, g_pre, w_in_p)


def _out_proj_kernel(x_ref, na_ref, mla_ref, yf_ref, yb_ref, bon_ref, g_ref, of_ref, ob_ref, og_ref,
                     mod_ref, gpost_ref, w_ref, lnw_ref, lnb_ref, gn_ref, ones_ref, o_ref):
    m = mod_ref[0, 0]
    ones = ones_ref[...]
    inv_n = 1.0 / HEAD_DIM
    y = yf_ref[0] + yb_ref[0]
    yc = y - _dot_sel_rhs(y, ones) * inv_n
    var = _dot_sel_rhs(yc * yc, ones) * inv_n
    y_rw = (yc * lax.rsqrt(var + RW_GN_EPS) * lnw_ref[...] + lnb_ref[...] + bon_ref[0]) * g_ref[0]
    o = of_ref[0] + ob_ref[0]
    ms = _dot_sel_rhs(o * o, ones) * inv_n
    og = og_ref[0]
    y_gla = (o * lax.rsqrt(ms + EPS) * gn_ref[...]) * (og * _sigmoid(og))
    y = jnp.concatenate([na_ref[0], mla_ref[0], y_rw, y_gla], axis=-1)
    y = _dot(y.astype(BF16), w_ref[...])
    o_ref[0] = x_ref[0] + m[2:3] * (_rms(y) * gpost_ref[...])


def _out_proj(xs, z, y_na, y_mla, yf, yb, bon, g, of, ob, modtab, g_post, w_out_b, ln_w, ln_b, gla_norm,
              ones_b, T, layer):
    B, S, D = xs.shape
    ntiles, nlat = _tok_tiles(S, T)
    grp = _tok_spec(GROUP_W)
    vec = _const_spec((1, GROUP_W), layer)
    return pl.pallas_call(
        _out_proj_kernel,
        grid=(B, ntiles),
        in_specs=[_tok_spec(D), grp, grp, grp, grp, grp, grp, grp, grp,
                  _tok_spec(GROUP_W, COL_GLA_O // GROUP_W),
                  _mod_spec(nlat, layer), _const_spec((1, D), layer), _const_spec((D, D), layer), vec, vec, vec,
                  _const_spec((GROUP_W, GROUP_W))],
        out_specs=_tok_spec(D),
        out_shape=jax.ShapeDtypeStruct((B, S, D), F32),
        compiler_params=_cparams(2),
    )(xs, y_na, y_mla, yf, yb, bon, g, of, ob, z, modtab, g_post, w_out_b, ln_w, ln_b, gla_norm, ones_b)


def _ffn_kernel(xp_ref, x_ref, xn_ref, mod_ref, gpre_ref, gpost_ref, wup_ref, cw_ref, cb_ref,
                wdn_ref, o_ref, *, nlat, tc):
    i = pl.program_id(1)
    prev_ok, next_ok = _neighbour_ok(i, nlat)
    m = mod_ref[0, 0]
    x = x_ref[0]
    xe = jnp.concatenate([xp_ref[0], x, xn_ref[0]], axis=0)
    n = TMT + 2 * HALO
    h = _norm_mod(xe, gpre_ref[...], m[3:4], m[4:5])
    row = lax.broadcasted_iota(jnp.int32, (n, 1), 0)
    valid = jnp.logical_or(jnp.logical_and(row >= HALO, row < HALO + _rows_in_tile(i, nlat, tc)),
                           jnp.logical_or(jnp.logical_and(row < HALO, prev_ok),
                                          jnp.logical_and(row >= HALO + TMT, next_ok)))
    hb = jnp.where(valid, h, 0.0).astype(BF16)
    nchunks = D_FF // FF_CHUNK

    def up(c):
        return [_dot(hb, wup_ref[:, base + c * FF_CHUNK:base + (c + 1) * FF_CHUNK]) for base in (0, D_FF)]

    def conv(z, lo):
        cw = cw_ref[:, lo:lo + FF_CHUNK]
        return (cb_ref[:, lo:lo + FF_CHUNK]
                + pltpu.roll(z, 1, 0)[HALO:HALO + TMT] * cw[0:1]
                + z[HALO:HALO + TMT] * cw[1:2]
                + pltpu.roll(z, n - 1, 0)[HALO:HALO + TMT] * cw[2:3])

    acc = jnp.zeros((TMT, D_MODEL), F32)
    z_next = up(0)
    for c in range(nchunks):
        z_val, z_gate = z_next
        if c + 1 < nchunks:
            z_next = up(c + 1)
        val = conv(z_val, c * FF_CHUNK)
        gate = conv(z_gate, D_FF + c * FF_CHUNK)
        a = (gate * _sigmoid(gate)) * val
        acc = acc + _dot(a.astype(BF16), wdn_ref[c * FF_CHUNK:(c + 1) * FF_CHUNK, :])
    o_ref[0] = x + m[5:6] * (_rms(acc) * gpost_ref[...])


def _ffn(xs, modtab, g_pre, g_post, w_up_b, conv_w, conv_b, w_dn_b, T, layer, latent_only):
    B, S, D = xs.shape
    ntiles, nlat = _tok_tiles(S, T)
    prev, nxt = _halo_specs(D, 0, S)
    return pl.pallas_call(
        functools.partial(_ffn_kernel, nlat=nlat, tc=S - T),
        grid=(B, nlat if latent_only else ntiles),
        in_specs=[prev, _tok_spec(D), nxt, _mod_spec(nlat, layer), _const_spec((1, D), layer),
                  _const_spec((1, D), layer), _const_spec((D, 2 * D_FF), layer),
                  _const_spec((3, 2 * D_FF), layer), _const_spec((1, 2 * D_FF), layer),
                  _const_spec((D_FF, D), layer)],
        out_specs=_tok_spec(D),
        out_shape=jax.ShapeDtypeStruct((B, T if latent_only else S, D), F32),
        compiler_params=_cparams(2),
    )(xs, xs, xs, modtab, g_pre, g_post, w_up_b, conv_w, conv_b, w_dn_b)


def _natten_bias(rpb, rows):
    rt = TQ // GRID_W
    j = np.arange(GRID_W)
    col_start = np.clip(j - NA_COLS // 2, 0, GRID_W - NA_COLS)
    col_in = (j[None, :] >= col_start[:, None]) & (j[None, :] < col_start[:, None] + NA_COLS)
    col_off = np.clip(j[None, :] - j[:, None], -(NA_COLS - 1), NA_COLS - 1) + (NA_COLS - 1)
    bq = jnp.where(col_in, rpb.astype(F32)[..., col_off], -jnp.inf)
    L = rpb.shape[0]
    cases = []
    for r0 in (0, rt, rows - rt):
        r = r0 + np.arange(rt)
        rs = np.clip(r - NA_ROWS // 2, 0, rows - NA_ROWS)
        us = min(max(r0 - NA_ROWS // 2, 0), rows - NA_UNION)
        krow = us + np.arange(NA_UNION)
        ok = (krow[None, :] >= rs[:, None]) & (krow[None, :] < rs[:, None] + NA_ROWS)
        ro = np.clip(krow[None, :] - r[:, None] + (NA_ROWS - 1), 0, 2 * NA_ROWS - 2)
        b = jnp.where(ok[:, :, None, None], bq[:, :, ro], -jnp.inf)
        cases.append(b.transpose(0, 1, 2, 4, 3, 5).reshape(L, N_HEADS, rt * GRID_W, NA_UNION * GRID_W))
    return jnp.stack(cases, axis=1)


def _natten_kernel(q_ref, k_ref, v_ref, qc_ref, kc_ref, vc_ref, bias_ref, o_ref, *, nlat, rows):
    j = pl.program_id(1)
    lane_h = _lane_head(GROUP_W, HEAD_DIM)
    kct = kc_ref[0].T.astype(BF16)
    vc = vc_ref[0].astype(BF16)
    scale = HEAD_DIM ** -0.5
    nwin = NA_UNION * GRID_W

    @pl.when(j < nlat)
    def _():
        us = jnp.clip(j * (TQ // GRID_W) - NA_ROWS // 2, 0, rows - NA_UNION)
        start = pl.multiple_of(us * GRID_W, GRID_W)
        q = q_ref[0] * scale
        kwt = k_ref[0, pl.ds(start, nwin), :].T.astype(BF16)
        vw = v_ref[0, pl.ds(start, nwin), :].astype(BF16)
        acc = jnp.zeros((TQ, GROUP_W), F32)

        def logits(h):
            qh = jnp.where(lane_h == h, q, 0.0).astype(BF16)
            return _dot(qh, kwt), _dot(qh, kct)

        s_next = logits(0)
        for h in range(N_HEADS):
            hm = lane_h == h
            s_w, s_c = s_next
            if h + 1 < N_HEADS:
                s_next = logits(h + 1)
            s_w = s_w + bias_ref[0, h]
            mx = jnp.maximum(jnp.max(s_w, axis=-1, keepdims=True), jnp.max(s_c, axis=-1, keepdims=True))
            p_w = jnp.exp(s_w - mx)
            p_c = jnp.exp(s_c - mx)
            den = jnp.sum(p_w, axis=-1, keepdims=True) + jnp.sum(p_c, axis=-1, keepdims=True)
            o = _dot(p_w.astype(BF16), vw) + _dot(p_c.astype(BF16), vc)
            acc = acc + jnp.where(hm, o * (1.0 / den), 0.0)
        o_ref[0] = acc

    @pl.when(j >= nlat)
    def _():
        q = qc_ref[0] * scale
        acc = jnp.zeros((TQ, GROUP_W), F32)
        for h in range(N_HEADS):
            hm = lane_h == h
            s = _dot(jnp.where(hm, q, 0.0).astype(BF16), kct)
            p = jnp.exp(s - jnp.max(s, axis=-1, keepdims=True))
            den = jnp.sum(p, axis=-1, keepdims=True)
            acc = acc + jnp.where(hm, _dot(p.astype(BF16), vc) * (1.0 / den), 0.0)
        o_ref[0] = acc


def _natten(z, bias, T, layer):
    B, S, _ = z.shape
    nlat = T // TQ
    rows = T // GRID_W
    assert rows >= 16 and rows % (TQ // GRID_W) == 0 and S - T == TQ
    cb = COL_NA // GROUP_W
    lat = lambda c: pl.BlockSpec((1, T, GROUP_W), lambda b, j: (b, 0, cb + c))
    ctx = lambda c: pl.BlockSpec((1, TQ, GROUP_W), lambda b, j: (b, nlat, cb + c))
    case = lambda j: jnp.where(j == 0, 0, jnp.where(j >= nlat - 1, 2, 1))
    return pl.pallas_call(
        functools.partial(_natten_kernel, nlat=nlat, rows=rows),
        grid=(B, nlat + 1),
        in_specs=[pl.BlockSpec((1, TQ, GROUP_W), lambda b, j: (b, jnp.minimum(j, nlat - 1), cb)),
                  lat(1), lat(2), ctx(0), ctx(1), ctx(2),
                  pl.BlockSpec((None, 1, N_HEADS, TQ, NA_UNION * GRID_W), lambda b, j: (layer, case(j), 0, 0, 0))],
        out_specs=pl.BlockSpec((1, TQ, GROUP_W), lambda b, j: (b, j, 0)),
        out_shape=jax.ShapeDtypeStruct((B, S, GROUP_W), F32),
        compiler_params=_cparams(2),
    )(z, z, z, z, z, z, bias)


def _rope_tables(T, Tc):
    t = np.arange(T)
    row = (t // GRID_W).astype(np.float32)
    col = (t % GRID_W).astype(np.float32)
    d = MLA_ROPE // 2
    inv = (np.float32(ROPE_THETA) ** (-np.arange(0, d, 2, dtype=np.float32) / np.float32(d))).astype(np.float32)
    cs, sn = [], []
    for pos in (row, col):
        ang = (pos[:, None] * inv[None, :]).astype(np.float32)
        cs += [np.cos(ang), np.cos(ang)]
        sn += [-np.sin(ang), np.sin(ang)]
    pad = MLA_HEAD_PAD - MLA_NOPE - MLA_ROPE
    f32 = np.float32
    cos = np.concatenate([np.ones((T, MLA_NOPE), f32)] + cs + [np.ones((T, pad), f32)], axis=1)
    sin = np.concatenate([np.zeros((T, MLA_NOPE), f32)] + sn + [np.zeros((T, pad), f32)], axis=1)
    cos = np.concatenate([cos, np.ones((Tc, MLA_HEAD_PAD), f32)], axis=0)
    sin = np.concatenate([sin, np.zeros((Tc, MLA_HEAD_PAD), f32)], axis=0)
    return jnp.asarray(cos, F32), jnp.asarray(sin, F32)


def _rope_swap_perm():
    q = MLA_ROPE // 4
    return np.concatenate([np.arange(q, 2 * q), np.arange(0, q), np.arange(3 * q, 4 * q), np.arange(2 * q, 3 * q)])


def _mla_weights(w_uq, w_ukv):
    L = w_uq.shape[0]
    wq = w_uq.reshape(L, MLA_Q_RANK, N_HEADS, MLA_NOPE + MLA_ROPE)
    pad = MLA_HEAD_PAD - MLA_NOPE - MLA_ROPE
    zq = jnp.zeros((L, MLA_Q_RANK, N_HEADS, pad), F32)
    wq1 = jnp.concatenate([wq, zq], axis=-1).reshape(L, MLA_Q_RANK, N_HEADS * MLA_HEAD_PAD)
    rope_sw = wq[..., MLA_NOPE:][..., _rope_swap_perm()]
    wq2 = jnp.concatenate([jnp.zeros((L, MLA_Q_RANK, N_HEADS, MLA_NOPE), F32), rope_sw, zq], axis=-1)
    wq2 = wq2.reshape(L, MLA_Q_RANK, N_HEADS * MLA_HEAD_PAD)
    wkv = w_ukv.reshape(L, MLA_KV_RANK, N_HEADS, 2 * MLA_NOPE)
    wk = jnp.concatenate([wkv[..., :MLA_NOPE], jnp.zeros((L, MLA_KV_RANK, N_HEADS, MLA_HEAD_PAD - MLA_NOPE), F32)], axis=-1)
    wk = wk.reshape(L, MLA_KV_RANK, N_HEADS * MLA_HEAD_PAD)
    wv = wkv[..., MLA_NOPE:].reshape(L, MLA_KV_RANK, N_HEADS * MLA_NOPE)
    return wq1.astype(BF16), wq2.astype(BF16), wk.astype(BF16), wv.astype(BF16)


def _rope_place():
    e = np.zeros((MLA_ROPE, N_HEADS * MLA_HEAD_PAD), np.float32)
    for h in range(N_HEADS):
        e[np.arange(MLA_ROPE), h * MLA_HEAD_PAD + MLA_NOPE + np.arange(MLA_ROPE)] = 1.0
    return jnp.asarray(e, BF16)


def _mla_up_kernel(z_ref, cos_ref, sin_ref, qn_ref, kvn_ref, wq1_ref, wq2_ref, wk_ref, wv_ref, e_ref,
                   q_ref, kt_ref, v_ref):
    z = z_ref[0]
    ckv = z[:, :MLA_KV_RANK]
    cq = z[:, MLA_KV_RANK:MLA_KV_RANK + MLA_Q_RANK]
    kr = z[:, MLA_KV_RANK + MLA_Q_RANK:MLA_KV_RANK + MLA_Q_RANK + MLA_ROPE]
    krs = z[:, MLA_KV_RANK + MLA_Q_RANK + MLA_ROPE:]
    cos = jnp.concatenate([cos_ref[...]] * N_HEADS, axis=-1)
    sin = jnp.concatenate([sin_ref[...]] * N_HEADS, axis=-1)
    nq = (_rms(cq) * qn_ref[...]).astype(BF16)
    nkv = (_rms(ckv) * kvn_ref[...]).astype(BF16)
    q = _dot(nq, wq1_ref[...]) * cos + _dot(nq, wq2_ref[...]) * sin
    scale = (MLA_NOPE + MLA_ROPE) ** -0.5
    q_ref[0] = (q * scale).astype(BF16)
    k = _dot(nkv, wk_ref[...]) + _dot_sel_rhs(kr, e_ref[...]) * cos + _dot_sel_rhs(krs, e_ref[...]) * sin
    kt_ref[0] = k.T.astype(BF16)
    v_ref[0] = _dot(nkv, wv_ref[...]).astype(BF16)


def _mla_up(z, cos, sin, q_norm, kv_norm, wq1, wq2, wk, wv, place, T, layer):
    B, S, _ = z.shape
    ntiles, _ = _tok_tiles(S, T)
    HP = N_HEADS * MLA_HEAD_PAD
    zw = MLA_KV_RANK + MLA_Q_RANK + 2 * MLA_ROPE
    tab = pl.BlockSpec((TMT, MLA_HEAD_PAD), lambda b, i: (i, 0))
    return pl.pallas_call(
        _mla_up_kernel,
        grid=(B, ntiles),
        in_specs=[_tok_spec(zw, COL_MLA // zw), tab, tab,
                  _const_spec((1, MLA_Q_RANK), layer), _const_spec((1, MLA_KV_RANK), layer),
                  _const_spec((MLA_Q_RANK, HP), layer), _const_spec((MLA_Q_RANK, HP), layer),
                  _const_spec((MLA_KV_RANK, HP), layer), _const_spec((MLA_KV_RANK, GROUP_W), layer),
                  _const_spec((MLA_ROPE, HP))],
        out_specs=[_tok_spec(HP), pl.BlockSpec((1, HP, TMT), lambda b, i: (b, 0, i)), _tok_spec(GROUP_W)],
        out_shape=[jax.ShapeDtypeStruct((B, S, HP), BF16),
                   jax.ShapeDtypeStruct((B, HP, S), BF16),
                   jax.ShapeDtypeStruct((B, S, GROUP_W), BF16)],
        compiler_params=_cparams(2),
    )(z, cos, sin, q_norm, kv_norm, wq1, wq2, wk, wv, place)


def _mla_attn_kernel(q_ref, kt_ref, v_ref, o_ref, *, nlat, T):
    j = pl.program_id(1)
    lane_h = _lane_head(GROUP_W, HEAD_DIM)

    def attend(lo, hi):
        acc = jnp.zeros((TQ, GROUP_W), F32)
        v = v_ref[0, lo:hi, :]

        def logits(h):
            q = q_ref[0, :, h * MLA_HEAD_PAD:(h + 1) * MLA_HEAD_PAD]
            return _dot(q, kt_ref[0, h * MLA_HEAD_PAD:(h + 1) * MLA_HEAD_PAD, lo:hi])

        s_next = logits(0)
        for h in range(N_HEADS):
            s = s_next
            if h + 1 < N_HEADS:
                s_next = logits(h + 1)
            p = jnp.exp(s - jnp.max(s, axis=-1, keepdims=True))
            den = jnp.sum(p, axis=-1, keepdims=True)
            acc = acc + jnp.where(lane_h == h, _dot(p.astype(BF16), v) * (1.0 / den), 0.0)
        o_ref[0] = acc

    @pl.when(j < nlat)
    def _():
        attend(0, T + TQ)

    @pl.when(j >= nlat)
    def _():
        attend(T, T + TQ)


def _mla_attn(q, kt, v, T):
    B, S, HP = q.shape
    nlat = T // TQ
    return pl.pallas_call(
        functools.partial(_mla_attn_kernel, nlat=nlat, T=T),
        grid=(B, S // TQ),
        in_specs=[pl.BlockSpec((1, TQ, HP), lambda b, j: (b, j, 0)),
                  pl.BlockSpec((1, HP, S), lambda b, j: (b, 0, 0)),
                  pl.BlockSpec((1, S, GROUP_W), lambda b, j: (b, 0, 0))],
        out_specs=pl.BlockSpec((1, TQ, GROUP_W), lambda b, j: (b, j, 0)),
        out_shape=jax.ShapeDtypeStruct((B, S, GROUP_W), F32),
        compiler_params=_cparams(2),
    )(q, kt, v)


def _scan_masks():
    t = np.arange(CHUNK)
    inc = np.stack([t[:, None] >= t[None, :], t[:, None] <= t[None, :]]).astype(np.float32)
    strict = np.stack([t[:, None] > t[None, :], t[:, None] < t[None, :]]).astype(np.float32)
    return jnp.asarray(np.tile(inc, (1, 1, N_HEADS))), jnp.asarray(np.tile(strict, (1, 1, N_HEADS)))


def _cumsum_rows(x, reverse):
    n = x.shape[0]
    row = lax.broadcasted_iota(jnp.int32, (n, 1), 0)
    sh = 1
    while sh < n:
        if reverse:
            x = x + jnp.where(row < n - sh, pltpu.roll(x, n - sh, 0), 0.0)
        else:
            x = x + jnp.where(row >= sh, pltpu.roll(x, sh, 0), 0.0)
        sh *= 2
    return x


def _block_diag(x, bm_b):
    return _tile_rows(x.astype(BF16)) * bm_b


def _dot_nt(a, b):
    return lax.dot_general(a, b, (((1,), (1,)), ((), ())), preferred_element_type=F32)


def _fwd_tile(p, nlat):
    return jnp.where(p == 0, nlat, p - 1)


def _bwd_tile(p, nlat):
    return jnp.where(p == 0, nlat, nlat - p)


def _chunk_order(d):
    nch = TQ // CHUNK
    return range(nch) if d == 0 else range(nch - 1, -1, -1)


def _stack_heads(x, lane_h):
    return jnp.concatenate([jnp.where(lane_h == h, x, 0.0) for h in range(N_HEADS)], axis=0)


def _tile_rows(x):
    return jnp.concatenate([x] * N_HEADS, axis=0)


def _collapse_heads(x):
    c = x.shape[0] // N_HEADS
    return (x[0:c] + x[c:2 * c]) + (x[2 * c:3 * c] + x[3 * c:4 * c])


def _rw_prep_kernel(zp_ref, z_ref, zn_ref, mu_ref, kk_ref, ka_ref, rk_ref, w0_ref, a0_ref, wup_ref,
                    aup_ref, gup_ref, ones_ref,
                    r_ref, v_ref, ah_ref, g_ref, bon_ref, lw_ref, kd_ref, bd_ref, *, nlat, tc):
    i = pl.program_id(1)
    prev_ok, next_ok = _neighbour_ok(i, nlat)
    z = z_ref[0]
    row = lax.broadcasted_iota(jnp.int32, (TMT, 1), 0)
    before = jnp.where(prev_ok, zp_ref[0, HALO - 1:HALO, :], 0.0)
    after = jnp.where(next_ok, zn_ref[0, 0:1, :], 0.0)
    zprev = jnp.where(row == 0, before, pltpu.roll(z, 1, 0))
    znext = jnp.where(row == _rows_in_tile(i, nlat, tc) - 1, after, pltpu.roll(z, TMT - 1, 0))
    zs = z + mu_ref[...] * (0.5 * (zprev + znext) - z)
    r = zs[:, 0:GROUP_W]
    k = zs[:, GROUP_W:2 * GROUP_W]
    v = zs[:, 2 * GROUP_W:3 * GROUP_W]
    low = zs[:, 3 * GROUP_W:3 * GROUP_W + 128]
    gd = zs[:, 3 * GROUP_W + 128:]
    ones = ones_ref[...]
    kk = k * kk_ref[...]
    kk = kk * lax.rsqrt(_dot_sel_rhs(kk * kk, ones) + 1e-12)
    wl = _dot(jnp.tanh(low).astype(BF16), wup_ref[...])
    al = _dot(low.astype(BF16), aup_ref[...])
    ksum = jnp.zeros((TMT, GROUP_W), F32)
    for d in range(2):
        w_raw = -_softplus(-(w0_ref[d:d + 1, :] + wl[:, d * GROUP_W:(d + 1) * GROUP_W])) - 0.5
        lw_ref[d, 0] = -jnp.exp(w_raw)
        a = _sigmoid(a0_ref[d:d + 1, :] + al[:, d * GROUP_W:(d + 1) * GROUP_W])
        kd = k * (1.0 + (a - 1.0) * ka_ref[...])
        kd_ref[d, 0] = kd
        bd_ref[d, 0] = kk * a
        ksum = ksum + kd
    r_ref[0] = r
    v_ref[0] = v
    ah_ref[0] = -kk
    g_ref[0] = _dot(_sigmoid(gd).astype(BF16), gup_ref[...])
    bon_ref[0] = _dot_sel_rhs(r * ksum * rk_ref[...], ones) * v


def _rw_prep(z, mu, k_k, k_a, r_k, w0, a0, wup_p, aup_p, gup_b, ones_b, T, layer):
    B, S, _ = z.shape
    ntiles, nlat = _tok_tiles(S, T)
    W = 4 * GROUP_W
    cb = COL_RW // W
    prev, nxt = _halo_specs(W, cb, S)
    one = _tok_spec(GROUP_W)
    two = pl.BlockSpec((2, 1, TMT, GROUP_W), lambda b, i: (0, b, i, 0))
    s1 = jax.ShapeDtypeStruct((B, S, GROUP_W), F32)
    s2 = jax.ShapeDtypeStruct((2, B, S, GROUP_W), F32)
    vec = _const_spec((1, GROUP_W), layer)
    return pl.pallas_call(
        functools.partial(_rw_prep_kernel, nlat=nlat, tc=S - T),
        grid=(B, ntiles),
        in_specs=[prev, _tok_spec(W, cb), nxt,
                  _const_spec((1, W), layer), vec, vec, vec,
                  _const_spec((2, GROUP_W), layer), _const_spec((2, GROUP_W), layer),
                  _const_spec((128, 2 * GROUP_W), layer), _const_spec((128, 2 * GROUP_W), layer),
                  _const_spec((128, GROUP_W), layer), _const_spec((GROUP_W, GROUP_W))],
        out_specs=[one, one, one, one, one, two, two, two],
        out_shape=[s1, s1, s1, s1, s1, s2, s2, s2],
        compiler_params=_cparams(2),
    )(z, z, z, mu, k_k, k_a, r_k, w0, a0, wup_p, aup_p, gup_b, ones_b)


def _rw_scan_kernel(rf_ref, rb_ref, vf_ref, vb_ref, af_ref, ab_ref, lwf_ref, lwb_ref, kdf_ref, kdb_ref,
                    bdf_ref, bdb_ref, inc_ref, strict_ref, bm_ref, yf_ref, yb_ref, s_ref):
    p = pl.program_id(1)

    @pl.when(p == 0)
    def _():
        s_ref[...] = jnp.zeros_like(s_ref)

    n = N_HEADS * CHUNK
    bm = bm_ref[...]
    eye = (lax.broadcasted_iota(jnp.int32, (CHUNK, n), 1) % CHUNK
           == lax.broadcasted_iota(jnp.int32, (CHUNK, n), 0)).astype(F32)
    refs = ((rf_ref, vf_ref, af_ref, lwf_ref, kdf_ref, bdf_ref, yf_ref),
            (rb_ref, vb_ref, ab_ref, lwb_ref, kdb_ref, bdb_ref, yb_ref))
    orders = [list(_chunk_order(d)) for d in range(2)]
    inst = [(step, d) for step in range(TQ // CHUNK) for d in range(2)]
    idx = range(len(inst))
    rows = [pl.ds(orders[d][step] * CHUNK, CHUNK) for step, d in inst]
    bm_b = bm.astype(BF16)
    bd_of = lambda m: _block_diag(m, bm_b)

    ar, r_t, v_s, a_s, b_s, k_s, bk_t, decay, v_in = [], [], [], [], [], [], [], [], []
    for (step, d), sl in zip(inst, rows):
        r_ref, v_ref, a_ref, lw_ref, kd_ref, bd_ref, _ = refs[d]
        lw = lw_ref[0, 0, sl, :]
        kd = kd_ref[0, 0, sl, :]
        bd = bd_ref[0, 0, sl, :]
        v = v_ref[0, sl, :]
        cs = _cumsum_rows(lw, reverse=(d == 1))
        tot = cs[CHUNK - 1:CHUNK, :] if d == 0 else cs[0:1, :]
        e_neg = jnp.exp(-cs)
        e_hat = jnp.exp(tot - cs)
        a_t = a_ref[0, sl, :] * jnp.exp(cs - lw)
        rt = r_ref[0, sl, :] * jnp.exp(cs)
        ar.append(jnp.concatenate([a_t, rt], axis=0).astype(BF16))
        r_t.append(rt)
        v_in.append(v)
        v_s.append(bd_of(v))
        a_s.append(bd_of(a_t))
        b_s.append(bd_of(bd * e_neg))
        k_s.append(bd_of(kd * e_neg))
        t = jnp.concatenate([bd * e_hat, kd * e_hat, jnp.broadcast_to(jnp.exp(tot), (2 * CHUNK, GROUP_W))], axis=0).T
        bk_t.append(t[:, :2 * CHUNK].astype(BF16))
        decay.append(jnp.concatenate([t[:, 2 * CHUNK:]] * 2, axis=1))
    g_b = [_dot_nt(ar[i], b_s[i]) for i in idx]
    g_k = [_dot_nt(ar[i], k_s[i]) for i in idx]
    pw = [g_b[i][:CHUNK] * strict_ref[inst[i][1]] for i in idx]
    ak = [(g_k[i][:CHUNK] * strict_ref[inst[i][1]]).astype(BF16) for i in idx]
    rbk = [jnp.concatenate([g_b[i][CHUNK:] * inc_ref[inst[i][1]], g_k[i][CHUNK:] * inc_ref[inst[i][1]]],
                           axis=1).astype(BF16) for i in idx]
    x = [eye + pw[i] for i in idx]
    pw = [_dot(pw[i].astype(BF16), bd_of(pw[i])) for i in idx]
    for _ in range(4):
        px = [_dot(jnp.concatenate([pw[i], x[i]], axis=0).astype(BF16), bd_of(pw[i])) for i in idx]
        x = [x[i] + px[i][CHUNK:] for i in idx]
        pw = [px[i][:CHUNK] for i in idx]
    x = [(x[i] + _dot(x[i].astype(BF16), bd_of(pw[i]))).astype(BF16) for i in idx]
    akv = [_dot(ak[i], v_s[i]) for i in idx]
    p12 = [_dot(x[i], jnp.concatenate([bd_of(akv[i]), a_s[i]], axis=1)) for i in idx]
    p1_s = [bd_of(p12[i][:, :GROUP_W]) for i in idx]
    p2_s = [bd_of(p12[i][:, GROUP_W:]) for i in idx]
    q = [r_t[i] + _dot(rbk[i][:, :n], p2_s[i]) for i in idx]
    y_c = [_dot(rbk[i], jnp.concatenate([p1_s[i], v_s[i]], axis=0)) for i in idx]
    zero = jnp.zeros((CHUNK, GROUP_W), F32)
    gc = [_dot(bk_t[i], jnp.concatenate(
        [jnp.concatenate([p12[i][:, GROUP_W:], p12[i][:, :GROUP_W]], axis=1),
         jnp.concatenate([zero, v_in[i]], axis=1)], axis=0).astype(BF16)) for i in idx]
    gq = [jnp.concatenate([gc[i][:, :GROUP_W] * bm, q[i]], axis=0).astype(BF16) for i in idx]
    c_s = [gc[i][:, GROUP_W:] * bm for i in idx]

    s = [s_ref[0], s_ref[1]]
    for i, ((step, d), sl) in enumerate(zip(inst, rows)):
        m = _dot(gq[i], s[d].astype(BF16))
        refs[d][6][0, sl, :] = m[n:] + y_c[i]
        s[d] = decay[i] * s[d] + m[:n] + c_s[i]
    s_ref[0] = s[0]
    s_ref[1] = s[1]


def _rw_scan(r, v, ah, lw, kd, bd, masks, bm, T):
    B, S, _ = r.shape
    nlat = T // TQ
    inc, strict = masks
    n = N_HEADS * CHUNK
    fwd = pl.BlockSpec((1, TQ, GROUP_W), lambda b, p: (b, _fwd_tile(p, nlat), 0))
    bwd = pl.BlockSpec((1, TQ, GROUP_W), lambda b, p: (b, _bwd_tile(p, nlat), 0))
    fwd2 = pl.BlockSpec((1, 1, TQ, GROUP_W), lambda b, p: (0, b, _fwd_tile(p, nlat), 0))
    bwd2 = pl.BlockSpec((1, 1, TQ, GROUP_W), lambda b, p: (1, b, _bwd_tile(p, nlat), 0))
    out = jax.ShapeDtypeStruct((B, S, GROUP_W), F32)
    return pl.pallas_call(
        _rw_scan_kernel,
        grid=(B, nlat + 1),
        in_specs=[fwd, bwd, fwd, bwd, fwd, bwd, fwd2, bwd2, fwd2, bwd2, fwd2, bwd2,
                  _const_spec((2, CHUNK, n)), _const_spec((2, CHUNK, n)), _const_spec((n, n))],
        out_specs=[fwd, bwd],
        out_shape=[out, out],
        scratch_shapes=[pltpu.VMEM((2, n, GROUP_W), F32)],
        compiler_params=_cparams(2),
    )(r, r, v, v, ah, ah, lw, lw, kd, kd, bd, bd, inc, strict, bm)


def _gla_scan_kernel(qf_ref, qb_ref, kf_ref, kb_ref, gf_ref, gb_ref, vf_ref, vb_ref, gup_ref, gbias_ref,
                     inc_ref, bm_ref, bmv_ref, of_ref, ob_ref, s_ref):
    p = pl.program_id(1)

    @pl.when(p == 0)
    def _():
        s_ref[...] = jnp.zeros_like(s_ref)

    wk = N_HEADS * GLA_DK
    n = N_HEADS * CHUNK
    bm = bm_ref[...]
    bmv = bmv_ref[...].astype(BF16)
    bmk = (lax.broadcasted_iota(jnp.int32, (n, wk), 0) // CHUNK
           == lax.broadcasted_iota(jnp.int32, (n, wk), 1) // GLA_DK).astype(F32).astype(BF16)
    refs = ((qf_ref, kf_ref, gf_ref, vf_ref, of_ref), (qb_ref, kb_ref, gb_ref, vb_ref, ob_ref))
    orders = [list(_chunk_order(d)) for d in range(2)]
    inst = [(step, d) for step in range(TQ // CHUNK) for d in range(2)]
    idx = range(len(inst))
    rows = [pl.ds(orders[d][step] * CHUNK, CHUNK) for step, d in inst]

    qe, ke_s, ks4, v_s, decay = [], [], [], [], []
    for (step, d), sl in zip(inst, rows):
        q_ref, k_ref, g_ref, v_ref, _ = refs[d]
        k = k_ref[0, sl, :]
        la = -_softplus(-(_dot(g_ref[0, sl, :].astype(BF16), gup_ref[d]) + gbias_ref[d])) * (1.0 / GLA_TAU)
        b = _cumsum_rows(la, reverse=(d == 1))
        tot = b[CHUNK - 1:CHUNK, :] if d == 0 else b[0:1, :]
        qe.append((q_ref[0, sl, :] * (GLA_DK ** -0.5) * jnp.exp(b)).astype(BF16))
        ke_s.append(_block_diag(k * jnp.exp(-b), bmk))
        ks4.append(_tile_rows(k * jnp.exp(tot - b)).T.astype(BF16))
        v_s.append(_block_diag(v_ref[0, sl, :], bmv))
        decay.append(_tile_rows(jnp.broadcast_to(jnp.exp(tot), (CHUNK, wk))).T)
    a_cat = [(_dot_nt(qe[i], ke_s[i]) * inc_ref[inst[i][1]]).astype(BF16) for i in idx]
    o_in = [_dot(a_cat[i], v_s[i]) for i in idx]
    kv = [_dot(ks4[i], v_s[i]) * bm for i in idx]

    s = [s_ref[0], s_ref[1]]
    for i, ((step, d), sl) in enumerate(zip(inst, rows)):
        refs[d][4][0, sl, :] = o_in[i] + _dot(qe[i], s[d].astype(BF16))
        s[d] = decay[i] * s[d] + kv[i]
    s_ref[0] = s[0]
    s_ref[1] = s[1]


def _gla_scan(z, gup_p, gb, masks, bm, bmv, T, layer):
    B, S, _ = z.shape
    nlat = T // TQ
    inc, _ = masks
    n = N_HEADS * CHUNK
    wk = N_HEADS * GLA_DK
    fwd = lambda w, col: pl.BlockSpec((1, TQ, w), lambda b, p: (b, _fwd_tile(p, nlat), col // w))
    bwd = lambda w, col: pl.BlockSpec((1, TQ, w), lambda b, p: (b, _bwd_tile(p, nlat), col // w))
    out = jax.ShapeDtypeStruct((B, S, GROUP_W), F32)
    return pl.pallas_call(
        _gla_scan_kernel,
        grid=(B, nlat + 1),
        in_specs=[fwd(wk, COL_GLA_Q), bwd(wk, COL_GLA_Q), fwd(wk, COL_GLA_K), bwd(wk, COL_GLA_K),
                  fwd(wk, COL_GLA_G), bwd(wk, COL_GLA_G), fwd(GROUP_W, COL_GLA_V), bwd(GROUP_W, COL_GLA_V),
                  _const_spec((2, wk, wk), layer), _const_spec((2, 1, wk), layer),
                  _const_spec((2, CHUNK, n)), _const_spec((wk, GROUP_W)), _const_spec((n, n))],
        out_specs=[fwd(GROUP_W, 0), bwd(GROUP_W, 0)],
        out_shape=[out, out],
        scratch_shapes=[pltpu.VMEM((2, wk, GROUP_W), F32)],
        compiler_params=_cparams(2),
    )(z, z, z, z, z, z, z, z, gup_p, gb, inc, bm, bmv)


def _pack_w_in(w_in):
    L, D, _ = w_in.shape
    na, mla, rw, gla = jnp.split(w_in, [768, 1120, 2144], axis=-1)
    cq, ckv, kr = jnp.split(mla, [MLA_Q_RANK, MLA_Q_RANK + MLA_KV_RANK], axis=-1)
    gq, gk, gv, gg, go = jnp.split(gla, [128, 256, 512, 528], axis=-1)
    pad = jnp.zeros((L, D, COL_GLA_V - COL_GLA_G - gg.shape[-1]), F32)
    packed = jnp.concatenate([na, ckv, cq, kr, kr[..., _rope_swap_perm()], gq, gk, gg, pad, gv, go, rw], axis=-1)
    assert packed.shape[-1] == Z_COLS
    return packed.astype(BF16)


def kernel(x, c, ctx, c_ctx, w_mod, b_mod, g_mix_pre, g_mix_post, g_ffn_pre, g_ffn_post, w_in, w_out, na_rpb, mla_q_norm, mla_w_uq, mla_kv_norm, mla_w_ukv, rw_mu, rw_w0, rw_w_up, rw_a0, rw_a_up, rw_g_up, rw_k_k, rw_k_a, rw_r_k, rw_ln_w, rw_ln_b, gla_gate_up, gla_gate_b, gla_norm, ffn_w_up, ffn_conv_w, ffn_conv_b, ffn_w_down):
    B, T, D = x.shape
    Tc = ctx.shape[1]
    L = w_in.shape[0]
    assert D == D_MODEL and Tc == TQ and T % TMT == 0 and B + 1 <= 8

    w_in_p = _pack_w_in(w_in)
    w_out_b = w_out.astype(BF16)
    wq1, wq2, wk, wv = _mla_weights(mla_w_uq, mla_w_ukv)
    place = _rope_place()
    cos, sin = _rope_tables(T, Tc)
    zero_lo = jnp.zeros((L, 64, 2 * GROUP_W), F32)
    rw_wup_p = jnp.concatenate([jnp.concatenate([rw_w_up[:, 0], rw_w_up[:, 1]], axis=-1), zero_lo], axis=1).astype(BF16)
    rw_aup_p = jnp.concatenate([zero_lo, jnp.concatenate([rw_a_up[:, 0], rw_a_up[:, 1]], axis=-1)], axis=1).astype(BF16)
    rw_gup_b = rw_g_up.astype(BF16)
    wk_gla = N_HEADS * GLA_DK
    gla_gup_p = jnp.concatenate([gla_gate_up, jnp.zeros((L, 2, wk_gla - gla_gate_up.shape[2], wk_gla), F32)], axis=2).astype(BF16)
    ffn_up_b = ffn_w_up.astype(BF16)
    ffn_dn_b = ffn_w_down.astype(BF16)
    ones_b = jnp.asarray(_block_ones(GROUP_W, HEAD_DIM), BF16)
    rw_bm = jnp.asarray(_block_ones(N_HEADS * CHUNK, CHUNK))
    gla_bm = jnp.asarray((np.arange(wk_gla)[:, None] // GLA_DK == np.arange(GROUP_W)[None, :] // HEAD_DIM).astype(np.float32))
    masks = _scan_masks()

    cvecs = jnp.zeros((8, D), F32).at[:B].set(c).at[B].set(c_ctx)
    mods = _modulation(cvecs, w_mod, b_mod).reshape(L, 8, 6, D)
    mods = jnp.pad(mods, ((0, 0), (0, 0), (0, 2), (0, 0)))
    modtabs = jnp.stack([jnp.broadcast_to(mods[:, B:B + 1], (L, B, 8, D)), mods[:, :B]], axis=2)

    xs = jnp.concatenate([x, ctx], axis=1)
    rows = lambda a: a.reshape(L, 1, -1)
    na_bias = _natten_bias(na_rpb, T // GRID_W)
    gla_gb = gla_gate_b[:, :, None, :]
    for i in range(L):
        z = _in_proj(xs, modtabs, rows(g_mix_pre), w_in_p, T, i)
        y_na = _natten(z, na_bias, T, i)
        q, kt, v = _mla_up(z, cos, sin, rows(mla_q_norm), rows(mla_kv_norm), wq1, wq2, wk, wv, place, T, i)
        y_mla = _mla_attn(q, kt, v, T)
        r, vv, ah, g, bon, lw, kd, bd = _rw_prep(z, rows(rw_mu), rows(rw_k_k), rows(rw_k_a), rows(rw_r_k),
                                                 rw_w0, rw_a0, rw_wup_p, rw_aup_p, rw_gup_b, ones_b, T, i)
        yf, yb = _rw_scan(r, vv, ah, lw, kd, bd, masks, rw_bm, T)
        of, ob = _gla_scan(z, gla_gup_p, gla_gb, masks, gla_bm, rw_bm, T, i)
        xs = _out_proj(xs, z, y_na, y_mla, yf, yb, bon, g, of, ob, modtabs, rows(g_mix_post), w_out_b,
                       rows(rw_ln_w), rows(rw_ln_b), rows(gla_norm), ones_b, T, i)
        xs = _ffn(xs, modtabs, rows(g_ffn_pre), rows(g_ffn_post), ffn_up_b, ffn_conv_w,
                  rows(ffn_conv_b), ffn_dn_b, T, i, latent_only=(i == L - 1))
    return xs
```

```python
import functools

import numpy as np
import jax
import jax.numpy as jnp
from jax import lax
from jax.experimental import pallas as pl
from jax.experimental.pallas import tpu as pltpu

F32 = jnp.float32
BF16 = jnp.bfloat16

D_MODEL = 1024
GRID_W = 64
EPS = 1e-6
N_HEADS = 4
HEAD_DIM = 64
GROUP_W = 256
NA_ROWS = 8
NA_COLS = 16
NA_UNION = 12
MLA_Q_RANK = 192
MLA_KV_RANK = 128
MLA_NOPE = 64
MLA_ROPE = 32
MLA_HEAD_PAD = 128
ROPE_THETA = 10000.0
RW_GN_EPS = 64e-5
GLA_DK = 32
GLA_TAU = 16.0
D_FF = 2816
CHUNK = 64
TQ = 256
TMT = 512
HALO = 8
FF_CHUNK = 256
FF_GROUP = 4
Z_COLS = 3072

COL_NA = 0
COL_MLA = 768
COL_GLA_Q = 1152
COL_GLA_K = 1280
COL_GLA_G = 1408
COL_GLA_V = 1536
COL_GLA_O = 1792
COL_RW = 2048

VMEM_LIMIT_V7X = 56 * 1024 * 1024


def _cparams(n_axes):
    return pltpu.CompilerParams(dimension_semantics=("arbitrary",) * n_axes,
                                vmem_limit_bytes=VMEM_LIMIT_V7X)


def _const_spec(shape, layer=None):
    nd = len(shape)
    if layer is None:
        return pl.BlockSpec(shape, lambda *_: (0,) * nd, pipeline_mode=pl.Buffered(1))
    return pl.BlockSpec((None,) + tuple(shape), lambda *_: (layer,) + (0,) * nd, pipeline_mode=pl.Buffered(1))


def _dot(a, b):
    return jnp.dot(a, b, preferred_element_type=F32)


def _split3(x):
    hi = x.astype(BF16)
    r1 = x - hi.astype(F32)
    mid = r1.astype(BF16)
    lo = (r1 - mid.astype(F32)).astype(BF16)
    return hi, mid, lo


def _dot_sel_lhs(m, x):
    hi, mid, lo = _split3(x)
    return _dot(m, hi) + (_dot(m, mid) + _dot(m, lo))


def _dot_sel_rhs(x, m):
    hi, mid, lo = _split3(x)
    return _dot(hi, m) + (_dot(mid, m) + _dot(lo, m))


def _bdot(a, b):
    return _dot(a.astype(BF16), b.astype(BF16))


def _sigmoid(x):
    return 1.0 / (1.0 + jnp.exp(-x))


def _silu_gain(x):
    return 0.5 + 0.5 * jnp.tanh(0.5 * x)


def _softplus(x):
    return jnp.maximum(x, 0.0) + jnp.log1p(jnp.exp(-jnp.abs(x)))


def _rms(x):
    return x * lax.rsqrt(jnp.mean(x * x, axis=-1, keepdims=True) + EPS)


def _norm_mod(x, g, shift, scale):
    return (_rms(x) * g) * (1.0 + scale) + shift


def _lane_head(width, per_head):
    return lax.broadcasted_iota(jnp.int32, (1, width), 1) // per_head


def _block_ones(n, blk):
    i = np.arange(n) // blk
    return (i[:, None] == i[None, :]).astype(np.float32)


def _mod_kernel(c_ref, w_ref, b_ref, o_ref):
    cv = c_ref[...]
    s = cv * _sigmoid(cv)
    o_ref[0] = _dot(s.astype(BF16), w_ref[0].astype(BF16)) + b_ref[0]


def _modulation(cvecs, w_mod, b_mod):
    L, D, N = w_mod.shape
    tn = 1536
    return pl.pallas_call(
        _mod_kernel,
        grid=(L, N // tn),
        in_specs=[pl.BlockSpec((8, D), lambda l, n: (0, 0)),
                  pl.BlockSpec((1, D, tn), lambda l, n: (l, 0, n)),
                  pl.BlockSpec((1, 1, tn), lambda l, n: (l, 0, n))],
        out_specs=pl.BlockSpec((1, 8, tn), lambda l, n: (l, 0, n)),
        out_shape=jax.ShapeDtypeStruct((L, 8, N), F32),
        compiler_params=_cparams(2),
    )(cvecs, w_mod, b_mod.reshape(L, 1, N))


def _tok_tiles(S, T):
    assert T % TMT == 0 and 0 < S - T <= TMT
    return T // TMT + 1, T // TMT


def _tok_spec(width, col_block=0):
    return pl.BlockSpec((1, TMT, width), lambda b, i: (b, i, col_block))


def _mod_spec(nlat, layer):
    return pl.BlockSpec((None, 1, 1, 8, D_MODEL), lambda b, i: (layer, b, jnp.where(i < nlat, 1, 0), 0, 0))


def _halo_specs(width, col_block, S):
    per = TMT // HALO
    last = S // HALO - 1
    prev = pl.BlockSpec((1, HALO, width), lambda b, i: (b, jnp.maximum(i * per - 1, 0), col_block))
    nxt = pl.BlockSpec((1, HALO, width), lambda b, i: (b, jnp.minimum((i + 1) * per, last), col_block))
    return prev, nxt


def _neighbour_ok(i, nlat):
    prev_ok = jnp.logical_and(i != 0, i != nlat)
    next_ok = i < nlat - 1
    return prev_ok, next_ok


def _rows_in_tile(i, nlat, tc):
    return jnp.where(i < nlat, TMT, tc)


def _in_proj_kernel(x_ref, mod_ref, g_ref, w_ref, z_ref):
    m = mod_ref[0, 0]
    h = _norm_mod(x_ref[0], g_ref[...], m[0:1], m[1:2])
    z_ref[0] = _dot(h.astype(BF16), w_ref[...])


def _in_proj(xs, modtab, g_pre, w_in_p, T, layer):
    B, S, D = xs.shape
    ntiles, nlat = _tok_tiles(S, T)
    return pl.pallas_call(
        _in_proj_kernel,
        grid=(B, ntiles),
        in_specs=[_tok_spec(D), _mod_spec(nlat, layer), _const_spec((1, D), layer),
                  _const_spec((D, Z_COLS), layer)],
        out_specs=_tok_spec(Z_COLS),
        out_shape=jax.ShapeDtypeStruct((B, S, Z_COLS), F32),
        compiler_params=_cparams(2),
    )(xs, modtab, g_pre, w_in_p)


def _out_proj_kernel(x_ref, na_ref, mla_ref, yf_ref, yb_ref, bon_ref, g_ref, of_ref, ob_ref, og_ref,
                     mod_ref, gpost_ref, w_ref, lnw_ref, lnb_ref, gn_ref, ones_ref, o_ref):
    m = mod_ref[0, 0]
    ones = ones_ref[...]
    inv_n = 1.0 / HEAD_DIM
    y = yf_ref[0] + yb_ref[0]
    yc = y - _dot_sel_rhs(y, ones) * inv_n
    var = _dot_sel_rhs(yc * yc, ones) * inv_n
    y_rw = (yc * lax.rsqrt(var + RW_GN_EPS) * lnw_ref[...] + lnb_ref[...] + bon_ref[0]) * g_ref[0]
    o = of_ref[0] + ob_ref[0]
    ms = _dot_sel_rhs(o * o, ones) * inv_n
    og = og_ref[0]
    y_gla = (o * lax.rsqrt(ms + EPS) * gn_ref[...]) * (og * _silu_gain(og))
    y = jnp.concatenate([na_ref[0], mla_ref[0], y_rw.astype(BF16), y_gla.astype(BF16)], axis=-1)
    y = _dot(y, w_ref[...])
    o_ref[0] = x_ref[0] + m[2:3] * (_rms(y) * gpost_ref[...])


def _out_proj(xs, z, y_na, y_mla, yf, yb, bon, g, of, ob, modtab, g_post, w_out_b, ln_w, ln_b, gla_norm,
              ones_b, T, layer):
    B, S, D = xs.shape
    ntiles, nlat = _tok_tiles(S, T)
    grp = _tok_spec(GROUP_W)
    vec = _const_spec((1, GROUP_W), layer)
    return pl.pallas_call(
        _out_proj_kernel,
        grid=(B, ntiles),
        in_specs=[_tok_spec(D), grp, grp, grp, grp, grp, grp, grp, grp,
                  _tok_spec(GROUP_W, COL_GLA_O // GROUP_W),
                  _mod_spec(nlat, layer), _const_spec((1, D), layer), _const_spec((D, D), layer), vec, vec, vec,
                  _const_spec((GROUP_W, GROUP_W))],
        out_specs=_tok_spec(D),
        out_shape=jax.ShapeDtypeStruct((B, S, D), F32),
        compiler_params=_cparams(2),
    )(xs, y_na, y_mla, yf, yb, bon, g, of, ob, z, modtab, g_post, w_out_b, ln_w, ln_b, gla_norm, ones_b)


def _ffn_kernel(xp_ref, x_ref, xn_ref, mod_ref, gpre_ref, gpost_ref, wup_ref, cw_ref, cb_ref,
                wdn_ref, o_ref, *, nlat, tc):
    i = pl.program_id(1)
    prev_ok, next_ok = _neighbour_ok(i, nlat)
    m = mod_ref[0, 0]
    x = x_ref[0]
    xe = jnp.concatenate([xp_ref[0], x, xn_ref[0]], axis=0)
    n = TMT + 2 * HALO
    h = _norm_mod(xe, gpre_ref[...], m[3:4], m[4:5])
    row = lax.broadcasted_iota(jnp.int32, (n, 1), 0)
    valid = jnp.logical_or(jnp.logical_and(row >= HALO, row < HALO + _rows_in_tile(i, nlat, tc)),
                           jnp.logical_or(jnp.logical_and(row < HALO, prev_ok),
                                          jnp.logical_and(row >= HALO + TMT, next_ok)))
    hb = jnp.where(valid, h, 0.0).astype(BF16)
    nchunks = D_FF // FF_CHUNK

    def up(c):
        return [_dot(hb, wup_ref[:, base + c * FF_CHUNK:base + (c + 1) * FF_CHUNK]) for base in (0, D_FF)]

    def conv(z, lo):
        cw = cw_ref[:, lo:lo + FF_CHUNK]
        return (cb_ref[:, lo:lo + FF_CHUNK]
                + pltpu.roll(z, 1, 0)[HALO:HALO + TMT] * cw[0:1]
                + z[HALO:HALO + TMT] * cw[1:2]
                + pltpu.roll(z, n - 1, 0)[HALO:HALO + TMT] * cw[2:3])

    acc = None
    group = []
    z_next = up(0)
    for c in range(nchunks):
        z_val, z_gate = z_next
        if c + 1 < nchunks:
            z_next = up(c + 1)
        val = conv(z_val, c * FF_CHUNK)
        gate = conv(z_gate, D_FF + c * FF_CHUNK)
        group.append(((gate * _silu_gain(gate)) * val).astype(BF16))
        if len(group) == FF_GROUP or c + 1 == nchunks:
            lo = (c + 1 - len(group)) * FF_CHUNK
            part = _dot(jnp.concatenate(group, axis=1), wdn_ref[lo:(c + 1) * FF_CHUNK, :])
            acc = part if acc is None else acc + part
            group = []
    o_ref[0] = x + m[5:6] * (_rms(acc) * gpost_ref[...])


def _ffn(xs, modtab, g_pre, g_post, w_up_b, conv_w, conv_b, w_dn_b, T, layer, latent_only):
    B, S, D = xs.shape
    ntiles, nlat = _tok_tiles(S, T)
    prev, nxt = _halo_specs(D, 0, S)
    return pl.pallas_call(
        functools.partial(_ffn_kernel, nlat=nlat, tc=S - T),
        grid=(B, nlat if latent_only else ntiles),
        in_specs=[prev, _tok_spec(D), nxt, _mod_spec(nlat, layer), _const_spec((1, D), layer),
                  _const_spec((1, D), layer), _const_spec((D, 2 * D_FF), layer),
                  _const_spec((3, 2 * D_FF), layer), _const_spec((1, 2 * D_FF), layer),
                  _const_spec((D_FF, D), layer)],
        out_specs=_tok_spec(D),
        out_shape=jax.ShapeDtypeStruct((B, T if latent_only else S, D), F32),
        compiler_params=_cparams(2),
    )(xs, xs, xs, modtab, g_pre, g_post, w_up_b, conv_w, conv_b, w_dn_b)


def _natten_bias(rpb, rows):
    rt = TQ // GRID_W
    j = np.arange(GRID_W)
    col_start = np.clip(j - NA_COLS // 2, 0, GRID_W - NA_COLS)
    col_in = (j[None, :] >= col_start[:, None]) & (j[None, :] < col_start[:, None] + NA_COLS)
    col_off = np.clip(j[None, :] - j[:, None], -(NA_COLS - 1), NA_COLS - 1) + (NA_COLS - 1)
    bq = jnp.where(col_in, rpb.astype(F32)[..., col_off], -jnp.inf)
    bq = bq.transpose(0, 1, 3, 2, 4)
    L = rpb.shape[0]
    cases = []
    for r0 in (0, rt, rows - rt):
        us = min(max(r0 - NA_ROWS // 2, 0), rows - NA_UNION)
        per_row = []
        for r in range(r0, r0 + rt):
            rs = min(max(r - NA_ROWS // 2, 0), rows - NA_ROWS)
            first = rs - r + (NA_ROWS - 1)
            pre, post = rs - us, us + NA_UNION - (rs + NA_ROWS)
            b = jnp.pad(bq[:, :, :, first:first + NA_ROWS], ((0, 0), (0, 0), (0, 0), (pre, post), (0, 0)),
                        constant_values=-jnp.inf)
            per_row.append(b.reshape(L, N_HEADS, GRID_W, NA_UNION * GRID_W))
        cases.append(jnp.concatenate(per_row, axis=2))
    return jnp.stack(cases, axis=1)


def _natten_kernel(q_ref, k_ref, v_ref, qc_ref, kc_ref, vc_ref, bias_ref, o_ref, *, nlat, rows):
    j = pl.program_id(1)
    lane_h = _lane_head(GROUP_W, HEAD_DIM)
    kct = kc_ref[0].T.astype(BF16)
    vc = vc_ref[0].astype(BF16)
    scale = HEAD_DIM ** -0.5
    nwin = NA_UNION * GRID_W

    @pl.when(j < nlat)
    def _():
        us = jnp.clip(j * (TQ // GRID_W) - NA_ROWS // 2, 0, rows - NA_UNION)
        start = pl.multiple_of(us * GRID_W, GRID_W)
        q = q_ref[0] * scale
        kwt = k_ref[0, pl.ds(start, nwin), :].T.astype(BF16)
        vw = v_ref[0, pl.ds(start, nwin), :].astype(BF16)
        acc = jnp.zeros((TQ, GROUP_W), F32)

        def logits(h):
            qh = jnp.where(lane_h == h, q, 0.0).astype(BF16)
            return _dot(qh, kwt), _dot(qh, kct)

        s_next = logits(0)
        for h in range(N_HEADS):
            hm = lane_h == h
            s_w, s_c = s_next
            if h + 1 < N_HEADS:
                s_next = logits(h + 1)
            s_w = s_w + bias_ref[0, h]
            mx = jnp.maximum(jnp.max(s_w, axis=-1, keepdims=True), jnp.max(s_c, axis=-1, keepdims=True))
            p_w = jnp.exp(s_w - mx)
            p_c = jnp.exp(s_c - mx)
            den = jnp.sum(p_w, axis=-1, keepdims=True) + jnp.sum(p_c, axis=-1, keepdims=True)
            o = _dot(p_w.astype(BF16), vw) + _dot(p_c.astype(BF16), vc)
            acc = acc + jnp.where(hm, o * (1.0 / den), 0.0)
        o_ref[0] = acc.astype(o_ref.dtype)

    @pl.when(j >= nlat)
    def _():
        q = qc_ref[0] * scale
        acc = jnp.zeros((TQ, GROUP_W), F32)
        for h in range(N_HEADS):
            hm = lane_h == h
            s = _dot(jnp.where(hm, q, 0.0).astype(BF16), kct)
            p = jnp.exp(s - jnp.max(s, axis=-1, keepdims=True))
            den = jnp.sum(p, axis=-1, keepdims=True)
            acc = acc + jnp.where(hm, _dot(p.astype(BF16), vc) * (1.0 / den), 0.0)
        o_ref[0] = acc.astype(o_ref.dtype)


def _natten(z, bias, T, layer):
    B, S, _ = z.shape
    nlat = T // TQ
    rows = T // GRID_W
    assert rows >= 16 and rows % (TQ // GRID_W) == 0 and S - T == TQ
    cb = COL_NA // GROUP_W
    lat = lambda c: pl.BlockSpec((1, T, GROUP_W), lambda b, j: (b, 0, cb + c))
    ctx = lambda c: pl.BlockSpec((1, TQ, GROUP_W), lambda b, j: (b, nlat, cb + c))
    case = lambda j: jnp.where(j == 0, 0, jnp.where(j >= nlat - 1, 2, 1))
    return pl.pallas_call(
        functools.partial(_natten_kernel, nlat=nlat, rows=rows),
        grid=(B, nlat + 1),
        in_specs=[pl.BlockSpec((1, TQ, GROUP_W), lambda b, j: (b, jnp.minimum(j, nlat - 1), cb)),
                  lat(1), lat(2), ctx(0), ctx(1), ctx(2),
                  pl.BlockSpec((None, 1, N_HEADS, TQ, NA_UNION * GRID_W), lambda b, j: (layer, case(j), 0, 0, 0))],
        out_specs=pl.BlockSpec((1, TQ, GROUP_W), lambda b, j: (b, j, 0)),
        out_shape=jax.ShapeDtypeStruct((B, S, GROUP_W), BF16),
        compiler_params=_cparams(2),
    )(z, z, z, z, z, z, bias)


def _rope_tables(T, Tc):
    t = np.arange(T)
    row = (t // GRID_W).astype(np.float32)
    col = (t % GRID_W).astype(np.float32)
    d = MLA_ROPE // 2
    inv = (np.float32(ROPE_THETA) ** (-np.arange(0, d, 2, dtype=np.float32) / np.float32(d))).astype(np.float32)
    cs, sn = [], []
    for pos in (row, col):
        ang = (pos[:, None] * inv[None, :]).astype(np.float32)
        cs += [np.cos(ang), np.cos(ang)]
        sn += [-np.sin(ang), np.sin(ang)]
    pad = MLA_HEAD_PAD - MLA_NOPE - MLA_ROPE
    f32 = np.float32
    cos = np.concatenate([np.ones((T, MLA_NOPE), f32)] + cs + [np.ones((T, pad), f32)], axis=1)
    sin = np.concatenate([np.zeros((T, MLA_NOPE), f32)] + sn + [np.zeros((T, pad), f32)], axis=1)
    cos = np.concatenate([cos, np.ones((Tc, MLA_HEAD_PAD), f32)], axis=0)
    sin = np.concatenate([sin, np.zeros((Tc, MLA_HEAD_PAD), f32)], axis=0)
    return jnp.asarray(cos, F32), jnp.asarray(sin, F32)


def _rope_swap_perm():
    q = MLA_ROPE // 4
    return np.concatenate([np.arange(q, 2 * q), np.arange(0, q), np.arange(3 * q, 4 * q), np.arange(2 * q, 3 * q)])


def _mla_weights(w_uq, w_ukv):
    L = w_uq.shape[0]
    wq = w_uq.reshape(L, MLA_Q_RANK, N_HEADS, MLA_NOPE + MLA_ROPE)
    pad = MLA_HEAD_PAD - MLA_NOPE - MLA_ROPE
    zq = jnp.zeros((L, MLA_Q_RANK, N_HEADS, pad), F32)
    wq1 = jnp.concatenate([wq, zq], axis=-1).reshape(L, MLA_Q_RANK, N_HEADS * MLA_HEAD_PAD)
    rope_sw = wq[..., MLA_NOPE:][..., _rope_swap_perm()]
    wq2 = jnp.concatenate([jnp.zeros((L, MLA_Q_RANK, N_HEADS, MLA_NOPE), F32), rope_sw, zq], axis=-1)
    wq2 = wq2.reshape(L, MLA_Q_RANK, N_HEADS * MLA_HEAD_PAD)
    wkv = w_ukv.reshape(L, MLA_KV_RANK, N_HEADS, 2 * MLA_NOPE)
    wk = jnp.concatenate([wkv[..., :MLA_NOPE], jnp.zeros((L, MLA_KV_RANK, N_HEADS, MLA_HEAD_PAD - MLA_NOPE), F32)], axis=-1)
    wk = wk.reshape(L, MLA_KV_RANK, N_HEADS * MLA_HEAD_PAD)
    wv = wkv[..., MLA_NOPE:].reshape(L, MLA_KV_RANK, N_HEADS * MLA_NOPE)
    return wq1.astype(BF16), wq2.astype(BF16), wk.astype(BF16), wv.astype(BF16)


def _rope_place():
    e = np.zeros((MLA_ROPE, N_HEADS * MLA_HEAD_PAD), np.float32)
    for h in range(N_HEADS):
        e[np.arange(MLA_ROPE), h * MLA_HEAD_PAD + MLA_NOPE + np.arange(MLA_ROPE)] = 1.0
    return jnp.asarray(e, BF16)


def _mla_up_kernel(z_ref, cos_ref, sin_ref, qn_ref, kvn_ref, wq1_ref, wq2_ref, wk_ref, wv_ref, e_ref,
                   q_ref, kt_ref, v_ref):
    z = z_ref[0]
    ckv = z[:, :MLA_KV_RANK]
    cq = z[:, MLA_KV_RANK:MLA_KV_RANK + MLA_Q_RANK]
    kr = z[:, MLA_KV_RANK + MLA_Q_RANK:MLA_KV_RANK + MLA_Q_RANK + MLA_ROPE]
    krs = z[:, MLA_KV_RANK + MLA_Q_RANK + MLA_ROPE:]
    cos = jnp.concatenate([cos_ref[...]] * N_HEADS, axis=-1)
    sin = jnp.concatenate([sin_ref[...]] * N_HEADS, axis=-1)
    nq = (_rms(cq) * qn_ref[...]).astype(BF16)
    nkv = (_rms(ckv) * kvn_ref[...]).astype(BF16)
    q = _dot(nq, wq1_ref[...]) * cos + _dot(nq, wq2_ref[...]) * sin
    scale = (MLA_NOPE + MLA_ROPE) ** -0.5
    q_ref[0] = (q * scale).astype(BF16)
    k = _dot(nkv, wk_ref[...]) + _dot_sel_rhs(kr, e_ref[...]) * cos + _dot_sel_rhs(krs, e_ref[...]) * sin
    kt_ref[0] = k.T.astype(BF16)
    v_ref[0] = _dot(nkv, wv_ref[...]).astype(BF16)


def _mla_up(z, cos, sin, q_norm, kv_norm, wq1, wq2, wk, wv, place, T, layer):
    B, S, _ = z.shape
    ntiles, _ = _tok_tiles(S, T)
    HP = N_HEADS * MLA_HEAD_PAD
    zw = MLA_KV_RANK + MLA_Q_RANK + 2 * MLA_ROPE
    tab = pl.BlockSpec((TMT, MLA_HEAD_PAD), lambda b, i: (i, 0))
    return pl.pallas_call(
        _mla_up_kernel,
        grid=(B, ntiles),
        in_specs=[_tok_spec(zw, COL_MLA // zw), tab, tab,
                  _const_spec((1, MLA_Q_RANK), layer), _const_spec((1, MLA_KV_RANK), layer),
                  _const_spec((MLA_Q_RANK, HP), layer), _const_spec((MLA_Q_RANK, HP), layer),
                  _const_spec((MLA_KV_RANK, HP), layer), _const_spec((MLA_KV_RANK, GROUP_W), layer),
                  _const_spec((MLA_ROPE, HP))],
        out_specs=[_tok_spec(HP), pl.BlockSpec((1, HP, TMT), lambda b, i: (b, 0, i)), _tok_spec(GROUP_W)],
        out_shape=[jax.ShapeDtypeStruct((B, S, HP), BF16),
                   jax.ShapeDtypeStruct((B, HP, S), BF16),
                   jax.ShapeDtypeStruct((B, S, GROUP_W), BF16)],
        compiler_params=_cparams(2),
    )(z, cos, sin, q_norm, kv_norm, wq1, wq2, wk, wv, place)


def _mla_attn_kernel(q_ref, kt_ref, v_ref, o_ref, *, nlat, T):
    j = pl.program_id(1)
    lane_h = _lane_head(GROUP_W, HEAD_DIM)

    def attend(lo, hi):
        acc = jnp.zeros((TQ, GROUP_W), F32)
        v = v_ref[0, lo:hi, :]

        def logits(h):
            q = q_ref[0, :, h * MLA_HEAD_PAD:(h + 1) * MLA_HEAD_PAD]
            return _dot(q, kt_ref[0, h * MLA_HEAD_PAD:(h + 1) * MLA_HEAD_PAD, lo:hi])

        ahead = 2
        pending = [logits(h) for h in range(ahead)]
        for h in range(N_HEADS):
            s = pending.pop(0)
            if h + ahead < N_HEADS:
                pending.append(logits(h + ahead))
            p = jnp.exp(s - jnp.max(s, axis=-1, keepdims=True))
            den = jnp.sum(p, axis=-1, keepdims=True)
            acc = acc + jnp.where(lane_h == h, _dot(p.astype(BF16), v) * (1.0 / den), 0.0)
        o_ref[0] = acc.astype(o_ref.dtype)

    @pl.when(j < nlat)
    def _():
        attend(0, T + TQ)

    @pl.when(j >= nlat)
    def _():
        attend(T, T + TQ)


def _mla_attn(q, kt, v, T):
    B, S, HP = q.shape
    nlat = T // TQ
    return pl.pallas_call(
        functools.partial(_mla_attn_kernel, nlat=nlat, T=T),
        grid=(B, S // TQ),
        in_specs=[pl.BlockSpec((1, TQ, HP), lambda b, j: (b, j, 0)),
                  pl.BlockSpec((1, HP, S), lambda b, j: (b, 0, 0)),
                  pl.BlockSpec((1, S, GROUP_W), lambda b, j: (b, 0, 0))],
        out_specs=pl.BlockSpec((1, TQ, GROUP_W), lambda b, j: (b, j, 0)),
        out_shape=jax.ShapeDtypeStruct((B, S, GROUP_W), BF16),
        compiler_params=_cparams(2),
    )(q, kt, v)


def _scan_masks():
    t = np.arange(CHUNK)
    inc = np.stack([t[:, None] >= t[None, :], t[:, None] <= t[None, :]]).astype(np.float32)
    strict = np.stack([t[:, None] > t[None, :], t[:, None] < t[None, :]]).astype(np.float32)
    return jnp.asarray(np.tile(inc, (1, 1, N_HEADS))), jnp.asarray(np.tile(strict, (1, 1, N_HEADS)))


def _cumsum_rows(x, reverse):
    n = x.shape[0]
    row = lax.broadcasted_iota(jnp.int32, (n, 1), 0)
    sh = 1
    while sh < n:
        if reverse:
            x = x + jnp.where(row < n - sh, pltpu.roll(x, n - sh, 0), 0.0)
        else:
            x = x + jnp.where(row >= sh, pltpu.roll(x, sh, 0), 0.0)
        sh *= 2
    return x


def _block_diag(x, bm_b):
    return _tile_rows(x.astype(BF16)) * bm_b


def _dot_nt(a, b):
    return lax.dot_general(a, b, (((1,), (1,)), ((), ())), preferred_element_type=F32)


def _fwd_tile(p, nlat):
    return jnp.where(p == 0, nlat, p - 1)


def _bwd_tile(p, nlat):
    return jnp.where(p == 0, nlat, nlat - p)


def _chunk_order(d):
    nch = TQ // CHUNK
    return range(nch) if d == 0 else range(nch - 1, -1, -1)


def _stack_heads(x, lane_h):
    return jnp.concatenate([jnp.where(lane_h == h, x, 0.0) for h in range(N_HEADS)], axis=0)


def _tile_rows(x):
    return jnp.concatenate([x] * N_HEADS, axis=0)


def _collapse_heads(x):
    c = x.shape[0] // N_HEADS
    return (x[0:c] + x[c:2 * c]) + (x[2 * c:3 * c] + x[3 * c:4 * c])


def _rw_prep_kernel(zp_ref, z_ref, zn_ref, mu_ref, kk_ref, ka_ref, rk_ref, w0_ref, a0_ref, wup_ref,
                    aup_ref, gup_ref, ones_ref,
                    r_ref, v_ref, ah_ref, g_ref, bon_ref, lw_ref, kd_ref, bd_ref, *, nlat, tc):
    i = pl.program_id(1)
    prev_ok, next_ok = _neighbour_ok(i, nlat)
    z = z_ref[0]
    row = lax.broadcasted_iota(jnp.int32, (TMT, 1), 0)
    before = jnp.where(prev_ok, zp_ref[0, HALO - 1:HALO, :], 0.0)
    after = jnp.where(next_ok, zn_ref[0, 0:1, :], 0.0)
    zprev = jnp.where(row == 0, before, pltpu.roll(z, 1, 0))
    znext = jnp.where(row == _rows_in_tile(i, nlat, tc) - 1, after, pltpu.roll(z, TMT - 1, 0))
    zs = z + mu_ref[...] * (0.5 * (zprev + znext) - z)
    r = zs[:, 0:GROUP_W]
    k = zs[:, GROUP_W:2 * GROUP_W]
    v = zs[:, 2 * GROUP_W:3 * GROUP_W]
    low = zs[:, 3 * GROUP_W:3 * GROUP_W + 128]
    gd = zs[:, 3 * GROUP_W + 128:]
    ones = ones_ref[...]
    kk = k * kk_ref[...]
    kk = kk * lax.rsqrt(_dot_sel_rhs(kk * kk, ones) + 1e-12)
    wl = _dot(jnp.tanh(low).astype(BF16), wup_ref[...])
    al = _dot(low.astype(BF16), aup_ref[...])
    ksum = jnp.zeros((TMT, GROUP_W), F32)
    for d in range(2):
        w_raw = -_softplus(-(w0_ref[d:d + 1, :] + wl[:, d * GROUP_W:(d + 1) * GROUP_W])) - 0.5
        lw_ref[d, 0] = -jnp.exp(w_raw)
        a = _sigmoid(a0_ref[d:d + 1, :] + al[:, d * GROUP_W:(d + 1) * GROUP_W])
        kd = k * (1.0 + (a - 1.0) * ka_ref[...])
        kd_ref[d, 0] = kd
        bd_ref[d, 0] = kk * a
        ksum = ksum + kd
    r_ref[0] = r
    v_ref[0] = v
    ah_ref[0] = -kk
    g_ref[0] = _dot(_sigmoid(gd).astype(BF16), gup_ref[...])
    bon_ref[0] = _dot_sel_rhs(r * ksum * rk_ref[...], ones) * v


def _rw_prep(z, mu, k_k, k_a, r_k, w0, a0, wup_p, aup_p, gup_b, ones_b, T, layer):
    B, S, _ = z.shape
    ntiles, nlat = _tok_tiles(S, T)
    W = 4 * GROUP_W
    cb = COL_RW // W
    prev, nxt = _halo_specs(W, cb, S)
    one = _tok_spec(GROUP_W)
    two = pl.BlockSpec((2, 1, TMT, GROUP_W), lambda b, i: (0, b, i, 0))
    s1 = jax.ShapeDtypeStruct((B, S, GROUP_W), F32)
    s2 = jax.ShapeDtypeStruct((2, B, S, GROUP_W), F32)
    vec = _const_spec((1, GROUP_W), layer)
    return pl.pallas_call(
        functools.partial(_rw_prep_kernel, nlat=nlat, tc=S - T),
        grid=(B, ntiles),
        in_specs=[prev, _tok_spec(W, cb), nxt,
                  _const_spec((1, W), layer), vec, vec, vec,
                  _const_spec((2, GROUP_W), layer), _const_spec((2, GROUP_W), layer),
                  _const_spec((128, 2 * GROUP_W), layer), _const_spec((128, 2 * GROUP_W), layer),
                  _const_spec((128, GROUP_W), layer), _const_spec((GROUP_W, GROUP_W))],
        out_specs=[one, one, one, one, one, two, two, two],
        out_shape=[s1, s1, s1, s1, s1, s2, s2, s2],
        compiler_params=_cparams(2),
    )(z, z, z, mu, k_k, k_a, r_k, w0, a0, wup_p, aup_p, gup_b, ones_b)


def _rw_scan_kernel(rf_ref, rb_ref, vf_ref, vb_ref, af_ref, ab_ref, lwf_ref, lwb_ref, kdf_ref, kdb_ref,
                    bdf_ref, bdb_ref, inc_ref, strict_ref, bm_ref, yf_ref, yb_ref, s_ref):
    p = pl.program_id(1)

    @pl.when(p == 0)
    def _():
        s_ref[...] = jnp.zeros_like(s_ref)

    n = N_HEADS * CHUNK
    bm = bm_ref[...]
    eye = (lax.broadcasted_iota(jnp.int32, (CHUNK, n), 1) % CHUNK
           == lax.broadcasted_iota(jnp.int32, (CHUNK, n), 0)).astype(F32)
    refs = ((rf_ref, vf_ref, af_ref, lwf_ref, kdf_ref, bdf_ref, yf_ref),
            (rb_ref, vb_ref, ab_ref, lwb_ref, kdb_ref, bdb_ref, yb_ref))
    orders = [list(_chunk_order(d)) for d in range(2)]
    inst = [(step, d) for step in range(TQ // CHUNK) for d in range(2)]
    idx = range(len(inst))
    rows = [pl.ds(orders[d][step] * CHUNK, CHUNK) for step, d in inst]
    bm_b = bm.astype(BF16)
    bd_of = lambda m: _block_diag(m, bm_b)

    ar, r_t, v_s, a_s, b_s, k_s, bk_t, decay, v_in = [], [], [], [], [], [], [], [], []
    for (step, d), sl in zip(inst, rows):
        r_ref, v_ref, a_ref, lw_ref, kd_ref, bd_ref, _ = refs[d]
        lw = lw_ref[0, 0, sl, :]
        kd = kd_ref[0, 0, sl, :]
        bd = bd_ref[0, 0, sl, :]
        v = v_ref[0, sl, :]
        cs = _cumsum_rows(lw, reverse=(d == 1))
        tot = cs[CHUNK - 1:CHUNK, :] if d == 0 else cs[0:1, :]
        e_neg = jnp.exp(-cs)
        e_hat = jnp.exp(tot - cs)
        a_t = a_ref[0, sl, :] * jnp.exp(cs - lw)
        rt = r_ref[0, sl, :] * jnp.exp(cs)
        ar.append(jnp.concatenate([a_t, rt], axis=0).astype(BF16))
        r_t.append(rt)
        v_in.append(v)
        v_s.append(bd_of(v))
        a_s.append(bd_of(a_t))
        b_s.append(bd_of(bd * e_neg))
        k_s.append(bd_of(kd * e_neg))
        t = jnp.concatenate([bd * e_hat, kd * e_hat, jnp.broadcast_to(jnp.exp(tot), (2 * CHUNK, GROUP_W))], axis=0).T
        bk_t.append(t[:, :2 * CHUNK].astype(BF16))
        decay.append(jnp.concatenate([t[:, 2 * CHUNK:]] * 2, axis=1))
    g_b = [_dot_nt(ar[i], b_s[i]) for i in idx]
    g_k = [_dot_nt(ar[i], k_s[i]) for i in idx]
    pw = [g_b[i][:CHUNK] * strict_ref[inst[i][1]] for i in idx]
    ak = [(g_k[i][:CHUNK] * strict_ref[inst[i][1]]).astype(BF16) for i in idx]
    rbk = [jnp.concatenate([g_b[i][CHUNK:] * inc_ref[inst[i][1]], g_k[i][CHUNK:] * inc_ref[inst[i][1]]],
                           axis=1).astype(BF16) for i in idx]
    x = [eye + pw[i] for i in idx]
    pw = [_dot(pw[i].astype(BF16), bd_of(pw[i])) for i in idx]
    for _ in range(4):
        px = [_dot(jnp.concatenate([pw[i], x[i]], axis=0).astype(BF16), bd_of(pw[i])) for i in idx]
        x = [x[i] + px[i][CHUNK:] for i in idx]
        pw = [px[i][:CHUNK] for i in idx]
    x = [(x[i] + _dot(x[i].astype(BF16), bd_of(pw[i]))).astype(BF16) for i in idx]
    akv = [_dot(ak[i], v_s[i]) for i in idx]
    p12 = [_dot(x[i], jnp.concatenate([bd_of(akv[i]), a_s[i]], axis=1)) for i in idx]
    p1_s = [bd_of(p12[i][:, :GROUP_W]) for i in idx]
    p2_s = [bd_of(p12[i][:, GROUP_W:]) for i in idx]
    q = [r_t[i] + _dot(rbk[i][:, :n], p2_s[i]) for i in idx]
    y_c = [_dot(rbk[i], jnp.concatenate([p1_s[i], v_s[i]], axis=0)) for i in idx]
    zero = jnp.zeros((CHUNK, GROUP_W), F32)
    gc = [_dot(bk_t[i], jnp.concatenate(
        [jnp.concatenate([p12[i][:, GROUP_W:], p12[i][:, :GROUP_W]], axis=1),
         jnp.concatenate([zero, v_in[i]], axis=1)], axis=0).astype(BF16)) for i in idx]
    gq = [jnp.concatenate([gc[i][:, :GROUP_W] * bm, q[i]], axis=0).astype(BF16) for i in idx]
    c_s = [gc[i][:, GROUP_W:] * bm for i in idx]

    s = [s_ref[0], s_ref[1]]
    for i, ((step, d), sl) in enumerate(zip(inst, rows)):
        m = _dot(gq[i], s[d].astype(BF16))
        refs[d][6][0, sl, :] = m[n:] + y_c[i]
        s[d] = decay[i] * s[d] + m[:n] + c_s[i]
    s_ref[0] = s[0]
    s_ref[1] = s[1]


def _rw_scan(r, v, ah, lw, kd, bd, masks, bm, T):
    B, S, _ = r.shape
    nlat = T // TQ
    inc, strict = masks
    n = N_HEADS * CHUNK
    fwd = pl.BlockSpec((1, TQ, GROUP_W), lambda b, p: (b, _fwd_tile(p, nlat), 0))
    bwd = pl.BlockSpec((1, TQ, GROUP_W), lambda b, p: (b, _bwd_tile(p, nlat), 0))
    fwd2 = pl.BlockSpec((1, 1, TQ, GROUP_W), lambda b, p: (0, b, _fwd_tile(p, nlat), 0))
    bwd2 = pl.BlockSpec((1, 1, TQ, GROUP_W), lambda b, p: (1, b, _bwd_tile(p, nlat), 0))
    out = jax.ShapeDtypeStruct((B, S, GROUP_W), F32)
    return pl.pallas_call(
        _rw_scan_kernel,
        grid=(B, nlat + 1),
        in_specs=[fwd, bwd, fwd, bwd, fwd, bwd, fwd2, bwd2, fwd2, bwd2, fwd2, bwd2,
                  _const_spec((2, CHUNK, n)), _const_spec((2, CHUNK, n)), _const_spec((n, n))],
        out_specs=[fwd, bwd],
        out_shape=[out, out],
        scratch_shapes=[pltpu.VMEM((2, n, GROUP_W), F32)],
        compiler_params=_cparams(2),
    )(r, r, v, v, ah, ah, lw, lw, kd, kd, bd, bd, inc, strict, bm)


def _gla_scan_kernel(qf_ref, qb_ref, kf_ref, kb_ref, gf_ref, gb_ref, vf_ref, vb_ref, gup_ref, gbias_ref,
                     inc_ref, bm_ref, bmv_ref, of_ref, ob_ref, s_ref):
    p = pl.program_id(1)

    @pl.when(p == 0)
    def _():
        s_ref[...] = jnp.zeros_like(s_ref)

    wk = N_HEADS * GLA_DK
    n = N_HEADS * CHUNK
    bm = bm_ref[...]
    bmv = bmv_ref[...].astype(BF16)
    bmk = (lax.broadcasted_iota(jnp.int32, (n, wk), 0) // CHUNK
           == lax.broadcasted_iota(jnp.int32, (n, wk), 1) // GLA_DK).astype(F32).astype(BF16)
    refs = ((qf_ref, kf_ref, gf_ref, vf_ref, of_ref), (qb_ref, kb_ref, gb_ref, vb_ref, ob_ref))
    orders = [list(_chunk_order(d)) for d in range(2)]
    inst = [(step, d) for step in range(TQ // CHUNK) for d in range(2)]
    idx = range(len(inst))
    rows = [pl.ds(orders[d][step] * CHUNK, CHUNK) for step, d in inst]

    qe, ke_s, ks4, v_s, decay = [], [], [], [], []
    for (step, d), sl in zip(inst, rows):
        q_ref, k_ref, g_ref, v_ref, _ = refs[d]
        k = k_ref[0, sl, :]
        la = -_softplus(-(_dot(g_ref[0, sl, :].astype(BF16), gup_ref[d]) + gbias_ref[d])) * (1.0 / GLA_TAU)
        b = _cumsum_rows(la, reverse=(d == 1))
        tot = b[CHUNK - 1:CHUNK, :] if d == 0 else b[0:1, :]
        qe.append((q_ref[0, sl, :] * (GLA_DK ** -0.5) * jnp.exp(b)).astype(BF16))
        ke_s.append(_block_diag(k * jnp.exp(-b), bmk))
        ks4.append(_tile_rows(k * jnp.exp(tot - b)).T.astype(BF16))
        v_s.append(_block_diag(v_ref[0, sl, :], bmv))
        decay.append(_tile_rows(jnp.broadcast_to(jnp.exp(tot), (CHUNK, wk))).T)
    a_cat = [(_dot_nt(qe[i], ke_s[i]) * inc_ref[inst[i][1]]).astype(BF16) for i in idx]
    o_in = [_dot(a_cat[i], v_s[i]) for i in idx]
    kv = [_dot(ks4[i], v_s[i]) * bm for i in idx]

    s = [s_ref[0], s_ref[1]]
    for i, ((step, d), sl) in enumerate(zip(inst, rows)):
        refs[d][4][0, sl, :] = o_in[i] + _dot(qe[i], s[d].astype(BF16))
        s[d] = decay[i] * s[d] + kv[i]
    s_ref[0] = s[0]
    s_ref[1] = s[1]


def _gla_scan(z, gup_p, gb, masks, bm, bmv, T, layer):
    B, S, _ = z.shape
    nlat = T // TQ
    inc, _ = masks
    n = N_HEADS * CHUNK
    wk = N_HEADS * GLA_DK
    fwd = lambda w, col: pl.BlockSpec((1, TQ, w), lambda b, p: (b, _fwd_tile(p, nlat), col // w))
    bwd = lambda w, col: pl.BlockSpec((1, TQ, w), lambda b, p: (b, _bwd_tile(p, nlat), col // w))
    out = jax.ShapeDtypeStruct((B, S, GROUP_W), F32)
    return pl.pallas_call(
        _gla_scan_kernel,
        grid=(B, nlat + 1),
        in_specs=[fwd(wk, COL_GLA_Q), bwd(wk, COL_GLA_Q), fwd(wk, COL_GLA_K), bwd(wk, COL_GLA_K),
                  fwd(wk, COL_GLA_G), bwd(wk, COL_GLA_G), fwd(GROUP_W, COL_GLA_V), bwd(GROUP_W, COL_GLA_V),
                  _const_spec((2, wk, wk), layer), _const_spec((2, 1, wk), layer),
                  _const_spec((2, CHUNK, n)), _const_spec((wk, GROUP_W)), _const_spec((n, n))],
        out_specs=[fwd(GROUP_W, 0), bwd(GROUP_W, 0)],
        out_shape=[out, out],
        scratch_shapes=[pltpu.VMEM((2, wk, GROUP_W), F32)],
        compiler_params=_cparams(2),
    )(z, z, z, z, z, z, z, z, gup_p, gb, inc, bm, bmv)


def _pack_w_in(w_in):
    L, D, _ = w_in.shape
    na, mla, rw, gla = jnp.split(w_in, [768, 1120, 2144], axis=-1)
    cq, ckv, kr = jnp.split(mla, [MLA_Q_RANK, MLA_Q_RANK + MLA_KV_RANK], axis=-1)
    gq, gk, gv, gg, go = jnp.split(gla, [128, 256, 512, 528], axis=-1)
    pad = jnp.zeros((L, D, COL_GLA_V - COL_GLA_G - gg.shape[-1]), F32)
    packed = jnp.concatenate([na, ckv, cq, kr, kr[..., _rope_swap_perm()], gq, gk, gg, pad, gv, go, rw], axis=-1)
    assert packed.shape[-1] == Z_COLS
    return packed.astype(BF16)


def kernel(x, c, ctx, c_ctx, w_mod, b_mod, g_mix_pre, g_mix_post, g_ffn_pre, g_ffn_post, w_in, w_out, na_rpb, mla_q_norm, mla_w_uq, mla_kv_norm, mla_w_ukv, rw_mu, rw_w0, rw_w_up, rw_a0, rw_a_up, rw_g_up, rw_k_k, rw_k_a, rw_r_k, rw_ln_w, rw_ln_b, gla_gate_up, gla_gate_b, gla_norm, ffn_w_up, ffn_conv_w, ffn_conv_b, ffn_w_down):
    B, T, D = x.shape
    Tc = ctx.shape[1]
    L = w_in.shape[0]
    assert D == D_MODEL and Tc == TQ and T % TMT == 0 and B + 1 <= 8

    w_in_p = _pack_w_in(w_in)
    w_out_b = w_out.astype(BF16)
    wq1, wq2, wk, wv = _mla_weights(mla_w_uq, mla_w_ukv)
    place = _rope_place()
    cos, sin = _rope_tables(T, Tc)
    zero_lo = jnp.zeros((L, 64, 2 * GROUP_W), F32)
    rw_wup_p = jnp.concatenate([jnp.concatenate([rw_w_up[:, 0], rw_w_up[:, 1]], axis=-1), zero_lo], axis=1).astype(BF16)
    rw_aup_p = jnp.concatenate([zero_lo, jnp.concatenate([rw_a_up[:, 0], rw_a_up[:, 1]], axis=-1)], axis=1).astype(BF16)
    rw_gup_b = rw_g_up.astype(BF16)
    wk_gla = N_HEADS * GLA_DK
    gla_gup_p = jnp.concatenate([gla_gate_up, jnp.zeros((L, 2, wk_gla - gla_gate_up.shape[2], wk_gla), F32)], axis=2).astype(BF16)
    ffn_up_b = ffn_w_up.astype(BF16)
    ffn_dn_b = ffn_w_down.astype(BF16)
    ones_b = jnp.asarray(_block_ones(GROUP_W, HEAD_DIM), BF16)
    rw_bm = jnp.asarray(_block_ones(N_HEADS * CHUNK, CHUNK))
    gla_bm = jnp.asarray((np.arange(wk_gla)[:, None] // GLA_DK == np.arange(GROUP_W)[None, :] // HEAD_DIM).astype(np.float32))
    masks = _scan_masks()

    cvecs = jnp.zeros((8, D), F32).at[:B].set(c).at[B].set(c_ctx)
    mods = _modulation(cvecs, w_mod, b_mod).reshape(L, 8, 6, D)
    mods = jnp.pad(mods, ((0, 0), (0, 0), (0, 2), (0, 0)))
    modtabs = jnp.stack([jnp.broadcast_to(mods[:, B:B + 1], (L, B, 8, D)), mods[:, :B]], axis=2)

    xs = jnp.concatenate([x, ctx], axis=1)
    rows = lambda a: a.reshape(L, 1, -1)
    na_bias = _natten_bias(na_rpb, T // GRID_W)
    gla_gb = gla_gate_b[:, :, None, :]
    for i in range(L):
        z = _in_proj(xs, modtabs, rows(g_mix_pre), w_in_p, T, i)
        y_na = _natten(z, na_bias, T, i)
        q, kt, v = _mla_up(z, cos, sin, rows(mla_q_norm), rows(mla_kv_norm), wq1, wq2, wk, wv, place, T, i)
        y_mla = _mla_attn(q, kt, v, T)
        r, vv, ah, g, bon, lw, kd, bd = _rw_prep(z, rows(rw_mu), rows(rw_k_k), rows(rw_k_a), rows(rw_r_k),
                                                 rw_w0, rw_a0, rw_wup_p, rw_aup_p, rw_gup_b, ones_b, T, i)
        yf, yb = _rw_scan(r, vv, ah, lw, kd, bd, masks, rw_bm, T)
        of, ob = _gla_scan(z, gla_gup_p, gla_gb, masks, gla_bm, rw_bm, T, i)
        xs = _out_proj(xs, z, y_na, y_mla, yf, yb, bon, g, of, ob, modtabs, rows(g_mix_post), w_out_b,
                       rows(rw_ln_w), rows(rw_ln_b), rows(gla_norm), ones_b, T, i)
        xs = _ffn(xs, modtabs, rows(g_ffn_pre), rows(g_ffn_post), ffn_up_b, ffn_conv_w,
                  rows(ffn_conv_b), ffn_dn_b, T, i, latent_only=(i == L - 1))
    return xs
```

```python
import functools

import numpy as np
import jax
import jax.numpy as jnp
from jax import lax
from jax.experimental import pallas as pl
from jax.experimental.pallas import tpu as pltpu

F32 = jnp.float32
BF16 = jnp.bfloat16

D_MODEL = 1024
GRID_W = 64
EPS = 1e-6
LOG2E = 1.4426950408889634
N_HEADS = 4
HEAD_DIM = 64
GROUP_W = 256
NA_ROWS = 8
NA_COLS = 16
NA_UNION = 12
MLA_Q_RANK = 192
MLA_KV_RANK = 128
MLA_NOPE = 64
MLA_ROPE = 32
MLA_HEAD_PAD = 128
ROPE_THETA = 10000.0
RW_GN_EPS = 64e-5
GLA_DK = 32
GLA_TAU = 16.0
D_FF = 2816
CHUNK = 64
TQ = 256
RW_SCAN_BATCH = 2
GLA_SCAN_BATCH = 4
TMT = 512
HALO = 8
FF_CHUNK = 256
FF_GROUP = 4
Z_COLS = 3072

COL_NA = 0
COL_MLA = 768
COL_GLA_Q = 1152
COL_GLA_K = 1280
COL_GLA_G = 1408
COL_GLA_V = 1536
COL_GLA_O = 1792
COL_RW = 2048

VMEM_LIMIT_V7X = 56 * 1024 * 1024


def _cparams(n_axes):
    return pltpu.CompilerParams(dimension_semantics=("arbitrary",) * n_axes,
                                vmem_limit_bytes=VMEM_LIMIT_V7X)


def _const_spec(shape, layer=None):
    nd = len(shape)
    if layer is None:
        return pl.BlockSpec(shape, lambda *_: (0,) * nd, pipeline_mode=pl.Buffered(1))
    return pl.BlockSpec((None,) + tuple(shape), lambda *_: (layer,) + (0,) * nd, pipeline_mode=pl.Buffered(1))


def _dot(a, b):
    return jnp.dot(a, b, preferred_element_type=F32)


def _split3(x):
    hi = x.astype(BF16)
    r1 = x - hi.astype(F32)
    mid = r1.astype(BF16)
    lo = (r1 - mid.astype(F32)).astype(BF16)
    return hi, mid, lo


def _dot_sel_lhs(m, x):
    hi, mid, lo = _split3(x)
    return _dot(m, hi) + (_dot(m, mid) + _dot(m, lo))


def _dot_sel_rhs(x, m):
    hi, mid, lo = _split3(x)
    return _dot(hi, m) + (_dot(mid, m) + _dot(lo, m))


def _bdot(a, b):
    return _dot(a.astype(BF16), b.astype(BF16))


def _sigmoid(x):
    return 1.0 / (1.0 + jnp.exp(-x))


def _silu_gain(x):
    return 0.5 + 0.5 * jnp.tanh(0.5 * x)


def _softplus(x):
    return jnp.maximum(x, 0.0) + jnp.log1p(jnp.exp(-jnp.abs(x)))


def _rms(x):
    return x * lax.rsqrt(jnp.mean(x * x, axis=-1, keepdims=True) + EPS)


def _norm_mod(x, g, shift, scale):
    return (_rms(x) * g) * (1.0 + scale) + shift


def _lane_head(width, per_head):
    return lax.broadcasted_iota(jnp.int32, (1, width), 1) // per_head


def _block_ones(n, blk):
    i = np.arange(n) // blk
    return (i[:, None] == i[None, :]).astype(np.float32)


def _mod_kernel(c_ref, w_ref, b_ref, o_ref):
    cv = c_ref[...]
    s = cv * _sigmoid(cv)
    o_ref[0] = _dot(s.astype(BF16), w_ref[0].astype(BF16)) + b_ref[0]


def _modulation(cvecs, w_mod, b_mod):
    L, D, N = w_mod.shape
    tn = 1536
    return pl.pallas_call(
        _mod_kernel,
        grid=(L, N // tn),
        in_specs=[pl.BlockSpec((8, D), lambda l, n: (0, 0)),
                  pl.BlockSpec((1, D, tn), lambda l, n: (l, 0, n)),
                  pl.BlockSpec((1, 1, tn), lambda l, n: (l, 0, n))],
        out_specs=pl.BlockSpec((1, 8, tn), lambda l, n: (l, 0, n)),
        out_shape=jax.ShapeDtypeStruct((L, 8, N), F32),
        compiler_params=_cparams(2),
    )(cvecs, w_mod, b_mod.reshape(L, 1, N))


def _tok_tiles(S, T):
    assert T % TMT == 0 and 0 < S - T <= TMT
    return T // TMT + 1, T // TMT


def _tok_spec(width, col_block=0):
    return pl.BlockSpec((1, TMT, width), lambda b, i: (b, i, col_block))


def _mod_spec(nlat, layer):
    return pl.BlockSpec((None, 1, 1, 8, D_MODEL), lambda b, i: (layer, b, jnp.where(i < nlat, 1, 0), 0, 0))


def _halo_specs(width, col_block, S):
    per = TMT // HALO
    last = S // HALO - 1
    prev = pl.BlockSpec((1, HALO, width), lambda b, i: (b, jnp.maximum(i * per - 1, 0), col_block))
    nxt = pl.BlockSpec((1, HALO, width), lambda b, i: (b, jnp.minimum((i + 1) * per, last), col_block))
    return prev, nxt


def _neighbour_ok(i, nlat):
    prev_ok = jnp.logical_and(i != 0, i != nlat)
    next_ok = i < nlat - 1
    return prev_ok, next_ok


def _rows_in_tile(i, nlat, tc):
    return jnp.where(i < nlat, TMT, tc)


def _in_proj_kernel(x_ref, mod_ref, g_ref, w_ref, z_ref):
    m = mod_ref[0, 0]
    h = _norm_mod(x_ref[0], g_ref[...], m[0:1], m[1:2])
    z_ref[0] = _dot(h.astype(BF16), w_ref[...])


def _in_proj(xs, modtab, g_pre, w_in_p, T, layer):
    B, S, D = xs.shape
    ntiles, nlat = _tok_tiles(S, T)
    return pl.pallas_call(
        _in_proj_kernel,
        grid=(B, ntiles),
        in_specs=[_tok_spec(D), _mod_spec(nlat, layer), _const_spec((1, D), layer),
                  _const_spec((D, Z_COLS), layer)],
        out_specs=_tok_spec(Z_COLS),
        out_shape=jax.ShapeDtypeStruct((B, S, Z_COLS), F32),
        compiler_params=_cparams(2),
    )(xs, modtab, g_pre, w_in_p)


def _out_proj_kernel(x_ref, na_ref, mla_ref, yf_ref, yb_ref, bon_ref, g_ref, of_ref, ob_ref, og_ref,
                     mod_ref, gpost_ref, w_ref, lnw_ref, lnb_ref, gn_ref, ones_ref, o_ref):
    m = mod_ref[0, 0]
    ones = ones_ref[...]
    inv_n = 1.0 / HEAD_DIM
    y = yf_ref[0] + yb_ref[0]
    yc = y - _dot_sel_rhs(y, ones) * inv_n
    var = _dot_sel_rhs(yc * yc, ones) * inv_n
    y_rw = (yc * lax.rsqrt(var + RW_GN_EPS) * lnw_ref[...] + lnb_ref[...] + bon_ref[0]) * g_ref[0]
    o = of_ref[0] + ob_ref[0]
    ms = _dot_sel_rhs(o * o, ones) * inv_n
    og = og_ref[0]
    y_gla = (o * lax.rsqrt(ms + EPS) * gn_ref[...]) * (og * _silu_gain(og))
    y = jnp.concatenate([na_ref[0], mla_ref[0], y_rw.astype(BF16), y_gla.astype(BF16)], axis=-1)
    y = _dot(y, w_ref[...])
    o_ref[0] = x_ref[0] + m[2:3] * (_rms(y) * gpost_ref[...])


def _out_proj(xs, z, y_na, y_mla, yf, yb, bon, g, of, ob, modtab, g_post, w_out_b, ln_w, ln_b, gla_norm,
              ones_b, T, layer):
    B, S, D = xs.shape
    ntiles, nlat = _tok_tiles(S, T)
    grp = _tok_spec(GROUP_W)
    vec = _const_spec((1, GROUP_W), layer)
    return pl.pallas_call(
        _out_proj_kernel,
        grid=(B, ntiles),
        in_specs=[_tok_spec(D), grp, grp, grp, grp, grp, grp, grp, grp,
                  _tok_spec(GROUP_W, COL_GLA_O // GROUP_W),
                  _mod_spec(nlat, layer), _const_spec((1, D), layer), _const_spec((D, D), layer), vec, vec, vec,
                  _const_spec((GROUP_W, GROUP_W))],
        out_specs=_tok_spec(D),
        out_shape=jax.ShapeDtypeStruct((B, S, D), F32),
        compiler_params=_cparams(2),
    )(xs, y_na, y_mla, yf, yb, bon, g, of, ob, z, modtab, g_post, w_out_b, ln_w, ln_b, gla_norm, ones_b)


def _ffn_kernel(xp_ref, x_ref, xn_ref, mod_ref, gpre_ref, gpost_ref, wup_ref, cw_ref, cb_ref,
                wdn_ref, o_ref, *, nlat, tc):
    i = pl.program_id(1)
    prev_ok, next_ok = _neighbour_ok(i, nlat)
    m = mod_ref[0, 0]
    x = x_ref[0]
    xe = jnp.concatenate([xp_ref[0], x, xn_ref[0]], axis=0)
    n = TMT + 2 * HALO
    h = _norm_mod(xe, gpre_ref[...], m[3:4], m[4:5])
    row = lax.broadcasted_iota(jnp.int32, (n, 1), 0)
    valid = jnp.logical_or(jnp.logical_and(row >= HALO, row < HALO + _rows_in_tile(i, nlat, tc)),
                           jnp.logical_or(jnp.logical_and(row < HALO, prev_ok),
                                          jnp.logical_and(row >= HALO + TMT, next_ok)))
    hb = jnp.where(valid, h, 0.0).astype(BF16)
    nchunks = D_FF // FF_CHUNK

    def up(c):
        return [_dot(hb, wup_ref[:, base + c * FF_CHUNK:base + (c + 1) * FF_CHUNK]) for base in (0, D_FF)]

    def conv(z, lo):
        cw = cw_ref[:, lo:lo + FF_CHUNK]
        return (cb_ref[:, lo:lo + FF_CHUNK]
                + pltpu.roll(z, 1, 0)[HALO:HALO + TMT] * cw[0:1]
                + z[HALO:HALO + TMT] * cw[1:2]
                + pltpu.roll(z, n - 1, 0)[HALO:HALO + TMT] * cw[2:3])

    acc = None
    group = []
    z_next = up(0)
    for c in range(nchunks):
        z_val, z_gate = z_next
        if c + 1 < nchunks:
            z_next = up(c + 1)
        val = conv(z_val, c * FF_CHUNK)
        gate = conv(z_gate, D_FF + c * FF_CHUNK)
        group.append(((gate * _silu_gain(gate)) * val).astype(BF16))
        if len(group) == FF_GROUP or c + 1 == nchunks:
            lo = (c + 1 - len(group)) * FF_CHUNK
            part = _dot(jnp.concatenate(group, axis=1), wdn_ref[lo:(c + 1) * FF_CHUNK, :])
            acc = part if acc is None else acc + part
            group = []
    o_ref[0] = x + m[5:6] * (_rms(acc) * gpost_ref[...])


def _ffn(xs, modtab, g_pre, g_post, w_up_b, conv_w, conv_b, w_dn_b, T, layer, latent_only):
    B, S, D = xs.shape
    ntiles, nlat = _tok_tiles(S, T)
    prev, nxt = _halo_specs(D, 0, S)
    return pl.pallas_call(
        functools.partial(_ffn_kernel, nlat=nlat, tc=S - T),
        grid=(B, nlat if latent_only else ntiles),
        in_specs=[prev, _tok_spec(D), nxt, _mod_spec(nlat, layer), _const_spec((1, D), layer),
                  _const_spec((1, D), layer), _const_spec((D, 2 * D_FF), layer),
                  _const_spec((3, 2 * D_FF), layer), _const_spec((1, 2 * D_FF), layer),
                  _const_spec((D_FF, D), layer)],
        out_specs=_tok_spec(D),
        out_shape=jax.ShapeDtypeStruct((B, T if latent_only else S, D), F32),
        compiler_params=_cparams(2),
    )(xs, xs, xs, modtab, g_pre, g_post, w_up_b, conv_w, conv_b, w_dn_b)


def _natten_bias(rpb, rows):
    rt = TQ // GRID_W
    j = np.arange(GRID_W)
    col_start = np.clip(j - NA_COLS // 2, 0, GRID_W - NA_COLS)
    col_in = (j[None, :] >= col_start[:, None]) & (j[None, :] < col_start[:, None] + NA_COLS)
    edge = GRID_W - NA_COLS
    ext = jnp.concatenate([jnp.repeat(rpb[..., :1], edge, axis=-1), rpb,
                           jnp.repeat(rpb[..., -1:], edge, axis=-1)], axis=-1).astype(F32) * LOG2E
    bq = jnp.stack([ext[..., GRID_W - 1 - q:2 * GRID_W - 1 - q] for q in range(GRID_W)], axis=2)
    bq = jnp.where(col_in[:, None, :], bq, -jnp.inf)
    L = rpb.shape[0]
    cases = []
    for r0 in (0, rt, rows - rt):
        us = min(max(r0 - NA_ROWS // 2, 0), rows - NA_UNION)
        per_row = []
        for r in range(r0, r0 + rt):
            rs = min(max(r - NA_ROWS // 2, 0), rows - NA_ROWS)
            first = rs - r + (NA_ROWS - 1)
            pre, post = rs - us, us + NA_UNION - (rs + NA_ROWS)
            b = jnp.pad(bq[:, :, :, first:first + NA_ROWS], ((0, 0), (0, 0), (0, 0), (pre, post), (0, 0)),
                        constant_values=-jnp.inf)
            per_row.append(b.reshape(L, N_HEADS, GRID_W, NA_UNION * GRID_W))
        cases.append(jnp.concatenate(per_row, axis=2))
    return jnp.stack(cases, axis=1)


def _natten_kernel(q_ref, k_ref, v_ref, qc_ref, kc_ref, vc_ref, bias_ref, o_ref, *, nlat, rows):
    j = pl.program_id(1)
    lane_h = _lane_head(GROUP_W, HEAD_DIM)
    kct = kc_ref[0].T.astype(BF16)
    vc = vc_ref[0].astype(BF16)
    scale = HEAD_DIM ** -0.5 * LOG2E
    nwin = NA_UNION * GRID_W

    @pl.when(j < nlat)
    def _():
        us = jnp.clip(j * (TQ // GRID_W) - NA_ROWS // 2, 0, rows - NA_UNION)
        start = pl.multiple_of(us * GRID_W, GRID_W)
        q = q_ref[0] * scale
        kwt = k_ref[0, pl.ds(start, nwin), :].T.astype(BF16)
        vw = v_ref[0, pl.ds(start, nwin), :].astype(BF16)
        acc = jnp.zeros((TQ, GROUP_W), F32)

        def logits(h):
            qh = jnp.where(lane_h == h, q, 0.0).astype(BF16)
            return _dot(qh, kwt), _dot(qh, kct)

        s_next = logits(0)
        for h in range(N_HEADS):
            hm = lane_h == h
            s_w, s_c = s_next
            if h + 1 < N_HEADS:
                s_next = logits(h + 1)
            s_w = s_w + bias_ref[0, h]
            mx = jnp.maximum(jnp.max(s_w, axis=-1, keepdims=True), jnp.max(s_c, axis=-1, keepdims=True))
            p_w = jnp.exp2(s_w - mx)
            p_c = jnp.exp2(s_c - mx)
            den = jnp.sum(p_w, axis=-1, keepdims=True) + jnp.sum(p_c, axis=-1, keepdims=True)
            o = _dot(p_w.astype(BF16), vw) + _dot(p_c.astype(BF16), vc)
            acc = acc + jnp.where(hm, o * (1.0 / den), 0.0)
        o_ref[0] = acc.astype(o_ref.dtype)

    @pl.when(j >= nlat)
    def _():
        q = qc_ref[0] * scale
        acc = jnp.zeros((TQ, GROUP_W), F32)
        for h in range(N_HEADS):
            hm = lane_h == h
            s = _dot(jnp.where(hm, q, 0.0).astype(BF16), kct)
            p = jnp.exp2(s - jnp.max(s, axis=-1, keepdims=True))
            den = jnp.sum(p, axis=-1, keepdims=True)
            acc = acc + jnp.where(hm, _dot(p.astype(BF16), vc) * (1.0 / den), 0.0)
        o_ref[0] = acc.astype(o_ref.dtype)


def _natten(z, bias, T, layer):
    B, S, _ = z.shape
    nlat = T // TQ
    rows = T // GRID_W
    assert rows >= 16 and rows % (TQ // GRID_W) == 0 and S - T == TQ
    cb = COL_NA // GROUP_W
    lat = lambda c: pl.BlockSpec((1, T, GROUP_W), lambda b, j: (b, 0, cb + c))
    ctx = lambda c: pl.BlockSpec((1, TQ, GROUP_W), lambda b, j: (b, nlat, cb + c))
    case = lambda j: jnp.where(j == 0, 0, jnp.where(j >= nlat - 1, 2, 1))
    return pl.pallas_call(
        functools.partial(_natten_kernel, nlat=nlat, rows=rows),
        grid=(B, nlat + 1),
        in_specs=[pl.BlockSpec((1, TQ, GROUP_W), lambda b, j: (b, jnp.minimum(j, nlat - 1), cb)),
                  lat(1), lat(2), ctx(0), ctx(1), ctx(2),
                  pl.BlockSpec((None, 1, N_HEADS, TQ, NA_UNION * GRID_W), lambda b, j: (layer, case(j), 0, 0, 0))],
        out_specs=pl.BlockSpec((1, TQ, GROUP_W), lambda b, j: (b, j, 0)),
        out_shape=jax.ShapeDtypeStruct((B, S, GROUP_W), BF16),
        compiler_params=_cparams(2),
    )(z, z, z, z, z, z, bias)


def _rope_tables(T, Tc):
    t = np.arange(T)
    row = (t // GRID_W).astype(np.float32)
    col = (t % GRID_W).astype(np.float32)
    d = MLA_ROPE // 2
    inv = (np.float32(ROPE_THETA) ** (-np.arange(0, d, 2, dtype=np.float32) / np.float32(d))).astype(np.float32)
    cs, sn = [], []
    for pos in (row, col):
        ang = (pos[:, None] * inv[None, :]).astype(np.float32)
        cs += [np.cos(ang), np.cos(ang)]
        sn += [-np.sin(ang), np.sin(ang)]
    pad = MLA_HEAD_PAD - MLA_NOPE - MLA_ROPE
    f32 = np.float32
    cos = np.concatenate([np.ones((T, MLA_NOPE), f32)] + cs + [np.ones((T, pad), f32)], axis=1)
    sin = np.concatenate([np.zeros((T, MLA_NOPE), f32)] + sn + [np.zeros((T, pad), f32)], axis=1)
    cos = np.concatenate([cos, np.ones((Tc, MLA_HEAD_PAD), f32)], axis=0)
    sin = np.concatenate([sin, np.zeros((Tc, MLA_HEAD_PAD), f32)], axis=0)
    return jnp.asarray(cos, F32), jnp.asarray(sin, F32)


def _rope_swap_perm():
    q = MLA_ROPE // 4
    return np.concatenate([np.arange(q, 2 * q), np.arange(0, q), np.arange(3 * q, 4 * q), np.arange(2 * q, 3 * q)])


def _mla_weights(w_uq, w_ukv):
    L = w_uq.shape[0]
    wq = w_uq.reshape(L, MLA_Q_RANK, N_HEADS, MLA_NOPE + MLA_ROPE)
    pad = MLA_HEAD_PAD - MLA_NOPE - MLA_ROPE
    zq = jnp.zeros((L, MLA_Q_RANK, N_HEADS, pad), F32)
    wq1 = jnp.concatenate([wq, zq], axis=-1).reshape(L, MLA_Q_RANK, N_HEADS * MLA_HEAD_PAD)
    rope_sw = wq[..., MLA_NOPE:][..., _rope_swap_perm()]
    wq2 = jnp.concatenate([jnp.zeros((L, MLA_Q_RANK, N_HEADS, MLA_NOPE), F32), rope_sw, zq], axis=-1)
    wq2 = wq2.reshape(L, MLA_Q_RANK, N_HEADS * MLA_HEAD_PAD)
    wkv = w_ukv.reshape(L, MLA_KV_RANK, N_HEADS, 2 * MLA_NOPE)
    wk = jnp.concatenate([wkv[..., :MLA_NOPE], jnp.zeros((L, MLA_KV_RANK, N_HEADS, MLA_HEAD_PAD - MLA_NOPE), F32)], axis=-1)
    wk = wk.reshape(L, MLA_KV_RANK, N_HEADS * MLA_HEAD_PAD)
    wv = wkv[..., MLA_NOPE:].reshape(L, MLA_KV_RANK, N_HEADS * MLA_NOPE)
    return wq1.astype(BF16), wq2.astype(BF16), wk.astype(BF16), wv.astype(BF16)


def _rope_place():
    e = np.zeros((MLA_ROPE, N_HEADS * MLA_HEAD_PAD), np.float32)
    for h in range(N_HEADS):
        e[np.arange(MLA_ROPE), h * MLA_HEAD_PAD + MLA_NOPE + np.arange(MLA_ROPE)] = 1.0
    return jnp.asarray(e, BF16)


def _mla_up_kernel(z_ref, cos_ref, sin_ref, qn_ref, kvn_ref, wq1_ref, wq2_ref, wk_ref, wv_ref, e_ref,
                   q_ref, kt_ref, v_ref):
    z = z_ref[0]
    ckv = z[:, :MLA_KV_RANK]
    cq = z[:, MLA_KV_RANK:MLA_KV_RANK + MLA_Q_RANK]
    kr = z[:, MLA_KV_RANK + MLA_Q_RANK:MLA_KV_RANK + MLA_Q_RANK + MLA_ROPE]
    krs = z[:, MLA_KV_RANK + MLA_Q_RANK + MLA_ROPE:]
    cos = jnp.concatenate([cos_ref[...]] * N_HEADS, axis=-1)
    sin = jnp.concatenate([sin_ref[...]] * N_HEADS, axis=-1)
    nq = (_rms(cq) * qn_ref[...]).astype(BF16)
    nkv = (_rms(ckv) * kvn_ref[...]).astype(BF16)
    q = _dot(nq, wq1_ref[...]) * cos + _dot(nq, wq2_ref[...]) * sin
    scale = (MLA_NOPE + MLA_ROPE) ** -0.5 * LOG2E
    q_ref[0] = (q * scale).astype(BF16)
    k = _dot(nkv, wk_ref[...]) + _dot_sel_rhs(kr, e_ref[...]) * cos + _dot_sel_rhs(krs, e_ref[...]) * sin
    kt_ref[0] = k.T.astype(BF16)
    v_ref[0] = _dot(nkv, wv_ref[...]).astype(BF16)


def _mla_up(z, cos, sin, q_norm, kv_norm, wq1, wq2, wk, wv, place, T, layer):
    B, S, _ = z.shape
    ntiles, _ = _tok_tiles(S, T)
    HP = N_HEADS * MLA_HEAD_PAD
    zw = MLA_KV_RANK + MLA_Q_RANK + 2 * MLA_ROPE
    tab = pl.BlockSpec((TMT, MLA_HEAD_PAD), lambda b, i: (i, 0))
    return pl.pallas_call(
        _mla_up_kernel,
        grid=(B, ntiles),
        in_specs=[_tok_spec(zw, COL_MLA // zw), tab, tab,
                  _const_spec((1, MLA_Q_RANK), layer), _const_spec((1, MLA_KV_RANK), layer),
                  _const_spec((MLA_Q_RANK, HP), layer), _const_spec((MLA_Q_RANK, HP), layer),
                  _const_spec((MLA_KV_RANK, HP), layer), _const_spec((MLA_KV_RANK, GROUP_W), layer),
                  _const_spec((MLA_ROPE, HP))],
        out_specs=[_tok_spec(HP), pl.BlockSpec((1, HP, TMT), lambda b, i: (b, 0, i)), _tok_spec(GROUP_W)],
        out_shape=[jax.ShapeDtypeStruct((B, S, HP), BF16),
                   jax.ShapeDtypeStruct((B, HP, S), BF16),
                   jax.ShapeDtypeStruct((B, S, GROUP_W), BF16)],
        compiler_params=_cparams(2),
    )(z, cos, sin, q_norm, kv_norm, wq1, wq2, wk, wv, place)


def _mla_attn_kernel(q_ref, kt_ref, v_ref, o_ref, *, nlat, T):
    j = pl.program_id(1)
    lane_h = _lane_head(GROUP_W, HEAD_DIM)

    def attend(lo, hi):
        acc = jnp.zeros((TQ, GROUP_W), F32)
        v = v_ref[0, lo:hi, :]

        def logits(h):
            q = q_ref[0, :, h * MLA_HEAD_PAD:(h + 1) * MLA_HEAD_PAD]
            return _dot(q, kt_ref[0, h * MLA_HEAD_PAD:(h + 1) * MLA_HEAD_PAD, lo:hi])

        ahead = 2
        pending = [logits(h) for h in range(ahead)]
        for h in range(N_HEADS):
            s = pending.pop(0)
            if h + ahead < N_HEADS:
                pending.append(logits(h + ahead))
            p = jnp.exp2(s - jnp.max(s, axis=-1, keepdims=True))
            den = jnp.sum(p, axis=-1, keepdims=True)
            acc = acc + jnp.where(lane_h == h, _dot(p.astype(BF16), v) * (1.0 / den), 0.0)
        o_ref[0] = acc.astype(o_ref.dtype)

    @pl.when(j < nlat)
    def _():
        attend(0, T + TQ)

    @pl.when(j >= nlat)
    def _():
        attend(T, T + TQ)


def _mla_attn(q, kt, v, T):
    B, S, HP = q.shape
    nlat = T // TQ
    return pl.pallas_call(
        functools.partial(_mla_attn_kernel, nlat=nlat, T=T),
        grid=(B, S // TQ),
        in_specs=[pl.BlockSpec((1, TQ, HP), lambda b, j: (b, j, 0)),
                  pl.BlockSpec((1, HP, S), lambda b, j: (b, 0, 0)),
                  pl.BlockSpec((1, S, GROUP_W), lambda b, j: (b, 0, 0))],
        out_specs=pl.BlockSpec((1, TQ, GROUP_W), lambda b, j: (b, j, 0)),
        out_shape=jax.ShapeDtypeStruct((B, S, GROUP_W), BF16),
        compiler_params=_cparams(2),
    )(q, kt, v)


def _scan_masks():
    t = np.arange(CHUNK)
    inc = np.stack([t[:, None] >= t[None, :], t[:, None] <= t[None, :]]).astype(np.float32)
    strict = np.stack([t[:, None] > t[None, :], t[:, None] < t[None, :]]).astype(np.float32)
    return jnp.asarray(np.tile(inc, (1, 1, N_HEADS))), jnp.asarray(np.tile(strict, (1, 1, N_HEADS)))


def _cumsum_rows(x, reverse):
    n = x.shape[0]
    row = lax.broadcasted_iota(jnp.int32, (n, 1), 0)
    sh = 1
    while sh < n:
        if reverse:
            x = x + jnp.where(row < n - sh, pltpu.roll(x, n - sh, 0), 0.0)
        else:
            x = x + jnp.where(row >= sh, pltpu.roll(x, sh, 0), 0.0)
        sh *= 2
    return x


def _block_diag(x, bm_b):
    return _tile_rows(x.astype(BF16)) * bm_b


def _dot_nt(a, b):
    return lax.dot_general(a, b, (((1,), (1,)), ((), ())), preferred_element_type=F32)


def _fwd_tile(p, nlat):
    return jnp.where(p == 0, nlat, p - 1)


def _bwd_tile(p, nlat):
    return jnp.where(p == 0, nlat, nlat - p)


def _chunk_order(d):
    nch = TQ // CHUNK
    return range(nch) if d == 0 else range(nch - 1, -1, -1)


def _stack_heads(x, lane_h):
    return jnp.concatenate([jnp.where(lane_h == h, x, 0.0) for h in range(N_HEADS)], axis=0)


def _tile_rows(x):
    return jnp.concatenate([x] * N_HEADS, axis=0)


def _collapse_heads(x):
    c = x.shape[0] // N_HEADS
    return (x[0:c] + x[c:2 * c]) + (x[2 * c:3 * c] + x[3 * c:4 * c])


def _rw_prep_kernel(zp_ref, z_ref, zn_ref, mu_ref, kk_ref, ka_ref, rk_ref, w0_ref, a0_ref, wup_ref,
                    aup_ref, gup_ref, ones_ref,
                    r_ref, v_ref, ah_ref, g_ref, bon_ref, lw_ref, kd_ref, bd_ref, *, nlat, tc):
    i = pl.program_id(1)
    prev_ok, next_ok = _neighbour_ok(i, nlat)
    z = z_ref[0]
    row = lax.broadcasted_iota(jnp.int32, (TMT, 1), 0)
    before = jnp.where(prev_ok, zp_ref[0, HALO - 1:HALO, :], 0.0)
    after = jnp.where(next_ok, zn_ref[0, 0:1, :], 0.0)
    zprev = jnp.where(row == 0, before, pltpu.roll(z, 1, 0))
    znext = jnp.where(row == _rows_in_tile(i, nlat, tc) - 1, after, pltpu.roll(z, TMT - 1, 0))
    zs = z + mu_ref[...] * (0.5 * (zprev + znext) - z)
    r = zs[:, 0:GROUP_W]
    k = zs[:, GROUP_W:2 * GROUP_W]
    v = zs[:, 2 * GROUP_W:3 * GROUP_W]
    low = zs[:, 3 * GROUP_W:3 * GROUP_W + 128]
    gd = zs[:, 3 * GROUP_W + 128:]
    ones = ones_ref[...]
    kk = k * kk_ref[...]
    kk = kk * lax.rsqrt(_dot_sel_rhs(kk * kk, ones) + 1e-12)
    wl = _dot(jnp.tanh(low).astype(BF16), wup_ref[...])
    al = _dot(low.astype(BF16), aup_ref[...])
    ksum = jnp.zeros((TMT, GROUP_W), F32)
    for d in range(2):
        w_raw = -_softplus(-(w0_ref[d:d + 1, :] + wl[:, d * GROUP_W:(d + 1) * GROUP_W])) - 0.5
        lw_ref[d, 0] = -jnp.exp(w_raw)
        a = _sigmoid(a0_ref[d:d + 1, :] + al[:, d * GROUP_W:(d + 1) * GROUP_W])
        kd = k * (1.0 + (a - 1.0) * ka_ref[...])
        kd_ref[d, 0] = kd
        bd_ref[d, 0] = kk * a
        ksum = ksum + kd
    r_ref[0] = r
    v_ref[0] = v
    ah_ref[0] = -kk
    g_ref[0] = _dot(_sigmoid(gd).astype(BF16), gup_ref[...])
    bon_ref[0] = _dot_sel_rhs(r * ksum * rk_ref[...], ones) * v


def _rw_prep(z, mu, k_k, k_a, r_k, w0, a0, wup_p, aup_p, gup_b, ones_b, T, layer):
    B, S, _ = z.shape
    ntiles, nlat = _tok_tiles(S, T)
    W = 4 * GROUP_W
    cb = COL_RW // W
    prev, nxt = _halo_specs(W, cb, S)
    one = _tok_spec(GROUP_W)
    two = pl.BlockSpec((2, 1, TMT, GROUP_W), lambda b, i: (0, b, i, 0))
    s1 = jax.ShapeDtypeStruct((B, S, GROUP_W), F32)
    s2 = jax.ShapeDtypeStruct((2, B, S, GROUP_W), F32)
    vec = _const_spec((1, GROUP_W), layer)
    return pl.pallas_call(
        functools.partial(_rw_prep_kernel, nlat=nlat, tc=S - T),
        grid=(B, ntiles),
        in_specs=[prev, _tok_spec(W, cb), nxt,
                  _const_spec((1, W), layer), vec, vec, vec,
                  _const_spec((2, GROUP_W), layer), _const_spec((2, GROUP_W), layer),
                  _const_spec((128, 2 * GROUP_W), layer), _const_spec((128, 2 * GROUP_W), layer),
                  _const_spec((128, GROUP_W), layer), _const_spec((GROUP_W, GROUP_W))],
        out_specs=[one, one, one, one, one, two, two, two],
        out_shape=[s1, s1, s1, s1, s1, s2, s2, s2],
        compiler_params=_cparams(2),
    )(z, z, z, mu, k_k, k_a, r_k, w0, a0, wup_p, aup_p, gup_b, ones_b)


def _rw_scan_kernel(rf_ref, rb_ref, vf_ref, vb_ref, af_ref, ab_ref, lwf_ref, lwb_ref, kdf_ref, kdb_ref,
                    bdf_ref, bdb_ref, inc_ref, strict_ref, bm_ref, yf_ref, yb_ref, s_ref):
    p = pl.program_id(1)

    @pl.when(p == 0)
    def _():
        s_ref[...] = jnp.zeros_like(s_ref)

    n = N_HEADS * CHUNK
    bm = bm_ref[...]
    eye = (lax.broadcasted_iota(jnp.int32, (CHUNK, n), 1) % CHUNK
           == lax.broadcasted_iota(jnp.int32, (CHUNK, n), 0)).astype(F32)
    refs = ((rf_ref, vf_ref, af_ref, lwf_ref, kdf_ref, bdf_ref, yf_ref),
            (rb_ref, vb_ref, ab_ref, lwb_ref, kdb_ref, bdb_ref, yb_ref))
    orders = [list(_chunk_order(d)) for d in range(2)]
    bm_b = bm.astype(BF16)
    bd_of = lambda m: _block_diag(m, bm_b)

    def prepare(inst, out):
        idx = range(len(inst))
        rows = [pl.ds(orders[d][step] * CHUNK, CHUNK) for _, step, d in inst]
        ar, r_t, v_s, a_s, b_s, k_s, bk_t, decay, v_in = [], [], [], [], [], [], [], [], []
        for (bi, step, d), sl in zip(inst, rows):
            r_ref, v_ref, a_ref, lw_ref, kd_ref, bd_ref, _ = refs[d]
            lw = lw_ref[0, bi, sl, :]
            kd = kd_ref[0, bi, sl, :]
            bd = bd_ref[0, bi, sl, :]
            v = v_ref[bi, sl, :]
            cs = _cumsum_rows(lw, reverse=(d == 1))
            tot = cs[CHUNK - 1:CHUNK, :] if d == 0 else cs[0:1, :]
            e_neg = jnp.exp(-cs)
            e_hat = jnp.exp(tot - cs)
            a_t = a_ref[bi, sl, :] * jnp.exp(cs - lw)
            rt = r_ref[bi, sl, :] * jnp.exp(cs)
            ar.append(jnp.concatenate([a_t, rt], axis=0).astype(BF16))
            r_t.append(rt)
            v_in.append(v)
            v_s.append(bd_of(v))
            a_s.append(bd_of(a_t))
            b_s.append(bd_of(bd * e_neg))
            k_s.append(bd_of(kd * e_neg))
            t = jnp.concatenate([bd * e_hat, kd * e_hat,
                                 jnp.broadcast_to(jnp.exp(tot), (2 * CHUNK, GROUP_W))], axis=0).T
            bk_t.append(t[:, :2 * CHUNK].astype(BF16))
            decay.append(jnp.concatenate([t[:, 2 * CHUNK:]] * 2, axis=1))
        yield
        g_b = [_dot_nt(ar[i], b_s[i]) for i in idx]
        g_k = [_dot_nt(ar[i], k_s[i]) for i in idx]
        yield
        pw = [g_b[i][:CHUNK] * strict_ref[inst[i][2]] for i in idx]
        ak = [(g_k[i][:CHUNK] * strict_ref[inst[i][2]]).astype(BF16) for i in idx]
        rbk = [jnp.concatenate([g_b[i][CHUNK:] * inc_ref[inst[i][2]], g_k[i][CHUNK:] * inc_ref[inst[i][2]]],
                               axis=1).astype(BF16) for i in idx]
        x = [eye + pw[i] for i in idx]
        pw = [_dot(pw[i].astype(BF16), bd_of(pw[i])) for i in idx]
        yield
        for _ in range(4):
            px = [_dot(jnp.concatenate([pw[i], x[i]], axis=0).astype(BF16), bd_of(pw[i])) for i in idx]
            x = [x[i] + px[i][CHUNK:] for i in idx]
            pw = [px[i][:CHUNK] for i in idx]
            yield
        x = [(x[i] + _dot(x[i].astype(BF16), bd_of(pw[i]))).astype(BF16) for i in idx]
        yield
        akv = [_dot(ak[i], v_s[i]) for i in idx]
        yield
        p12 = [_dot(x[i], jnp.concatenate([bd_of(akv[i]), a_s[i]], axis=1)) for i in idx]
        yield
        p1_s = [bd_of(p12[i][:, :GROUP_W]) for i in idx]
        p2_s = [bd_of(p12[i][:, GROUP_W:]) for i in idx]
        q = [r_t[i] + _dot(rbk[i][:, :n], p2_s[i]) for i in idx]
        y_c = [_dot(rbk[i], jnp.concatenate([p1_s[i], v_s[i]], axis=0)) for i in idx]
        yield
        zero = jnp.zeros((CHUNK, GROUP_W), F32)
        gc = [_dot(bk_t[i], jnp.concatenate(
            [jnp.concatenate([p12[i][:, GROUP_W:], p12[i][:, :GROUP_W]], axis=1),
             jnp.concatenate([zero, v_in[i]], axis=1)], axis=0).astype(BF16)) for i in idx]
        for i in idx:
            gq = jnp.concatenate([gc[i][:, :GROUP_W] * bm, q[i]], axis=0).astype(BF16)
            out.append((rows[i], inst[i][0], inst[i][2], gq, gc[i][:, GROUP_W:] * bm, y_c[i], decay[i]))

    nb = s_ref.shape[0] // 2
    s = [s_ref[k] for k in range(2 * nb)]

    def recur(item):
        sl, bi, d, gq, c_s, y_c, decay = item
        k = 2 * bi + d
        m = _dot(gq, s[k].astype(BF16))
        refs[d][6][bi, sl, :] = m[n:] + y_c
        s[k] = decay * s[k] + m[:n] + c_s

    done = []
    for _ in prepare([(bi, step, d) for step in range(TQ // CHUNK) for bi in range(nb) for d in range(2)], done):
        pass
    for item in done:
        recur(item)
    for k in range(2 * nb):
        s_ref[k] = s[k]


def _rw_scan(r, v, ah, lw, kd, bd, masks, bm, T):
    B, S, _ = r.shape
    nlat = T // TQ
    inc, strict = masks
    n = N_HEADS * CHUNK
    nb = RW_SCAN_BATCH if B % RW_SCAN_BATCH == 0 else 1
    fwd = pl.BlockSpec((nb, TQ, GROUP_W), lambda b, p: (b, _fwd_tile(p, nlat), 0))
    bwd = pl.BlockSpec((nb, TQ, GROUP_W), lambda b, p: (b, _bwd_tile(p, nlat), 0))
    fwd2 = pl.BlockSpec((1, nb, TQ, GROUP_W), lambda b, p: (0, b, _fwd_tile(p, nlat), 0))
    bwd2 = pl.BlockSpec((1, nb, TQ, GROUP_W), lambda b, p: (1, b, _bwd_tile(p, nlat), 0))
    out = jax.ShapeDtypeStruct((B, S, GROUP_W), F32)
    return pl.pallas_call(
        _rw_scan_kernel,
        grid=(B // nb, nlat + 1),
        in_specs=[fwd, bwd, fwd, bwd, fwd, bwd, fwd2, bwd2, fwd2, bwd2, fwd2, bwd2,
                  _const_spec((2, CHUNK, n)), _const_spec((2, CHUNK, n)), _const_spec((n, n))],
        out_specs=[fwd, bwd],
        out_shape=[out, out],
        scratch_shapes=[pltpu.VMEM((2 * nb, n, GROUP_W), F32)],
        compiler_params=_cparams(2),
    )(r, r, v, v, ah, ah, lw, lw, kd, kd, bd, bd, inc, strict, bm)


def _gla_scan_kernel(qf_ref, qb_ref, kf_ref, kb_ref, gf_ref, gb_ref, vf_ref, vb_ref, gup_ref, gbias_ref,
                     inc_ref, bm_ref, bmv_ref, of_ref, ob_ref, s_ref):
    p = pl.program_id(1)

    @pl.when(p == 0)
    def _():
        s_ref[...] = jnp.zeros_like(s_ref)

    wk = N_HEADS * GLA_DK
    n = N_HEADS * CHUNK
    bm = bm_ref[...]
    bmv = bmv_ref[...].astype(BF16)
    bmk = (lax.broadcasted_iota(jnp.int32, (n, wk), 0) // CHUNK
           == lax.broadcasted_iota(jnp.int32, (n, wk), 1) // GLA_DK).astype(F32).astype(BF16)
    refs = ((qf_ref, kf_ref, gf_ref, vf_ref, of_ref), (qb_ref, kb_ref, gb_ref, vb_ref, ob_ref))
    orders = [list(_chunk_order(d)) for d in range(2)]
    nb = s_ref.shape[0] // 2
    inst = [(bi, step, d) for step in range(TQ // CHUNK) for bi in range(nb) for d in range(2)]
    idx = range(len(inst))
    rows = [pl.ds(orders[d][step] * CHUNK, CHUNK) for _, step, d in inst]

    qe, ke_s, ks4, v_s, decay = [], [], [], [], []
    for (bi, step, d), sl in zip(inst, rows):
        q_ref, k_ref, g_ref, v_ref, _ = refs[d]
        k = k_ref[bi, sl, :]
        la = -_softplus(-(_dot(g_ref[bi, sl, :].astype(BF16), gup_ref[d]) + gbias_ref[d])) * (1.0 / GLA_TAU)
        b = _cumsum_rows(la, reverse=(d == 1))
        tot = b[CHUNK - 1:CHUNK, :] if d == 0 else b[0:1, :]
        qe.append((q_ref[bi, sl, :] * (GLA_DK ** -0.5) * jnp.exp(b)).astype(BF16))
        ke_s.append(_block_diag(k * jnp.exp(-b), bmk))
        ks4.append(_tile_rows(k * jnp.exp(tot - b)).T.astype(BF16))
        v_s.append(_block_diag(v_ref[bi, sl, :], bmv))
        decay.append(_tile_rows(jnp.broadcast_to(jnp.exp(tot), (CHUNK, wk))).T)
    a_cat = [(_dot_nt(qe[i], ke_s[i]) * inc_ref[inst[i][2]]).astype(BF16) for i in idx]
    o_in = [_dot(a_cat[i], v_s[i]) for i in idx]
    kv = [_dot(ks4[i], v_s[i]) * bm for i in idx]

    s = [s_ref[k] for k in range(2 * nb)]
    for i, ((bi, step, d), sl) in enumerate(zip(inst, rows)):
        k = 2 * bi + d
        refs[d][4][bi, sl, :] = o_in[i] + _dot(qe[i], s[k].astype(BF16))
        s[k] = decay[i] * s[k] + kv[i]
    for k in range(2 * nb):
        s_ref[k] = s[k]


def _gla_scan(z, gup_p, gb, masks, bm, bmv, T, layer):
    B, S, _ = z.shape
    nlat = T // TQ
    inc, _ = masks
    n = N_HEADS * CHUNK
    wk = N_HEADS * GLA_DK
    nb = GLA_SCAN_BATCH if B % GLA_SCAN_BATCH == 0 else 1
    fwd = lambda w, col: pl.BlockSpec((nb, TQ, w), lambda b, p: (b, _fwd_tile(p, nlat), col // w))
    bwd = lambda w, col: pl.BlockSpec((nb, TQ, w), lambda b, p: (b, _bwd_tile(p, nlat), col // w))
    out = jax.ShapeDtypeStruct((B, S, GROUP_W), F32)
    return pl.pallas_call(
        _gla_scan_kernel,
        grid=(B // nb, nlat + 1),
        in_specs=[fwd(wk, COL_GLA_Q), bwd(wk, COL_GLA_Q), fwd(wk, COL_GLA_K), bwd(wk, COL_GLA_K),
                  fwd(wk, COL_GLA_G), bwd(wk, COL_GLA_G), fwd(GROUP_W, COL_GLA_V), bwd(GROUP_W, COL_GLA_V),
                  _const_spec((2, wk, wk), layer), _const_spec((2, 1, wk), layer),
                  _const_spec((2, CHUNK, n)), _const_spec((wk, GROUP_W)), _const_spec((n, n))],
        out_specs=[fwd(GROUP_W, 0), bwd(GROUP_W, 0)],
        out_shape=[out, out],
        scratch_shapes=[pltpu.VMEM((2 * nb, wk, GROUP_W), F32)],
        compiler_params=_cparams(2),
    )(z, z, z, z, z, z, z, z, gup_p, gb, inc, bm, bmv)


def _pack_w_in(w_in):
    L, D, _ = w_in.shape
    na, mla, rw, gla = jnp.split(w_in, [768, 1120, 2144], axis=-1)
    cq, ckv, kr = jnp.split(mla, [MLA_Q_RANK, MLA_Q_RANK + MLA_KV_RANK], axis=-1)
    gq, gk, gv, gg, go = jnp.split(gla, [128, 256, 512, 528], axis=-1)
    pad = jnp.zeros((L, D, COL_GLA_V - COL_GLA_G - gg.shape[-1]), F32)
    packed = jnp.concatenate([na, ckv, cq, kr, kr[..., _rope_swap_perm()], gq, gk, gg, pad, gv, go, rw], axis=-1)
    assert packed.shape[-1] == Z_COLS
    return packed.astype(BF16)


def kernel(x, c, ctx, c_ctx, w_mod, b_mod, g_mix_pre, g_mix_post, g_ffn_pre, g_ffn_post, w_in, w_out, na_rpb, mla_q_norm, mla_w_uq, mla_kv_norm, mla_w_ukv, rw_mu, rw_w0, rw_w_up, rw_a0, rw_a_up, rw_g_up, rw_k_k, rw_k_a, rw_r_k, rw_ln_w, rw_ln_b, gla_gate_up, gla_gate_b, gla_norm, ffn_w_up, ffn_conv_w, ffn_conv_b, ffn_w_down):
    B, T, D = x.shape
    Tc = ctx.shape[1]
    L = w_in.shape[0]
    assert D == D_MODEL and Tc == TQ and T % TMT == 0 and B + 1 <= 8

    w_in_p = _pack_w_in(w_in)
    w_out_b = w_out.astype(BF16)
    wq1, wq2, wk, wv = _mla_weights(mla_w_uq, mla_w_ukv)
    place = _rope_place()
    cos, sin = _rope_tables(T, Tc)
    zero_lo = jnp.zeros((L, 64, 2 * GROUP_W), F32)
    rw_wup_p = jnp.concatenate([jnp.concatenate([rw_w_up[:, 0], rw_w_up[:, 1]], axis=-1), zero_lo], axis=1).astype(BF16)
    rw_aup_p = jnp.concatenate([zero_lo, jnp.concatenate([rw_a_up[:, 0], rw_a_up[:, 1]], axis=-1)], axis=1).astype(BF16)
    rw_gup_b = rw_g_up.astype(BF16)
    wk_gla = N_HEADS * GLA_DK
    gla_gup_p = jnp.concatenate([gla_gate_up, jnp.zeros((L, 2, wk_gla - gla_gate_up.shape[2], wk_gla), F32)], axis=2).astype(BF16)
    ffn_up_b = ffn_w_up.astype(BF16)
    ffn_dn_b = ffn_w_down.astype(BF16)
    ones_b = jnp.asarray(_block_ones(GROUP_W, HEAD_DIM), BF16)
    rw_bm = jnp.asarray(_block_ones(N_HEADS * CHUNK, CHUNK))
    gla_bm = jnp.asarray((np.arange(wk_gla)[:, None] // GLA_DK == np.arange(GROUP_W)[None, :] // HEAD_DIM).astype(np.float32))
    masks = _scan_masks()

    cvecs = jnp.zeros((8, D), F32).at[:B].set(c).at[B].set(c_ctx)
    mods = _modulation(cvecs, w_mod, b_mod).reshape(L, 8, 6, D)
    mods = jnp.pad(mods, ((0, 0), (0, 0), (0, 2), (0, 0)))
    modtabs = jnp.stack([jnp.broadcast_to(mods[:, B:B + 1], (L, B, 8, D)), mods[:, :B]], axis=2)

    xs = jnp.concatenate([x, ctx], axis=1)
    rows = lambda a: a.reshape(L, 1, -1)
    na_bias = _natten_bias(na_rpb, T // GRID_W)
    gla_gb = gla_gate_b[:, :, None, :]
    for i in range(L):
        z = _in_proj(xs, modtabs, rows(g_mix_pre), w_in_p, T, i)
        y_na = _natten(z, na_bias, T, i)
        q, kt, v = _mla_up(z, cos, sin, rows(mla_q_norm), rows(mla_kv_norm), wq1, wq2, wk, wv, place, T, i)
        y_mla = _mla_attn(q, kt, v, T)
        r, vv, ah, g, bon, lw, kd, bd = _rw_prep(z, rows(rw_mu), rows(rw_k_k), rows(rw_k_a), rows(rw_r_k),
                                                 rw_w0, rw_a0, rw_wup_p, rw_aup_p, rw_gup_b, ones_b, T, i)
        yf, yb = _rw_scan(r, vv, ah, lw, kd, bd, masks, rw_bm, T)
        of, ob = _gla_scan(z, gla_gup_p, gla_gb, masks, gla_bm, rw_bm, T, i)
        xs = _out_proj(xs, z, y_na, y_mla, yf, yb, bon, g, of, ob, modtabs, rows(g_mix_post), w_out_b,
                       rows(rw_ln_w), rows(rw_ln_b), rows(gla_norm), ones_b, T, i)
        xs = _ffn(xs, modtabs, rows(g_ffn_pre), rows(g_ffn_post), ffn_up_b, ffn_conv_w,
                  rows(ffn_conv_b), ffn_dn_b, T, i, latent_only=(i == L - 1))
    return xs
```

```python
import functools

import numpy as np
import jax
import jax.numpy as jnp
from jax import lax
from jax.experimental import pallas as pl
from jax.experimental.pallas import tpu as pltpu

F32 = jnp.float32
BF16 = jnp.bfloat16

D_MODEL = 1024
GRID_W = 64
EPS = 1e-6
LOG2E = 1.4426950408889634
N_HEADS = 4
HEAD_DIM = 64
GROUP_W = 256
NA_ROWS = 8
NA_COLS = 16
NA_UNION = 12
MLA_Q_RANK = 192
MLA_KV_RANK = 128
MLA_NOPE = 64
MLA_ROPE = 32
MLA_HEAD_PAD = 128
ROPE_THETA = 10000.0
RW_GN_EPS = 64e-5
GLA_DK = 32
GLA_TAU = 16.0
D_FF = 2816
CHUNK = 64
TQ = 256
RW_SCAN_BATCH = 2
GLA_SCAN_BATCH = 4
TMT = 512
HALO = 8
FF_CHUNK = 256
FF_GROUP = 4
Z_COLS = 3072

COL_NA = 0
COL_MLA = 768
COL_GLA_Q = 1152
COL_GLA_K = 1280
COL_GLA_G = 1408
COL_GLA_V = 1536
COL_GLA_O = 1792
COL_RW = 2048

VMEM_LIMIT_V7X = 56 * 1024 * 1024


def _cparams(n_axes):
    return pltpu.CompilerParams(dimension_semantics=("arbitrary",) * n_axes,
                                vmem_limit_bytes=VMEM_LIMIT_V7X)


def _const_spec(shape, layer=None):
    nd = len(shape)
    if layer is None:
        return pl.BlockSpec(shape, lambda *_: (0,) * nd, pipeline_mode=pl.Buffered(1))
    return pl.BlockSpec((None,) + tuple(shape), lambda *_: (layer,) + (0,) * nd, pipeline_mode=pl.Buffered(1))


def _dot(a, b):
    return jnp.dot(a, b, preferred_element_type=F32)


def _split3(x):
    hi = x.astype(BF16)
    r1 = x - hi.astype(F32)
    mid = r1.astype(BF16)
    lo = (r1 - mid.astype(F32)).astype(BF16)
    return hi, mid, lo


def _dot_sel_lhs(m, x):
    hi, mid, lo = _split3(x)
    return _dot(m, hi) + (_dot(m, mid) + _dot(m, lo))


def _dot_sel_rhs(x, m):
    hi, mid, lo = _split3(x)
    return _dot(hi, m) + (_dot(mid, m) + _dot(lo, m))


def _bdot(a, b):
    return _dot(a.astype(BF16), b.astype(BF16))


def _sigmoid(x):
    return 1.0 / (1.0 + jnp.exp(-x))


def _silu_gain(x):
    return 0.5 + 0.5 * jnp.tanh(0.5 * x)


def _softplus(x):
    return jnp.maximum(x, 0.0) + jnp.log1p(jnp.exp(-jnp.abs(x)))


def _rms(x):
    return x * lax.rsqrt(jnp.mean(x * x, axis=-1, keepdims=True) + EPS)


def _norm_mod(x, g, shift, scale):
    return (_rms(x) * g) * (1.0 + scale) + shift


def _lane_head(width, per_head):
    return lax.broadcasted_iota(jnp.int32, (1, width), 1) // per_head


def _block_ones(n, blk):
    i = np.arange(n) // blk
    return (i[:, None] == i[None, :]).astype(np.float32)


def _mod_kernel(c_ref, w_ref, b_ref, o_ref):
    cv = c_ref[...]
    s = cv * _sigmoid(cv)
    o_ref[0] = _dot(s.astype(BF16), w_ref[0].astype(BF16)) + b_ref[0]


def _modulation(cvecs, w_mod, b_mod):
    L, D, N = w_mod.shape
    tn = 1536
    return pl.pallas_call(
        _mod_kernel,
        grid=(L, N // tn),
        in_specs=[pl.BlockSpec((8, D), lambda l, n: (0, 0)),
                  pl.BlockSpec((1, D, tn), lambda l, n: (l, 0, n)),
                  pl.BlockSpec((1, 1, tn), lambda l, n: (l, 0, n))],
        out_specs=pl.BlockSpec((1, 8, tn), lambda l, n: (l, 0, n)),
        out_shape=jax.ShapeDtypeStruct((L, 8, N), F32),
        compiler_params=_cparams(2),
    )(cvecs, w_mod, b_mod.reshape(L, 1, N))


def _tok_tiles(S, T):
    assert T % TMT == 0 and 0 < S - T <= TMT
    return T // TMT + 1, T // TMT


def _tok_spec(width, col_block=0):
    return pl.BlockSpec((1, TMT, width), lambda b, i: (b, i, col_block))


def _mod_spec(nlat, layer):
    return pl.BlockSpec((None, 1, 1, 8, D_MODEL), lambda b, i: (layer, b, jnp.where(i < nlat, 1, 0), 0, 0))


def _halo_specs(width, col_block, S):
    per = TMT // HALO
    last = S // HALO - 1
    prev = pl.BlockSpec((1, HALO, width), lambda b, i: (b, jnp.maximum(i * per - 1, 0), col_block))
    nxt = pl.BlockSpec((1, HALO, width), lambda b, i: (b, jnp.minimum((i + 1) * per, last), col_block))
    return prev, nxt


def _neighbour_ok(i, nlat):
    prev_ok = jnp.logical_and(i != 0, i != nlat)
    next_ok = i < nlat - 1
    return prev_ok, next_ok


def _rows_in_tile(i, nlat, tc):
    return jnp.where(i < nlat, TMT, tc)


def _in_proj_kernel(x_ref, mod_ref, g_ref, w_ref, z_ref):
    m = mod_ref[0, 0]
    h = _norm_mod(x_ref[0], g_ref[...], m[0:1], m[1:2])
    z_ref[0] = _dot(h.astype(BF16), w_ref[...])


def _in_proj(xs, modtab, g_pre, w_in_p, T, layer):
    B, S, D = xs.shape
    ntiles, nlat = _tok_tiles(S, T)
    return pl.pallas_call(
        _in_proj_kernel,
        grid=(B, ntiles),
        in_specs=[_tok_spec(D), _mod_spec(nlat, layer), _const_spec((1, D), layer),
                  _const_spec((D, Z_COLS), layer)],
        out_specs=_tok_spec(Z_COLS),
        out_shape=jax.ShapeDtypeStruct((B, S, Z_COLS), F32),
        compiler_params=_cparams(2),
    )(xs, modtab, g_pre, w_in_p)


def _out_proj_kernel(x_ref, na_ref, mla_ref, yf_ref, yb_ref, bon_ref, g_ref, of_ref, ob_ref, og_ref,
                     mod_ref, gpost_ref, w_ref, lnw_ref, lnb_ref, gn_ref, ones_ref, o_ref):
    m = mod_ref[0, 0]
    ones = ones_ref[...]
    inv_n = 1.0 / HEAD_DIM
    y = yf_ref[0] + yb_ref[0]
    yc = y - _dot_sel_rhs(y, ones) * inv_n
    var = _dot_sel_rhs(yc * yc, ones) * inv_n
    y_rw = (yc * lax.rsqrt(var + RW_GN_EPS) * lnw_ref[...] + lnb_ref[...] + bon_ref[0]) * g_ref[0]
    o = of_ref[0] + ob_ref[0]
    ms = _dot_sel_rhs(o * o, ones) * inv_n
    og = og_ref[0]
    y_gla = (o * lax.rsqrt(ms + EPS) * gn_ref[...]) * (og * _silu_gain(og))
    y = jnp.concatenate([na_ref[0], mla_ref[0], y_rw.astype(BF16), y_gla.astype(BF16)], axis=-1)
    y = _dot(y, w_ref[...])
    o_ref[0] = x_ref[0] + m[2:3] * (_rms(y) * gpost_ref[...])


def _out_proj(xs, z, y_na, y_mla, yf, yb, bon, g, of, ob, modtab, g_post, w_out_b, ln_w, ln_b, gla_norm,
              ones_b, T, layer):
    B, S, D = xs.shape
    ntiles, nlat = _tok_tiles(S, T)
    grp = _tok_spec(GROUP_W)
    vec = _const_spec((1, GROUP_W), layer)
    return pl.pallas_call(
        _out_proj_kernel,
        grid=(B, ntiles),
        in_specs=[_tok_spec(D), grp, grp, grp, grp, grp, grp, grp, grp,
                  _tok_spec(GROUP_W, COL_GLA_O // GROUP_W),
                  _mod_spec(nlat, layer), _const_spec((1, D), layer), _const_spec((D, D), layer), vec, vec, vec,
                  _const_spec((GROUP_W, GROUP_W))],
        out_specs=_tok_spec(D),
        out_shape=jax.ShapeDtypeStruct((B, S, D), F32),
        compiler_params=_cparams(2),
    )(xs, y_na, y_mla, yf, yb, bon, g, of, ob, z, modtab, g_post, w_out_b, ln_w, ln_b, gla_norm, ones_b)


def _ffn_kernel(xp_ref, x_ref, xn_ref, mod_ref, gpre_ref, gpost_ref, wup_ref, cw_ref, cb_ref,
                wdn_ref, o_ref, *, nlat, tc):
    i = pl.program_id(1)
    prev_ok, next_ok = _neighbour_ok(i, nlat)
    m = mod_ref[0, 0]
    x = x_ref[0]
    xe = jnp.concatenate([xp_ref[0], x, xn_ref[0]], axis=0)
    n = TMT + 2 * HALO
    h = _norm_mod(xe, gpre_ref[...], m[3:4], m[4:5])
    row = lax.broadcasted_iota(jnp.int32, (n, 1), 0)
    valid = jnp.logical_or(jnp.logical_and(row >= HALO, row < HALO + _rows_in_tile(i, nlat, tc)),
                           jnp.logical_or(jnp.logical_and(row < HALO, prev_ok),
                                          jnp.logical_and(row >= HALO + TMT, next_ok)))
    hb = jnp.where(valid, h, 0.0).astype(BF16)
    nchunks = D_FF // FF_CHUNK

    def up(c):
        return [_dot(hb, wup_ref[:, base + c * FF_CHUNK:base + (c + 1) * FF_CHUNK]) for base in (0, D_FF)]

    def conv(z, lo):
        cw = cw_ref[:, lo:lo + FF_CHUNK]
        return (cb_ref[:, lo:lo + FF_CHUNK]
                + pltpu.roll(z, 1, 0)[HALO:HALO + TMT] * cw[0:1]
                + z[HALO:HALO + TMT] * cw[1:2]
                + pltpu.roll(z, n - 1, 0)[HALO:HALO + TMT] * cw[2:3])

    acc = None
    group = []
    z_next = up(0)
    for c in range(nchunks):
        z_val, z_gate = z_next
        if c + 1 < nchunks:
            z_next = up(c + 1)
        val = conv(z_val, c * FF_CHUNK)
        gate = conv(z_gate, D_FF + c * FF_CHUNK)
        group.append(((gate * _silu_gain(gate)) * val).astype(BF16))
        if len(group) == FF_GROUP or c + 1 == nchunks:
            lo = (c + 1 - len(group)) * FF_CHUNK
            part = _dot(jnp.concatenate(group, axis=1), wdn_ref[lo:(c + 1) * FF_CHUNK, :])
            acc = part if acc is None else acc + part
            group = []
    o_ref[0] = x + m[5:6] * (_rms(acc) * gpost_ref[...])


def _ffn(xs, modtab, g_pre, g_post, w_up_b, conv_w, conv_b, w_dn_b, T, layer, latent_only):
    B, S, D = xs.shape
    ntiles, nlat = _tok_tiles(S, T)
    prev, nxt = _halo_specs(D, 0, S)
    return pl.pallas_call(
        functools.partial(_ffn_kernel, nlat=nlat, tc=S - T),
        grid=(B, nlat if latent_only else ntiles),
        in_specs=[prev, _tok_spec(D), nxt, _mod_spec(nlat, layer), _const_spec((1, D), layer),
                  _const_spec((1, D), layer), _const_spec((D, 2 * D_FF), layer),
                  _const_spec((3, 2 * D_FF), layer), _const_spec((1, 2 * D_FF), layer),
                  _const_spec((D_FF, D), layer)],
        out_specs=_tok_spec(D),
        out_shape=jax.ShapeDtypeStruct((B, T if latent_only else S, D), F32),
        compiler_params=_cparams(2),
    )(xs, xs, xs, modtab, g_pre, g_post, w_up_b, conv_w, conv_b, w_dn_b)


def _natten_bias(rpb, rows):
    rt = TQ // GRID_W
    j = np.arange(GRID_W)
    col_start = np.clip(j - NA_COLS // 2, 0, GRID_W - NA_COLS)
    col_in = (j[None, :] >= col_start[:, None]) & (j[None, :] < col_start[:, None] + NA_COLS)
    edge = GRID_W - NA_COLS
    ext = jnp.concatenate([jnp.repeat(rpb[..., :1], edge, axis=-1), rpb,
                           jnp.repeat(rpb[..., -1:], edge, axis=-1)], axis=-1).astype(F32) * LOG2E
    bq = jnp.stack([ext[..., GRID_W - 1 - q:2 * GRID_W - 1 - q] for q in range(GRID_W)], axis=2)
    bq = jnp.where(col_in[:, None, :], bq, -jnp.inf)
    L = rpb.shape[0]
    cases = []
    for r0 in (0, rt, rows - rt):
        us = min(max(r0 - NA_ROWS // 2, 0), rows - NA_UNION)
        per_row = []
        for r in range(r0, r0 + rt):
            rs = min(max(r - NA_ROWS // 2, 0), rows - NA_ROWS)
            first = rs - r + (NA_ROWS - 1)
            pre, post = rs - us, us + NA_UNION - (rs + NA_ROWS)
            b = jnp.pad(bq[:, :, :, first:first + NA_ROWS], ((0, 0), (0, 0), (0, 0), (pre, post), (0, 0)),
                        constant_values=-jnp.inf)
            per_row.append(b.reshape(L, N_HEADS, GRID_W, NA_UNION * GRID_W))
        cases.append(jnp.concatenate(per_row, axis=2))
    return jnp.stack(cases, axis=1)


def _natten_kernel(q_ref, k_ref, v_ref, qc_ref, kc_ref, vc_ref, bias_ref, o_ref, *, nlat, rows):
    j = pl.program_id(1)
    lane_h = _lane_head(GROUP_W, HEAD_DIM)
    kct = kc_ref[0].T.astype(BF16)
    vc = vc_ref[0].astype(BF16)
    scale = HEAD_DIM ** -0.5 * LOG2E
    nwin = NA_UNION * GRID_W

    @pl.when(j < nlat)
    def _():
        us = jnp.clip(j * (TQ // GRID_W) - NA_ROWS // 2, 0, rows - NA_UNION)
        start = pl.multiple_of(us * GRID_W, GRID_W)
        q = q_ref[0] * scale
        kwt = k_ref[0, pl.ds(start, nwin), :].T.astype(BF16)
        vw = v_ref[0, pl.ds(start, nwin), :].astype(BF16)
        acc = jnp.zeros((TQ, GROUP_W), F32)

        def logits(h):
            qh = jnp.where(lane_h == h, q, 0.0).astype(BF16)
            return _dot(qh, kwt), _dot(qh, kct)

        s_next = logits(0)
        for h in range(N_HEADS):
            hm = lane_h == h
            s_w, s_c = s_next
            if h + 1 < N_HEADS:
                s_next = logits(h + 1)
            s_w = s_w + bias_ref[0, h]
            mx = jnp.maximum(jnp.max(s_w, axis=-1, keepdims=True), jnp.max(s_c, axis=-1, keepdims=True))
            p_w = jnp.exp2(s_w - mx)
            p_c = jnp.exp2(s_c - mx)
            den = jnp.sum(p_w, axis=-1, keepdims=True) + jnp.sum(p_c, axis=-1, keepdims=True)
            o = _dot(p_w.astype(BF16), vw) + _dot(p_c.astype(BF16), vc)
            acc = acc + jnp.where(hm, o * (1.0 / den), 0.0)
        o_ref[0] = acc.astype(o_ref.dtype)

    @pl.when(j >= nlat)
    def _():
        q = qc_ref[0] * scale
        acc = jnp.zeros((TQ, GROUP_W), F32)
        for h in range(N_HEADS):
            hm = lane_h == h
            s = _dot(jnp.where(hm, q, 0.0).astype(BF16), kct)
            p = jnp.exp2(s - jnp.max(s, axis=-1, keepdims=True))
            den = jnp.sum(p, axis=-1, keepdims=True)
            acc = acc + jnp.where(hm, _dot(p.astype(BF16), vc) * (1.0 / den), 0.0)
        o_ref[0] = acc.astype(o_ref.dtype)


def _natten(z, bias, T, layer):
    B, S, _ = z.shape
    nlat = T // TQ
    rows = T // GRID_W
    assert rows >= 16 and rows % (TQ // GRID_W) == 0 and S - T == TQ
    cb = COL_NA // GROUP_W
    lat = lambda c: pl.BlockSpec((1, T, GROUP_W), lambda b, j: (b, 0, cb + c))
    ctx = lambda c: pl.BlockSpec((1, TQ, GROUP_W), lambda b, j: (b, nlat, cb + c))
    case = lambda j: jnp.where(j == 0, 0, jnp.where(j >= nlat - 1, 2, 1))
    return pl.pallas_call(
        functools.partial(_natten_kernel, nlat=nlat, rows=rows),
        grid=(B, nlat + 1),
        in_specs=[pl.BlockSpec((1, TQ, GROUP_W), lambda b, j: (b, jnp.minimum(j, nlat - 1), cb)),
                  lat(1), lat(2), ctx(0), ctx(1), ctx(2),
                  pl.BlockSpec((None, 1, N_HEADS, TQ, NA_UNION * GRID_W), lambda b, j: (layer, case(j), 0, 0, 0))],
        out_specs=pl.BlockSpec((1, TQ, GROUP_W), lambda b, j: (b, j, 0)),
        out_shape=jax.ShapeDtypeStruct((B, S, GROUP_W), BF16),
        compiler_params=_cparams(2),
    )(z, z, z, z, z, z, bias)


def _rope_tables(T, Tc):
    t = np.arange(T)
    row = (t // GRID_W).astype(np.float32)
    col = (t % GRID_W).astype(np.float32)
    d = MLA_ROPE // 2
    inv = (np.float32(ROPE_THETA) ** (-np.arange(0, d, 2, dtype=np.float32) / np.float32(d))).astype(np.float32)
    cs, sn = [], []
    for pos in (row, col):
        ang = (pos[:, None] * inv[None, :]).astype(np.float32)
        cs += [np.cos(ang), np.cos(ang)]
        sn += [-np.sin(ang), np.sin(ang)]
    pad = MLA_HEAD_PAD - MLA_NOPE - MLA_ROPE
    f32 = np.float32
    cos = np.concatenate([np.ones((T, MLA_NOPE), f32)] + cs + [np.ones((T, pad), f32)], axis=1)
    sin = np.concatenate([np.zeros((T, MLA_NOPE), f32)] + sn + [np.zeros((T, pad), f32)], axis=1)
    cos = np.concatenate([cos, np.ones((Tc, MLA_HEAD_PAD), f32)], axis=0)
    sin = np.concatenate([sin, np.zeros((Tc, MLA_HEAD_PAD), f32)], axis=0)
    return jnp.asarray(cos, F32), jnp.asarray(sin, F32)


def _rope_swap_perm():
    q = MLA_ROPE // 4
    return np.concatenate([np.arange(q, 2 * q), np.arange(0, q), np.arange(3 * q, 4 * q), np.arange(2 * q, 3 * q)])


def _mla_weights(w_uq, w_ukv):
    L = w_uq.shape[0]
    wq = w_uq.reshape(L, MLA_Q_RANK, N_HEADS, MLA_NOPE + MLA_ROPE)
    pad = MLA_HEAD_PAD - MLA_NOPE - MLA_ROPE
    zq = jnp.zeros((L, MLA_Q_RANK, N_HEADS, pad), F32)
    wq1 = jnp.concatenate([wq, zq], axis=-1).reshape(L, MLA_Q_RANK, N_HEADS * MLA_HEAD_PAD)
    rope_sw = wq[..., MLA_NOPE:][..., _rope_swap_perm()]
    wq2 = jnp.concatenate([jnp.zeros((L, MLA_Q_RANK, N_HEADS, MLA_NOPE), F32), rope_sw, zq], axis=-1)
    wq2 = wq2.reshape(L, MLA_Q_RANK, N_HEADS * MLA_HEAD_PAD)
    wkv = w_ukv.reshape(L, MLA_KV_RANK, N_HEADS, 2 * MLA_NOPE)
    wk = jnp.concatenate([wkv[..., :MLA_NOPE], jnp.zeros((L, MLA_KV_RANK, N_HEADS, MLA_HEAD_PAD - MLA_NOPE), F32)], axis=-1)
    wk = wk.reshape(L, MLA_KV_RANK, N_HEADS * MLA_HEAD_PAD)
    wv_t = wkv[..., MLA_NOPE:].reshape(L, MLA_KV_RANK, N_HEADS * MLA_NOPE).transpose(0, 2, 1)
    return wq1.astype(BF16), wq2.astype(BF16), wk.astype(BF16), wv_t.astype(BF16)


def _rope_place():
    e = np.zeros((MLA_ROPE, N_HEADS * MLA_HEAD_PAD), np.float32)
    for h in range(N_HEADS):
        e[np.arange(MLA_ROPE), h * MLA_HEAD_PAD + MLA_NOPE + np.arange(MLA_ROPE)] = 1.0
    return jnp.asarray(e, BF16)


def _mla_up_kernel(z_ref, cos_ref, sin_ref, qn_ref, kvn_ref, wq1_ref, wq2_ref, wk_ref, wv_ref, e_ref,
                   qt_ref, k_ref, vt_ref):
    z = z_ref[0]
    ckv = z[:, :MLA_KV_RANK]
    cq = z[:, MLA_KV_RANK:MLA_KV_RANK + MLA_Q_RANK]
    kr = z[:, MLA_KV_RANK + MLA_Q_RANK:MLA_KV_RANK + MLA_Q_RANK + MLA_ROPE]
    krs = z[:, MLA_KV_RANK + MLA_Q_RANK + MLA_ROPE:]
    cos = jnp.concatenate([cos_ref[...]] * N_HEADS, axis=-1)
    sin = jnp.concatenate([sin_ref[...]] * N_HEADS, axis=-1)
    nq = (_rms(cq) * qn_ref[...]).astype(BF16)
    nkv_f = _rms(ckv) * kvn_ref[...]
    nkv = nkv_f.astype(BF16)
    q = _dot(nq, wq1_ref[...]) * cos + _dot(nq, wq2_ref[...]) * sin
    scale = (MLA_NOPE + MLA_ROPE) ** -0.5 * LOG2E
    qt_ref[0] = (q * scale).T.astype(BF16)
    k = _dot(nkv, wk_ref[...]) + _dot_sel_rhs(kr, e_ref[...]) * cos + _dot_sel_rhs(krs, e_ref[...]) * sin
    k_ref[0] = k.astype(BF16)
    vt_ref[0] = _dot(wv_ref[...], nkv_f.T.astype(BF16)).astype(BF16)


def _mla_up(z, cos, sin, q_norm, kv_norm, wq1, wq2, wk, wv, place, T, layer):
    B, S, _ = z.shape
    ntiles, _ = _tok_tiles(S, T)
    HP = N_HEADS * MLA_HEAD_PAD
    zw = MLA_KV_RANK + MLA_Q_RANK + 2 * MLA_ROPE
    tab = pl.BlockSpec((TMT, MLA_HEAD_PAD), lambda b, i: (i, 0))
    return pl.pallas_call(
        _mla_up_kernel,
        grid=(B, ntiles),
        in_specs=[_tok_spec(zw, COL_MLA // zw), tab, tab,
                  _const_spec((1, MLA_Q_RANK), layer), _const_spec((1, MLA_KV_RANK), layer),
                  _const_spec((MLA_Q_RANK, HP), layer), _const_spec((MLA_Q_RANK, HP), layer),
                  _const_spec((MLA_KV_RANK, HP), layer), _const_spec((GROUP_W, MLA_KV_RANK), layer),
                  _const_spec((MLA_ROPE, HP))],
        out_specs=[pl.BlockSpec((1, HP, TMT), lambda b, i: (b, 0, i)), _tok_spec(HP),
                   pl.BlockSpec((1, GROUP_W, TMT), lambda b, i: (b, 0, i))],
        out_shape=[jax.ShapeDtypeStruct((B, HP, S), BF16),
                   jax.ShapeDtypeStruct((B, S, HP), BF16),
                   jax.ShapeDtypeStruct((B, GROUP_W, S), BF16)],
        compiler_params=_cparams(2),
    )(z, cos, sin, q_norm, kv_norm, wq1, wq2, wk, wv, place)


def _mla_attn_kernel(qt_ref, k_ref, vt_ref, o_ref, *, nlat, T):
    j = pl.program_id(1)

    def attend(lo, hi):
        def logits(h):
            hp = slice(h * MLA_HEAD_PAD, (h + 1) * MLA_HEAD_PAD)
            return _dot(k_ref[0, lo:hi, hp], qt_ref[0, hp, :])

        ahead = 2
        pending = [logits(h) for h in range(ahead)]
        pieces = []
        for h in range(N_HEADS):
            s = pending.pop(0)
            if h + ahead < N_HEADS:
                pending.append(logits(h + ahead))
            p = jnp.exp2(s - jnp.max(s, axis=0, keepdims=True))
            den = jnp.sum(p, axis=0, keepdims=True)
            o = _dot(vt_ref[0, h * HEAD_DIM:(h + 1) * HEAD_DIM, lo:hi], p.astype(BF16))
            pieces.append(o * (1.0 / den))
        o_ref[0] = jnp.concatenate(pieces, axis=0).T.astype(o_ref.dtype)

    @pl.when(j < nlat)
    def _():
        attend(0, T + TQ)

    @pl.when(j >= nlat)
    def _():
        attend(T, T + TQ)


def _mla_attn(qt, k, vt, T):
    B, S, HP = k.shape
    nlat = T // TQ
    return pl.pallas_call(
        functools.partial(_mla_attn_kernel, nlat=nlat, T=T),
        grid=(B, S // TQ),
        in_specs=[pl.BlockSpec((1, HP, TQ), lambda b, j: (b, 0, j)),
                  pl.BlockSpec((1, S, HP), lambda b, j: (b, 0, 0)),
                  pl.BlockSpec((1, GROUP_W, S), lambda b, j: (b, 0, 0))],
        out_specs=pl.BlockSpec((1, TQ, GROUP_W), lambda b, j: (b, j, 0)),
        out_shape=jax.ShapeDtypeStruct((B, S, GROUP_W), BF16),
        compiler_params=_cparams(2),
    )(qt, k, vt)


def _scan_masks():
    t = np.arange(CHUNK)
    inc = np.stack([t[:, None] >= t[None, :], t[:, None] <= t[None, :]]).astype(np.float32)
    strict = np.stack([t[:, None] > t[None, :], t[:, None] < t[None, :]]).astype(np.float32)
    return jnp.asarray(np.tile(inc, (1, 1, N_HEADS))), jnp.asarray(np.tile(strict, (1, 1, N_HEADS)))


def _cumsum_rows(x, reverse):
    n = x.shape[0]
    row = lax.broadcasted_iota(jnp.int32, (n, 1), 0)
    sh = 1
    while sh < n:
        if reverse:
            x = x + jnp.where(row < n - sh, pltpu.roll(x, n - sh, 0), 0.0)
        else:
            x = x + jnp.where(row >= sh, pltpu.roll(x, sh, 0), 0.0)
        sh *= 2
    return x


def _block_diag(x, bm_b):
    return _tile_rows(x.astype(BF16)) * bm_b


def _dot_nt(a, b):
    return lax.dot_general(a, b, (((1,), (1,)), ((), ())), preferred_element_type=F32)


def _fwd_tile(p, nlat):
    return jnp.where(p == 0, nlat, p - 1)


def _bwd_tile(p, nlat):
    return jnp.where(p == 0, nlat, nlat - p)


def _chunk_order(d):
    nch = TQ // CHUNK
    return range(nch) if d == 0 else range(nch - 1, -1, -1)


def _stack_heads(x, lane_h):
    return jnp.concatenate([jnp.where(lane_h == h, x, 0.0) for h in range(N_HEADS)], axis=0)


def _tile_rows(x):
    return jnp.concatenate([x] * N_HEADS, axis=0)


def _collapse_heads(x):
    c = x.shape[0] // N_HEADS
    return (x[0:c] + x[c:2 * c]) + (x[2 * c:3 * c] + x[3 * c:4 * c])


def _rw_prep_kernel(zp_ref, z_ref, zn_ref, mu_ref, kk_ref, ka_ref, rk_ref, w0_ref, a0_ref, wup_ref,
                    aup_ref, gup_ref, ones_ref,
                    r_ref, v_ref, ah_ref, g_ref, bon_ref, lw_ref, kd_ref, bd_ref, *, nlat, tc):
    i = pl.program_id(1)
    prev_ok, next_ok = _neighbour_ok(i, nlat)
    z = z_ref[0]
    row = lax.broadcasted_iota(jnp.int32, (TMT, 1), 0)
    before = jnp.where(prev_ok, zp_ref[0, HALO - 1:HALO, :], 0.0)
    after = jnp.where(next_ok, zn_ref[0, 0:1, :], 0.0)
    zprev = jnp.where(row == 0, before, pltpu.roll(z, 1, 0))
    znext = jnp.where(row == _rows_in_tile(i, nlat, tc) - 1, after, pltpu.roll(z, TMT - 1, 0))
    zs = z + mu_ref[...] * (0.5 * (zprev + znext) - z)
    r = zs[:, 0:GROUP_W]
    k = zs[:, GROUP_W:2 * GROUP_W]
    v = zs[:, 2 * GROUP_W:3 * GROUP_W]
    low = zs[:, 3 * GROUP_W:3 * GROUP_W + 128]
    gd = zs[:, 3 * GROUP_W + 128:]
    ones = ones_ref[...]
    kk = k * kk_ref[...]
    kk = kk * lax.rsqrt(_dot_sel_rhs(kk * kk, ones) + 1e-12)
    wl = _dot(jnp.tanh(low).astype(BF16), wup_ref[...])
    al = _dot(low.astype(BF16), aup_ref[...])
    ksum = jnp.zeros((TMT, GROUP_W), F32)
    for d in range(2):
        w_raw = -_softplus(-(w0_ref[d:d + 1, :] + wl[:, d * GROUP_W:(d + 1) * GROUP_W])) - 0.5
        lw_ref[d, 0] = -jnp.exp(w_raw)
        a = _sigmoid(a0_ref[d:d + 1, :] + al[:, d * GROUP_W:(d + 1) * GROUP_W])
        kd = k * (1.0 + (a - 1.0) * ka_ref[...])
        kd_ref[d, 0] = kd
        bd_ref[d, 0] = kk * a
        ksum = ksum + kd
    r_ref[0] = r
    v_ref[0] = v
    ah_ref[0] = -kk
    g_ref[0] = _dot(_sigmoid(gd).astype(BF16), gup_ref[...])
    bon_ref[0] = _dot_sel_rhs(r * ksum * rk_ref[...], ones) * v


def _rw_prep(z, mu, k_k, k_a, r_k, w0, a0, wup_p, aup_p, gup_b, ones_b, T, layer):
    B, S, _ = z.shape
    ntiles, nlat = _tok_tiles(S, T)
    W = 4 * GROUP_W
    cb = COL_RW // W
    prev, nxt = _halo_specs(W, cb, S)
    one = _tok_spec(GROUP_W)
    two = pl.BlockSpec((2, 1, TMT, GROUP_W), lambda b, i: (0, b, i, 0))
    s1 = jax.ShapeDtypeStruct((B, S, GROUP_W), F32)
    s2 = jax.ShapeDtypeStruct((2, B, S, GROUP_W), F32)
    vec = _const_spec((1, GROUP_W), layer)
    return pl.pallas_call(
        functools.partial(_rw_prep_kernel, nlat=nlat, tc=S - T),
        grid=(B, ntiles),
        in_specs=[prev, _tok_spec(W, cb), nxt,
                  _const_spec((1, W), layer), vec, vec, vec,
                  _const_spec((2, GROUP_W), layer), _const_spec((2, GROUP_W), layer),
                  _const_spec((128, 2 * GROUP_W), layer), _const_spec((128, 2 * GROUP_W), layer),
                  _const_spec((128, GROUP_W), layer), _const_spec((GROUP_W, GROUP_W))],
        out_specs=[one, one, one, one, one, two, two, two],
        out_shape=[s1, s1, s1, s1, s1, s2, s2, s2],
        compiler_params=_cparams(2),
    )(z, z, z, mu, k_k, k_a, r_k, w0, a0, wup_p, aup_p, gup_b, ones_b)


def _rw_scan_kernel(rf_ref, rb_ref, vf_ref, vb_ref, af_ref, ab_ref, lwf_ref, lwb_ref, kdf_ref, kdb_ref,
                    bdf_ref, bdb_ref, inc_ref, strict_ref, bm_ref, yf_ref, yb_ref, s_ref):
    p = pl.program_id(1)

    @pl.when(p == 0)
    def _():
        s_ref[...] = jnp.zeros_like(s_ref)

    n = N_HEADS * CHUNK
    bm = bm_ref[...]
    eye = (lax.broadcasted_iota(jnp.int32, (CHUNK, n), 1) % CHUNK
           == lax.broadcasted_iota(jnp.int32, (CHUNK, n), 0)).astype(F32)
    refs = ((rf_ref, vf_ref, af_ref, lwf_ref, kdf_ref, bdf_ref, yf_ref),
            (rb_ref, vb_ref, ab_ref, lwb_ref, kdb_ref, bdb_ref, yb_ref))
    orders = [list(_chunk_order(d)) for d in range(2)]
    bm_b = bm.astype(BF16)
    bd_of = lambda m: _block_diag(m, bm_b)

    def prepare(inst, out):
        idx = range(len(inst))
        rows = [pl.ds(orders[d][step] * CHUNK, CHUNK) for _, step, d in inst]
        ar, r_t, v_s, a_s, b_s, k_s, bk_t, decay, v_in = [], [], [], [], [], [], [], [], []
        for (bi, step, d), sl in zip(inst, rows):
            r_ref, v_ref, a_ref, lw_ref, kd_ref, bd_ref, _ = refs[d]
            lw = lw_ref[0, bi, sl, :]
            kd = kd_ref[0, bi, sl, :]
            bd = bd_ref[0, bi, sl, :]
            v = v_ref[bi, sl, :]
            cs = _cumsum_rows(lw, reverse=(d == 1))
            tot = cs[CHUNK - 1:CHUNK, :] if d == 0 else cs[0:1, :]
            e_neg = jnp.exp(-cs)
            e_hat = jnp.exp(tot - cs)
            a_t = a_ref[bi, sl, :] * jnp.exp(cs - lw)
            rt = r_ref[bi, sl, :] * jnp.exp(cs)
            ar.append(jnp.concatenate([a_t, rt], axis=0).astype(BF16))
            r_t.append(rt)
            v_in.append(v)
            v_s.append(bd_of(v))
            a_s.append(bd_of(a_t))
            b_s.append(bd_of(bd * e_neg))
            k_s.append(bd_of(kd * e_neg))
            t = jnp.concatenate([bd * e_hat, kd * e_hat,
                                 jnp.broadcast_to(jnp.exp(tot), (2 * CHUNK, GROUP_W))], axis=0).T
            bk_t.append(t[:, :2 * CHUNK].astype(BF16))
            decay.append(jnp.concatenate([t[:, 2 * CHUNK:]] * 2, axis=1))
        yield
        g_b = [_dot_nt(ar[i], b_s[i]) for i in idx]
        g_k = [_dot_nt(ar[i], k_s[i]) for i in idx]
        yield
        pw = [g_b[i][:CHUNK] * strict_ref[inst[i][2]] for i in idx]
        ak = [(g_k[i][:CHUNK] * strict_ref[inst[i][2]]).astype(BF16) for i in idx]
        rbk = [jnp.concatenate([g_b[i][CHUNK:] * inc_ref[inst[i][2]], g_k[i][CHUNK:] * inc_ref[inst[i][2]]],
                               axis=1).astype(BF16) for i in idx]
        x = [eye + pw[i] for i in idx]
        pw = [_dot(pw[i].astype(BF16), bd_of(pw[i])) for i in idx]
        yield
        for _ in range(4):
            px = [_dot(jnp.concatenate([pw[i], x[i]], axis=0).astype(BF16), bd_of(pw[i])) for i in idx]
            x = [x[i] + px[i][CHUNK:] for i in idx]
            pw = [px[i][:CHUNK] for i in idx]
            yield
        x = [(x[i] + _dot(x[i].astype(BF16), bd_of(pw[i]))).astype(BF16) for i in idx]
        yield
        akv = [_dot(ak[i], v_s[i]) for i in idx]
        yield
        p12 = [_dot(x[i], jnp.concatenate([bd_of(akv[i]), a_s[i]], axis=1)) for i in idx]
        yield
        p1_s = [bd_of(p12[i][:, :GROUP_W]) for i in idx]
        p2_s = [bd_of(p12[i][:, GROUP_W:]) for i in idx]
        q = [r_t[i] + _dot(rbk[i][:, :n], p2_s[i]) for i in idx]
        y_c = [_dot(rbk[i], jnp.concatenate([p1_s[i], v_s[i]], axis=0)) for i in idx]
        yield
        zero = jnp.zeros((CHUNK, GROUP_W), F32)
        gc = [_dot(bk_t[i], jnp.concatenate(
            [jnp.concatenate([p12[i][:, GROUP_W:], p12[i][:, :GROUP_W]], axis=1),
             jnp.concatenate([zero, v_in[i]], axis=1)], axis=0).astype(BF16)) for i in idx]
        for i in idx:
            gq = jnp.concatenate([gc[i][:, :GROUP_W] * bm, q[i]], axis=0).astype(BF16)
            out.append((rows[i], inst[i][0], inst[i][2], gq, gc[i][:, GROUP_W:] * bm, y_c[i], decay[i]))

    nb = s_ref.shape[0] // 2
    s = [s_ref[k] for k in range(2 * nb)]

    def recur(item):
        sl, bi, d, gq, c_s, y_c, decay = item
        k = 2 * bi + d
        m = _dot(gq, s[k].astype(BF16))
        refs[d][6][bi, sl, :] = m[n:] + y_c
        s[k] = decay * s[k] + m[:n] + c_s

    done = []
    for _ in prepare([(bi, step, d) for step in range(TQ // CHUNK) for bi in range(nb) for d in range(2)], done):
        pass
    for item in done:
        recur(item)
    for k in range(2 * nb):
        s_ref[k] = s[k]


def _rw_scan(r, v, ah, lw, kd, bd, masks, bm, T):
    B, S, _ = r.shape
    nlat = T // TQ
    inc, strict = masks
    n = N_HEADS * CHUNK
    nb = RW_SCAN_BATCH if B % RW_SCAN_BATCH == 0 else 1
    fwd = pl.BlockSpec((nb, TQ, GROUP_W), lambda b, p: (b, _fwd_tile(p, nlat), 0))
    bwd = pl.BlockSpec((nb, TQ, GROUP_W), lambda b, p: (b, _bwd_tile(p, nlat), 0))
    fwd2 = pl.BlockSpec((1, nb, TQ, GROUP_W), lambda b, p: (0, b, _fwd_tile(p, nlat), 0))
    bwd2 = pl.BlockSpec((1, nb, TQ, GROUP_W), lambda b, p: (1, b, _bwd_tile(p, nlat), 0))
    out = jax.ShapeDtypeStruct((B, S, GROUP_W), F32)
    return pl.pallas_call(
        _rw_scan_kernel,
        grid=(B // nb, nlat + 1),
        in_specs=[fwd, bwd, fwd, bwd, fwd, bwd, fwd2, bwd2, fwd2, bwd2, fwd2, bwd2,
                  _const_spec((2, CHUNK, n)), _const_spec((2, CHUNK, n)), _const_spec((n, n))],
        out_specs=[fwd, bwd],
        out_shape=[out, out],
        scratch_shapes=[pltpu.VMEM((2 * nb, n, GROUP_W), F32)],
        compiler_params=_cparams(2),
    )(r, r, v, v, ah, ah, lw, lw, kd, kd, bd, bd, inc, strict, bm)


def _gla_scan_kernel(qf_ref, qb_ref, kf_ref, kb_ref, gf_ref, gb_ref, vf_ref, vb_ref, gup_ref, gbias_ref,
                     inc_ref, bm_ref, bmv_ref, of_ref, ob_ref, s_ref):
    p = pl.program_id(1)

    @pl.when(p == 0)
    def _():
        s_ref[...] = jnp.zeros_like(s_ref)

    wk = N_HEADS * GLA_DK
    n = N_HEADS * CHUNK
    bm = bm_ref[...]
    bmv = bmv_ref[...].astype(BF16)
    bmk = (lax.broadcasted_iota(jnp.int32, (n, wk), 0) // CHUNK
           == lax.broadcasted_iota(jnp.int32, (n, wk), 1) // GLA_DK).astype(F32).astype(BF16)
    refs = ((qf_ref, kf_ref, gf_ref, vf_ref, of_ref), (qb_ref, kb_ref, gb_ref, vb_ref, ob_ref))
    orders = [list(_chunk_order(d)) for d in range(2)]
    nb = s_ref.shape[0] // 2
    inst = [(bi, step, d) for step in range(TQ // CHUNK) for bi in range(nb) for d in range(2)]
    idx = range(len(inst))
    rows = [pl.ds(orders[d][step] * CHUNK, CHUNK) for _, step, d in inst]

    qe, ke_s, ks4, v_s, decay = [], [], [], [], []
    for (bi, step, d), sl in zip(inst, rows):
        q_ref, k_ref, g_ref, v_ref, _ = refs[d]
        k = k_ref[bi, sl, :]
        la = -_softplus(-(_dot(g_ref[bi, sl, :].astype(BF16), gup_ref[d]) + gbias_ref[d])) * (1.0 / GLA_TAU)
        b = _cumsum_rows(la, reverse=(d == 1))
        tot = b[CHUNK - 1:CHUNK, :] if d == 0 else b[0:1, :]
        qe.append((q_ref[bi, sl, :] * (GLA_DK ** -0.5) * jnp.exp(b)).astype(BF16))
        ke_s.append(_block_diag(k * jnp.exp(-b), bmk))
        ks4.append(_tile_rows(k * jnp.exp(tot - b)).T.astype(BF16))
        v_s.append(_block_diag(v_ref[bi, sl, :], bmv))
        decay.append(_tile_rows(jnp.broadcast_to(jnp.exp(tot), (CHUNK, wk))).T)
    a_cat = [(_dot_nt(qe[i], ke_s[i]) * inc_ref[inst[i][2]]).astype(BF16) for i in idx]
    o_in = [_dot(a_cat[i], v_s[i]) for i in idx]
    kv = [_dot(ks4[i], v_s[i]) * bm for i in idx]

    s = [s_ref[k] for k in range(2 * nb)]
    for i, ((bi, step, d), sl) in enumerate(zip(inst, rows)):
        k = 2 * bi + d
        refs[d][4][bi, sl, :] = o_in[i] + _dot(qe[i], s[k].astype(BF16))
        s[k] = decay[i] * s[k] + kv[i]
    for k in range(2 * nb):
        s_ref[k] = s[k]


def _gla_scan(z, gup_p, gb, masks, bm, bmv, T, layer):
    B, S, _ = z.shape
    nlat = T // TQ
    inc, _ = masks
    n = N_HEADS * CHUNK
    wk = N_HEADS * GLA_DK
    nb = GLA_SCAN_BATCH if B % GLA_SCAN_BATCH == 0 else 1
    fwd = lambda w, col: pl.BlockSpec((nb, TQ, w), lambda b, p: (b, _fwd_tile(p, nlat), col // w))
    bwd = lambda w, col: pl.BlockSpec((nb, TQ, w), lambda b, p: (b, _bwd_tile(p, nlat), col // w))
    out = jax.ShapeDtypeStruct((B, S, GROUP_W), F32)
    return pl.pallas_call(
        _gla_scan_kernel,
        grid=(B // nb, nlat + 1),
        in_specs=[fwd(wk, COL_GLA_Q), bwd(wk, COL_GLA_Q), fwd(wk, COL_GLA_K), bwd(wk, COL_GLA_K),
                  fwd(wk, COL_GLA_G), bwd(wk, COL_GLA_G), fwd(GROUP_W, COL_GLA_V), bwd(GROUP_W, COL_GLA_V),
                  _const_spec((2, wk, wk), layer), _const_spec((2, 1, wk), layer),
                  _const_spec((2, CHUNK, n)), _const_spec((wk, GROUP_W)), _const_spec((n, n))],
        out_specs=[fwd(GROUP_W, 0), bwd(GROUP_W, 0)],
        out_shape=[out, out],
        scratch_shapes=[pltpu.VMEM((2 * nb, wk, GROUP_W), F32)],
        compiler_params=_cparams(2),
    )(z, z, z, z, z, z, z, z, gup_p, gb, inc, bm, bmv)


def _pack_w_in(w_in):
    L, D, _ = w_in.shape
    na, mla, rw, gla = jnp.split(w_in, [768, 1120, 2144], axis=-1)
    cq, ckv, kr = jnp.split(mla, [MLA_Q_RANK, MLA_Q_RANK + MLA_KV_RANK], axis=-1)
    gq, gk, gv, gg, go = jnp.split(gla, [128, 256, 512, 528], axis=-1)
    pad = jnp.zeros((L, D, COL_GLA_V - COL_GLA_G - gg.shape[-1]), F32)
    packed = jnp.concatenate([na, ckv, cq, kr, kr[..., _rope_swap_perm()], gq, gk, gg, pad, gv, go, rw], axis=-1)
    assert packed.shape[-1] == Z_COLS
    return packed.astype(BF16)


def kernel(x, c, ctx, c_ctx, w_mod, b_mod, g_mix_pre, g_mix_post, g_ffn_pre, g_ffn_post, w_in, w_out, na_rpb, mla_q_norm, mla_w_uq, mla_kv_norm, mla_w_ukv, rw_mu, rw_w0, rw_w_up, rw_a0, rw_a_up, rw_g_up, rw_k_k, rw_k_a, rw_r_k, rw_ln_w, rw_ln_b, gla_gate_up, gla_gate_b, gla_norm, ffn_w_up, ffn_conv_w, ffn_conv_b, ffn_w_down):
    B, T, D = x.shape
    Tc = ctx.shape[1]
    L = w_in.shape[0]
    assert D == D_MODEL and Tc == TQ and T % TMT == 0 and B + 1 <= 8

    w_in_p = _pack_w_in(w_in)
    w_out_b = w_out.astype(BF16)
    wq1, wq2, wk, wv = _mla_weights(mla_w_uq, mla_w_ukv)
    place = _rope_place()
    cos, sin = _rope_tables(T, Tc)
    zero_lo = jnp.zeros((L, 64, 2 * GROUP_W), F32)
    rw_wup_p = jnp.concatenate([jnp.concatenate([rw_w_up[:, 0], rw_w_up[:, 1]], axis=-1), zero_lo], axis=1).astype(BF16)
    rw_aup_p = jnp.concatenate([zero_lo, jnp.concatenate([rw_a_up[:, 0], rw_a_up[:, 1]], axis=-1)], axis=1).astype(BF16)
    rw_gup_b = rw_g_up.astype(BF16)
    wk_gla = N_HEADS * GLA_DK
    gla_gup_p = jnp.concatenate([gla_gate_up, jnp.zeros((L, 2, wk_gla - gla_gate_up.shape[2], wk_gla), F32)], axis=2).astype(BF16)
    ffn_up_b = ffn_w_up.astype(BF16)
    ffn_dn_b = ffn_w_down.astype(BF16)
    ones_b = jnp.asarray(_block_ones(GROUP_W, HEAD_DIM), BF16)
    rw_bm = jnp.asarray(_block_ones(N_HEADS * CHUNK, CHUNK))
    gla_bm = jnp.asarray((np.arange(wk_gla)[:, None] // GLA_DK == np.arange(GROUP_W)[None, :] // HEAD_DIM).astype(np.float32))
    masks = _scan_masks()

    cvecs = jnp.zeros((8, D), F32).at[:B].set(c).at[B].set(c_ctx)
    mods = _modulation(cvecs, w_mod, b_mod).reshape(L, 8, 6, D)
    mods = jnp.pad(mods, ((0, 0), (0, 0), (0, 2), (0, 0)))
    modtabs = jnp.stack([jnp.broadcast_to(mods[:, B:B + 1], (L, B, 8, D)), mods[:, :B]], axis=2)

    xs = jnp.concatenate([x, ctx], axis=1)
    rows = lambda a: a.reshape(L, 1, -1)
    na_bias = _natten_bias(na_rpb, T // GRID_W)
    gla_gb = gla_gate_b[:, :, None, :]
    for i in range(L):
        z = _in_proj(xs, modtabs, rows(g_mix_pre), w_in_p, T, i)
        y_na = _natten(z, na_bias, T, i)
        qt, k, vt = _mla_up(z, cos, sin, rows(mla_q_norm), rows(mla_kv_norm), wq1, wq2, wk, wv, place, T, i)
        y_mla = _mla_attn(qt, k, vt, T)
        r, vv, ah, g, bon, lw, kd, bd = _rw_prep(z, rows(rw_mu), rows(rw_k_k), rows(rw_k_a), rows(rw_r_k),
                                                 rw_w0, rw_a0, rw_wup_p, rw_aup_p, rw_gup_b, ones_b, T, i)
        yf, yb = _rw_scan(r, vv, ah, lw, kd, bd, masks, rw_bm, T)
        of, ob = _gla_scan(z, gla_gup_p, gla_gb, masks, gla_bm, rw_bm, T, i)
        xs = _out_proj(xs, z, y_na, y_mla, yf, yb, bon, g, of, ob, modtabs, rows(g_mix_post), w_out_b,
                       rows(rw_ln_w), rows(rw_ln_b), rows(gla_norm), ones_b, T, i)
        xs = _ffn(xs, modtabs, rows(g_ffn_pre), rows(g_ffn_post), ffn_up_b, ffn_conv_w,
                  rows(ffn_conv_b), ffn_dn_b, T, i, latent_only=(i == L - 1))
    return xs
```

```python
import functools

import numpy as np
import jax
import jax.numpy as jnp
from jax import lax
from jax.experimental import pallas as pl
from jax.experimental.pallas import tpu as pltpu

F32 = jnp.float32
BF16 = jnp.bfloat16

D_MODEL = 1024
GRID_W = 64
EPS = 1e-6
LOG2E = 1.4426950408889634
N_HEADS = 4
HEAD_DIM = 64
GROUP_W = 256
NA_ROWS = 8
NA_COLS = 16
NA_UNION = 12
MLA_Q_RANK = 192
MLA_KV_RANK = 128
MLA_NOPE = 64
MLA_ROPE = 32
MLA_HEAD_PAD = 128
ROPE_THETA = 10000.0
RW_GN_EPS = 64e-5
GLA_DK = 32
GLA_TAU = 16.0
D_FF = 2816
CHUNK = 64
TQ = 256
MLA_TQ = 512
RW_SCAN_BATCH = 2
GLA_SCAN_BATCH = 4
TMT = 512
HALO = 8
FF_CHUNK = 256
FF_GROUP = 4
Z_COLS = 3072

COL_NA = 0
COL_MLA = 768
COL_GLA_Q = 1152
COL_GLA_K = 1280
COL_GLA_G = 1408
COL_GLA_V = 1536
COL_GLA_O = 1792
COL_RW = 2048

VMEM_LIMIT_V7X = 56 * 1024 * 1024


def _cparams(n_axes):
    return pltpu.CompilerParams(dimension_semantics=("arbitrary",) * n_axes,
                                vmem_limit_bytes=VMEM_LIMIT_V7X)


def _const_spec(shape, layer=None):
    nd = len(shape)
    if layer is None:
        return pl.BlockSpec(shape, lambda *_: (0,) * nd, pipeline_mode=pl.Buffered(1))
    return pl.BlockSpec((None,) + tuple(shape), lambda *_: (layer,) + (0,) * nd, pipeline_mode=pl.Buffered(1))


def _dot(a, b):
    return jnp.dot(a, b, preferred_element_type=F32)


def _split3(x):
    hi = x.astype(BF16)
    r1 = x - hi.astype(F32)
    mid = r1.astype(BF16)
    lo = (r1 - mid.astype(F32)).astype(BF16)
    return hi, mid, lo


def _dot_sel_lhs(m, x):
    hi, mid, lo = _split3(x)
    return _dot(m, hi) + (_dot(m, mid) + _dot(m, lo))


def _dot_sel_rhs(x, m):
    hi, mid, lo = _split3(x)
    return _dot(hi, m) + (_dot(mid, m) + _dot(lo, m))


def _bdot(a, b):
    return _dot(a.astype(BF16), b.astype(BF16))


def _sigmoid(x):
    return 1.0 / (1.0 + jnp.exp(-x))


def _silu_gain(x):
    return 0.5 + 0.5 * jnp.tanh(0.5 * x)


def _softplus(x):
    return jnp.maximum(x, 0.0) + jnp.log1p(jnp.exp(-jnp.abs(x)))


def _rms(x):
    return x * lax.rsqrt(jnp.mean(x * x, axis=-1, keepdims=True) + EPS)


def _norm_mod(x, g, shift, scale):
    return (_rms(x) * g) * (1.0 + scale) + shift


def _lane_head(width, per_head):
    return lax.broadcasted_iota(jnp.int32, (1, width), 1) // per_head


def _block_ones(n, blk):
    i = np.arange(n) // blk
    return (i[:, None] == i[None, :]).astype(np.float32)


def _mod_kernel(c_ref, w_ref, b_ref, o_ref):
    cv = c_ref[...]
    s = cv * _sigmoid(cv)
    o_ref[0] = _dot(s.astype(BF16), w_ref[0].astype(BF16)) + b_ref[0]


def _modulation(cvecs, w_mod, b_mod):
    L, D, N = w_mod.shape
    tn = 1536
    return pl.pallas_call(
        _mod_kernel,
        grid=(L, N // tn),
        in_specs=[pl.BlockSpec((8, D), lambda l, n: (0, 0)),
                  pl.BlockSpec((1, D, tn), lambda l, n: (l, 0, n)),
                  pl.BlockSpec((1, 1, tn), lambda l, n: (l, 0, n))],
        out_specs=pl.BlockSpec((1, 8, tn), lambda l, n: (l, 0, n)),
        out_shape=jax.ShapeDtypeStruct((L, 8, N), F32),
        compiler_params=_cparams(2),
    )(cvecs, w_mod, b_mod.reshape(L, 1, N))


def _tok_tiles(S, T):
    assert T % TMT == 0 and 0 < S - T <= TMT
    return T // TMT + 1, T // TMT


def _tok_spec(width, col_block=0):
    return pl.BlockSpec((1, TMT, width), lambda b, i: (b, i, col_block))


def _mod_spec(nlat, layer):
    return pl.BlockSpec((None, 1, 1, 8, D_MODEL), lambda b, i: (layer, b, jnp.where(i < nlat, 1, 0), 0, 0))


def _halo_specs(width, col_block, S):
    per = TMT // HALO
    last = S // HALO - 1
    prev = pl.BlockSpec((1, HALO, width), lambda b, i: (b, jnp.maximum(i * per - 1, 0), col_block))
    nxt = pl.BlockSpec((1, HALO, width), lambda b, i: (b, jnp.minimum((i + 1) * per, last), col_block))
    return prev, nxt


def _neighbour_ok(i, nlat):
    prev_ok = jnp.logical_and(i != 0, i != nlat)
    next_ok = i < nlat - 1
    return prev_ok, next_ok


def _rows_in_tile(i, nlat, tc):
    return jnp.where(i < nlat, TMT, tc)


def _in_proj_kernel(x_ref, mod_ref, g_ref, w_ref, z_ref):
    m = mod_ref[0, 0]
    h = _norm_mod(x_ref[0], g_ref[...], m[0:1], m[1:2])
    z_ref[0] = _dot(h.astype(BF16), w_ref[...])


def _in_proj(xs, modtab, g_pre, w_in_p, T, layer):
    B, S, D = xs.shape
    ntiles, nlat = _tok_tiles(S, T)
    return pl.pallas_call(
        _in_proj_kernel,
        grid=(B, ntiles),
        in_specs=[_tok_spec(D), _mod_spec(nlat, layer), _const_spec((1, D), layer),
                  _const_spec((D, Z_COLS), layer)],
        out_specs=_tok_spec(Z_COLS),
        out_shape=jax.ShapeDtypeStruct((B, S, Z_COLS), F32),
        compiler_params=_cparams(2),
    )(xs, modtab, g_pre, w_in_p)


def _out_proj_kernel(x_ref, na_ref, mla_ref, yf_ref, yb_ref, bon_ref, g_ref, of_ref, ob_ref, og_ref,
                     mod_ref, gpost_ref, w_ref, lnw_ref, lnb_ref, gn_ref, ones_ref, o_ref):
    m = mod_ref[0, 0]
    ones = ones_ref[...]
    inv_n = 1.0 / HEAD_DIM
    y = yf_ref[0] + yb_ref[0]
    yc = y - _dot_sel_rhs(y, ones) * inv_n
    var = _dot_sel_rhs(yc * yc, ones) * inv_n
    y_rw = (yc * lax.rsqrt(var + RW_GN_EPS) * lnw_ref[...] + lnb_ref[...] + bon_ref[0]) * g_ref[0]
    o = of_ref[0] + ob_ref[0]
    ms = _dot_sel_rhs(o * o, ones) * inv_n
    og = og_ref[0]
    y_gla = (o * lax.rsqrt(ms + EPS) * gn_ref[...]) * (og * _silu_gain(og))
    y = jnp.concatenate([na_ref[0], mla_ref[0], y_rw.astype(BF16), y_gla.astype(BF16)], axis=-1)
    y = _dot(y, w_ref[...])
    o_ref[0] = x_ref[0] + m[2:3] * (_rms(y) * gpost_ref[...])


def _out_proj(xs, z, y_na, y_mla, yf, yb, bon, g, of, ob, modtab, g_post, w_out_b, ln_w, ln_b, gla_norm,
              ones_b, T, layer):
    B, S, D = xs.shape
    ntiles, nlat = _tok_tiles(S, T)
    grp = _tok_spec(GROUP_W)
    vec = _const_spec((1, GROUP_W), layer)
    return pl.pallas_call(
        _out_proj_kernel,
        grid=(B, ntiles),
        in_specs=[_tok_spec(D), grp, grp, grp, grp, grp, grp, grp, grp,
                  _tok_spec(GROUP_W, COL_GLA_O // GROUP_W),
                  _mod_spec(nlat, layer), _const_spec((1, D), layer), _const_spec((D, D), layer), vec, vec, vec,
                  _const_spec((GROUP_W, GROUP_W))],
        out_specs=_tok_spec(D),
        out_shape=jax.ShapeDtypeStruct((B, S, D), F32),
        compiler_params=_cparams(2),
    )(xs, y_na, y_mla, yf, yb, bon, g, of, ob, z, modtab, g_post, w_out_b, ln_w, ln_b, gla_norm, ones_b)


def _ffn_kernel(xp_ref, x_ref, xn_ref, mod_ref, gpre_ref, gpost_ref, wup_ref, cw_ref, cb_ref,
                wdn_ref, o_ref, *, nlat, tc):
    i = pl.program_id(1)
    prev_ok, next_ok = _neighbour_ok(i, nlat)
    m = mod_ref[0, 0]
    x = x_ref[0]
    xe = jnp.concatenate([xp_ref[0], x, xn_ref[0]], axis=0)
    n = TMT + 2 * HALO
    h = _norm_mod(xe, gpre_ref[...], m[3:4], m[4:5])
    row = lax.broadcasted_iota(jnp.int32, (n, 1), 0)
    valid = jnp.logical_or(jnp.logical_and(row >= HALO, row < HALO + _rows_in_tile(i, nlat, tc)),
                           jnp.logical_or(jnp.logical_and(row < HALO, prev_ok),
                                          jnp.logical_and(row >= HALO + TMT, next_ok)))
    hb = jnp.where(valid, h, 0.0).astype(BF16)
    nchunks = D_FF // FF_CHUNK

    def up(c):
        return [_dot(hb, wup_ref[:, base + c * FF_CHUNK:base + (c + 1) * FF_CHUNK]) for base in (0, D_FF)]

    def conv(z, lo):
        cw = cw_ref[:, lo:lo + FF_CHUNK]
        return (cb_ref[:, lo:lo + FF_CHUNK]
                + pltpu.roll(z, 1, 0)[HALO:HALO + TMT] * cw[0:1]
                + z[HALO:HALO + TMT] * cw[1:2]
                + pltpu.roll(z, n - 1, 0)[HALO:HALO + TMT] * cw[2:3])

    acc = None
    group = []
    z_next = up(0)
    for c in range(nchunks):
        z_val, z_gate = z_next
        if c + 1 < nchunks:
            z_next = up(c + 1)
        val = conv(z_val, c * FF_CHUNK)
        gate = conv(z_gate, D_FF + c * FF_CHUNK)
        group.append(((gate * _silu_gain(gate)) * val).astype(BF16))
        if len(group) == FF_GROUP or c + 1 == nchunks:
            lo = (c + 1 - len(group)) * FF_CHUNK
            part = _dot(jnp.concatenate(group, axis=1), wdn_ref[lo:(c + 1) * FF_CHUNK, :])
            acc = part if acc is None else acc + part
            group = []
    o_ref[0] = x + m[5:6] * (_rms(acc) * gpost_ref[...])


def _ffn(xs, modtab, g_pre, g_post, w_up_b, conv_w, conv_b, w_dn_b, T, layer, latent_only):
    B, S, D = xs.shape
    ntiles, nlat = _tok_tiles(S, T)
    prev, nxt = _halo_specs(D, 0, S)
    return pl.pallas_call(
        functools.partial(_ffn_kernel, nlat=nlat, tc=S - T),
        grid=(B, nlat if latent_only else ntiles),
        in_specs=[prev, _tok_spec(D), nxt, _mod_spec(nlat, layer), _const_spec((1, D), layer),
                  _const_spec((1, D), layer), _const_spec((D, 2 * D_FF), layer),
                  _const_spec((3, 2 * D_FF), layer), _const_spec((1, 2 * D_FF), layer),
                  _const_spec((D_FF, D), layer)],
        out_specs=_tok_spec(D),
        out_shape=jax.ShapeDtypeStruct((B, T if latent_only else S, D), F32),
        compiler_params=_cparams(2),
    )(xs, xs, xs, modtab, g_pre, g_post, w_up_b, conv_w, conv_b, w_dn_b)


def _natten_bias(rpb, rows):
    rt = TQ // GRID_W
    j = np.arange(GRID_W)
    col_start = np.clip(j - NA_COLS // 2, 0, GRID_W - NA_COLS)
    col_in = (j[None, :] >= col_start[:, None]) & (j[None, :] < col_start[:, None] + NA_COLS)
    edge = GRID_W - NA_COLS
    ext = jnp.concatenate([jnp.repeat(rpb[..., :1], edge, axis=-1), rpb,
                           jnp.repeat(rpb[..., -1:], edge, axis=-1)], axis=-1).astype(F32) * LOG2E
    bq = jnp.stack([ext[..., GRID_W - 1 - q:2 * GRID_W - 1 - q] for q in range(GRID_W)], axis=2)
    L = rpb.shape[0]
    bq = jnp.where(col_in[:, None, :], bq, -jnp.inf).reshape(L, N_HEADS, GRID_W, -1)
    cases = []
    for r0 in (0, rt, rows - rt):
        us = min(max(r0 - NA_ROWS // 2, 0), rows - NA_UNION)
        per_row = []
        for r in range(r0, r0 + rt):
            rs = min(max(r - NA_ROWS // 2, 0), rows - NA_ROWS)
            first = rs - r + (NA_ROWS - 1)
            pre, post = rs - us, us + NA_UNION - (rs + NA_ROWS)
            per_row.append(jnp.pad(bq[..., first * GRID_W:(first + NA_ROWS) * GRID_W],
                                   ((0, 0), (0, 0), (0, 0), (pre * GRID_W, post * GRID_W)),
                                   constant_values=-jnp.inf))
        cases.append(jnp.concatenate(per_row, axis=2))
    return jnp.stack(cases, axis=1)


def _natten_kernel(q_ref, k_ref, v_ref, qc_ref, kc_ref, vc_ref, bias_ref, o_ref, *, nlat, rows):
    j = pl.program_id(1)
    lane_h = _lane_head(GROUP_W, HEAD_DIM)
    kct = kc_ref[0].T.astype(BF16)
    vc = vc_ref[0].astype(BF16)
    scale = HEAD_DIM ** -0.5 * LOG2E
    nwin = NA_UNION * GRID_W

    @pl.when(j < nlat)
    def _():
        us = jnp.clip(j * (TQ // GRID_W) - NA_ROWS // 2, 0, rows - NA_UNION)
        start = pl.multiple_of(us * GRID_W, GRID_W)
        q = q_ref[0] * scale
        kwt = k_ref[0, pl.ds(start, nwin), :].T.astype(BF16)
        vw = v_ref[0, pl.ds(start, nwin), :].astype(BF16)
        acc = jnp.zeros((TQ, GROUP_W), F32)

        def logits(h):
            qh = jnp.where(lane_h == h, q, 0.0).astype(BF16)
            return _dot(qh, kwt), _dot(qh, kct)

        s_next = logits(0)
        for h in range(N_HEADS):
            hm = lane_h == h
            s_w, s_c = s_next
            if h + 1 < N_HEADS:
                s_next = logits(h + 1)
            s_w = s_w + bias_ref[0, h]
            mx = jnp.maximum(jnp.max(s_w, axis=-1, keepdims=True), jnp.max(s_c, axis=-1, keepdims=True))
            p_w = jnp.exp2(s_w - mx)
            p_c = jnp.exp2(s_c - mx)
            den = jnp.sum(p_w, axis=-1, keepdims=True) + jnp.sum(p_c, axis=-1, keepdims=True)
            o = _dot(p_w.astype(BF16), vw) + _dot(p_c.astype(BF16), vc)
            acc = acc + jnp.where(hm, o * (1.0 / den), 0.0)
        o_ref[0] = acc.astype(o_ref.dtype)

    @pl.when(j >= nlat)
    def _():
        q = qc_ref[0] * scale
        acc = jnp.zeros((TQ, GROUP_W), F32)
        for h in range(N_HEADS):
            hm = lane_h == h
            s = _dot(jnp.where(hm, q, 0.0).astype(BF16), kct)
            p = jnp.exp2(s - jnp.max(s, axis=-1, keepdims=True))
            den = jnp.sum(p, axis=-1, keepdims=True)
            acc = acc + jnp.where(hm, _dot(p.astype(BF16), vc) * (1.0 / den), 0.0)
        o_ref[0] = acc.astype(o_ref.dtype)


def _natten(z, bias, T, layer):
    B, S, _ = z.shape
    nlat = T // TQ
    rows = T // GRID_W
    assert rows >= 16 and rows % (TQ // GRID_W) == 0 and S - T == TQ
    cb = COL_NA // GROUP_W
    lat = lambda c: pl.BlockSpec((1, T, GROUP_W), lambda b, j: (b, 0, cb + c))
    ctx = lambda c: pl.BlockSpec((1, TQ, GROUP_W), lambda b, j: (b, nlat, cb + c))
    case = lambda j: jnp.where(j == 0, 0, jnp.where(j >= nlat - 1, 2, 1))
    return pl.pallas_call(
        functools.partial(_natten_kernel, nlat=nlat, rows=rows),
        grid=(B, nlat + 1),
        in_specs=[pl.BlockSpec((1, TQ, GROUP_W), lambda b, j: (b, jnp.minimum(j, nlat - 1), cb)),
                  lat(1), lat(2), ctx(0), ctx(1), ctx(2),
                  pl.BlockSpec((None, 1, N_HEADS, TQ, NA_UNION * GRID_W), lambda b, j: (layer, case(j), 0, 0, 0))],
        out_specs=pl.BlockSpec((1, TQ, GROUP_W), lambda b, j: (b, j, 0)),
        out_shape=jax.ShapeDtypeStruct((B, S, GROUP_W), BF16),
        compiler_params=_cparams(2),
    )(z, z, z, z, z, z, bias)


def _rope_tables(T, Tc):
    t = np.arange(T)
    row = (t // GRID_W).astype(np.float32)
    col = (t % GRID_W).astype(np.float32)
    d = MLA_ROPE // 2
    inv = (np.float32(ROPE_THETA) ** (-np.arange(0, d, 2, dtype=np.float32) / np.float32(d))).astype(np.float32)
    cs, sn = [], []
    for pos in (row, col):
        ang = (pos[:, None] * inv[None, :]).astype(np.float32)
        cs += [np.cos(ang), np.cos(ang)]
        sn += [-np.sin(ang), np.sin(ang)]
    pad = MLA_HEAD_PAD - MLA_NOPE - MLA_ROPE
    f32 = np.float32
    cos = np.concatenate([np.ones((T, MLA_NOPE), f32)] + cs + [np.ones((T, pad), f32)], axis=1)
    sin = np.concatenate([np.zeros((T, MLA_NOPE), f32)] + sn + [np.zeros((T, pad), f32)], axis=1)
    cos = np.concatenate([cos, np.ones((Tc, MLA_HEAD_PAD), f32)], axis=0)
    sin = np.concatenate([sin, np.zeros((Tc, MLA_HEAD_PAD), f32)], axis=0)
    return jnp.asarray(cos, F32), jnp.asarray(sin, F32)


def _rope_swap_perm():
    q = MLA_ROPE // 4
    return np.concatenate([np.arange(q, 2 * q), np.arange(0, q), np.arange(3 * q, 4 * q), np.arange(2 * q, 3 * q)])


def _mla_weights(w_uq, w_ukv):
    L = w_uq.shape[0]
    wq = w_uq.reshape(L, MLA_Q_RANK, N_HEADS, MLA_NOPE + MLA_ROPE)
    pad = MLA_HEAD_PAD - MLA_NOPE - MLA_ROPE
    zq = jnp.zeros((L, MLA_Q_RANK, N_HEADS, pad), F32)
    wq1 = jnp.concatenate([wq, zq], axis=-1).reshape(L, MLA_Q_RANK, N_HEADS * MLA_HEAD_PAD)
    rope_sw = wq[..., MLA_NOPE:][..., _rope_swap_perm()]
    wq2 = jnp.concatenate([jnp.zeros((L, MLA_Q_RANK, N_HEADS, MLA_NOPE), F32), rope_sw, zq], axis=-1)
    wq2 = wq2.reshape(L, MLA_Q_RANK, N_HEADS * MLA_HEAD_PAD)
    wkv = w_ukv.reshape(L, MLA_KV_RANK, N_HEADS, 2 * MLA_NOPE)
    wk = jnp.concatenate([wkv[..., :MLA_NOPE], jnp.zeros((L, MLA_KV_RANK, N_HEADS, MLA_HEAD_PAD - MLA_NOPE), F32)], axis=-1)
    wk = wk.reshape(L, MLA_KV_RANK, N_HEADS * MLA_HEAD_PAD)
    wv_t = wkv[..., MLA_NOPE:].reshape(L, MLA_KV_RANK, N_HEADS * MLA_NOPE).transpose(0, 2, 1)
    return wq1.astype(BF16), wq2.astype(BF16), wk.astype(BF16), wv_t.astype(BF16)


def _rope_place():
    e = np.zeros((MLA_ROPE, N_HEADS * MLA_HEAD_PAD), np.float32)
    for h in range(N_HEADS):
        e[np.arange(MLA_ROPE), h * MLA_HEAD_PAD + MLA_NOPE + np.arange(MLA_ROPE)] = 1.0
    return jnp.asarray(e, BF16)


def _mla_up_kernel(z_ref, cos_ref, sin_ref, qn_ref, kvn_ref, wq1_ref, wq2_ref, wk_ref, wv_ref, e_ref,
                   qt_ref, k_ref, vt_ref):
    z = z_ref[0]
    ckv = z[:, :MLA_KV_RANK]
    cq = z[:, MLA_KV_RANK:MLA_KV_RANK + MLA_Q_RANK]
    kr = z[:, MLA_KV_RANK + MLA_Q_RANK:MLA_KV_RANK + MLA_Q_RANK + MLA_ROPE]
    krs = z[:, MLA_KV_RANK + MLA_Q_RANK + MLA_ROPE:]
    cos = jnp.concatenate([cos_ref[...]] * N_HEADS, axis=-1)
    sin = jnp.concatenate([sin_ref[...]] * N_HEADS, axis=-1)
    nq = (_rms(cq) * qn_ref[...]).astype(BF16)
    nkv_f = _rms(ckv) * kvn_ref[...]
    nkv = nkv_f.astype(BF16)
    q = _dot(nq, wq1_ref[...]) * cos + _dot(nq, wq2_ref[...]) * sin
    scale = (MLA_NOPE + MLA_ROPE) ** -0.5 * LOG2E
    qt_ref[0] = (q * scale).T.astype(BF16)
    k = _dot(nkv, wk_ref[...]) + _dot_sel_rhs(kr, e_ref[...]) * cos + _dot_sel_rhs(krs, e_ref[...]) * sin
    k_ref[0] = k.astype(BF16)
    vt_ref[0] = _dot(wv_ref[...], nkv_f.T.astype(BF16)).astype(BF16)


def _mla_up(z, cos, sin, q_norm, kv_norm, wq1, wq2, wk, wv, place, T, layer):
    B, S, _ = z.shape
    ntiles, _ = _tok_tiles(S, T)
    HP = N_HEADS * MLA_HEAD_PAD
    zw = MLA_KV_RANK + MLA_Q_RANK + 2 * MLA_ROPE
    tab = pl.BlockSpec((TMT, MLA_HEAD_PAD), lambda b, i: (i, 0))
    return pl.pallas_call(
        _mla_up_kernel,
        grid=(B, ntiles),
        in_specs=[_tok_spec(zw, COL_MLA // zw), tab, tab,
                  _const_spec((1, MLA_Q_RANK), layer), _const_spec((1, MLA_KV_RANK), layer),
                  _const_spec((MLA_Q_RANK, HP), layer), _const_spec((MLA_Q_RANK, HP), layer),
                  _const_spec((MLA_KV_RANK, HP), layer), _const_spec((GROUP_W, MLA_KV_RANK), layer),
                  _const_spec((MLA_ROPE, HP))],
        out_specs=[pl.BlockSpec((1, HP, TMT), lambda b, i: (b, 0, i)), _tok_spec(HP),
                   pl.BlockSpec((1, GROUP_W, TMT), lambda b, i: (b, 0, i))],
        out_shape=[jax.ShapeDtypeStruct((B, HP, S), BF16),
                   jax.ShapeDtypeStruct((B, S, HP), BF16),
                   jax.ShapeDtypeStruct((B, GROUP_W, S), BF16)],
        compiler_params=_cparams(2),
    )(z, cos, sin, q_norm, kv_norm, wq1, wq2, wk, wv, place)


def _mla_attn_kernel(qt_ref, k_ref, vt_ref, o_ref, *, nlat, T):
    j = pl.program_id(1)

    def attend(lo, hi):
        def logits(h):
            hp = slice(h * MLA_HEAD_PAD, (h + 1) * MLA_HEAD_PAD)
            return _dot(k_ref[0, lo:hi, hp], qt_ref[0, hp, :])

        ahead = 2
        pending = [logits(h) for h in range(ahead)]
        pieces = []
        for h in range(N_HEADS):
            s = pending.pop(0)
            if h + ahead < N_HEADS:
                pending.append(logits(h + ahead))
            p = jnp.exp2(s - jnp.max(s, axis=0, keepdims=True))
            den = jnp.sum(p, axis=0, keepdims=True)
            o = _dot(vt_ref[0, h * HEAD_DIM:(h + 1) * HEAD_DIM, lo:hi], p.astype(BF16))
            pieces.append(o * (1.0 / den))
        o_ref[0] = jnp.concatenate(pieces, axis=0).T.astype(o_ref.dtype)

    @pl.when(j < nlat)
    def _():
        attend(0, T + TQ)

    @pl.when(j >= nlat)
    def _():
        attend(T, T + TQ)


def _mla_attn(qt, k, vt, T):
    B, S, HP = k.shape
    assert T % MLA_TQ == 0 and S - T <= MLA_TQ
    nlat = T // MLA_TQ
    return pl.pallas_call(
        functools.partial(_mla_attn_kernel, nlat=nlat, T=T),
        grid=(B, nlat + 1),
        in_specs=[pl.BlockSpec((1, HP, MLA_TQ), lambda b, j: (b, 0, j)),
                  pl.BlockSpec((1, S, HP), lambda b, j: (b, 0, 0)),
                  pl.BlockSpec((1, GROUP_W, S), lambda b, j: (b, 0, 0))],
        out_specs=pl.BlockSpec((1, MLA_TQ, GROUP_W), lambda b, j: (b, j, 0)),
        out_shape=jax.ShapeDtypeStruct((B, S, GROUP_W), BF16),
        compiler_params=_cparams(2),
    )(qt, k, vt)


def _scan_masks():
    t = np.arange(CHUNK)
    inc = np.stack([t[:, None] >= t[None, :], t[:, None] <= t[None, :]]).astype(np.float32)
    strict = np.stack([t[:, None] > t[None, :], t[:, None] < t[None, :]]).astype(np.float32)
    return jnp.asarray(np.tile(inc, (1, 1, N_HEADS))), jnp.asarray(np.tile(strict, (1, 1, N_HEADS)))


def _cumsum_rows(x, reverse):
    n = x.shape[0]
    row = lax.broadcasted_iota(jnp.int32, (n, 1), 0)
    sh = 1
    while sh < n:
        if reverse:
            x = x + jnp.where(row < n - sh, pltpu.roll(x, n - sh, 0), 0.0)
        else:
            x = x + jnp.where(row >= sh, pltpu.roll(x, sh, 0), 0.0)
        sh *= 2
    return x


def _block_diag(x, bm_b):
    return _tile_rows(x.astype(BF16)) * bm_b


def _dot_nt(a, b):
    return lax.dot_general(a, b, (((1,), (1,)), ((), ())), preferred_element_type=F32)


def _fwd_tile(p, nlat):
    return jnp.where(p == 0, nlat, p - 1)


def _bwd_tile(p, nlat):
    return jnp.where(p == 0, nlat, nlat - p)


def _chunk_order(d):
    nch = TQ // CHUNK
    return range(nch) if d == 0 else range(nch - 1, -1, -1)


def _stack_heads(x, lane_h):
    return jnp.concatenate([jnp.where(lane_h == h, x, 0.0) for h in range(N_HEADS)], axis=0)


def _tile_rows(x):
    return jnp.concatenate([x] * N_HEADS, axis=0)


def _collapse_heads(x):
    c = x.shape[0] // N_HEADS
    return (x[0:c] + x[c:2 * c]) + (x[2 * c:3 * c] + x[3 * c:4 * c])


def _rw_prep_kernel(zp_ref, z_ref, zn_ref, mu_ref, kk_ref, ka_ref, rk_ref, w0_ref, a0_ref, wup_ref,
                    aup_ref, gup_ref, ones_ref,
                    r_ref, v_ref, ah_ref, g_ref, bon_ref, lw_ref, kd_ref, bd_ref, *, nlat, tc):
    i = pl.program_id(1)
    prev_ok, next_ok = _neighbour_ok(i, nlat)
    z = z_ref[0]
    row = lax.broadcasted_iota(jnp.int32, (TMT, 1), 0)
    before = jnp.where(prev_ok, zp_ref[0, HALO - 1:HALO, :], 0.0)
    after = jnp.where(next_ok, zn_ref[0, 0:1, :], 0.0)
    zprev = jnp.where(row == 0, before, pltpu.roll(z, 1, 0))
    znext = jnp.where(row == _rows_in_tile(i, nlat, tc) - 1, after, pltpu.roll(z, TMT - 1, 0))
    zs = z + mu_ref[...] * (0.5 * (zprev + znext) - z)
    r = zs[:, 0:GROUP_W]
    k = zs[:, GROUP_W:2 * GROUP_W]
    v = zs[:, 2 * GROUP_W:3 * GROUP_W]
    low = zs[:, 3 * GROUP_W:3 * GROUP_W + 128]
    gd = zs[:, 3 * GROUP_W + 128:]
    ones = ones_ref[...]
    kk = k * kk_ref[...]
    kk = kk * lax.rsqrt(_dot_sel_rhs(kk * kk, ones) + 1e-12)
    wl = _dot(jnp.tanh(low).astype(BF16), wup_ref[...])
    al = _dot(low.astype(BF16), aup_ref[...])
    ksum = jnp.zeros((TMT, GROUP_W), F32)
    for d in range(2):
        w_raw = -_softplus(-(w0_ref[d:d + 1, :] + wl[:, d * GROUP_W:(d + 1) * GROUP_W])) - 0.5
        lw_ref[d, 0] = -jnp.exp(w_raw)
        a = _sigmoid(a0_ref[d:d + 1, :] + al[:, d * GROUP_W:(d + 1) * GROUP_W])
        kd = k * (1.0 + (a - 1.0) * ka_ref[...])
        kd_ref[d, 0] = kd
        bd_ref[d, 0] = kk * a
        ksum = ksum + kd
    r_ref[0] = r
    v_ref[0] = v
    ah_ref[0] = -kk
    g_ref[0] = _dot(_sigmoid(gd).astype(BF16), gup_ref[...])
    bon_ref[0] = _dot_sel_rhs(r * ksum * rk_ref[...], ones) * v


def _rw_prep(z, mu, k_k, k_a, r_k, w0, a0, wup_p, aup_p, gup_b, ones_b, T, layer):
    B, S, _ = z.shape
    ntiles, nlat = _tok_tiles(S, T)
    W = 4 * GROUP_W
    cb = COL_RW // W
    prev, nxt = _halo_specs(W, cb, S)
    one = _tok_spec(GROUP_W)
    two = pl.BlockSpec((2, 1, TMT, GROUP_W), lambda b, i: (0, b, i, 0))
    s1 = jax.ShapeDtypeStruct((B, S, GROUP_W), F32)
    s2 = jax.ShapeDtypeStruct((2, B, S, GROUP_W), F32)
    vec = _const_spec((1, GROUP_W), layer)
    return pl.pallas_call(
        functools.partial(_rw_prep_kernel, nlat=nlat, tc=S - T),
        grid=(B, ntiles),
        in_specs=[prev, _tok_spec(W, cb), nxt,
                  _const_spec((1, W), layer), vec, vec, vec,
                  _const_spec((2, GROUP_W), layer), _const_spec((2, GROUP_W), layer),
                  _const_spec((128, 2 * GROUP_W), layer), _const_spec((128, 2 * GROUP_W), layer),
                  _const_spec((128, GROUP_W), layer), _const_spec((GROUP_W, GROUP_W))],
        out_specs=[one, one, one, one, one, two, two, two],
        out_shape=[s1, s1, s1, s1, s1, s2, s2, s2],
        compiler_params=_cparams(2),
    )(z, z, z, mu, k_k, k_a, r_k, w0, a0, wup_p, aup_p, gup_b, ones_b)


def _rw_scan_kernel(rf_ref, rb_ref, vf_ref, vb_ref, af_ref, ab_ref, lwf_ref, lwb_ref, kdf_ref, kdb_ref,
                    bdf_ref, bdb_ref, inc_ref, strict_ref, bm_ref, yf_ref, yb_ref, s_ref):
    p = pl.program_id(1)

    @pl.when(p == 0)
    def _():
        s_ref[...] = jnp.zeros_like(s_ref)

    n = N_HEADS * CHUNK
    bm = bm_ref[...]
    eye = (lax.broadcasted_iota(jnp.int32, (CHUNK, n), 1) % CHUNK
           == lax.broadcasted_iota(jnp.int32, (CHUNK, n), 0)).astype(F32)
    refs = ((rf_ref, vf_ref, af_ref, lwf_ref, kdf_ref, bdf_ref, yf_ref),
            (rb_ref, vb_ref, ab_ref, lwb_ref, kdb_ref, bdb_ref, yb_ref))
    orders = [list(_chunk_order(d)) for d in range(2)]
    bm_b = bm.astype(BF16)
    bd_of = lambda m: _block_diag(m, bm_b)

    def prepare(inst, out):
        idx = range(len(inst))
        rows = [pl.ds(orders[d][step] * CHUNK, CHUNK) for _, step, d in inst]
        ar, r_t, v_s, a_s, b_s, k_s, bk_t, decay, v_in = [], [], [], [], [], [], [], [], []
        for (bi, step, d), sl in zip(inst, rows):
            r_ref, v_ref, a_ref, lw_ref, kd_ref, bd_ref, _ = refs[d]
            lw = lw_ref[0, bi, sl, :]
            kd = kd_ref[0, bi, sl, :]
            bd = bd_ref[0, bi, sl, :]
            v = v_ref[bi, sl, :]
            cs = _cumsum_rows(lw, reverse=(d == 1))
            tot = cs[CHUNK - 1:CHUNK, :] if d == 0 else cs[0:1, :]
            e_neg = jnp.exp(-cs)
            e_hat = jnp.exp(tot - cs)
            a_t = a_ref[bi, sl, :] * jnp.exp(cs - lw)
            rt = r_ref[bi, sl, :] * jnp.exp(cs)
            ar.append(jnp.concatenate([a_t, rt], axis=0).astype(BF16))
            r_t.append(rt)
            v_in.append(v)
            v_s.append(bd_of(v))
            a_s.append(bd_of(a_t))
            b_s.append(bd_of(bd * e_neg))
            k_s.append(bd_of(kd * e_neg))
            t = jnp.concatenate([bd * e_hat, kd * e_hat,
                                 jnp.broadcast_to(jnp.exp(tot), (2 * CHUNK, GROUP_W))], axis=0).T
            bk_t.append(t[:, :2 * CHUNK].astype(BF16))
            decay.append(jnp.concatenate([t[:, 2 * CHUNK:]] * 2, axis=1))
        yield
        g_b = [_dot_nt(ar[i], b_s[i]) for i in idx]
        g_k = [_dot_nt(ar[i], k_s[i]) for i in idx]
        yield
        pw = [g_b[i][:CHUNK] * strict_ref[inst[i][2]] for i in idx]
        ak = [(g_k[i][:CHUNK] * strict_ref[inst[i][2]]).astype(BF16) for i in idx]
        rbk = [jnp.concatenate([g_b[i][CHUNK:] * inc_ref[inst[i][2]], g_k[i][CHUNK:] * inc_ref[inst[i][2]]],
                               axis=1).astype(BF16) for i in idx]
        x = [eye + pw[i] for i in idx]
        pw = [_dot(pw[i].astype(BF16), bd_of(pw[i])) for i in idx]
        yield
        for _ in range(4):
            px = [_dot(jnp.concatenate([pw[i], x[i]], axis=0).astype(BF16), bd_of(pw[i])) for i in idx]
            x = [x[i] + px[i][CHUNK:] for i in idx]
            pw = [px[i][:CHUNK] for i in idx]
            yield
        x = [(x[i] + _dot(x[i].astype(BF16), bd_of(pw[i]))).astype(BF16) for i in idx]
        yield
        akv = [_dot(ak[i], v_s[i]) for i in idx]
        yield
        p12 = [_dot(x[i], jnp.concatenate([bd_of(akv[i]), a_s[i]], axis=1)) for i in idx]
        yield
        p1_s = [bd_of(p12[i][:, :GROUP_W]) for i in idx]
        p2_s = [bd_of(p12[i][:, GROUP_W:]) for i in idx]
        q = [r_t[i] + _dot(rbk[i][:, :n], p2_s[i]) for i in idx]
        y_c = [_dot(rbk[i], jnp.concatenate([p1_s[i], v_s[i]], axis=0)) for i in idx]
        yield
        zero = jnp.zeros((CHUNK, GROUP_W), F32)
        gc = [_dot(bk_t[i], jnp.concatenate(
            [jnp.concatenate([p12[i][:, GROUP_W:], p12[i][:, :GROUP_W]], axis=1),
             jnp.concatenate([zero, v_in[i]], axis=1)], axis=0).astype(BF16)) for i in idx]
        for i in idx:
            gq = jnp.concatenate([gc[i][:, :GROUP_W] * bm, q[i]], axis=0).astype(BF16)
            out.append((rows[i], inst[i][0], inst[i][2], gq, gc[i][:, GROUP_W:] * bm, y_c[i], decay[i]))

    nb = s_ref.shape[0] // 2
    s = [s_ref[k] for k in range(2 * nb)]

    def recur(item):
        sl, bi, d, gq, c_s, y_c, decay = item
        k = 2 * bi + d
        m = _dot(gq, s[k].astype(BF16))
        refs[d][6][bi, sl, :] = m[n:] + y_c
        s[k] = decay * s[k] + m[:n] + c_s

    done = []
    for _ in prepare([(bi, step, d) for step in range(TQ // CHUNK) for bi in range(nb) for d in range(2)], done):
        pass
    for item in done:
        recur(item)
    for k in range(2 * nb):
        s_ref[k] = s[k]


def _rw_scan(r, v, ah, lw, kd, bd, masks, bm, T):
    B, S, _ = r.shape
    nlat = T // TQ
    inc, strict = masks
    n = N_HEADS * CHUNK
    nb = RW_SCAN_BATCH if B % RW_SCAN_BATCH == 0 else 1
    fwd = pl.BlockSpec((nb, TQ, GROUP_W), lambda b, p: (b, _fwd_tile(p, nlat), 0))
    bwd = pl.BlockSpec((nb, TQ, GROUP_W), lambda b, p: (b, _bwd_tile(p, nlat), 0))
    fwd2 = pl.BlockSpec((1, nb, TQ, GROUP_W), lambda b, p: (0, b, _fwd_tile(p, nlat), 0))
    bwd2 = pl.BlockSpec((1, nb, TQ, GROUP_W), lambda b, p: (1, b, _bwd_tile(p, nlat), 0))
    out = jax.ShapeDtypeStruct((B, S, GROUP_W), F32)
    return pl.pallas_call(
        _rw_scan_kernel,
        grid=(B // nb, nlat + 1),
        in_specs=[fwd, bwd, fwd, bwd, fwd, bwd, fwd2, bwd2, fwd2, bwd2, fwd2, bwd2,
                  _const_spec((2, CHUNK, n)), _const_spec((2, CHUNK, n)), _const_spec((n, n))],
        out_specs=[fwd, bwd],
        out_shape=[out, out],
        scratch_shapes=[pltpu.VMEM((2 * nb, n, GROUP_W), F32)],
        compiler_params=_cparams(2),
    )(r, r, v, v, ah, ah, lw, lw, kd, kd, bd, bd, inc, strict, bm)


def _gla_scan_kernel(qf_ref, qb_ref, kf_ref, kb_ref, gf_ref, gb_ref, vf_ref, vb_ref, gup_ref, gbias_ref,
                     inc_ref, bm_ref, bmv_ref, of_ref, ob_ref, s_ref):
    p = pl.program_id(1)

    @pl.when(p == 0)
    def _():
        s_ref[...] = jnp.zeros_like(s_ref)

    wk = N_HEADS * GLA_DK
    n = N_HEADS * CHUNK
    bm = bm_ref[...]
    bmv = bmv_ref[...].astype(BF16)
    bmk = (lax.broadcasted_iota(jnp.int32, (n, wk), 0) // CHUNK
           == lax.broadcasted_iota(jnp.int32, (n, wk), 1) // GLA_DK).astype(F32).astype(BF16)
    refs = ((qf_ref, kf_ref, gf_ref, vf_ref, of_ref), (qb_ref, kb_ref, gb_ref, vb_ref, ob_ref))
    orders = [list(_chunk_order(d)) for d in range(2)]
    nb = s_ref.shape[0] // 2
    inst = [(bi, step, d) for step in range(TQ // CHUNK) for bi in range(nb) for d in range(2)]
    idx = range(len(inst))
    rows = [pl.ds(orders[d][step] * CHUNK, CHUNK) for _, step, d in inst]

    qe, ke_s, ks4, v_s, decay = [], [], [], [], []
    for (bi, step, d), sl in zip(inst, rows):
        q_ref, k_ref, g_ref, v_ref, _ = refs[d]
        k = k_ref[bi, sl, :]
        la = -_softplus(-(_dot(g_ref[bi, sl, :].astype(BF16), gup_ref[d]) + gbias_ref[d])) * (1.0 / GLA_TAU)
        b = _cumsum_rows(la, reverse=(d == 1))
        tot = b[CHUNK - 1:CHUNK, :] if d == 0 else b[0:1, :]
        qe.append((q_ref[bi, sl, :] * (GLA_DK ** -0.5) * jnp.exp(b)).astype(BF16))
        ke_s.append(_block_diag(k * jnp.exp(-b), bmk))
        ks4.append(_tile_rows(k * jnp.exp(tot - b)).T.astype(BF16))
        v_s.append(_block_diag(v_ref[bi, sl, :], bmv))
        decay.append(_tile_rows(jnp.broadcast_to(jnp.exp(tot), (CHUNK, wk))).T)
    a_cat = [(_dot_nt(qe[i], ke_s[i]) * inc_ref[inst[i][2]]).astype(BF16) for i in idx]
    o_in = [_dot(a_cat[i], v_s[i]) for i in idx]
    kv = [_dot(ks4[i], v_s[i]) * bm for i in idx]

    s = [s_ref[k] for k in range(2 * nb)]
    for i, ((bi, step, d), sl) in enumerate(zip(inst, rows)):
        k = 2 * bi + d
        refs[d][4][bi, sl, :] = o_in[i] + _dot(qe[i], s[k].astype(BF16))
        s[k] = decay[i] * s[k] + kv[i]
    for k in range(2 * nb):
        s_ref[k] = s[k]


def _gla_scan(z, gup_p, gb, masks, bm, bmv, T, layer):
    B, S, _ = z.shape
    nlat = T // TQ
    inc, _ = masks
    n = N_HEADS * CHUNK
    wk = N_HEADS * GLA_DK
    nb = GLA_SCAN_BATCH if B % GLA_SCAN_BATCH == 0 else 1
    fwd = lambda w, col: pl.BlockSpec((nb, TQ, w), lambda b, p: (b, _fwd_tile(p, nlat), col // w))
    bwd = lambda w, col: pl.BlockSpec((nb, TQ, w), lambda b, p: (b, _bwd_tile(p, nlat), col // w))
    out = jax.ShapeDtypeStruct((B, S, GROUP_W), F32)
    return pl.pallas_call(
        _gla_scan_kernel,
        grid=(B // nb, nlat + 1),
        in_specs=[fwd(wk, COL_GLA_Q), bwd(wk, COL_GLA_Q), fwd(wk, COL_GLA_K), bwd(wk, COL_GLA_K),
                  fwd(wk, COL_GLA_G), bwd(wk, COL_GLA_G), fwd(GROUP_W, COL_GLA_V), bwd(GROUP_W, COL_GLA_V),
                  _const_spec((2, wk, wk), layer), _const_spec((2, 1, wk), layer),
                  _const_spec((2, CHUNK, n)), _const_spec((wk, GROUP_W)), _const_spec((n, n))],
        out_specs=[fwd(GROUP_W, 0), bwd(GROUP_W, 0)],
        out_shape=[out, out],
        scratch_shapes=[pltpu.VMEM((2 * nb, wk, GROUP_W), F32)],
        compiler_params=_cparams(2),
    )(z, z, z, z, z, z, z, z, gup_p, gb, inc, bm, bmv)


def _pack_w_in(w_in):
    L, D, _ = w_in.shape
    na, mla, rw, gla = jnp.split(w_in.astype(BF16), [768, 1120, 2144], axis=-1)
    cq, ckv, kr = jnp.split(mla, [MLA_Q_RANK, MLA_Q_RANK + MLA_KV_RANK], axis=-1)
    gq, gk, gv, gg, go = jnp.split(gla, [128, 256, 512, 528], axis=-1)
    pad = jnp.zeros((L, D, COL_GLA_V - COL_GLA_G - gg.shape[-1]), BF16)
    packed = jnp.concatenate([na, ckv, cq, kr, kr[..., _rope_swap_perm()], gq, gk, gg, pad, gv, go, rw], axis=-1)
    assert packed.shape[-1] == Z_COLS
    return packed


def kernel(x, c, ctx, c_ctx, w_mod, b_mod, g_mix_pre, g_mix_post, g_ffn_pre, g_ffn_post, w_in, w_out, na_rpb, mla_q_norm, mla_w_uq, mla_kv_norm, mla_w_ukv, rw_mu, rw_w0, rw_w_up, rw_a0, rw_a_up, rw_g_up, rw_k_k, rw_k_a, rw_r_k, rw_ln_w, rw_ln_b, gla_gate_up, gla_gate_b, gla_norm, ffn_w_up, ffn_conv_w, ffn_conv_b, ffn_w_down):
    B, T, D = x.shape
    Tc = ctx.shape[1]
    L = w_in.shape[0]
    assert D == D_MODEL and Tc == TQ and T % TMT == 0 and B + 1 <= 8

    w_in_p = _pack_w_in(w_in)
    w_out_b = w_out.astype(BF16)
    wq1, wq2, wk, wv = _mla_weights(mla_w_uq, mla_w_ukv)
    place = _rope_place()
    cos, sin = _rope_tables(T, Tc)
    zero_lo = jnp.zeros((L, 64, 2 * GROUP_W), F32)
    rw_wup_p = jnp.concatenate([jnp.concatenate([rw_w_up[:, 0], rw_w_up[:, 1]], axis=-1), zero_lo], axis=1).astype(BF16)
    rw_aup_p = jnp.concatenate([zero_lo, jnp.concatenate([rw_a_up[:, 0], rw_a_up[:, 1]], axis=-1)], axis=1).astype(BF16)
    rw_gup_b = rw_g_up.astype(BF16)
    wk_gla = N_HEADS * GLA_DK
    gla_gup_p = jnp.concatenate([gla_gate_up, jnp.zeros((L, 2, wk_gla - gla_gate_up.shape[2], wk_gla), F32)], axis=2).astype(BF16)
    ffn_up_b = ffn_w_up.astype(BF16)
    ffn_dn_b = ffn_w_down.astype(BF16)
    ones_b = jnp.asarray(_block_ones(GROUP_W, HEAD_DIM), BF16)
    rw_bm = jnp.asarray(_block_ones(N_HEADS * CHUNK, CHUNK))
    gla_bm = jnp.asarray((np.arange(wk_gla)[:, None] // GLA_DK == np.arange(GROUP_W)[None, :] // HEAD_DIM).astype(np.float32))
    masks = _scan_masks()

    cvecs = jnp.zeros((8, D), F32).at[:B].set(c).at[B].set(c_ctx)
    mods = _modulation(cvecs, w_mod, b_mod).reshape(L, 8, 6, D)
    mods = jnp.pad(mods, ((0, 0), (0, 0), (0, 2), (0, 0)))
    modtabs = jnp.stack([jnp.broadcast_to(mods[:, B:B + 1], (L, B, 8, D)), mods[:, :B]], axis=2)

    xs = jnp.concatenate([x, ctx], axis=1)
    rows = lambda a: a.reshape(L, 1, -1)
    na_bias = _natten_bias(na_rpb, T // GRID_W)
    gla_gb = gla_gate_b[:, :, None, :]
    for i in range(L):
        z = _in_proj(xs, modtabs, rows(g_mix_pre), w_in_p, T, i)
        y_na = _natten(z, na_bias, T, i)
        qt, k, vt = _mla_up(z, cos, sin, rows(mla_q_norm), rows(mla_kv_norm), wq1, wq2, wk, wv, place, T, i)
        y_mla = _mla_attn(qt, k, vt, T)
        r, vv, ah, g, bon, lw, kd, bd = _rw_prep(z, rows(rw_mu), rows(rw_k_k), rows(rw_k_a), rows(rw_r_k),
                                                 rw_w0, rw_a0, rw_wup_p, rw_aup_p, rw_gup_b, ones_b, T, i)
        yf, yb = _rw_scan(r, vv, ah, lw, kd, bd, masks, rw_bm, T)
        of, ob = _gla_scan(z, gla_gup_p, gla_gb, masks, gla_bm, rw_bm, T, i)
        xs = _out_proj(xs, z, y_na, y_mla, yf, yb, bon, g, of, ob, modtabs, rows(g_mix_post), w_out_b,
                       rows(rw_ln_w), rows(rw_ln_b), rows(gla_norm), ones_b, T, i)
        xs = _ffn(xs, modtabs, rows(g_ffn_pre), rows(g_ffn_post), ffn_up_b, ffn_conv_w,
                  rows(ffn_conv_b), ffn_dn_b, T, i, latent_only=(i == L - 1))
    return xs
```

```python
import functools

import numpy as np
import jax
import jax.numpy as jnp
from jax import lax
from jax.experimental import pallas as pl
from jax.experimental.pallas import tpu as pltpu

F32 = jnp.float32
BF16 = jnp.bfloat16

D_MODEL = 1024
GRID_W = 64
EPS = 1e-6
LOG2E = 1.4426950408889634
N_HEADS = 4
HEAD_DIM = 64
GROUP_W = 256
NA_ROWS = 8
NA_COLS = 16
NA_UNION = 12
MLA_Q_RANK = 192
MLA_KV_RANK = 128
MLA_NOPE = 64
MLA_ROPE = 32
MLA_HEAD_PAD = 128
ROPE_THETA = 10000.0
RW_GN_EPS = 64e-5
GLA_DK = 32
GLA_TAU = 16.0
D_FF = 2816
CHUNK = 64
TQ = 256
MLA_TQ = 512
RW_SCAN_BATCH = 2
GLA_SCAN_BATCH = 4
TMT = 512
HALO = 8
FF_CHUNK = 256
FF_GROUP = 4
Z_COLS = 3072

COL_NA = 0
COL_MLA = 768
COL_GLA_Q = 1152
COL_GLA_K = 1280
COL_GLA_G = 1408
COL_GLA_V = 1536
COL_GLA_O = 1792
COL_RW = 2048

VMEM_LIMIT_V7X = 56 * 1024 * 1024


def _cparams(n_axes):
    return pltpu.CompilerParams(dimension_semantics=("arbitrary",) * n_axes,
                                vmem_limit_bytes=VMEM_LIMIT_V7X)


def _const_spec(shape, layer=None):
    nd = len(shape)
    if layer is None:
        return pl.BlockSpec(shape, lambda *_: (0,) * nd, pipeline_mode=pl.Buffered(1))
    return pl.BlockSpec((None,) + tuple(shape), lambda *_: (layer,) + (0,) * nd, pipeline_mode=pl.Buffered(1))


def _dot(a, b):
    return jnp.dot(a, b, preferred_element_type=F32)


def _split3(x):
    hi = x.astype(BF16)
    r1 = x - hi.astype(F32)
    mid = r1.astype(BF16)
    lo = (r1 - mid.astype(F32)).astype(BF16)
    return hi, mid, lo


def _dot_sel_rhs(x, m):
    hi, mid, lo = _split3(x)
    return _dot(hi, m) + (_dot(mid, m) + _dot(lo, m))


def _sigmoid(x):
    return 1.0 / (1.0 + jnp.exp(-x))


def _silu_gain(x):
    return 0.5 + 0.5 * jnp.tanh(0.5 * x)


def _softplus(x):
    return jnp.maximum(x, 0.0) + jnp.log1p(jnp.exp(-jnp.abs(x)))


def _rms(x):
    return x * lax.rsqrt(jnp.mean(x * x, axis=-1, keepdims=True) + EPS)


def _norm_mod(x, g, shift, scale):
    return (_rms(x) * g) * (1.0 + scale) + shift


def _lane_head(width, per_head):
    return lax.broadcasted_iota(jnp.int32, (1, width), 1) // per_head


def _block_ones(n, blk):
    i = np.arange(n) // blk
    return (i[:, None] == i[None, :]).astype(np.float32)


def _mod_kernel(c_ref, w_ref, b_ref, o_ref):
    cv = c_ref[...]
    s = cv * _sigmoid(cv)
    o_ref[0] = _dot(s.astype(BF16), w_ref[0].astype(BF16)) + b_ref[0]


def _modulation(cvecs, w_mod, b_mod):
    L, D, N = w_mod.shape
    tn = 1536
    return pl.pallas_call(
        _mod_kernel,
        grid=(L, N // tn),
        in_specs=[pl.BlockSpec((8, D), lambda l, n: (0, 0)),
                  pl.BlockSpec((1, D, tn), lambda l, n: (l, 0, n)),
                  pl.BlockSpec((1, 1, tn), lambda l, n: (l, 0, n))],
        out_specs=pl.BlockSpec((1, 8, tn), lambda l, n: (l, 0, n)),
        out_shape=jax.ShapeDtypeStruct((L, 8, N), F32),
        compiler_params=_cparams(2),
    )(cvecs, w_mod, b_mod.reshape(L, 1, N))


def _tok_tiles(S, T):
    assert T % TMT == 0 and 0 < S - T <= TMT
    return T // TMT + 1, T // TMT


def _tok_spec(width, col_block=0):
    return pl.BlockSpec((1, TMT, width), lambda b, i: (b, i, col_block))


def _mod_spec(nlat, layer):
    return pl.BlockSpec((None, 1, 1, 8, D_MODEL), lambda b, i: (layer, b, jnp.where(i < nlat, 1, 0), 0, 0))


def _halo_specs(width, col_block, S):
    per = TMT // HALO
    last = S // HALO - 1
    prev = pl.BlockSpec((1, HALO, width), lambda b, i: (b, jnp.maximum(i * per - 1, 0), col_block))
    nxt = pl.BlockSpec((1, HALO, width), lambda b, i: (b, jnp.minimum((i + 1) * per, last), col_block))
    return prev, nxt


def _neighbour_ok(i, nlat):
    prev_ok = jnp.logical_and(i != 0, i != nlat)
    next_ok = i < nlat - 1
    return prev_ok, next_ok


def _rows_in_tile(i, nlat, tc):
    return jnp.where(i < nlat, TMT, tc)


def _in_proj_kernel(x_ref, mod_ref, g_ref, w_ref, z_ref):
    m = mod_ref[0, 0]
    h = _norm_mod(x_ref[0], g_ref[...], m[0:1], m[1:2])
    z_ref[0] = _dot(h.astype(BF16), w_ref[...])


def _in_proj(xs, modtab, g_pre, w_in_p, T, layer):
    B, S, D = xs.shape
    ntiles, nlat = _tok_tiles(S, T)
    return pl.pallas_call(
        _in_proj_kernel,
        grid=(B, ntiles),
        in_specs=[_tok_spec(D), _mod_spec(nlat, layer), _const_spec((1, D), layer),
                  _const_spec((D, Z_COLS), layer)],
        out_specs=_tok_spec(Z_COLS),
        out_shape=jax.ShapeDtypeStruct((B, S, Z_COLS), F32),
        compiler_params=_cparams(2),
    )(xs, modtab, g_pre, w_in_p)


def _out_proj_kernel(x_ref, na_ref, mla_ref, yf_ref, yb_ref, bon_ref, g_ref, of_ref, ob_ref, og_ref,
                     mod_ref, gpost_ref, w_ref, lnw_ref, lnb_ref, gn_ref, ones_ref, o_ref):
    m = mod_ref[0, 0]
    ones = ones_ref[...]
    inv_n = 1.0 / HEAD_DIM
    y = yf_ref[0] + yb_ref[0]
    yc = y - _dot_sel_rhs(y, ones) * inv_n
    var = _dot_sel_rhs(yc * yc, ones) * inv_n
    y_rw = (yc * lax.rsqrt(var + RW_GN_EPS) * lnw_ref[...] + lnb_ref[...] + bon_ref[0]) * g_ref[0]
    o = of_ref[0] + ob_ref[0]
    ms = _dot_sel_rhs(o * o, ones) * inv_n
    og = og_ref[0]
    y_gla = (o * lax.rsqrt(ms + EPS) * gn_ref[...]) * (og * _silu_gain(og))
    y = jnp.concatenate([na_ref[0], mla_ref[0], y_rw.astype(BF16), y_gla.astype(BF16)], axis=-1)
    y = _dot(y, w_ref[...])
    o_ref[0] = x_ref[0] + m[2:3] * (_rms(y) * gpost_ref[...])


def _out_proj(xs, z, y_na, y_mla, yf, yb, bon, g, of, ob, modtab, g_post, w_out_b, ln_w, ln_b, gla_norm,
              ones_b, T, layer):
    B, S, D = xs.shape
    ntiles, nlat = _tok_tiles(S, T)
    grp = _tok_spec(GROUP_W)
    vec = _const_spec((1, GROUP_W), layer)
    return pl.pallas_call(
        _out_proj_kernel,
        grid=(B, ntiles),
        in_specs=[_tok_spec(D), grp, grp, grp, grp, grp, grp, grp, grp,
                  _tok_spec(GROUP_W, COL_GLA_O // GROUP_W),
                  _mod_spec(nlat, layer), _const_spec((1, D), layer), _const_spec((D, D), layer), vec, vec, vec,
                  _const_spec((GROUP_W, GROUP_W))],
        out_specs=_tok_spec(D),
        out_shape=jax.ShapeDtypeStruct((B, S, D), F32),
        compiler_params=_cparams(2),
    )(xs, y_na, y_mla, yf, yb, bon, g, of, ob, z, modtab, g_post, w_out_b, ln_w, ln_b, gla_norm, ones_b)


def _ffn_kernel(xp_ref, x_ref, xn_ref, mod_ref, gpre_ref, gpost_ref, wup_ref, cw_ref, cb_ref,
                wdn_ref, o_ref, *, nlat, tc):
    i = pl.program_id(1)
    prev_ok, next_ok = _neighbour_ok(i, nlat)
    m = mod_ref[0, 0]
    x = x_ref[0]
    xe = jnp.concatenate([xp_ref[0], x, xn_ref[0]], axis=0)
    n = TMT + 2 * HALO
    h = _norm_mod(xe, gpre_ref[...], m[3:4], m[4:5])
    row = lax.broadcasted_iota(jnp.int32, (n, 1), 0)
    valid = jnp.logical_or(jnp.logical_and(row >= HALO, row < HALO + _rows_in_tile(i, nlat, tc)),
                           jnp.logical_or(jnp.logical_and(row < HALO, prev_ok),
                                          jnp.logical_and(row >= HALO + TMT, next_ok)))
    hb = jnp.where(valid, h, 0.0).astype(BF16)
    nchunks = D_FF // FF_CHUNK

    def up(c):
        return [_dot(hb, wup_ref[:, base + c * FF_CHUNK:base + (c + 1) * FF_CHUNK]) for base in (0, D_FF)]

    def conv(z, lo):
        cw = cw_ref[:, lo:lo + FF_CHUNK]
        return (cb_ref[:, lo:lo + FF_CHUNK]
                + pltpu.roll(z, 1, 0)[HALO:HALO + TMT] * cw[0:1]
                + z[HALO:HALO + TMT] * cw[1:2]
                + pltpu.roll(z, n - 1, 0)[HALO:HALO + TMT] * cw[2:3])

    acc = None
    group = []
    z_next = up(0)
    for c in range(nchunks):
        z_val, z_gate = z_next
        if c + 1 < nchunks:
            z_next = up(c + 1)
        val = conv(z_val, c * FF_CHUNK)
        gate = conv(z_gate, D_FF + c * FF_CHUNK)
        group.append(((gate * _silu_gain(gate)) * val).astype(BF16))
        if len(group) == FF_GROUP or c + 1 == nchunks:
            lo = (c + 1 - len(group)) * FF_CHUNK
            part = _dot(jnp.concatenate(group, axis=1), wdn_ref[lo:(c + 1) * FF_CHUNK, :])
            acc = part if acc is None else acc + part
            group = []
    o_ref[0] = x + m[5:6] * (_rms(acc) * gpost_ref[...])


def _ffn(xs, modtab, g_pre, g_post, w_up_b, conv_w, conv_b, w_dn_b, T, layer, latent_only):
    B, S, D = xs.shape
    ntiles, nlat = _tok_tiles(S, T)
    prev, nxt = _halo_specs(D, 0, S)
    return pl.pallas_call(
        functools.partial(_ffn_kernel, nlat=nlat, tc=S - T),
        grid=(B, nlat if latent_only else ntiles),
        in_specs=[prev, _tok_spec(D), nxt, _mod_spec(nlat, layer), _const_spec((1, D), layer),
                  _const_spec((1, D), layer), _const_spec((D, 2 * D_FF), layer),
                  _const_spec((3, 2 * D_FF), layer), _const_spec((1, 2 * D_FF), layer),
                  _const_spec((D_FF, D), layer)],
        out_specs=_tok_spec(D),
        out_shape=jax.ShapeDtypeStruct((B, T if latent_only else S, D), F32),
        compiler_params=_cparams(2),
    )(xs, xs, xs, modtab, g_pre, g_post, w_up_b, conv_w, conv_b, w_dn_b)


def _natten_bias(rpb, rows):
    rt = TQ // GRID_W
    j = np.arange(GRID_W)
    col_start = np.clip(j - NA_COLS // 2, 0, GRID_W - NA_COLS)
    col_in = (j[None, :] >= col_start[:, None]) & (j[None, :] < col_start[:, None] + NA_COLS)
    edge = GRID_W - NA_COLS
    ext = jnp.concatenate([jnp.repeat(rpb[..., :1], edge, axis=-1), rpb,
                           jnp.repeat(rpb[..., -1:], edge, axis=-1)], axis=-1).astype(F32) * LOG2E
    bq = jnp.stack([ext[..., GRID_W - 1 - q:2 * GRID_W - 1 - q] for q in range(GRID_W)], axis=2)
    L = rpb.shape[0]
    bq = jnp.where(col_in[:, None, :], bq, -jnp.inf).reshape(L, N_HEADS, GRID_W, -1)
    cases = []
    for r0 in (0, rt, rows - rt):
        us = min(max(r0 - NA_ROWS // 2, 0), rows - NA_UNION)
        per_row = []
        for r in range(r0, r0 + rt):
            rs = min(max(r - NA_ROWS // 2, 0), rows - NA_ROWS)
            first = rs - r + (NA_ROWS - 1)
            pre, post = rs - us, us + NA_UNION - (rs + NA_ROWS)
            per_row.append(jnp.pad(bq[..., first * GRID_W:(first + NA_ROWS) * GRID_W],
                                   ((0, 0), (0, 0), (0, 0), (pre * GRID_W, post * GRID_W)),
                                   constant_values=-jnp.inf))
        cases.append(jnp.concatenate(per_row, axis=2))
    return jnp.stack(cases, axis=1)


def _natten_kernel(q_ref, k_ref, v_ref, qc_ref, kc_ref, vc_ref, bias_ref, o_ref, *, nlat, rows):
    j = pl.program_id(1)
    lane_h = _lane_head(GROUP_W, HEAD_DIM)
    kct = kc_ref[0].T.astype(BF16)
    vc = vc_ref[0].astype(BF16)
    scale = HEAD_DIM ** -0.5 * LOG2E
    nwin = NA_UNION * GRID_W

    @pl.when(j < nlat)
    def _():
        us = jnp.clip(j * (TQ // GRID_W) - NA_ROWS // 2, 0, rows - NA_UNION)
        start = pl.multiple_of(us * GRID_W, GRID_W)
        q = q_ref[0] * scale
        kwt = k_ref[0, pl.ds(start, nwin), :].T.astype(BF16)
        vw = v_ref[0, pl.ds(start, nwin), :].astype(BF16)
        acc = jnp.zeros((TQ, GROUP_W), F32)

        def logits(h):
            qh = jnp.where(lane_h == h, q, 0.0).astype(BF16)
            return _dot(qh, kwt), _dot(qh, kct)

        s_next = logits(0)
        for h in range(N_HEADS):
            hm = lane_h == h
            s_w, s_c = s_next
            if h + 1 < N_HEADS:
                s_next = logits(h + 1)
            s_w = s_w + bias_ref[0, h]
            mx = jnp.maximum(jnp.max(s_w, axis=-1, keepdims=True), jnp.max(s_c, axis=-1, keepdims=True))
            p_w = jnp.exp2(s_w - mx)
            p_c = jnp.exp2(s_c - mx)
            den = jnp.sum(p_w, axis=-1, keepdims=True) + jnp.sum(p_c, axis=-1, keepdims=True)
            o = _dot(p_w.astype(BF16), vw) + _dot(p_c.astype(BF16), vc)
            acc = acc + jnp.where(hm, o * (1.0 / den), 0.0)
        o_ref[0] = acc.astype(o_ref.dtype)

    @pl.when(j >= nlat)
    def _():
        q = qc_ref[0] * scale
        acc = jnp.zeros((TQ, GROUP_W), F32)
        for h in range(N_HEADS):
            hm = lane_h == h
            s = _dot(jnp.where(hm, q, 0.0).astype(BF16), kct)
            p = jnp.exp2(s - jnp.max(s, axis=-1, keepdims=True))
            den = jnp.sum(p, axis=-1, keepdims=True)
            acc = acc + jnp.where(hm, _dot(p.astype(BF16), vc) * (1.0 / den), 0.0)
        o_ref[0] = acc.astype(o_ref.dtype)


def _natten(z, bias, T, layer):
    B, S, _ = z.shape
    nlat = T // TQ
    rows = T // GRID_W
    assert rows >= 16 and rows % (TQ // GRID_W) == 0 and S - T == TQ
    cb = COL_NA // GROUP_W
    lat = lambda c: pl.BlockSpec((1, T, GROUP_W), lambda b, j: (b, 0, cb + c))
    ctx = lambda c: pl.BlockSpec((1, TQ, GROUP_W), lambda b, j: (b, nlat, cb + c))
    case = lambda j: jnp.where(j == 0, 0, jnp.where(j >= nlat - 1, 2, 1))
    return pl.pallas_call(
        functools.partial(_natten_kernel, nlat=nlat, rows=rows),
        grid=(B, nlat + 1),
        in_specs=[pl.BlockSpec((1, TQ, GROUP_W), lambda b, j: (b, jnp.minimum(j, nlat - 1), cb)),
                  lat(1), lat(2), ctx(0), ctx(1), ctx(2),
                  pl.BlockSpec((None, 1, N_HEADS, TQ, NA_UNION * GRID_W), lambda b, j: (layer, case(j), 0, 0, 0))],
        out_specs=pl.BlockSpec((1, TQ, GROUP_W), lambda b, j: (b, j, 0)),
        out_shape=jax.ShapeDtypeStruct((B, S, GROUP_W), BF16),
        compiler_params=_cparams(2),
    )(z, z, z, z, z, z, bias)


def _rope_tables(T, Tc):
    t = np.arange(T)
    row = (t // GRID_W).astype(np.float32)
    col = (t % GRID_W).astype(np.float32)
    d = MLA_ROPE // 2
    inv = (np.float32(ROPE_THETA) ** (-np.arange(0, d, 2, dtype=np.float32) / np.float32(d))).astype(np.float32)
    cs, sn = [], []
    for pos in (row, col):
        ang = (pos[:, None] * inv[None, :]).astype(np.float32)
        cs += [np.cos(ang), np.cos(ang)]
        sn += [-np.sin(ang), np.sin(ang)]
    pad = MLA_HEAD_PAD - MLA_NOPE - MLA_ROPE
    f32 = np.float32
    cos = np.concatenate([np.ones((T, MLA_NOPE), f32)] + cs + [np.ones((T, pad), f32)], axis=1)
    sin = np.concatenate([np.zeros((T, MLA_NOPE), f32)] + sn + [np.zeros((T, pad), f32)], axis=1)
    cos = np.concatenate([cos, np.ones((Tc, MLA_HEAD_PAD), f32)], axis=0)
    sin = np.concatenate([sin, np.zeros((Tc, MLA_HEAD_PAD), f32)], axis=0)
    return jnp.asarray(cos, F32), jnp.asarray(sin, F32)


def _rope_swap_perm():
    q = MLA_ROPE // 4
    return np.concatenate([np.arange(q, 2 * q), np.arange(0, q), np.arange(3 * q, 4 * q), np.arange(2 * q, 3 * q)])


def _mla_weights(w_uq, w_ukv):
    L = w_uq.shape[0]
    wq = w_uq.reshape(L, MLA_Q_RANK, N_HEADS, MLA_NOPE + MLA_ROPE)
    pad = MLA_HEAD_PAD - MLA_NOPE - MLA_ROPE
    zq = jnp.zeros((L, MLA_Q_RANK, N_HEADS, pad), F32)
    wq1 = jnp.concatenate([wq, zq], axis=-1).reshape(L, MLA_Q_RANK, N_HEADS * MLA_HEAD_PAD)
    rope_sw = wq[..., MLA_NOPE:][..., _rope_swap_perm()]
    wq2 = jnp.concatenate([jnp.zeros((L, MLA_Q_RANK, N_HEADS, MLA_NOPE), F32), rope_sw, zq], axis=-1)
    wq2 = wq2.reshape(L, MLA_Q_RANK, N_HEADS * MLA_HEAD_PAD)
    wkv = w_ukv.reshape(L, MLA_KV_RANK, N_HEADS, 2 * MLA_NOPE)
    wk = jnp.concatenate([wkv[..., :MLA_NOPE], jnp.zeros((L, MLA_KV_RANK, N_HEADS, MLA_HEAD_PAD - MLA_NOPE), F32)], axis=-1)
    wk = wk.reshape(L, MLA_KV_RANK, N_HEADS * MLA_HEAD_PAD)
    wv_t = wkv[..., MLA_NOPE:].reshape(L, MLA_KV_RANK, N_HEADS * MLA_NOPE).transpose(0, 2, 1)
    return wq1.astype(BF16), wq2.astype(BF16), wk.astype(BF16), wv_t.astype(BF16)


def _rope_place():
    e = np.zeros((MLA_ROPE, N_HEADS * MLA_HEAD_PAD), np.float32)
    for h in range(N_HEADS):
        e[np.arange(MLA_ROPE), h * MLA_HEAD_PAD + MLA_NOPE + np.arange(MLA_ROPE)] = 1.0
    return jnp.asarray(e, BF16)


def _mla_up_kernel(z_ref, cos_ref, sin_ref, qn_ref, kvn_ref, wq1_ref, wq2_ref, wk_ref, wv_ref, e_ref,
                   qt_ref, k_ref, vt_ref):
    z = z_ref[0]
    ckv = z[:, :MLA_KV_RANK]
    cq = z[:, MLA_KV_RANK:MLA_KV_RANK + MLA_Q_RANK]
    kr = z[:, MLA_KV_RANK + MLA_Q_RANK:MLA_KV_RANK + MLA_Q_RANK + MLA_ROPE]
    krs = z[:, MLA_KV_RANK + MLA_Q_RANK + MLA_ROPE:]
    cos = jnp.concatenate([cos_ref[...]] * N_HEADS, axis=-1)
    sin = jnp.concatenate([sin_ref[...]] * N_HEADS, axis=-1)
    nq = (_rms(cq) * qn_ref[...]).astype(BF16)
    nkv_f = _rms(ckv) * kvn_ref[...]
    nkv = nkv_f.astype(BF16)
    q = _dot(nq, wq1_ref[...]) * cos + _dot(nq, wq2_ref[...]) * sin
    scale = (MLA_NOPE + MLA_ROPE) ** -0.5 * LOG2E
    qt_ref[0] = (q * scale).T.astype(BF16)
    k = _dot(nkv, wk_ref[...]) + _dot_sel_rhs(kr, e_ref[...]) * cos + _dot_sel_rhs(krs, e_ref[...]) * sin
    k_ref[0] = k.astype(BF16)
    vt_ref[0] = _dot(wv_ref[...], nkv_f.T.astype(BF16)).astype(BF16)


def _mla_up(z, cos, sin, q_norm, kv_norm, wq1, wq2, wk, wv, place, T, layer):
    B, S, _ = z.shape
    ntiles, _ = _tok_tiles(S, T)
    HP = N_HEADS * MLA_HEAD_PAD
    zw = MLA_KV_RANK + MLA_Q_RANK + 2 * MLA_ROPE
    tab = pl.BlockSpec((TMT, MLA_HEAD_PAD), lambda b, i: (i, 0))
    return pl.pallas_call(
        _mla_up_kernel,
        grid=(B, ntiles),
        in_specs=[_tok_spec(zw, COL_MLA // zw), tab, tab,
                  _const_spec((1, MLA_Q_RANK), layer), _const_spec((1, MLA_KV_RANK), layer),
                  _const_spec((MLA_Q_RANK, HP), layer), _const_spec((MLA_Q_RANK, HP), layer),
                  _const_spec((MLA_KV_RANK, HP), layer), _const_spec((GROUP_W, MLA_KV_RANK), layer),
                  _const_spec((MLA_ROPE, HP))],
        out_specs=[pl.BlockSpec((1, HP, TMT), lambda b, i: (b, 0, i)), _tok_spec(HP),
                   pl.BlockSpec((1, GROUP_W, TMT), lambda b, i: (b, 0, i))],
        out_shape=[jax.ShapeDtypeStruct((B, HP, S), BF16),
                   jax.ShapeDtypeStruct((B, S, HP), BF16),
                   jax.ShapeDtypeStruct((B, GROUP_W, S), BF16)],
        compiler_params=_cparams(2),
    )(z, cos, sin, q_norm, kv_norm, wq1, wq2, wk, wv, place)


def _mla_attn_kernel(qt_ref, k_ref, vt_ref, o_ref, *, nlat, T):
    j = pl.program_id(1)

    def attend(lo, hi):
        def logits(h):
            hp = slice(h * MLA_HEAD_PAD, (h + 1) * MLA_HEAD_PAD)
            return _dot(k_ref[0, lo:hi, hp], qt_ref[0, hp, :])

        ahead = 2
        pending = [logits(h) for h in range(ahead)]
        pieces = []
        for h in range(N_HEADS):
            s = pending.pop(0)
            if h + ahead < N_HEADS:
                pending.append(logits(h + ahead))
            p = jnp.exp2(s - jnp.max(s, axis=0, keepdims=True))
            den = jnp.sum(p, axis=0, keepdims=True)
            o = _dot(vt_ref[0, h * HEAD_DIM:(h + 1) * HEAD_DIM, lo:hi], p.astype(BF16))
            pieces.append(o * (1.0 / den))
        o_ref[0] = jnp.concatenate(pieces, axis=0).T.astype(o_ref.dtype)

    @pl.when(j < nlat)
    def _():
        attend(0, T + TQ)

    @pl.when(j >= nlat)
    def _():
        attend(T, T + TQ)


def _mla_attn(qt, k, vt, T):
    B, S, HP = k.shape
    assert T % MLA_TQ == 0 and S - T <= MLA_TQ
    nlat = T // MLA_TQ
    return pl.pallas_call(
        functools.partial(_mla_attn_kernel, nlat=nlat, T=T),
        grid=(B, nlat + 1),
        in_specs=[pl.BlockSpec((1, HP, MLA_TQ), lambda b, j: (b, 0, j)),
                  pl.BlockSpec((1, S, HP), lambda b, j: (b, 0, 0)),
                  pl.BlockSpec((1, GROUP_W, S), lambda b, j: (b, 0, 0))],
        out_specs=pl.BlockSpec((1, MLA_TQ, GROUP_W), lambda b, j: (b, j, 0)),
        out_shape=jax.ShapeDtypeStruct((B, S, GROUP_W), BF16),
        compiler_params=_cparams(2),
    )(qt, k, vt)


def _scan_masks():
    t = np.arange(CHUNK)
    inc = np.stack([t[:, None] >= t[None, :], t[:, None] <= t[None, :]]).astype(np.float32)
    strict = np.stack([t[:, None] > t[None, :], t[:, None] < t[None, :]]).astype(np.float32)
    return jnp.asarray(np.tile(inc, (1, 1, N_HEADS))), jnp.asarray(np.tile(strict, (1, 1, N_HEADS)))


def _cumsum_rows(x, reverse):
    n = x.shape[0]
    row = lax.broadcasted_iota(jnp.int32, (n, 1), 0)
    sh = 1
    while sh < n:
        if reverse:
            x = x + jnp.where(row < n - sh, pltpu.roll(x, n - sh, 0), 0.0)
        else:
            x = x + jnp.where(row >= sh, pltpu.roll(x, sh, 0), 0.0)
        sh *= 2
    return x


def _block_diag(x, bm_b):
    return _tile_rows(x.astype(BF16)) * bm_b


def _dot_nt(a, b):
    return lax.dot_general(a, b, (((1,), (1,)), ((), ())), preferred_element_type=F32)


def _fwd_tile(p, nlat):
    return jnp.where(p == 0, nlat, p - 1)


def _bwd_tile(p, nlat):
    return jnp.where(p == 0, nlat, nlat - p)


def _chunk_order(d):
    nch = TQ // CHUNK
    return range(nch) if d == 0 else range(nch - 1, -1, -1)


def _tile_rows(x):
    return jnp.concatenate([x] * N_HEADS, axis=0)


def _rw_prep_kernel(zp_ref, z_ref, zn_ref, mu_ref, kk_ref, ka_ref, rk_ref, w0_ref, a0_ref, wup_ref,
                    aup_ref, gup_ref, ones_ref,
                    r_ref, v_ref, ah_ref, g_ref, bon_ref, lw_ref, kd_ref, bd_ref, *, nlat, tc):
    i = pl.program_id(1)
    prev_ok, next_ok = _neighbour_ok(i, nlat)
    z = z_ref[0]
    row = lax.broadcasted_iota(jnp.int32, (TMT, 1), 0)
    before = jnp.where(prev_ok, zp_ref[0, HALO - 1:HALO, :], 0.0)
    after = jnp.where(next_ok, zn_ref[0, 0:1, :], 0.0)
    zprev = jnp.where(row == 0, before, pltpu.roll(z, 1, 0))
    znext = jnp.where(row == _rows_in_tile(i, nlat, tc) - 1, after, pltpu.roll(z, TMT - 1, 0))
    zs = z + mu_ref[...] * (0.5 * (zprev + znext) - z)
    r = zs[:, 0:GROUP_W]
    k = zs[:, GROUP_W:2 * GROUP_W]
    v = zs[:, 2 * GROUP_W:3 * GROUP_W]
    low = zs[:, 3 * GROUP_W:3 * GROUP_W + 128]
    gd = zs[:, 3 * GROUP_W + 128:]
    ones = ones_ref[...]
    kk = k * kk_ref[...]
    kk = kk * lax.rsqrt(_dot_sel_rhs(kk * kk, ones) + 1e-12)
    wl = _dot(jnp.tanh(low).astype(BF16), wup_ref[...])
    al = _dot(low.astype(BF16), aup_ref[...])
    ksum = jnp.zeros((TMT, GROUP_W), F32)
    for d in range(2):
        w_raw = -_softplus(-(w0_ref[d:d + 1, :] + wl[:, d * GROUP_W:(d + 1) * GROUP_W])) - 0.5
        lw_ref[d, 0] = -jnp.exp(w_raw)
        a = _sigmoid(a0_ref[d:d + 1, :] + al[:, d * GROUP_W:(d + 1) * GROUP_W])
        kd = k * (1.0 + (a - 1.0) * ka_ref[...])
        kd_ref[d, 0] = kd
        bd_ref[d, 0] = kk * a
        ksum = ksum + kd
    r_ref[0] = r
    v_ref[0] = v
    ah_ref[0] = -kk
    g_ref[0] = _dot(_sigmoid(gd).astype(BF16), gup_ref[...])
    bon_ref[0] = _dot_sel_rhs(r * ksum * rk_ref[...], ones) * v


def _rw_prep(z, mu, k_k, k_a, r_k, w0, a0, wup_p, aup_p, gup_b, ones_b, T, layer):
    B, S, _ = z.shape
    ntiles, nlat = _tok_tiles(S, T)
    W = 4 * GROUP_W
    cb = COL_RW // W
    prev, nxt = _halo_specs(W, cb, S)
    one = _tok_spec(GROUP_W)
    two = pl.BlockSpec((2, 1, TMT, GROUP_W), lambda b, i: (0, b, i, 0))
    s1 = jax.ShapeDtypeStruct((B, S, GROUP_W), F32)
    s2 = jax.ShapeDtypeStruct((2, B, S, GROUP_W), F32)
    vec = _const_spec((1, GROUP_W), layer)
    return pl.pallas_call(
        functools.partial(_rw_prep_kernel, nlat=nlat, tc=S - T),
        grid=(B, ntiles),
        in_specs=[prev, _tok_spec(W, cb), nxt,
                  _const_spec((1, W), layer), vec, vec, vec,
                  _const_spec((2, GROUP_W), layer), _const_spec((2, GROUP_W), layer),
                  _const_spec((128, 2 * GROUP_W), layer), _const_spec((128, 2 * GROUP_W), layer),
                  _const_spec((128, GROUP_W), layer), _const_spec((GROUP_W, GROUP_W))],
        out_specs=[one, one, one, one, one, two, two, two],
        out_shape=[s1, s1, s1, s1, s1, s2, s2, s2],
        compiler_params=_cparams(2),
    )(z, z, z, mu, k_k, k_a, r_k, w0, a0, wup_p, aup_p, gup_b, ones_b)


def _rw_scan_kernel(rf_ref, rb_ref, vf_ref, vb_ref, af_ref, ab_ref, lwf_ref, lwb_ref, kdf_ref, kdb_ref,
                    bdf_ref, bdb_ref, inc_ref, strict_ref, bm_ref, yf_ref, yb_ref, s_ref):
    p = pl.program_id(1)

    @pl.when(p == 0)
    def _():
        s_ref[...] = jnp.zeros_like(s_ref)

    n = N_HEADS * CHUNK
    bm = bm_ref[...]
    eye = (lax.broadcasted_iota(jnp.int32, (CHUNK, n), 1) % CHUNK
           == lax.broadcasted_iota(jnp.int32, (CHUNK, n), 0)).astype(F32)
    refs = ((rf_ref, vf_ref, af_ref, lwf_ref, kdf_ref, bdf_ref, yf_ref),
            (rb_ref, vb_ref, ab_ref, lwb_ref, kdb_ref, bdb_ref, yb_ref))
    orders = [list(_chunk_order(d)) for d in range(2)]
    bm_b = bm.astype(BF16)
    bd_of = lambda m: _block_diag(m, bm_b)

    nb = s_ref.shape[0] // 2
    inst = [(bi, step, d) for step in range(TQ // CHUNK) for bi in range(nb) for d in range(2)]
    idx = range(len(inst))
    rows = [pl.ds(orders[d][step] * CHUNK, CHUNK) for _, step, d in inst]

    ar, r_t, v_s, a_s, b_s, k_s, bk_t, decay, v_in = [], [], [], [], [], [], [], [], []
    for (bi, step, d), sl in zip(inst, rows):
        r_ref, v_ref, a_ref, lw_ref, kd_ref, bd_ref, _ = refs[d]
        lw = lw_ref[0, bi, sl, :]
        kd = kd_ref[0, bi, sl, :]
        bd = bd_ref[0, bi, sl, :]
        v = v_ref[bi, sl, :]
        cs = _cumsum_rows(lw, reverse=(d == 1))
        tot = cs[CHUNK - 1:CHUNK, :] if d == 0 else cs[0:1, :]
        e_neg = jnp.exp(-cs)
        e_hat = jnp.exp(tot - cs)
        a_t = a_ref[bi, sl, :] * jnp.exp(cs - lw)
        rt = r_ref[bi, sl, :] * jnp.exp(cs)
        ar.append(jnp.concatenate([a_t, rt], axis=0).astype(BF16))
        r_t.append(rt)
        v_in.append(v)
        v_s.append(bd_of(v))
        a_s.append(bd_of(a_t))
        b_s.append(bd_of(bd * e_neg))
        k_s.append(bd_of(kd * e_neg))
        t = jnp.concatenate([bd * e_hat, kd * e_hat,
                             jnp.broadcast_to(jnp.exp(tot), (2 * CHUNK, GROUP_W))], axis=0).T
        bk_t.append(t[:, :2 * CHUNK].astype(BF16))
        decay.append(jnp.concatenate([t[:, 2 * CHUNK:]] * 2, axis=1))
    g_b = [_dot_nt(ar[i], b_s[i]) for i in idx]
    g_k = [_dot_nt(ar[i], k_s[i]) for i in idx]
    pw = [g_b[i][:CHUNK] * strict_ref[inst[i][2]] for i in idx]
    ak = [(g_k[i][:CHUNK] * strict_ref[inst[i][2]]).astype(BF16) for i in idx]
    rbk = [jnp.concatenate([g_b[i][CHUNK:] * inc_ref[inst[i][2]], g_k[i][CHUNK:] * inc_ref[inst[i][2]]],
                           axis=1).astype(BF16) for i in idx]
    x = [eye + pw[i] for i in idx]
    pw = [_dot(pw[i].astype(BF16), bd_of(pw[i])) for i in idx]
    for _ in range(4):
        px = [_dot(jnp.concatenate([pw[i], x[i]], axis=0).astype(BF16), bd_of(pw[i])) for i in idx]
        x = [x[i] + px[i][CHUNK:] for i in idx]
        pw = [px[i][:CHUNK] for i in idx]
    x = [(x[i] + _dot(x[i].astype(BF16), bd_of(pw[i]))).astype(BF16) for i in idx]
    akv = [_dot(ak[i], v_s[i]) for i in idx]
    p12 = [_dot(x[i], jnp.concatenate([bd_of(akv[i]), a_s[i]], axis=1)) for i in idx]
    p1_s = [bd_of(p12[i][:, :GROUP_W]) for i in idx]
    p2_s = [bd_of(p12[i][:, GROUP_W:]) for i in idx]
    q = [r_t[i] + _dot(rbk[i][:, :n], p2_s[i]) for i in idx]
    y_c = [_dot(rbk[i], jnp.concatenate([p1_s[i], v_s[i]], axis=0)) for i in idx]
    zero = jnp.zeros((CHUNK, GROUP_W), F32)
    gc = [_dot(bk_t[i], jnp.concatenate(
        [jnp.concatenate([p12[i][:, GROUP_W:], p12[i][:, :GROUP_W]], axis=1),
         jnp.concatenate([zero, v_in[i]], axis=1)], axis=0).astype(BF16)) for i in idx]
    gq = [jnp.concatenate([gc[i][:, :GROUP_W] * bm, q[i]], axis=0).astype(BF16) for i in idx]
    c_s = [gc[i][:, GROUP_W:] * bm for i in idx]

    s = [s_ref[k] for k in range(2 * nb)]
    for i, ((bi, step, d), sl) in enumerate(zip(inst, rows)):
        k = 2 * bi + d
        m = _dot(gq[i], s[k].astype(BF16))
        refs[d][6][bi, sl, :] = m[n:] + y_c[i]
        s[k] = decay[i] * s[k] + m[:n] + c_s[i]
    for k in range(2 * nb):
        s_ref[k] = s[k]


def _rw_scan(r, v, ah, lw, kd, bd, masks, bm, T):
    B, S, _ = r.shape
    nlat = T // TQ
    inc, strict = masks
    n = N_HEADS * CHUNK
    nb = RW_SCAN_BATCH if B % RW_SCAN_BATCH == 0 else 1
    fwd = pl.BlockSpec((nb, TQ, GROUP_W), lambda b, p: (b, _fwd_tile(p, nlat), 0))
    bwd = pl.BlockSpec((nb, TQ, GROUP_W), lambda b, p: (b, _bwd_tile(p, nlat), 0))
    fwd2 = pl.BlockSpec((1, nb, TQ, GROUP_W), lambda b, p: (0, b, _fwd_tile(p, nlat), 0))
    bwd2 = pl.BlockSpec((1, nb, TQ, GROUP_W), lambda b, p: (1, b, _bwd_tile(p, nlat), 0))
    out = jax.ShapeDtypeStruct((B, S, GROUP_W), F32)
    return pl.pallas_call(
        _rw_scan_kernel,
        grid=(B // nb, nlat + 1),
        in_specs=[fwd, bwd, fwd, bwd, fwd, bwd, fwd2, bwd2, fwd2, bwd2, fwd2, bwd2,
                  _const_spec((2, CHUNK, n)), _const_spec((2, CHUNK, n)), _const_spec((n, n))],
        out_specs=[fwd, bwd],
        out_shape=[out, out],
        scratch_shapes=[pltpu.VMEM((2 * nb, n, GROUP_W), F32)],
        compiler_params=_cparams(2),
    )(r, r, v, v, ah, ah, lw, lw, kd, kd, bd, bd, inc, strict, bm)


def _gla_scan_kernel(qf_ref, qb_ref, kf_ref, kb_ref, gf_ref, gb_ref, vf_ref, vb_ref, gup_ref, gbias_ref,
                     inc_ref, bm_ref, bmv_ref, of_ref, ob_ref, s_ref):
    p = pl.program_id(1)

    @pl.when(p == 0)
    def _():
        s_ref[...] = jnp.zeros_like(s_ref)

    wk = N_HEADS * GLA_DK
    n = N_HEADS * CHUNK
    bm = bm_ref[...]
    bmv = bmv_ref[...].astype(BF16)
    bmk = (lax.broadcasted_iota(jnp.int32, (n, wk), 0) // CHUNK
           == lax.broadcasted_iota(jnp.int32, (n, wk), 1) // GLA_DK).astype(F32).astype(BF16)
    refs = ((qf_ref, kf_ref, gf_ref, vf_ref, of_ref), (qb_ref, kb_ref, gb_ref, vb_ref, ob_ref))
    orders = [list(_chunk_order(d)) for d in range(2)]
    nb = s_ref.shape[0] // 2
    inst = [(bi, step, d) for step in range(TQ // CHUNK) for bi in range(nb) for d in range(2)]
    idx = range(len(inst))
    rows = [pl.ds(orders[d][step] * CHUNK, CHUNK) for _, step, d in inst]

    qe, ke_s, ks4, v_s, decay = [], [], [], [], []
    for (bi, step, d), sl in zip(inst, rows):
        q_ref, k_ref, g_ref, v_ref, _ = refs[d]
        k = k_ref[bi, sl, :]
        la = -_softplus(-(_dot(g_ref[bi, sl, :].astype(BF16), gup_ref[d]) + gbias_ref[d])) * (1.0 / GLA_TAU)
        b = _cumsum_rows(la, reverse=(d == 1))
        tot = b[CHUNK - 1:CHUNK, :] if d == 0 else b[0:1, :]
        qe.append((q_ref[bi, sl, :] * (GLA_DK ** -0.5) * jnp.exp(b)).astype(BF16))
        ke_s.append(_block_diag(k * jnp.exp(-b), bmk))
        ks4.append(_tile_rows(k * jnp.exp(tot - b)).T.astype(BF16))
        v_s.append(_block_diag(v_ref[bi, sl, :], bmv))
        decay.append(_tile_rows(jnp.broadcast_to(jnp.exp(tot), (CHUNK, wk))).T)
    a_cat = [(_dot_nt(qe[i], ke_s[i]) * inc_ref[inst[i][2]]).astype(BF16) for i in idx]
    o_in = [_dot(a_cat[i], v_s[i]) for i in idx]
    kv = [_dot(ks4[i], v_s[i]) * bm for i in idx]

    s = [s_ref[k] for k in range(2 * nb)]
    for i, ((bi, step, d), sl) in enumerate(zip(inst, rows)):
        k = 2 * bi + d
        refs[d][4][bi, sl, :] = o_in[i] + _dot(qe[i], s[k].astype(BF16))
        s[k] = decay[i] * s[k] + kv[i]
    for k in range(2 * nb):
        s_ref[k] = s[k]


def _gla_scan(z, gup_p, gb, masks, bm, bmv, T, layer):
    B, S, _ = z.shape
    nlat = T // TQ
    inc, _ = masks
    n = N_HEADS * CHUNK
    wk = N_HEADS * GLA_DK
    nb = GLA_SCAN_BATCH if B % GLA_SCAN_BATCH == 0 else 1
    fwd = lambda w, col: pl.BlockSpec((nb, TQ, w), lambda b, p: (b, _fwd_tile(p, nlat), col // w))
    bwd = lambda w, col: pl.BlockSpec((nb, TQ, w), lambda b, p: (b, _bwd_tile(p, nlat), col // w))
    out = jax.ShapeDtypeStruct((B, S, GROUP_W), F32)
    return pl.pallas_call(
        _gla_scan_kernel,
        grid=(B // nb, nlat + 1),
        in_specs=[fwd(wk, COL_GLA_Q), bwd(wk, COL_GLA_Q), fwd(wk, COL_GLA_K), bwd(wk, COL_GLA_K),
                  fwd(wk, COL_GLA_G), bwd(wk, COL_GLA_G), fwd(GROUP_W, COL_GLA_V), bwd(GROUP_W, COL_GLA_V),
                  _const_spec((2, wk, wk), layer), _const_spec((2, 1, wk), layer),
                  _const_spec((2, CHUNK, n)), _const_spec((wk, GROUP_W)), _const_spec((n, n))],
        out_specs=[fwd(GROUP_W, 0), bwd(GROUP_W, 0)],
        out_shape=[out, out],
        scratch_shapes=[pltpu.VMEM((2 * nb, wk, GROUP_W), F32)],
        compiler_params=_cparams(2),
    )(z, z, z, z, z, z, z, z, gup_p, gb, inc, bm, bmv)


def _pack_w_in(w_in):
    L, D, _ = w_in.shape
    na, mla, rw, gla = jnp.split(w_in.astype(BF16), [768, 1120, 2144], axis=-1)
    cq, ckv, kr = jnp.split(mla, [MLA_Q_RANK, MLA_Q_RANK + MLA_KV_RANK], axis=-1)
    gq, gk, gv, gg, go = jnp.split(gla, [128, 256, 512, 528], axis=-1)
    pad = jnp.zeros((L, D, COL_GLA_V - COL_GLA_G - gg.shape[-1]), BF16)
    packed = jnp.concatenate([na, ckv, cq, kr, kr[..., _rope_swap_perm()], gq, gk, gg, pad, gv, go, rw], axis=-1)
    assert packed.shape[-1] == Z_COLS
    return packed


def kernel(x, c, ctx, c_ctx, w_mod, b_mod, g_mix_pre, g_mix_post, g_ffn_pre, g_ffn_post, w_in, w_out, na_rpb, mla_q_norm, mla_w_uq, mla_kv_norm, mla_w_ukv, rw_mu, rw_w0, rw_w_up, rw_a0, rw_a_up, rw_g_up, rw_k_k, rw_k_a, rw_r_k, rw_ln_w, rw_ln_b, gla_gate_up, gla_gate_b, gla_norm, ffn_w_up, ffn_conv_w, ffn_conv_b, ffn_w_down):
    B, T, D = x.shape
    Tc = ctx.shape[1]
    L = w_in.shape[0]
    assert D == D_MODEL and Tc == TQ and T % TMT == 0 and B + 1 <= 8

    w_in_p = _pack_w_in(w_in)
    w_out_b = w_out.astype(BF16)
    wq1, wq2, wk, wv = _mla_weights(mla_w_uq, mla_w_ukv)
    place = _rope_place()
    cos, sin = _rope_tables(T, Tc)
    zero_lo = jnp.zeros((L, 64, 2 * GROUP_W), F32)
    rw_wup_p = jnp.concatenate([jnp.concatenate([rw_w_up[:, 0], rw_w_up[:, 1]], axis=-1), zero_lo], axis=1).astype(BF16)
    rw_aup_p = jnp.concatenate([zero_lo, jnp.concatenate([rw_a_up[:, 0], rw_a_up[:, 1]], axis=-1)], axis=1).astype(BF16)
    rw_gup_b = rw_g_up.astype(BF16)
    wk_gla = N_HEADS * GLA_DK
    gla_gup_p = jnp.concatenate([gla_gate_up, jnp.zeros((L, 2, wk_gla - gla_gate_up.shape[2], wk_gla), F32)], axis=2).astype(BF16)
    ffn_up_b = ffn_w_up.astype(BF16)
    ffn_dn_b = ffn_w_down.astype(BF16)
    ones_b = jnp.asarray(_block_ones(GROUP_W, HEAD_DIM), BF16)
    rw_bm = jnp.asarray(_block_ones(N_HEADS * CHUNK, CHUNK))
    gla_bm = jnp.asarray((np.arange(wk_gla)[:, None] // GLA_DK == np.arange(GROUP_W)[None, :] // HEAD_DIM).astype(np.float32))
    masks = _scan_masks()

    cvecs = jnp.zeros((8, D), F32).at[:B].set(c).at[B].set(c_ctx)
    mods = _modulation(cvecs, w_mod, b_mod).reshape(L, 8, 6, D)
    mods = jnp.pad(mods, ((0, 0), (0, 0), (0, 2), (0, 0)))
    modtabs = jnp.stack([jnp.broadcast_to(mods[:, B:B + 1], (L, B, 8, D)), mods[:, :B]], axis=2)

    xs = jnp.concatenate([x, ctx], axis=1)
    rows = lambda a: a.reshape(L, 1, -1)
    na_bias = _natten_bias(na_rpb, T // GRID_W)
    gla_gb = gla_gate_b[:, :, None, :]
    for i in range(L):
        z = _in_proj(xs, modtabs, rows(g_mix_pre), w_in_p, T, i)
        y_na = _natten(z, na_bias, T, i)
        qt, k, vt = _mla_up(z, cos, sin, rows(mla_q_norm), rows(mla_kv_norm), wq1, wq2, wk, wv, place, T, i)
        y_mla = _mla_attn(qt, k, vt, T)
        r, vv, ah, g, bon, lw, kd, bd = _rw_prep(z, rows(rw_mu), rows(rw_k_k), rows(rw_k_a), rows(rw_r_k),
                                                 rw_w0, rw_a0, rw_wup_p, rw_aup_p, rw_gup_b, ones_b, T, i)
        yf, yb = _rw_scan(r, vv, ah, lw, kd, bd, masks, rw_bm, T)
        of, ob = _gla_scan(z, gla_gup_p, gla_gb, masks, gla_bm, rw_bm, T, i)
        xs = _out_proj(xs, z, y_na, y_mla, yf, yb, bon, g, of, ob, modtabs, rows(g_mix_post), w_out_b,
                       rows(rw_ln_w), rows(rw_ln_b), rows(gla_norm), ones_b, T, i)
        xs = _ffn(xs, modtabs, rows(g_ffn_pre), rows(g_ffn_post), ffn_up_b, ffn_conv_w,
                  rows(ffn_conv_b), ffn_dn_b, T, i, latent_only=(i == L - 1))
    return xs
```

```python
import functools

import numpy as np
import jax
import jax.numpy as jnp
from jax import lax
from jax.experimental import pallas as pl
from jax.experimental.pallas import tpu as pltpu

F32 = jnp.float32
BF16 = jnp.bfloat16

D_MODEL = 1024
GRID_W = 64
EPS = 1e-6
LOG2E = 1.4426950408889634
N_HEADS = 4
HEAD_DIM = 64
GROUP_W = 256
NA_ROWS = 8
NA_COLS = 16
NA_UNION = 12
MLA_Q_RANK = 192
MLA_KV_RANK = 128
MLA_NOPE = 64
MLA_ROPE = 32
MLA_HEAD_PAD = 128
ROPE_THETA = 10000.0
RW_GN_EPS = 64e-5
GLA_DK = 32
GLA_TAU = 16.0
D_FF = 2816
CHUNK = 64
TQ = 256
MLA_TQ = 512
RW_SCAN_BATCH = 2
GLA_SCAN_BATCH = 4
TMT = 512
HALO = 8
FF_CHUNK = 256
FF_GROUP = 4
Z_COLS = 3072

COL_NA = 0
COL_MLA = 768
COL_GLA_Q = 1152
COL_GLA_K = 1280
COL_GLA_G = 1408
COL_GLA_V = 1536
COL_GLA_O = 1792
COL_RW = 2048

VMEM_LIMIT_V7X = 56 * 1024 * 1024


def _cparams(n_axes):
    return pltpu.CompilerParams(dimension_semantics=("arbitrary",) * n_axes,
                                vmem_limit_bytes=VMEM_LIMIT_V7X)


def _const_spec(shape, layer=None):
    nd = len(shape)
    if layer is None:
        return pl.BlockSpec(shape, lambda *_: (0,) * nd, pipeline_mode=pl.Buffered(1))
    return pl.BlockSpec((None,) + tuple(shape), lambda *_: (layer,) + (0,) * nd, pipeline_mode=pl.Buffered(1))


def _dot(a, b):
    return jnp.dot(a, b, preferred_element_type=F32)


def _split3(x):
    hi = x.astype(BF16)
    r1 = x - hi.astype(F32)
    mid = r1.astype(BF16)
    lo = (r1 - mid.astype(F32)).astype(BF16)
    return hi, mid, lo


def _dot_sel_rhs(x, m):
    hi, mid, lo = _split3(x)
    return _dot(hi, m) + (_dot(mid, m) + _dot(lo, m))


def _sigmoid(x):
    return 1.0 / (1.0 + jnp.exp(-x))


def _silu_gain(x):
    return 0.5 + 0.5 * jnp.tanh(0.5 * x)


def _softplus(x):
    return jnp.maximum(x, 0.0) + jnp.log1p(jnp.exp(-jnp.abs(x)))


def _rms(x):
    return x * lax.rsqrt(jnp.mean(x * x, axis=-1, keepdims=True) + EPS)


def _norm_mod(x, g, shift, scale):
    return (_rms(x) * g) * (1.0 + scale) + shift


def _lane_head(width, per_head):
    return lax.broadcasted_iota(jnp.int32, (1, width), 1) // per_head


def _block_ones(n, blk):
    i = np.arange(n) // blk
    return (i[:, None] == i[None, :]).astype(np.float32)


def _mod_kernel(c_ref, w_ref, b_ref, o_ref):
    cv = c_ref[...]
    s = cv * _sigmoid(cv)
    o_ref[0] = _dot(s.astype(BF16), w_ref[0].astype(BF16)) + b_ref[0]


def _modulation(cvecs, w_mod, b_mod):
    L, D, N = w_mod.shape
    tn = 1536
    return pl.pallas_call(
        _mod_kernel,
        grid=(L, N // tn),
        in_specs=[pl.BlockSpec((8, D), lambda l, n: (0, 0)),
                  pl.BlockSpec((1, D, tn), lambda l, n: (l, 0, n)),
                  pl.BlockSpec((1, 1, tn), lambda l, n: (l, 0, n))],
        out_specs=pl.BlockSpec((1, 8, tn), lambda l, n: (l, 0, n)),
        out_shape=jax.ShapeDtypeStruct((L, 8, N), F32),
        compiler_params=_cparams(2),
    )(cvecs, w_mod, b_mod.reshape(L, 1, N))


def _tok_tiles(S, T):
    assert T % TMT == 0 and 0 < S - T <= TMT
    return T // TMT + 1, T // TMT


def _tok_spec(width, col_block=0):
    return pl.BlockSpec((1, TMT, width), lambda b, i: (b, i, col_block))


def _mod_spec(nlat, layer):
    return pl.BlockSpec((None, 1, 1, 8, D_MODEL), lambda b, i: (layer, b, jnp.where(i < nlat, 1, 0), 0, 0))


def _halo_specs(width, col_block, S):
    per = TMT // HALO
    last = S // HALO - 1
    prev = pl.BlockSpec((1, HALO, width), lambda b, i: (b, jnp.maximum(i * per - 1, 0), col_block))
    nxt = pl.BlockSpec((1, HALO, width), lambda b, i: (b, jnp.minimum((i + 1) * per, last), col_block))
    return prev, nxt


def _neighbour_ok(i, nlat):
    prev_ok = jnp.logical_and(i != 0, i != nlat)
    next_ok = i < nlat - 1
    return prev_ok, next_ok


def _for_tile_rows(nlat, tc, body):
    i = pl.program_id(1)

    @pl.when(i < nlat)
    def _():
        body(TMT)

    @pl.when(i >= nlat)
    def _():
        body(tc)


def _in_proj_kernel(x_ref, mod_ref, g_ref, w_ref, z_ref, *, nlat, tc):
    def body(nr):
        m = mod_ref[0, 0]
        h = _norm_mod(x_ref[0, :nr], g_ref[...], m[0:1], m[1:2])
        z_ref[0, :nr] = _dot(h.astype(BF16), w_ref[...])

    _for_tile_rows(nlat, tc, body)


def _in_proj(xs, modtab, g_pre, w_in_p, T, layer):
    B, S, D = xs.shape
    ntiles, nlat = _tok_tiles(S, T)
    return pl.pallas_call(
        functools.partial(_in_proj_kernel, nlat=nlat, tc=S - T),
        grid=(B, ntiles),
        in_specs=[_tok_spec(D), _mod_spec(nlat, layer), _const_spec((1, D), layer),
                  _const_spec((D, Z_COLS), layer)],
        out_specs=_tok_spec(Z_COLS),
        out_shape=jax.ShapeDtypeStruct((B, S, Z_COLS), F32),
        compiler_params=_cparams(2),
    )(xs, modtab, g_pre, w_in_p)


def _out_proj_kernel(x_ref, na_ref, mla_ref, yf_ref, yb_ref, bon_ref, g_ref, of_ref, ob_ref, og_ref,
                     mod_ref, gpost_ref, w_ref, lnw_ref, lnb_ref, gn_ref, ones_ref, o_ref, *, nlat, tc):
    def body(nr):
        m = mod_ref[0, 0]
        ones = ones_ref[...]
        inv_n = 1.0 / HEAD_DIM
        y = yf_ref[0, :nr] + yb_ref[0, :nr]
        yc = y - _dot_sel_rhs(y, ones) * inv_n
        var = _dot_sel_rhs(yc * yc, ones) * inv_n
        y_rw = (yc * lax.rsqrt(var + RW_GN_EPS) * lnw_ref[...] + lnb_ref[...] + bon_ref[0, :nr]) * g_ref[0, :nr]
        o = of_ref[0, :nr] + ob_ref[0, :nr]
        ms = _dot_sel_rhs(o * o, ones) * inv_n
        og = og_ref[0, :nr]
        y_gla = (o * lax.rsqrt(ms + EPS) * gn_ref[...]) * (og * _silu_gain(og))
        y = jnp.concatenate([na_ref[0, :nr], mla_ref[0, :nr], y_rw.astype(BF16), y_gla.astype(BF16)], axis=-1)
        y = _dot(y, w_ref[...])
        o_ref[0, :nr] = x_ref[0, :nr] + m[2:3] * (_rms(y) * gpost_ref[...])

    _for_tile_rows(nlat, tc, body)


def _out_proj(xs, z, y_na, y_mla, yf, yb, bon, g, of, ob, modtab, g_post, w_out_b, ln_w, ln_b, gla_norm,
              ones_b, T, layer):
    B, S, D = xs.shape
    ntiles, nlat = _tok_tiles(S, T)
    grp = _tok_spec(GROUP_W)
    vec = _const_spec((1, GROUP_W), layer)
    return pl.pallas_call(
        functools.partial(_out_proj_kernel, nlat=nlat, tc=S - T),
        grid=(B, ntiles),
        in_specs=[_tok_spec(D), grp, grp, grp, grp, grp, grp, grp, grp,
                  _tok_spec(GROUP_W, COL_GLA_O // GROUP_W),
                  _mod_spec(nlat, layer), _const_spec((1, D), layer), _const_spec((D, D), layer), vec, vec, vec,
                  _const_spec((GROUP_W, GROUP_W))],
        out_specs=_tok_spec(D),
        out_shape=jax.ShapeDtypeStruct((B, S, D), F32),
        compiler_params=_cparams(2),
    )(xs, y_na, y_mla, yf, yb, bon, g, of, ob, z, modtab, g_post, w_out_b, ln_w, ln_b, gla_norm, ones_b)


def _ffn_kernel(xp_ref, x_ref, xn_ref, mod_ref, gpre_ref, gpost_ref, wup_ref, cw_ref, cb_ref,
                wdn_ref, o_ref, *, nlat, tc):
    i = pl.program_id(1)
    prev_ok, next_ok = _neighbour_ok(i, nlat)
    nchunks = D_FF // FF_CHUNK

    def body(nr):
        m = mod_ref[0, 0]
        x = x_ref[0, :nr]
        xe = jnp.concatenate([xp_ref[0], x, xn_ref[0]], axis=0)
        n = nr + 2 * HALO
        h = _norm_mod(xe, gpre_ref[...], m[3:4], m[4:5])
        row = lax.broadcasted_iota(jnp.int32, (n, 1), 0)
        valid = jnp.logical_or(jnp.logical_and(row >= HALO, row < HALO + nr),
                               jnp.logical_or(jnp.logical_and(row < HALO, prev_ok),
                                              jnp.logical_and(row >= HALO + nr, next_ok)))
        hb = jnp.where(valid, h, 0.0).astype(BF16)

        def up(c):
            return [_dot(hb, wup_ref[:, base + c * FF_CHUNK:base + (c + 1) * FF_CHUNK]) for base in (0, D_FF)]

        def conv(z, lo):
            cw = cw_ref[:, lo:lo + FF_CHUNK]
            return (cb_ref[:, lo:lo + FF_CHUNK]
                    + pltpu.roll(z, 1, 0)[HALO:HALO + nr] * cw[0:1]
                    + z[HALO:HALO + nr] * cw[1:2]
                    + pltpu.roll(z, n - 1, 0)[HALO:HALO + nr] * cw[2:3])

        acc = None
        group = []
        z_next = up(0)
        for c in range(nchunks):
            z_val, z_gate = z_next
            if c + 1 < nchunks:
                z_next = up(c + 1)
            val = conv(z_val, c * FF_CHUNK)
            gate = conv(z_gate, D_FF + c * FF_CHUNK)
            group.append(((gate * _silu_gain(gate)) * val).astype(BF16))
            if len(group) == FF_GROUP or c + 1 == nchunks:
                lo = (c + 1 - len(group)) * FF_CHUNK
                part = _dot(jnp.concatenate(group, axis=1), wdn_ref[lo:(c + 1) * FF_CHUNK, :])
                acc = part if acc is None else acc + part
                group = []
        o_ref[0, :nr] = x + m[5:6] * (_rms(acc) * gpost_ref[...])

    _for_tile_rows(nlat, tc, body)


def _ffn(xs, modtab, g_pre, g_post, w_up_b, conv_w, conv_b, w_dn_b, T, layer, latent_only):
    B, S, D = xs.shape
    ntiles, nlat = _tok_tiles(S, T)
    prev, nxt = _halo_specs(D, 0, S)
    return pl.pallas_call(
        functools.partial(_ffn_kernel, nlat=nlat, tc=S - T),
        grid=(B, nlat if latent_only else ntiles),
        in_specs=[prev, _tok_spec(D), nxt, _mod_spec(nlat, layer), _const_spec((1, D), layer),
                  _const_spec((1, D), layer), _const_spec((D, 2 * D_FF), layer),
                  _const_spec((3, 2 * D_FF), layer), _const_spec((1, 2 * D_FF), layer),
                  _const_spec((D_FF, D), layer)],
        out_specs=_tok_spec(D),
        out_shape=jax.ShapeDtypeStruct((B, T if latent_only else S, D), F32),
        compiler_params=_cparams(2),
    )(xs, xs, xs, modtab, g_pre, g_post, w_up_b, conv_w, conv_b, w_dn_b)


def _natten_bias(rpb, rows):
    rt = TQ // GRID_W
    j = np.arange(GRID_W)
    col_start = np.clip(j - NA_COLS // 2, 0, GRID_W - NA_COLS)
    col_in = (j[None, :] >= col_start[:, None]) & (j[None, :] < col_start[:, None] + NA_COLS)
    edge = GRID_W - NA_COLS
    ext = jnp.concatenate([jnp.repeat(rpb[..., :1], edge, axis=-1), rpb,
                           jnp.repeat(rpb[..., -1:], edge, axis=-1)], axis=-1).astype(F32) * LOG2E
    bq = jnp.stack([ext[..., GRID_W - 1 - q:2 * GRID_W - 1 - q] for q in range(GRID_W)], axis=2)
    L = rpb.shape[0]
    bq = jnp.where(col_in[:, None, :], bq, -jnp.inf).reshape(L, N_HEADS, GRID_W, -1)
    cases = []
    for r0 in (0, rt, rows - rt):
        us = min(max(r0 - NA_ROWS // 2, 0), rows - NA_UNION)
        per_row = []
        for r in range(r0, r0 + rt):
            rs = min(max(r - NA_ROWS // 2, 0), rows - NA_ROWS)
            first = rs - r + (NA_ROWS - 1)
            pre, post = rs - us, us + NA_UNION - (rs + NA_ROWS)
            per_row.append(jnp.pad(bq[..., first * GRID_W:(first + NA_ROWS) * GRID_W],
                                   ((0, 0), (0, 0), (0, 0), (pre * GRID_W, post * GRID_W)),
                                   constant_values=-jnp.inf))
        cases.append(jnp.concatenate(per_row, axis=2))
    return jnp.stack(cases, axis=1)


def _natten_kernel(q_ref, k_ref, v_ref, qc_ref, kc_ref, vc_ref, bias_ref, o_ref, *, nlat, rows):
    j = pl.program_id(1)
    lane_h = _lane_head(GROUP_W, HEAD_DIM)
    kct = kc_ref[0].T.astype(BF16)
    vc = vc_ref[0].astype(BF16)
    scale = HEAD_DIM ** -0.5 * LOG2E
    nwin = NA_UNION * GRID_W

    @pl.when(j < nlat)
    def _():
        us = jnp.clip(j * (TQ // GRID_W) - NA_ROWS // 2, 0, rows - NA_UNION)
        start = pl.multiple_of(us * GRID_W, GRID_W)
        q = q_ref[0] * scale
        kwt = k_ref[0, pl.ds(start, nwin), :].T.astype(BF16)
        vw = v_ref[0, pl.ds(start, nwin), :].astype(BF16)
        acc = jnp.zeros((TQ, GROUP_W), F32)

        def logits(h):
            qh = jnp.where(lane_h == h, q, 0.0).astype(BF16)
            return _dot(qh, kwt), _dot(qh, kct)

        s_next = logits(0)
        for h in range(N_HEADS):
            hm = lane_h == h
            s_w, s_c = s_next
            if h + 1 < N_HEADS:
                s_next = logits(h + 1)
            s_w = s_w + bias_ref[0, h]
            mx = jnp.maximum(jnp.max(s_w, axis=-1, keepdims=True), jnp.max(s_c, axis=-1, keepdims=True))
            p_w = jnp.exp2(s_w - mx)
            p_c = jnp.exp2(s_c - mx)
            den = jnp.sum(p_w, axis=-1, keepdims=True) + jnp.sum(p_c, axis=-1, keepdims=True)
            o = _dot(p_w.astype(BF16), vw) + _dot(p_c.astype(BF16), vc)
            acc = acc + jnp.where(hm, o * (1.0 / den), 0.0)
        o_ref[0] = acc.astype(o_ref.dtype)

    @pl.when(j >= nlat)
    def _():
        q = qc_ref[0] * scale
        acc = jnp.zeros((TQ, GROUP_W), F32)
        for h in range(N_HEADS):
            hm = lane_h == h
            s = _dot(jnp.where(hm, q, 0.0).astype(BF16), kct)
            p = jnp.exp2(s - jnp.max(s, axis=-1, keepdims=True))
            den = jnp.sum(p, axis=-1, keepdims=True)
            acc = acc + jnp.where(hm, _dot(p.astype(BF16), vc) * (1.0 / den), 0.0)
        o_ref[0] = acc.astype(o_ref.dtype)


def _natten(z, bias, T, layer):
    B, S, _ = z.shape
    nlat = T // TQ
    rows = T // GRID_W
    assert rows >= 16 and rows % (TQ // GRID_W) == 0 and S - T == TQ
    cb = COL_NA // GROUP_W
    lat = lambda c: pl.BlockSpec((1, T, GROUP_W), lambda b, j: (b, 0, cb + c))
    ctx = lambda c: pl.BlockSpec((1, TQ, GROUP_W), lambda b, j: (b, nlat, cb + c))
    case = lambda j: jnp.where(j == 0, 0, jnp.where(j >= nlat - 1, 2, 1))
    return pl.pallas_call(
        functools.partial(_natten_kernel, nlat=nlat, rows=rows),
        grid=(B, nlat + 1),
        in_specs=[pl.BlockSpec((1, TQ, GROUP_W), lambda b, j: (b, jnp.minimum(j, nlat - 1), cb)),
                  lat(1), lat(2), ctx(0), ctx(1), ctx(2),
                  pl.BlockSpec((None, 1, N_HEADS, TQ, NA_UNION * GRID_W), lambda b, j: (layer, case(j), 0, 0, 0))],
        out_specs=pl.BlockSpec((1, TQ, GROUP_W), lambda b, j: (b, j, 0)),
        out_shape=jax.ShapeDtypeStruct((B, S, GROUP_W), BF16),
        compiler_params=_cparams(2),
    )(z, z, z, z, z, z, bias)


def _rope_tables(T, Tc):
    t = np.arange(T)
    row = (t // GRID_W).astype(np.float32)
    col = (t % GRID_W).astype(np.float32)
    d = MLA_ROPE // 2
    inv = (np.float32(ROPE_THETA) ** (-np.arange(0, d, 2, dtype=np.float32) / np.float32(d))).astype(np.float32)
    cs, sn = [], []
    for pos in (row, col):
        ang = (pos[:, None] * inv[None, :]).astype(np.float32)
        cs += [np.cos(ang), np.cos(ang)]
        sn += [-np.sin(ang), np.sin(ang)]
    pad = MLA_HEAD_PAD - MLA_NOPE - MLA_ROPE
    f32 = np.float32
    cos = np.concatenate([np.ones((T, MLA_NOPE), f32)] + cs + [np.ones((T, pad), f32)], axis=1)
    sin = np.concatenate([np.zeros((T, MLA_NOPE), f32)] + sn + [np.zeros((T, pad), f32)], axis=1)
    cos = np.concatenate([cos, np.ones((Tc, MLA_HEAD_PAD), f32)], axis=0)
    sin = np.concatenate([sin, np.zeros((Tc, MLA_HEAD_PAD), f32)], axis=0)
    return jnp.asarray(cos, F32), jnp.asarray(sin, F32)


def _rope_swap_perm():
    q = MLA_ROPE // 4
    return np.concatenate([np.arange(q, 2 * q), np.arange(0, q), np.arange(3 * q, 4 * q), np.arange(2 * q, 3 * q)])


def _mla_weights(w_uq, w_ukv):
    L = w_uq.shape[0]
    wq = w_uq.reshape(L, MLA_Q_RANK, N_HEADS, MLA_NOPE + MLA_ROPE)
    pad = MLA_HEAD_PAD - MLA_NOPE - MLA_ROPE
    zq = jnp.zeros((L, MLA_Q_RANK, N_HEADS, pad), F32)
    wq1 = jnp.concatenate([wq, zq], axis=-1).reshape(L, MLA_Q_RANK, N_HEADS * MLA_HEAD_PAD)
    rope_sw = wq[..., MLA_NOPE:][..., _rope_swap_perm()]
    wq2 = jnp.concatenate([jnp.zeros((L, MLA_Q_RANK, N_HEADS, MLA_NOPE), F32), rope_sw, zq], axis=-1)
    wq2 = wq2.reshape(L, MLA_Q_RANK, N_HEADS * MLA_HEAD_PAD)
    wkv = w_ukv.reshape(L, MLA_KV_RANK, N_HEADS, 2 * MLA_NOPE)
    wk = jnp.concatenate([wkv[..., :MLA_NOPE], jnp.zeros((L, MLA_KV_RANK, N_HEADS, MLA_HEAD_PAD - MLA_NOPE), F32)], axis=-1)
    wk = wk.reshape(L, MLA_KV_RANK, N_HEADS * MLA_HEAD_PAD)
    wv_t = wkv[..., MLA_NOPE:].reshape(L, MLA_KV_RANK, N_HEADS * MLA_NOPE).transpose(0, 2, 1)
    return wq1.astype(BF16), wq2.astype(BF16), wk.astype(BF16), wv_t.astype(BF16)


def _rope_place():
    e = np.zeros((MLA_ROPE, N_HEADS * MLA_HEAD_PAD), np.float32)
    for h in range(N_HEADS):
        e[np.arange(MLA_ROPE), h * MLA_HEAD_PAD + MLA_NOPE + np.arange(MLA_ROPE)] = 1.0
    return jnp.asarray(e, BF16)


def _mla_up_kernel(z_ref, cos_ref, sin_ref, qn_ref, kvn_ref, wq1_ref, wq2_ref, wk_ref, wv_ref, e_ref,
                   qt_ref, k_ref, vt_ref, *, nlat, tc):
    def body(nr):
        z = z_ref[0, :nr]
        ckv = z[:, :MLA_KV_RANK]
        cq = z[:, MLA_KV_RANK:MLA_KV_RANK + MLA_Q_RANK]
        kr = z[:, MLA_KV_RANK + MLA_Q_RANK:MLA_KV_RANK + MLA_Q_RANK + MLA_ROPE]
        krs = z[:, MLA_KV_RANK + MLA_Q_RANK + MLA_ROPE:]
        cos = jnp.concatenate([cos_ref[:nr]] * N_HEADS, axis=-1)
        sin = jnp.concatenate([sin_ref[:nr]] * N_HEADS, axis=-1)
        nq = (_rms(cq) * qn_ref[...]).astype(BF16)
        nkv_f = _rms(ckv) * kvn_ref[...]
        nkv = nkv_f.astype(BF16)
        q = _dot(nq, wq1_ref[...]) * cos + _dot(nq, wq2_ref[...]) * sin
        scale = (MLA_NOPE + MLA_ROPE) ** -0.5 * LOG2E
        qt_ref[0, :, :nr] = (q * scale).T.astype(BF16)
        k = _dot(nkv, wk_ref[...]) + _dot_sel_rhs(kr, e_ref[...]) * cos + _dot_sel_rhs(krs, e_ref[...]) * sin
        k_ref[0, :nr] = k.astype(BF16)
        vt_ref[0, :, :nr] = _dot(wv_ref[...], nkv_f.T.astype(BF16)).astype(BF16)

    _for_tile_rows(nlat, tc, body)


def _mla_up(z, cos, sin, q_norm, kv_norm, wq1, wq2, wk, wv, place, T, layer):
    B, S, _ = z.shape
    ntiles, nlat = _tok_tiles(S, T)
    HP = N_HEADS * MLA_HEAD_PAD
    zw = MLA_KV_RANK + MLA_Q_RANK + 2 * MLA_ROPE
    tab = pl.BlockSpec((TMT, MLA_HEAD_PAD), lambda b, i: (i, 0))
    return pl.pallas_call(
        functools.partial(_mla_up_kernel, nlat=nlat, tc=S - T),
        grid=(B, ntiles),
        in_specs=[_tok_spec(zw, COL_MLA // zw), tab, tab,
                  _const_spec((1, MLA_Q_RANK), layer), _const_spec((1, MLA_KV_RANK), layer),
                  _const_spec((MLA_Q_RANK, HP), layer), _const_spec((MLA_Q_RANK, HP), layer),
                  _const_spec((MLA_KV_RANK, HP), layer), _const_spec((GROUP_W, MLA_KV_RANK), layer),
                  _const_spec((MLA_ROPE, HP))],
        out_specs=[pl.BlockSpec((1, HP, TMT), lambda b, i: (b, 0, i)), _tok_spec(HP),
                   pl.BlockSpec((1, GROUP_W, TMT), lambda b, i: (b, 0, i))],
        out_shape=[jax.ShapeDtypeStruct((B, HP, S), BF16),
                   jax.ShapeDtypeStruct((B, S, HP), BF16),
                   jax.ShapeDtypeStruct((B, GROUP_W, S), BF16)],
        compiler_params=_cparams(2),
    )(z, cos, sin, q_norm, kv_norm, wq1, wq2, wk, wv, place)


def _mla_attn_kernel(qt_ref, k_ref, vt_ref, o_ref, *, nlat, T):
    j = pl.program_id(1)

    def attend(lo, hi):
        def logits(h):
            hp = slice(h * MLA_HEAD_PAD, (h + 1) * MLA_HEAD_PAD)
            return _dot(k_ref[0, lo:hi, hp], qt_ref[0, hp, :])

        ahead = 2
        pending = [logits(h) for h in range(ahead)]
        pieces = []
        for h in range(N_HEADS):
            s = pending.pop(0)
            if h + ahead < N_HEADS:
                pending.append(logits(h + ahead))
            p = jnp.exp2(s - jnp.max(s, axis=0, keepdims=True))
            den = jnp.sum(p, axis=0, keepdims=True)
            o = _dot(vt_ref[0, h * HEAD_DIM:(h + 1) * HEAD_DIM, lo:hi], p.astype(BF16))
            pieces.append(o * (1.0 / den))
        o_ref[0] = jnp.concatenate(pieces, axis=0).T.astype(o_ref.dtype)

    @pl.when(j < nlat)
    def _():
        attend(0, T + TQ)

    @pl.when(j >= nlat)
    def _():
        attend(T, T + TQ)


def _mla_attn(qt, k, vt, T):
    B, S, HP = k.shape
    assert T % MLA_TQ == 0 and S - T <= MLA_TQ
    nlat = T // MLA_TQ
    return pl.pallas_call(
        functools.partial(_mla_attn_kernel, nlat=nlat, T=T),
        grid=(B, nlat + 1),
        in_specs=[pl.BlockSpec((1, HP, MLA_TQ), lambda b, j: (b, 0, j)),
                  pl.BlockSpec((1, S, HP), lambda b, j: (b, 0, 0)),
                  pl.BlockSpec((1, GROUP_W, S), lambda b, j: (b, 0, 0))],
        out_specs=pl.BlockSpec((1, MLA_TQ, GROUP_W), lambda b, j: (b, j, 0)),
        out_shape=jax.ShapeDtypeStruct((B, S, GROUP_W), BF16),
        compiler_params=_cparams(2),
    )(qt, k, vt)


def _scan_masks():
    t = np.arange(CHUNK)
    inc = np.stack([t[:, None] >= t[None, :], t[:, None] <= t[None, :]]).astype(np.float32)
    strict = np.stack([t[:, None] > t[None, :], t[:, None] < t[None, :]]).astype(np.float32)
    return jnp.asarray(np.tile(inc, (1, 1, N_HEADS))), jnp.asarray(np.tile(strict, (1, 1, N_HEADS)))


def _cumsum_rows(x, reverse):
    n = x.shape[0]
    row = lax.broadcasted_iota(jnp.int32, (n, 1), 0)
    sh = 1
    while sh < n:
        if reverse:
            x = x + jnp.where(row < n - sh, pltpu.roll(x, n - sh, 0), 0.0)
        else:
            x = x + jnp.where(row >= sh, pltpu.roll(x, sh, 0), 0.0)
        sh *= 2
    return x


def _block_diag(x, bm_b):
    return _tile_rows(x.astype(BF16)) * bm_b


def _dot_nt(a, b):
    return lax.dot_general(a, b, (((1,), (1,)), ((), ())), preferred_element_type=F32)


def _fwd_tile(p, nlat):
    return jnp.where(p == 0, nlat, p - 1)


def _bwd_tile(p, nlat):
    return jnp.where(p == 0, nlat, nlat - p)


def _chunk_order(d):
    nch = TQ // CHUNK
    return range(nch) if d == 0 else range(nch - 1, -1, -1)


def _tile_rows(x):
    return jnp.concatenate([x] * N_HEADS, axis=0)


def _rw_prep_kernel(zp_ref, z_ref, zn_ref, mu_ref, kk_ref, ka_ref, rk_ref, w0_ref, a0_ref, wup_ref,
                    aup_ref, gup_ref, ones_ref,
                    r_ref, v_ref, ah_ref, g_ref, bon_ref, lw_ref, kd_ref, bd_ref, *, nlat, tc):
    i = pl.program_id(1)
    prev_ok, next_ok = _neighbour_ok(i, nlat)

    def body(nr):
        z = z_ref[0, :nr]
        row = lax.broadcasted_iota(jnp.int32, (nr, 1), 0)
        before = jnp.where(prev_ok, zp_ref[0, HALO - 1:HALO, :], 0.0)
        after = jnp.where(next_ok, zn_ref[0, 0:1, :], 0.0)
        zprev = jnp.where(row == 0, before, pltpu.roll(z, 1, 0))
        znext = jnp.where(row == nr - 1, after, pltpu.roll(z, nr - 1, 0))
        zs = z + mu_ref[...] * (0.5 * (zprev + znext) - z)
        r = zs[:, 0:GROUP_W]
        k = zs[:, GROUP_W:2 * GROUP_W]
        v = zs[:, 2 * GROUP_W:3 * GROUP_W]
        low = zs[:, 3 * GROUP_W:3 * GROUP_W + 128]
        gd = zs[:, 3 * GROUP_W + 128:]
        ones = ones_ref[...]
        kk = k * kk_ref[...]
        kk = kk * lax.rsqrt(_dot_sel_rhs(kk * kk, ones) + 1e-12)
        wl = _dot(jnp.tanh(low).astype(BF16), wup_ref[...])
        al = _dot(low.astype(BF16), aup_ref[...])
        ksum = jnp.zeros((nr, GROUP_W), F32)
        for d in range(2):
            w_raw = -_softplus(-(w0_ref[d:d + 1, :] + wl[:, d * GROUP_W:(d + 1) * GROUP_W])) - 0.5
            lw_ref[d, 0, :nr] = -jnp.exp(w_raw)
            a = _sigmoid(a0_ref[d:d + 1, :] + al[:, d * GROUP_W:(d + 1) * GROUP_W])
            kd = k * (1.0 + (a - 1.0) * ka_ref[...])
            kd_ref[d, 0, :nr] = kd
            bd_ref[d, 0, :nr] = kk * a
            ksum = ksum + kd
        r_ref[0, :nr] = r
        v_ref[0, :nr] = v
        ah_ref[0, :nr] = -kk
        g_ref[0, :nr] = _dot(_sigmoid(gd).astype(BF16), gup_ref[...])
        bon_ref[0, :nr] = _dot_sel_rhs(r * ksum * rk_ref[...], ones) * v

    _for_tile_rows(nlat, tc, body)


def _rw_prep(z, mu, k_k, k_a, r_k, w0, a0, wup_p, aup_p, gup_b, ones_b, T, layer):
    B, S, _ = z.shape
    ntiles, nlat = _tok_tiles(S, T)
    W = 4 * GROUP_W
    cb = COL_RW // W
    prev, nxt = _halo_specs(W, cb, S)
    one = _tok_spec(GROUP_W)
    two = pl.BlockSpec((2, 1, TMT, GROUP_W), lambda b, i: (0, b, i, 0))
    s1 = jax.ShapeDtypeStruct((B, S, GROUP_W), F32)
    s2 = jax.ShapeDtypeStruct((2, B, S, GROUP_W), F32)
    vec = _const_spec((1, GROUP_W), layer)
    return pl.pallas_call(
        functools.partial(_rw_prep_kernel, nlat=nlat, tc=S - T),
        grid=(B, ntiles),
        in_specs=[prev, _tok_spec(W, cb), nxt,
                  _const_spec((1, W), layer), vec, vec, vec,
                  _const_spec((2, GROUP_W), layer), _const_spec((2, GROUP_W), layer),
                  _const_spec((128, 2 * GROUP_W), layer), _const_spec((128, 2 * GROUP_W), layer),
                  _const_spec((128, GROUP_W), layer), _const_spec((GROUP_W, GROUP_W))],
        out_specs=[one, one, one, one, one, two, two, two],
        out_shape=[s1, s1, s1, s1, s1, s2, s2, s2],
        compiler_params=_cparams(2),
    )(z, z, z, mu, k_k, k_a, r_k, w0, a0, wup_p, aup_p, gup_b, ones_b)


def _rw_scan_kernel(rf_ref, rb_ref, vf_ref, vb_ref, af_ref, ab_ref, lwf_ref, lwb_ref, kdf_ref, kdb_ref,
                    bdf_ref, bdb_ref, inc_ref, strict_ref, bm_ref, yf_ref, yb_ref, s_ref):
    p = pl.program_id(1)

    @pl.when(p == 0)
    def _():
        s_ref[...] = jnp.zeros_like(s_ref)

    n = N_HEADS * CHUNK
    bm = bm_ref[...]
    eye = (lax.broadcasted_iota(jnp.int32, (CHUNK, n), 1) % CHUNK
           == lax.broadcasted_iota(jnp.int32, (CHUNK, n), 0)).astype(F32)
    refs = ((rf_ref, vf_ref, af_ref, lwf_ref, kdf_ref, bdf_ref, yf_ref),
            (rb_ref, vb_ref, ab_ref, lwb_ref, kdb_ref, bdb_ref, yb_ref))
    orders = [list(_chunk_order(d)) for d in range(2)]
    bm_b = bm.astype(BF16)
    bd_of = lambda m: _block_diag(m, bm_b)

    nb = s_ref.shape[0] // 2
    inst = [(bi, step, d) for step in range(TQ // CHUNK) for bi in range(nb) for d in range(2)]
    idx = range(len(inst))
    rows = [pl.ds(orders[d][step] * CHUNK, CHUNK) for _, step, d in inst]

    ar, r_t, v_s, a_s, b_s, k_s, bk_t, decay, v_in = [], [], [], [], [], [], [], [], []
    for (bi, step, d), sl in zip(inst, rows):
        r_ref, v_ref, a_ref, lw_ref, kd_ref, bd_ref, _ = refs[d]
        lw = lw_ref[0, bi, sl, :]
        kd = kd_ref[0, bi, sl, :]
        bd = bd_ref[0, bi, sl, :]
        v = v_ref[bi, sl, :]
        cs = _cumsum_rows(lw, reverse=(d == 1))
        tot = cs[CHUNK - 1:CHUNK, :] if d == 0 else cs[0:1, :]
        e_neg = jnp.exp(-cs)
        e_hat = jnp.exp(tot - cs)
        a_t = a_ref[bi, sl, :] * jnp.exp(cs - lw)
        rt = r_ref[bi, sl, :] * jnp.exp(cs)
        ar.append(jnp.concatenate([a_t, rt], axis=0).astype(BF16))
        r_t.append(rt)
        v_in.append(v)
        v_s.append(bd_of(v))
        a_s.append(bd_of(a_t))
        b_s.append(bd_of(bd * e_neg))
        k_s.append(bd_of(kd * e_neg))
        t = jnp.concatenate([bd * e_hat, kd * e_hat,
                             jnp.broadcast_to(jnp.exp(tot), (2 * CHUNK, GROUP_W))], axis=0).T
        bk_t.append(t[:, :2 * CHUNK].astype(BF16))
        decay.append(jnp.concatenate([t[:, 2 * CHUNK:]] * 2, axis=1))
    g_b = [_dot_nt(ar[i], b_s[i]) for i in idx]
    g_k = [_dot_nt(ar[i], k_s[i]) for i in idx]
    pw = [g_b[i][:CHUNK] * strict_ref[inst[i][2]] for i in idx]
    ak = [(g_k[i][:CHUNK] * strict_ref[inst[i][2]]).astype(BF16) for i in idx]
    rbk = [jnp.concatenate([g_b[i][CHUNK:] * inc_ref[inst[i][2]], g_k[i][CHUNK:] * inc_ref[inst[i][2]]],
                           axis=1).astype(BF16) for i in idx]
    x = [eye + pw[i] for i in idx]
    pw = [_dot(pw[i].astype(BF16), bd_of(pw[i])) for i in idx]
    for _ in range(4):
        px = [_dot(jnp.concatenate([pw[i], x[i]], axis=0).astype(BF16), bd_of(pw[i])) for i in idx]
        x = [x[i] + px[i][CHUNK:] for i in idx]
        pw = [px[i][:CHUNK] for i in idx]
    x = [(x[i] + _dot(x[i].astype(BF16), bd_of(pw[i]))).astype(BF16) for i in idx]
    akv = [_dot(ak[i], v_s[i]) for i in idx]
    p12 = [_dot(x[i], jnp.concatenate([bd_of(akv[i]), a_s[i]], axis=1)) for i in idx]
    p1_s = [bd_of(p12[i][:, :GROUP_W]) for i in idx]
    p2_s = [bd_of(p12[i][:, GROUP_W:]) for i in idx]
    q = [r_t[i] + _dot(rbk[i][:, :n], p2_s[i]) for i in idx]
    y_c = [_dot(rbk[i], jnp.concatenate([p1_s[i], v_s[i]], axis=0)) for i in idx]
    zero = jnp.zeros((CHUNK, GROUP_W), F32)
    gc = [_dot(bk_t[i], jnp.concatenate(
        [jnp.concatenate([p12[i][:, GROUP_W:], p12[i][:, :GROUP_W]], axis=1),
         jnp.concatenate([zero, v_in[i]], axis=1)], axis=0).astype(BF16)) for i in idx]
    gq = [jnp.concatenate([gc[i][:, :GROUP_W] * bm, q[i]], axis=0).astype(BF16) for i in idx]
    c_s = [gc[i][:, GROUP_W:] * bm for i in idx]

    s = [s_ref[k] for k in range(2 * nb)]
    for i, ((bi, step, d), sl) in enumerate(zip(inst, rows)):
        k = 2 * bi + d
        m = _dot(gq[i], s[k].astype(BF16))
        refs[d][6][bi, sl, :] = m[n:] + y_c[i]
        s[k] = decay[i] * s[k] + m[:n] + c_s[i]
    for k in range(2 * nb):
        s_ref[k] = s[k]


def _rw_scan(r, v, ah, lw, kd, bd, masks, bm, T):
    B, S, _ = r.shape
    nlat = T // TQ
    inc, strict = masks
    n = N_HEADS * CHUNK
    nb = RW_SCAN_BATCH if B % RW_SCAN_BATCH == 0 else 1
    fwd = pl.BlockSpec((nb, TQ, GROUP_W), lambda b, p: (b, _fwd_tile(p, nlat), 0))
    bwd = pl.BlockSpec((nb, TQ, GROUP_W), lambda b, p: (b, _bwd_tile(p, nlat), 0))
    fwd2 = pl.BlockSpec((1, nb, TQ, GROUP_W), lambda b, p: (0, b, _fwd_tile(p, nlat), 0))
    bwd2 = pl.BlockSpec((1, nb, TQ, GROUP_W), lambda b, p: (1, b, _bwd_tile(p, nlat), 0))
    out = jax.ShapeDtypeStruct((B, S, GROUP_W), F32)
    return pl.pallas_call(
        _rw_scan_kernel,
        grid=(B // nb, nlat + 1),
        in_specs=[fwd, bwd, fwd, bwd, fwd, bwd, fwd2, bwd2, fwd2, bwd2, fwd2, bwd2,
                  _const_spec((2, CHUNK, n)), _const_spec((2, CHUNK, n)), _const_spec((n, n))],
        out_specs=[fwd, bwd],
        out_shape=[out, out],
        scratch_shapes=[pltpu.VMEM((2 * nb, n, GROUP_W), F32)],
        compiler_params=_cparams(2),
    )(r, r, v, v, ah, ah, lw, lw, kd, kd, bd, bd, inc, strict, bm)


def _gla_scan_kernel(qf_ref, qb_ref, kf_ref, kb_ref, gf_ref, gb_ref, vf_ref, vb_ref, gup_ref, gbias_ref,
                     inc_ref, bm_ref, bmv_ref, of_ref, ob_ref, s_ref):
    p = pl.program_id(1)

    @pl.when(p == 0)
    def _():
        s_ref[...] = jnp.zeros_like(s_ref)

    wk = N_HEADS * GLA_DK
    n = N_HEADS * CHUNK
    bm = bm_ref[...]
    bmv = bmv_ref[...].astype(BF16)
    bmk = (lax.broadcasted_iota(jnp.int32, (n, wk), 0) // CHUNK
           == lax.broadcasted_iota(jnp.int32, (n, wk), 1) // GLA_DK).astype(F32).astype(BF16)
    refs = ((qf_ref, kf_ref, gf_ref, vf_ref, of_ref), (qb_ref, kb_ref, gb_ref, vb_ref, ob_ref))
    orders = [list(_chunk_order(d)) for d in range(2)]
    nb = s_ref.shape[0] // 2
    inst = [(bi, step, d) for step in range(TQ // CHUNK) for bi in range(nb) for d in range(2)]
    idx = range(len(inst))
    rows = [pl.ds(orders[d][step] * CHUNK, CHUNK) for _, step, d in inst]

    qe, ke_s, ks4, v_s, decay = [], [], [], [], []
    for (bi, step, d), sl in zip(inst, rows):
        q_ref, k_ref, g_ref, v_ref, _ = refs[d]
        k = k_ref[bi, sl, :]
        la = -_softplus(-(_dot(g_ref[bi, sl, :].astype(BF16), gup_ref[d]) + gbias_ref[d])) * (1.0 / GLA_TAU)
        b = _cumsum_rows(la, reverse=(d == 1))
        tot = b[CHUNK - 1:CHUNK, :] if d == 0 else b[0:1, :]
        qe.append((q_ref[bi, sl, :] * (GLA_DK ** -0.5) * jnp.exp(b)).astype(BF16))
        ke_s.append(_block_diag(k * jnp.exp(-b), bmk))
        ks4.append(_tile_rows(k * jnp.exp(tot - b)).T.astype(BF16))
        v_s.append(_block_diag(v_ref[bi, sl, :], bmv))
        decay.append(_tile_rows(jnp.broadcast_to(jnp.exp(tot), (CHUNK, wk))).T)
    a_cat = [(_dot_nt(qe[i], ke_s[i]) * inc_ref[inst[i][2]]).astype(BF16) for i in idx]
    o_in = [_dot(a_cat[i], v_s[i]) for i in idx]
    kv = [_dot(ks4[i], v_s[i]) * bm for i in idx]

    s = [s_ref[k] for k in range(2 * nb)]
    for i, ((bi, step, d), sl) in enumerate(zip(inst, rows)):
        k = 2 * bi + d
        refs[d][4][bi, sl, :] = o_in[i] + _dot(qe[i], s[k].astype(BF16))
        s[k] = decay[i] * s[k] + kv[i]
    for k in range(2 * nb):
        s_ref[k] = s[k]


def _gla_scan(z, gup_p, gb, masks, bm, bmv, T, layer):
    B, S, _ = z.shape
    nlat = T // TQ
    inc, _ = masks
    n = N_HEADS * CHUNK
    wk = N_HEADS * GLA_DK
    nb = GLA_SCAN_BATCH if B % GLA_SCAN_BATCH == 0 else 1
    fwd = lambda w, col: pl.BlockSpec((nb, TQ, w), lambda b, p: (b, _fwd_tile(p, nlat), col // w))
    bwd = lambda w, col: pl.BlockSpec((nb, TQ, w), lambda b, p: (b, _bwd_tile(p, nlat), col // w))
    out = jax.ShapeDtypeStruct((B, S, GROUP_W), F32)
    return pl.pallas_call(
        _gla_scan_kernel,
        grid=(B // nb, nlat + 1),
        in_specs=[fwd(wk, COL_GLA_Q), bwd(wk, COL_GLA_Q), fwd(wk, COL_GLA_K), bwd(wk, COL_GLA_K),
                  fwd(wk, COL_GLA_G), bwd(wk, COL_GLA_G), fwd(GROUP_W, COL_GLA_V), bwd(GROUP_W, COL_GLA_V),
                  _const_spec((2, wk, wk), layer), _const_spec((2, 1, wk), layer),
                  _const_spec((2, CHUNK, n)), _const_spec((wk, GROUP_W)), _const_spec((n, n))],
        out_specs=[fwd(GROUP_W, 0), bwd(GROUP_W, 0)],
        out_shape=[out, out],
        scratch_shapes=[pltpu.VMEM((2 * nb, wk, GROUP_W), F32)],
        compiler_params=_cparams(2),
    )(z, z, z, z, z, z, z, z, gup_p, gb, inc, bm, bmv)


def _pack_w_in(w_in):
    L, D, _ = w_in.shape
    na, mla, rw, gla = jnp.split(w_in.astype(BF16), [768, 1120, 2144], axis=-1)
    cq, ckv, kr = jnp.split(mla, [MLA_Q_RANK, MLA_Q_RANK + MLA_KV_RANK], axis=-1)
    gq, gk, gv, gg, go = jnp.split(gla, [128, 256, 512, 528], axis=-1)
    pad = jnp.zeros((L, D, COL_GLA_V - COL_GLA_G - gg.shape[-1]), BF16)
    packed = jnp.concatenate([na, ckv, cq, kr, kr[..., _rope_swap_perm()], gq, gk, gg, pad, gv, go, rw], axis=-1)
    assert packed.shape[-1] == Z_COLS
    return packed


def kernel(x, c, ctx, c_ctx, w_mod, b_mod, g_mix_pre, g_mix_post, g_ffn_pre, g_ffn_post, w_in, w_out, na_rpb, mla_q_norm, mla_w_uq, mla_kv_norm, mla_w_ukv, rw_mu, rw_w0, rw_w_up, rw_a0, rw_a_up, rw_g_up, rw_k_k, rw_k_a, rw_r_k, rw_ln_w, rw_ln_b, gla_gate_up, gla_gate_b, gla_norm, ffn_w_up, ffn_conv_w, ffn_conv_b, ffn_w_down):
    B, T, D = x.shape
    Tc = ctx.shape[1]
    L = w_in.shape[0]
    assert D == D_MODEL and Tc == TQ and T % TMT == 0 and B + 1 <= 8

    w_in_p = _pack_w_in(w_in)
    w_out_b = w_out.astype(BF16)
    wq1, wq2, wk, wv = _mla_weights(mla_w_uq, mla_w_ukv)
    place = _rope_place()
    cos, sin = _rope_tables(T, Tc)
    zero_lo = jnp.zeros((L, 64, 2 * GROUP_W), F32)
    rw_wup_p = jnp.concatenate([jnp.concatenate([rw_w_up[:, 0], rw_w_up[:, 1]], axis=-1), zero_lo], axis=1).astype(BF16)
    rw_aup_p = jnp.concatenate([zero_lo, jnp.concatenate([rw_a_up[:, 0], rw_a_up[:, 1]], axis=-1)], axis=1).astype(BF16)
    rw_gup_b = rw_g_up.astype(BF16)
    wk_gla = N_HEADS * GLA_DK
    gla_gup_p = jnp.concatenate([gla_gate_up, jnp.zeros((L, 2, wk_gla - gla_gate_up.shape[2], wk_gla), F32)], axis=2).astype(BF16)
    ffn_up_b = ffn_w_up.astype(BF16)
    ffn_dn_b = ffn_w_down.astype(BF16)
    ones_b = jnp.asarray(_block_ones(GROUP_W, HEAD_DIM), BF16)
    rw_bm = jnp.asarray(_block_ones(N_HEADS * CHUNK, CHUNK))
    gla_bm = jnp.asarray((np.arange(wk_gla)[:, None] // GLA_DK == np.arange(GROUP_W)[None, :] // HEAD_DIM).astype(np.float32))
    masks = _scan_masks()

    cvecs = jnp.zeros((8, D), F32).at[:B].set(c).at[B].set(c_ctx)
    mods = _modulation(cvecs, w_mod, b_mod).reshape(L, 8, 6, D)
    mods = jnp.pad(mods, ((0, 0), (0, 0), (0, 2), (0, 0)))
    modtabs = jnp.stack([jnp.broadcast_to(mods[:, B:B + 1], (L, B, 8, D)), mods[:, :B]], axis=2)

    xs = jnp.concatenate([x, ctx], axis=1)
    rows = lambda a: a.reshape(L, 1, -1)
    na_bias = _natten_bias(na_rpb, T // GRID_W)
    gla_gb = gla_gate_b[:, :, None, :]
    for i in range(L):
        z = _in_proj(xs, modtabs, rows(g_mix_pre), w_in_p, T, i)
        y_na = _natten(z, na_bias, T, i)
        qt, k, vt = _mla_up(z, cos, sin, rows(mla_q_norm), rows(mla_kv_norm), wq1, wq2, wk, wv, place, T, i)
        y_mla = _mla_attn(qt, k, vt, T)
        r, vv, ah, g, bon, lw, kd, bd = _rw_prep(z, rows(rw_mu), rows(rw_k_k), rows(rw_k_a), rows(rw_r_k),
                                                 rw_w0, rw_a0, rw_wup_p, rw_aup_p, rw_gup_b, ones_b, T, i)
        yf, yb = _rw_scan(r, vv, ah, lw, kd, bd, masks, rw_bm, T)
        of, ob = _gla_scan(z, gla_gup_p, gla_gb, masks, gla_bm, rw_bm, T, i)
        xs = _out_proj(xs, z, y_na, y_mla, yf, yb, bon, g, of, ob, modtabs, rows(g_mix_post), w_out_b,
                       rows(rw_ln_w), rows(rw_ln_b), rows(gla_norm), ones_b, T, i)
        xs = _ffn(xs, modtabs, rows(g_ffn_pre), rows(g_ffn_post), ffn_up_b, ffn_conv_w,
                  rows(ffn_conv_b), ffn_dn_b, T, i, latent_only=(i == L - 1))
    return xs
```

```python
import functools

import numpy as np
import jax
import jax.numpy as jnp
from jax import lax
from jax.experimental import pallas as pl
from jax.experimental.pallas import tpu as pltpu

F32 = jnp.float32
BF16 = jnp.bfloat16

D_MODEL = 1024
GRID_W = 64
EPS = 1e-6
LOG2E = 1.4426950408889634
N_HEADS = 4
HEAD_DIM = 64
GROUP_W = 256
NA_ROWS = 8
NA_COLS = 16
NA_UNION = 12
MLA_Q_RANK = 192
MLA_KV_RANK = 128
MLA_NOPE = 64
MLA_ROPE = 32
MLA_HEAD_PAD = 128
ROPE_THETA = 10000.0
RW_GN_EPS = 64e-5
GLA_DK = 32
GLA_TAU = 16.0
D_FF = 2816
CHUNK = 64
TQ = 256
MLA_TQ = 512
RW_SCAN_BATCH = 2
GLA_SCAN_BATCH = 4
TMT = 512
TM_WIDE = 1024
HALO = 8
FF_CHUNK = 256
FF_GROUP = 4
Z_COLS = 3072

COL_NA = 0
COL_MLA = 768
COL_GLA_Q = 1152
COL_GLA_K = 1280
COL_GLA_G = 1408
COL_GLA_V = 1536
COL_GLA_O = 1792
COL_RW = 2048

VMEM_LIMIT_V7X = 56 * 1024 * 1024


def _cparams(n_axes):
    return pltpu.CompilerParams(dimension_semantics=("arbitrary",) * n_axes,
                                vmem_limit_bytes=VMEM_LIMIT_V7X)


def _const_spec(shape, layer=None):
    nd = len(shape)
    if layer is None:
        return pl.BlockSpec(shape, lambda *_: (0,) * nd, pipeline_mode=pl.Buffered(1))
    return pl.BlockSpec((None,) + tuple(shape), lambda *_: (layer,) + (0,) * nd, pipeline_mode=pl.Buffered(1))


def _dot(a, b):
    return jnp.dot(a, b, preferred_element_type=F32)


def _split3(x):
    hi = x.astype(BF16)
    r1 = x - hi.astype(F32)
    mid = r1.astype(BF16)
    lo = (r1 - mid.astype(F32)).astype(BF16)
    return hi, mid, lo


def _dot_sel_rhs(x, m):
    hi, mid, lo = _split3(x)
    return _dot(hi, m) + (_dot(mid, m) + _dot(lo, m))


def _sigmoid(x):
    return 1.0 / (1.0 + jnp.exp(-x))


def _silu_gain(x):
    return 0.5 + 0.5 * jnp.tanh(0.5 * x)


def _softplus(x):
    return jnp.maximum(x, 0.0) + jnp.log1p(jnp.exp(-jnp.abs(x)))


def _rms(x):
    return x * lax.rsqrt(jnp.mean(x * x, axis=-1, keepdims=True) + EPS)


def _norm_mod(x, g, shift, scale):
    return (_rms(x) * g) * (1.0 + scale) + shift


def _lane_head(width, per_head):
    return lax.broadcasted_iota(jnp.int32, (1, width), 1) // per_head


def _block_ones(n, blk):
    i = np.arange(n) // blk
    return (i[:, None] == i[None, :]).astype(np.float32)


def _mod_kernel(c_ref, w_ref, b_ref, o_ref):
    cv = c_ref[...]
    s = cv * _sigmoid(cv)
    o_ref[0] = _dot(s.astype(BF16), w_ref[0].astype(BF16)) + b_ref[0]


def _modulation(cvecs, w_mod, b_mod):
    L, D, N = w_mod.shape
    tn = 1536
    return pl.pallas_call(
        _mod_kernel,
        grid=(L, N // tn),
        in_specs=[pl.BlockSpec((8, D), lambda l, n: (0, 0)),
                  pl.BlockSpec((1, D, tn), lambda l, n: (l, 0, n)),
                  pl.BlockSpec((1, 1, tn), lambda l, n: (l, 0, n))],
        out_specs=pl.BlockSpec((1, 8, tn), lambda l, n: (l, 0, n)),
        out_shape=jax.ShapeDtypeStruct((L, 8, N), F32),
        compiler_params=_cparams(2),
    )(cvecs, w_mod, b_mod.reshape(L, 1, N))


def _tok_tiles(S, T, tm=TMT):
    assert T % tm == 0 and 0 < S - T <= tm
    return T // tm + 1, T // tm


def _tok_spec(width, col_block=0, tm=TMT):
    return pl.BlockSpec((1, tm, width), lambda b, i: (b, i, col_block))


def _mod_spec(nlat, layer):
    return pl.BlockSpec((None, 1, 1, 8, D_MODEL), lambda b, i: (layer, b, jnp.where(i < nlat, 1, 0), 0, 0))


def _halo_specs(width, col_block, S, tm=TMT):
    per = tm // HALO
    last = S // HALO - 1
    prev = pl.BlockSpec((1, HALO, width), lambda b, i: (b, jnp.maximum(i * per - 1, 0), col_block))
    nxt = pl.BlockSpec((1, HALO, width), lambda b, i: (b, jnp.minimum((i + 1) * per, last), col_block))
    return prev, nxt


def _neighbour_ok(i, nlat):
    prev_ok = jnp.logical_and(i != 0, i != nlat)
    next_ok = i < nlat - 1
    return prev_ok, next_ok


def _for_tile_rows(nlat, tc, body, tm=TMT):
    i = pl.program_id(1)

    @pl.when(i < nlat)
    def _():
        body(tm)

    @pl.when(i >= nlat)
    def _():
        body(tc)


def _in_proj_kernel(x_ref, mod_ref, g_ref, w_ref, z_ref, *, nlat, tc, tm):
    def body(nr):
        m = mod_ref[0, 0]
        h = _norm_mod(x_ref[0, :nr], g_ref[...], m[0:1], m[1:2])
        z_ref[0, :nr] = _dot(h.astype(BF16), w_ref[...])

    _for_tile_rows(nlat, tc, body, tm)


def _in_proj(xs, modtab, g_pre, w_in_p, T, layer):
    B, S, D = xs.shape
    tm = TM_WIDE
    ntiles, nlat = _tok_tiles(S, T, tm)
    return pl.pallas_call(
        functools.partial(_in_proj_kernel, nlat=nlat, tc=S - T, tm=tm),
        grid=(B, ntiles),
        in_specs=[_tok_spec(D, 0, tm), _mod_spec(nlat, layer), _const_spec((1, D), layer),
                  _const_spec((D, Z_COLS), layer)],
        out_specs=_tok_spec(Z_COLS, 0, tm),
        out_shape=jax.ShapeDtypeStruct((B, S, Z_COLS), F32),
        compiler_params=_cparams(2),
    )(xs, modtab, g_pre, w_in_p)


def _out_proj_kernel(x_ref, na_ref, mla_ref, yf_ref, yb_ref, bon_ref, g_ref, of_ref, ob_ref, og_ref,
                     mod_ref, gpost_ref, w_ref, lnw_ref, lnb_ref, gn_ref, ones_ref, o_ref, *, nlat, tc, tm):
    def body(nr):
        m = mod_ref[0, 0]
        ones = ones_ref[...]
        inv_n = 1.0 / HEAD_DIM
        y = yf_ref[0, :nr] + yb_ref[0, :nr]
        yc = y - _dot_sel_rhs(y, ones) * inv_n
        var = _dot_sel_rhs(yc * yc, ones) * inv_n
        y_rw = (yc * lax.rsqrt(var + RW_GN_EPS) * lnw_ref[...] + lnb_ref[...] + bon_ref[0, :nr]) * g_ref[0, :nr]
        o = of_ref[0, :nr] + ob_ref[0, :nr]
        ms = _dot_sel_rhs(o * o, ones) * inv_n
        og = og_ref[0, :nr]
        y_gla = (o * lax.rsqrt(ms + EPS) * gn_ref[...]) * (og * _silu_gain(og))
        y = jnp.concatenate([na_ref[0, :nr], mla_ref[0, :nr], y_rw.astype(BF16), y_gla.astype(BF16)], axis=-1)
        y = _dot(y, w_ref[...])
        o_ref[0, :nr] = x_ref[0, :nr] + m[2:3] * (_rms(y) * gpost_ref[...])

    _for_tile_rows(nlat, tc, body, tm)


def _out_proj(xs, z, y_na, y_mla, yf, yb, bon, g, of, ob, modtab, g_post, w_out_b, ln_w, ln_b, gla_norm,
              ones_b, T, layer):
    B, S, D = xs.shape
    tm = TM_WIDE
    ntiles, nlat = _tok_tiles(S, T, tm)
    grp = _tok_spec(GROUP_W, 0, tm)
    vec = _const_spec((1, GROUP_W), layer)
    return pl.pallas_call(
        functools.partial(_out_proj_kernel, nlat=nlat, tc=S - T, tm=tm),
        grid=(B, ntiles),
        in_specs=[_tok_spec(D, 0, tm), grp, grp, grp, grp, grp, grp, grp, grp,
                  _tok_spec(GROUP_W, COL_GLA_O // GROUP_W, tm),
                  _mod_spec(nlat, layer), _const_spec((1, D), layer), _const_spec((D, D), layer), vec, vec, vec,
                  _const_spec((GROUP_W, GROUP_W))],
        out_specs=_tok_spec(D, 0, tm),
        out_shape=jax.ShapeDtypeStruct((B, S, D), F32),
        compiler_params=_cparams(2),
    )(xs, y_na, y_mla, yf, yb, bon, g, of, ob, z, modtab, g_post, w_out_b, ln_w, ln_b, gla_norm, ones_b)


def _ffn_kernel(xp_ref, x_ref, xn_ref, mod_ref, gpre_ref, gpost_ref, wup_ref, cw_ref, cb_ref,
                wdn_ref, o_ref, *, nlat, tc):
    i = pl.program_id(1)
    prev_ok, next_ok = _neighbour_ok(i, nlat)
    nchunks = D_FF // FF_CHUNK

    def body(nr):
        m = mod_ref[0, 0]
        x = x_ref[0, :nr]
        xe = jnp.concatenate([xp_ref[0], x, xn_ref[0]], axis=0)
        n = nr + 2 * HALO
        h = _norm_mod(xe, gpre_ref[...], m[3:4], m[4:5])
        row = lax.broadcasted_iota(jnp.int32, (n, 1), 0)
        valid = jnp.logical_or(jnp.logical_and(row >= HALO, row < HALO + nr),
                               jnp.logical_or(jnp.logical_and(row < HALO, prev_ok),
                                              jnp.logical_and(row >= HALO + nr, next_ok)))
        hb = jnp.where(valid, h, 0.0).astype(BF16)

        def up(c):
            return [_dot(hb, wup_ref[:, base + c * FF_CHUNK:base + (c + 1) * FF_CHUNK]) for base in (0, D_FF)]

        def conv(z, lo):
            cw = cw_ref[:, lo:lo + FF_CHUNK]
            return (cb_ref[:, lo:lo + FF_CHUNK]
                    + pltpu.roll(z, 1, 0)[HALO:HALO + nr] * cw[0:1]
                    + z[HALO:HALO + nr] * cw[1:2]
                    + pltpu.roll(z, n - 1, 0)[HALO:HALO + nr] * cw[2:3])

        acc = None
        group = []
        z_next = up(0)
        for c in range(nchunks):
            z_val, z_gate = z_next
            if c + 1 < nchunks:
                z_next = up(c + 1)
            val = conv(z_val, c * FF_CHUNK)
            gate = conv(z_gate, D_FF + c * FF_CHUNK)
            group.append(((gate * _silu_gain(gate)) * val).astype(BF16))
            if len(group) == FF_GROUP or c + 1 == nchunks:
                lo = (c + 1 - len(group)) * FF_CHUNK
                part = _dot(jnp.concatenate(group, axis=1), wdn_ref[lo:(c + 1) * FF_CHUNK, :])
                acc = part if acc is None else acc + part
                group = []
        o_ref[0, :nr] = x + m[5:6] * (_rms(acc) * gpost_ref[...])

    _for_tile_rows(nlat, tc, body)


def _ffn(xs, modtab, g_pre, g_post, w_up_b, conv_w, conv_b, w_dn_b, T, layer, latent_only):
    B, S, D = xs.shape
    ntiles, nlat = _tok_tiles(S, T)
    prev, nxt = _halo_specs(D, 0, S)
    return pl.pallas_call(
        functools.partial(_ffn_kernel, nlat=nlat, tc=S - T),
        grid=(B, nlat if latent_only else ntiles),
        in_specs=[prev, _tok_spec(D), nxt, _mod_spec(nlat, layer), _const_spec((1, D), layer),
                  _const_spec((1, D), layer), _const_spec((D, 2 * D_FF), layer),
                  _const_spec((3, 2 * D_FF), layer), _const_spec((1, 2 * D_FF), layer),
                  _const_spec((D_FF, D), layer)],
        out_specs=_tok_spec(D),
        out_shape=jax.ShapeDtypeStruct((B, T if latent_only else S, D), F32),
        compiler_params=_cparams(2),
    )(xs, xs, xs, modtab, g_pre, g_post, w_up_b, conv_w, conv_b, w_dn_b)


def _natten_bias(rpb, rows):
    rt = TQ // GRID_W
    j = np.arange(GRID_W)
    col_start = np.clip(j - NA_COLS // 2, 0, GRID_W - NA_COLS)
    col_in = (j[None, :] >= col_start[:, None]) & (j[None, :] < col_start[:, None] + NA_COLS)
    edge = GRID_W - NA_COLS
    ext = jnp.concatenate([jnp.repeat(rpb[..., :1], edge, axis=-1), rpb,
                           jnp.repeat(rpb[..., -1:], edge, axis=-1)], axis=-1).astype(F32) * LOG2E
    bq = jnp.stack([ext[..., GRID_W - 1 - q:2 * GRID_W - 1 - q] for q in range(GRID_W)], axis=2)
    L = rpb.shape[0]
    bq = jnp.where(col_in[:, None, :], bq, -jnp.inf).reshape(L, N_HEADS, GRID_W, -1)
    cases = []
    for r0 in (0, rt, rows - rt):
        us = min(max(r0 - NA_ROWS // 2, 0), rows - NA_UNION)
        per_row = []
        for r in range(r0, r0 + rt):
            rs = min(max(r - NA_ROWS // 2, 0), rows - NA_ROWS)
            first = rs - r + (NA_ROWS - 1)
            pre, post = rs - us, us + NA_UNION - (rs + NA_ROWS)
            per_row.append(jnp.pad(bq[..., first * GRID_W:(first + NA_ROWS) * GRID_W],
                                   ((0, 0), (0, 0), (0, 0), (pre * GRID_W, post * GRID_W)),
                                   constant_values=-jnp.inf))
        cases.append(jnp.concatenate(per_row, axis=2))
    return jnp.stack(cases, axis=1)


def _natten_kernel(q_ref, k_ref, v_ref, qc_ref, kc_ref, vc_ref, bias_ref, o_ref, *, nlat, rows):
    j = pl.program_id(1)
    lane_h = _lane_head(GROUP_W, HEAD_DIM)
    kct = kc_ref[0].T.astype(BF16)
    vc = vc_ref[0].astype(BF16)
    scale = HEAD_DIM ** -0.5 * LOG2E
    nwin = NA_UNION * GRID_W

    @pl.when(j < nlat)
    def _():
        us = jnp.clip(j * (TQ // GRID_W) - NA_ROWS // 2, 0, rows - NA_UNION)
        start = pl.multiple_of(us * GRID_W, GRID_W)
        q = q_ref[0] * scale
        kwt = k_ref[0, pl.ds(start, nwin), :].T.astype(BF16)
        vw = v_ref[0, pl.ds(start, nwin), :].astype(BF16)
        acc = jnp.zeros((TQ, GROUP_W), F32)

        def logits(h):
            qh = jnp.where(lane_h == h, q, 0.0).astype(BF16)
            return _dot(qh, kwt), _dot(qh, kct)

        s_next = logits(0)
        for h in range(N_HEADS):
            hm = lane_h == h
            s_w, s_c = s_next
            if h + 1 < N_HEADS:
                s_next = logits(h + 1)
            s_w = s_w + bias_ref[0, h]
            mx = jnp.maximum(jnp.max(s_w, axis=-1, keepdims=True), jnp.max(s_c, axis=-1, keepdims=True))
            p_w = jnp.exp2(s_w - mx)
            p_c = jnp.exp2(s_c - mx)
            den = jnp.sum(p_w, axis=-1, keepdims=True) + jnp.sum(p_c, axis=-1, keepdims=True)
            o = _dot(p_w.astype(BF16), vw) + _dot(p_c.astype(BF16), vc)
            acc = acc + jnp.where(hm, o * (1.0 / den), 0.0)
        o_ref[0] = acc.astype(o_ref.dtype)

    @pl.when(j >= nlat)
    def _():
        q = qc_ref[0] * scale
        acc = jnp.zeros((TQ, GROUP_W), F32)
        for h in range(N_HEADS):
            hm = lane_h == h
            s = _dot(jnp.where(hm, q, 0.0).astype(BF16), kct)
            p = jnp.exp2(s - jnp.max(s, axis=-1, keepdims=True))
            den = jnp.sum(p, axis=-1, keepdims=True)
            acc = acc + jnp.where(hm, _dot(p.astype(BF16), vc) * (1.0 / den), 0.0)
        o_ref[0] = acc.astype(o_ref.dtype)


def _natten(z, bias, T, layer):
    B, S, _ = z.shape
    nlat = T // TQ
    rows = T // GRID_W
    assert rows >= 16 and rows % (TQ // GRID_W) == 0 and S - T == TQ
    cb = COL_NA // GROUP_W
    lat = lambda c: pl.BlockSpec((1, T, GROUP_W), lambda b, j: (b, 0, cb + c))
    ctx = lambda c: pl.BlockSpec((1, TQ, GROUP_W), lambda b, j: (b, nlat, cb + c))
    case = lambda j: jnp.where(j == 0, 0, jnp.where(j >= nlat - 1, 2, 1))
    return pl.pallas_call(
        functools.partial(_natten_kernel, nlat=nlat, rows=rows),
        grid=(B, nlat + 1),
        in_specs=[pl.BlockSpec((1, TQ, GROUP_W), lambda b, j: (b, jnp.minimum(j, nlat - 1), cb)),
                  lat(1), lat(2), ctx(0), ctx(1), ctx(2),
                  pl.BlockSpec((None, 1, N_HEADS, TQ, NA_UNION * GRID_W), lambda b, j: (layer, case(j), 0, 0, 0))],
        out_specs=pl.BlockSpec((1, TQ, GROUP_W), lambda b, j: (b, j, 0)),
        out_shape=jax.ShapeDtypeStruct((B, S, GROUP_W), BF16),
        compiler_params=_cparams(2),
    )(z, z, z, z, z, z, bias)


def _rope_tables(T, Tc):
    t = np.arange(T)
    row = (t // GRID_W).astype(np.float32)
    col = (t % GRID_W).astype(np.float32)
    d = MLA_ROPE // 2
    inv = (np.float32(ROPE_THETA) ** (-np.arange(0, d, 2, dtype=np.float32) / np.float32(d))).astype(np.float32)
    cs, sn = [], []
    for pos in (row, col):
        ang = (pos[:, None] * inv[None, :]).astype(np.float32)
        cs += [np.cos(ang), np.cos(ang)]
        sn += [-np.sin(ang), np.sin(ang)]
    pad = MLA_HEAD_PAD - MLA_NOPE - MLA_ROPE
    f32 = np.float32
    cos = np.concatenate([np.ones((T, MLA_NOPE), f32)] + cs + [np.ones((T, pad), f32)], axis=1)
    sin = np.concatenate([np.zeros((T, MLA_NOPE), f32)] + sn + [np.zeros((T, pad), f32)], axis=1)
    cos = np.concatenate([cos, np.ones((Tc, MLA_HEAD_PAD), f32)], axis=0)
    sin = np.concatenate([sin, np.zeros((Tc, MLA_HEAD_PAD), f32)], axis=0)
    return jnp.asarray(cos, F32), jnp.asarray(sin, F32)


def _rope_swap_perm():
    q = MLA_ROPE // 4
    return np.concatenate([np.arange(q, 2 * q), np.arange(0, q), np.arange(3 * q, 4 * q), np.arange(2 * q, 3 * q)])


def _mla_weights(w_uq, w_ukv):
    L = w_uq.shape[0]
    wq = w_uq.reshape(L, MLA_Q_RANK, N_HEADS, MLA_NOPE + MLA_ROPE)
    pad = MLA_HEAD_PAD - MLA_NOPE - MLA_ROPE
    zq = jnp.zeros((L, MLA_Q_RANK, N_HEADS, pad), F32)
    wq1 = jnp.concatenate([wq, zq], axis=-1).reshape(L, MLA_Q_RANK, N_HEADS * MLA_HEAD_PAD)
    rope_sw = wq[..., MLA_NOPE:][..., _rope_swap_perm()]
    wq2 = jnp.concatenate([jnp.zeros((L, MLA_Q_RANK, N_HEADS, MLA_NOPE), F32), rope_sw, zq], axis=-1)
    wq2 = wq2.reshape(L, MLA_Q_RANK, N_HEADS * MLA_HEAD_PAD)
    wkv = w_ukv.reshape(L, MLA_KV_RANK, N_HEADS, 2 * MLA_NOPE)
    wk = jnp.concatenate([wkv[..., :MLA_NOPE], jnp.zeros((L, MLA_KV_RANK, N_HEADS, MLA_HEAD_PAD - MLA_NOPE), F32)], axis=-1)
    wk = wk.reshape(L, MLA_KV_RANK, N_HEADS * MLA_HEAD_PAD)
    wv_t = wkv[..., MLA_NOPE:].reshape(L, MLA_KV_RANK, N_HEADS * MLA_NOPE).transpose(0, 2, 1)
    return wq1.astype(BF16), wq2.astype(BF16), wk.astype(BF16), wv_t.astype(BF16)


def _rope_place():
    e = np.zeros((MLA_ROPE, N_HEADS * MLA_HEAD_PAD), np.float32)
    for h in range(N_HEADS):
        e[np.arange(MLA_ROPE), h * MLA_HEAD_PAD + MLA_NOPE + np.arange(MLA_ROPE)] = 1.0
    return jnp.asarray(e, BF16)


def _mla_up_kernel(z_ref, cos_ref, sin_ref, qn_ref, kvn_ref, wq1_ref, wq2_ref, wk_ref, wv_ref, e_ref,
                   qt_ref, k_ref, vt_ref, *, nlat, tc):
    def body(nr):
        z = z_ref[0, :nr]
        ckv = z[:, :MLA_KV_RANK]
        cq = z[:, MLA_KV_RANK:MLA_KV_RANK + MLA_Q_RANK]
        kr = z[:, MLA_KV_RANK + MLA_Q_RANK:MLA_KV_RANK + MLA_Q_RANK + MLA_ROPE]
        krs = z[:, MLA_KV_RANK + MLA_Q_RANK + MLA_ROPE:]
        cos = jnp.concatenate([cos_ref[:nr]] * N_HEADS, axis=-1)
        sin = jnp.concatenate([sin_ref[:nr]] * N_HEADS, axis=-1)
        nq = (_rms(cq) * qn_ref[...]).astype(BF16)
        nkv_f = _rms(ckv) * kvn_ref[...]
        nkv = nkv_f.astype(BF16)
        q = _dot(nq, wq1_ref[...]) * cos + _dot(nq, wq2_ref[...]) * sin
        scale = (MLA_NOPE + MLA_ROPE) ** -0.5 * LOG2E
        qt_ref[0, :, :nr] = (q * scale).T.astype(BF16)
        k = _dot(nkv, wk_ref[...]) + _dot_sel_rhs(kr, e_ref[...]) * cos + _dot_sel_rhs(krs, e_ref[...]) * sin
        k_ref[0, :nr] = k.astype(BF16)
        vt_ref[0, :, :nr] = _dot(wv_ref[...], nkv_f.T.astype(BF16)).astype(BF16)

    _for_tile_rows(nlat, tc, body)


def _mla_up(z, cos, sin, q_norm, kv_norm, wq1, wq2, wk, wv, place, T, layer):
    B, S, _ = z.shape
    ntiles, nlat = _tok_tiles(S, T)
    HP = N_HEADS * MLA_HEAD_PAD
    zw = MLA_KV_RANK + MLA_Q_RANK + 2 * MLA_ROPE
    tab = pl.BlockSpec((TMT, MLA_HEAD_PAD), lambda b, i: (i, 0))
    return pl.pallas_call(
        functools.partial(_mla_up_kernel, nlat=nlat, tc=S - T),
        grid=(B, ntiles),
        in_specs=[_tok_spec(zw, COL_MLA // zw), tab, tab,
                  _const_spec((1, MLA_Q_RANK), layer), _const_spec((1, MLA_KV_RANK), layer),
                  _const_spec((MLA_Q_RANK, HP), layer), _const_spec((MLA_Q_RANK, HP), layer),
                  _const_spec((MLA_KV_RANK, HP), layer), _const_spec((GROUP_W, MLA_KV_RANK), layer),
                  _const_spec((MLA_ROPE, HP))],
        out_specs=[pl.BlockSpec((1, HP, TMT), lambda b, i: (b, 0, i)), _tok_spec(HP),
                   pl.BlockSpec((1, GROUP_W, TMT), lambda b, i: (b, 0, i))],
        out_shape=[jax.ShapeDtypeStruct((B, HP, S), BF16),
                   jax.ShapeDtypeStruct((B, S, HP), BF16),
                   jax.ShapeDtypeStruct((B, GROUP_W, S), BF16)],
        compiler_params=_cparams(2),
    )(z, cos, sin, q_norm, kv_norm, wq1, wq2, wk, wv, place)


def _mla_attn_kernel(qt_ref, k_ref, vt_ref, o_ref, *, nlat, T):
    j = pl.program_id(1)

    def attend(lo, hi):
        def logits(h):
            hp = slice(h * MLA_HEAD_PAD, (h + 1) * MLA_HEAD_PAD)
            return _dot(k_ref[0, lo:hi, hp], qt_ref[0, hp, :])

        ahead = 2
        pending = [logits(h) for h in range(ahead)]
        pieces = []
        for h in range(N_HEADS):
            s = pending.pop(0)
            if h + ahead < N_HEADS:
                pending.append(logits(h + ahead))
            p = jnp.exp2(s - jnp.max(s, axis=0, keepdims=True))
            den = jnp.sum(p, axis=0, keepdims=True)
            o = _dot(vt_ref[0, h * HEAD_DIM:(h + 1) * HEAD_DIM, lo:hi], p.astype(BF16))
            pieces.append(o * (1.0 / den))
        o_ref[0] = jnp.concatenate(pieces, axis=0).T.astype(o_ref.dtype)

    @pl.when(j < nlat)
    def _():
        attend(0, T + TQ)

    @pl.when(j >= nlat)
    def _():
        attend(T, T + TQ)


def _mla_attn(qt, k, vt, T):
    B, S, HP = k.shape
    assert T % MLA_TQ == 0 and S - T <= MLA_TQ
    nlat = T // MLA_TQ
    return pl.pallas_call(
        functools.partial(_mla_attn_kernel, nlat=nlat, T=T),
        grid=(B, nlat + 1),
        in_specs=[pl.BlockSpec((1, HP, MLA_TQ), lambda b, j: (b, 0, j)),
                  pl.BlockSpec((1, S, HP), lambda b, j: (b, 0, 0)),
                  pl.BlockSpec((1, GROUP_W, S), lambda b, j: (b, 0, 0))],
        out_specs=pl.BlockSpec((1, MLA_TQ, GROUP_W), lambda b, j: (b, j, 0)),
        out_shape=jax.ShapeDtypeStruct((B, S, GROUP_W), BF16),
        compiler_params=_cparams(2),
    )(qt, k, vt)


def _scan_masks():
    t = np.arange(CHUNK)
    inc = np.stack([t[:, None] >= t[None, :], t[:, None] <= t[None, :]]).astype(np.float32)
    strict = np.stack([t[:, None] > t[None, :], t[:, None] < t[None, :]]).astype(np.float32)
    return jnp.asarray(np.tile(inc, (1, 1, N_HEADS))), jnp.asarray(np.tile(strict, (1, 1, N_HEADS)))


def _cumsum_rows(x, reverse):
    n = x.shape[0]
    row = lax.broadcasted_iota(jnp.int32, (n, 1), 0)
    sh = 1
    while sh < n:
        if reverse:
            x = x + jnp.where(row < n - sh, pltpu.roll(x, n - sh, 0), 0.0)
        else:
            x = x + jnp.where(row >= sh, pltpu.roll(x, sh, 0), 0.0)
        sh *= 2
    return x


def _block_diag(x, bm_b):
    return _tile_rows(x.astype(BF16)) * bm_b


def _dot_nt(a, b):
    return lax.dot_general(a, b, (((1,), (1,)), ((), ())), preferred_element_type=F32)


def _fwd_tile(p, nlat):
    return jnp.where(p == 0, nlat, p - 1)


def _bwd_tile(p, nlat):
    return jnp.where(p == 0, nlat, nlat - p)


def _chunk_order(d):
    nch = TQ // CHUNK
    return range(nch) if d == 0 else range(nch - 1, -1, -1)


def _tile_rows(x):
    return jnp.concatenate([x] * N_HEADS, axis=0)


def _rw_prep_kernel(zp_ref, z_ref, zn_ref, mu_ref, kk_ref, ka_ref, rk_ref, w0_ref, a0_ref, wup_ref,
                    aup_ref, gup_ref, ones_ref,
                    r_ref, v_ref, ah_ref, g_ref, bon_ref, lw_ref, kd_ref, bd_ref, *, nlat, tc):
    i = pl.program_id(1)
    prev_ok, next_ok = _neighbour_ok(i, nlat)

    def body(nr):
        z = z_ref[0, :nr]
        row = lax.broadcasted_iota(jnp.int32, (nr, 1), 0)
        before = jnp.where(prev_ok, zp_ref[0, HALO - 1:HALO, :], 0.0)
        after = jnp.where(next_ok, zn_ref[0, 0:1, :], 0.0)
        zprev = jnp.where(row == 0, before, pltpu.roll(z, 1, 0))
        znext = jnp.where(row == nr - 1, after, pltpu.roll(z, nr - 1, 0))
        zs = z + mu_ref[...] * (0.5 * (zprev + znext) - z)
        r = zs[:, 0:GROUP_W]
        k = zs[:, GROUP_W:2 * GROUP_W]
        v = zs[:, 2 * GROUP_W:3 * GROUP_W]
        low = zs[:, 3 * GROUP_W:3 * GROUP_W + 128]
        gd = zs[:, 3 * GROUP_W + 128:]
        ones = ones_ref[...]
        kk = k * kk_ref[...]
        kk = kk * lax.rsqrt(_dot_sel_rhs(kk * kk, ones) + 1e-12)
        wl = _dot(jnp.tanh(low).astype(BF16), wup_ref[...])
        al = _dot(low.astype(BF16), aup_ref[...])
        ksum = jnp.zeros((nr, GROUP_W), F32)
        for d in range(2):
            w_raw = -_softplus(-(w0_ref[d:d + 1, :] + wl[:, d * GROUP_W:(d + 1) * GROUP_W])) - 0.5
            lw_ref[d, 0, :nr] = -jnp.exp(w_raw)
            a = _sigmoid(a0_ref[d:d + 1, :] + al[:, d * GROUP_W:(d + 1) * GROUP_W])
            kd = k * (1.0 + (a - 1.0) * ka_ref[...])
            kd_ref[d, 0, :nr] = kd
            bd_ref[d, 0, :nr] = kk * a
            ksum = ksum + kd
        r_ref[0, :nr] = r
        v_ref[0, :nr] = v
        ah_ref[0, :nr] = -kk
        g_ref[0, :nr] = _dot(_sigmoid(gd).astype(BF16), gup_ref[...])
        bon_ref[0, :nr] = _dot_sel_rhs(r * ksum * rk_ref[...], ones) * v

    _for_tile_rows(nlat, tc, body)


def _rw_prep(z, mu, k_k, k_a, r_k, w0, a0, wup_p, aup_p, gup_b, ones_b, T, layer):
    B, S, _ = z.shape
    ntiles, nlat = _tok_tiles(S, T)
    W = 4 * GROUP_W
    cb = COL_RW // W
    prev, nxt = _halo_specs(W, cb, S)
    one = _tok_spec(GROUP_W)
    two = pl.BlockSpec((2, 1, TMT, GROUP_W), lambda b, i: (0, b, i, 0))
    s1 = jax.ShapeDtypeStruct((B, S, GROUP_W), F32)
    s2 = jax.ShapeDtypeStruct((2, B, S, GROUP_W), F32)
    vec = _const_spec((1, GROUP_W), layer)
    return pl.pallas_call(
        functools.partial(_rw_prep_kernel, nlat=nlat, tc=S - T),
        grid=(B, ntiles),
        in_specs=[prev, _tok_spec(W, cb), nxt,
                  _const_spec((1, W), layer), vec, vec, vec,
                  _const_spec((2, GROUP_W), layer), _const_spec((2, GROUP_W), layer),
                  _const_spec((128, 2 * GROUP_W), layer), _const_spec((128, 2 * GROUP_W), layer),
                  _const_spec((128, GROUP_W), layer), _const_spec((GROUP_W, GROUP_W))],
        out_specs=[one, one, one, one, one, two, two, two],
        out_shape=[s1, s1, s1, s1, s1, s2, s2, s2],
        compiler_params=_cparams(2),
    )(z, z, z, mu, k_k, k_a, r_k, w0, a0, wup_p, aup_p, gup_b, ones_b)


def _rw_scan_kernel(rf_ref, rb_ref, vf_ref, vb_ref, af_ref, ab_ref, lwf_ref, lwb_ref, kdf_ref, kdb_ref,
                    bdf_ref, bdb_ref, inc_ref, strict_ref, bm_ref, yf_ref, yb_ref, s_ref):
    p = pl.program_id(1)

    @pl.when(p == 0)
    def _():
        s_ref[...] = jnp.zeros_like(s_ref)

    n = N_HEADS * CHUNK
    bm = bm_ref[...]
    eye = (lax.broadcasted_iota(jnp.int32, (CHUNK, n), 1) % CHUNK
           == lax.broadcasted_iota(jnp.int32, (CHUNK, n), 0)).astype(F32)
    refs = ((rf_ref, vf_ref, af_ref, lwf_ref, kdf_ref, bdf_ref, yf_ref),
            (rb_ref, vb_ref, ab_ref, lwb_ref, kdb_ref, bdb_ref, yb_ref))
    orders = [list(_chunk_order(d)) for d in range(2)]
    bm_b = bm.astype(BF16)
    bd_of = lambda m: _block_diag(m, bm_b)

    nb = s_ref.shape[0] // 2
    inst = [(bi, step, d) for step in range(TQ // CHUNK) for bi in range(nb) for d in range(2)]
    idx = range(len(inst))
    rows = [pl.ds(orders[d][step] * CHUNK, CHUNK) for _, step, d in inst]

    ar, r_t, v_s, a_s, b_s, k_s, bk_t, decay, v_in = [], [], [], [], [], [], [], [], []
    for (bi, step, d), sl in zip(inst, rows):
        r_ref, v_ref, a_ref, lw_ref, kd_ref, bd_ref, _ = refs[d]
        lw = lw_ref[0, bi, sl, :]
        kd = kd_ref[0, bi, sl, :]
        bd = bd_ref[0, bi, sl, :]
        v = v_ref[bi, sl, :]
        cs = _cumsum_rows(lw, reverse=(d == 1))
        tot = cs[CHUNK - 1:CHUNK, :] if d == 0 else cs[0:1, :]
        e_neg = jnp.exp(-cs)
        e_hat = jnp.exp(tot - cs)
        a_t = a_ref[bi, sl, :] * jnp.exp(cs - lw)
        rt = r_ref[bi, sl, :] * jnp.exp(cs)
        ar.append(jnp.concatenate([a_t, rt], axis=0).astype(BF16))
        r_t.append(rt)
        v_in.append(v)
        v_s.append(bd_of(v))
        a_s.append(bd_of(a_t))
        b_s.append(bd_of(bd * e_neg))
        k_s.append(bd_of(kd * e_neg))
        t = jnp.concatenate([bd * e_hat, kd * e_hat,
                             jnp.broadcast_to(jnp.exp(tot), (2 * CHUNK, GROUP_W))], axis=0).T
        bk_t.append(t[:, :2 * CHUNK].astype(BF16))
        decay.append(jnp.concatenate([t[:, 2 * CHUNK:]] * 2, axis=1))
    g_b = [_dot_nt(ar[i], b_s[i]) for i in idx]
    g_k = [_dot_nt(ar[i], k_s[i]) for i in idx]
    pw = [g_b[i][:CHUNK] * strict_ref[inst[i][2]] for i in idx]
    ak = [(g_k[i][:CHUNK] * strict_ref[inst[i][2]]).astype(BF16) for i in idx]
    rbk = [jnp.concatenate([g_b[i][CHUNK:] * inc_ref[inst[i][2]], g_k[i][CHUNK:] * inc_ref[inst[i][2]]],
                           axis=1).astype(BF16) for i in idx]
    x = [eye + pw[i] for i in idx]
    pw = [_dot(pw[i].astype(BF16), bd_of(pw[i])) for i in idx]
    for _ in range(4):
        px = [_dot(jnp.concatenate([pw[i], x[i]], axis=0).astype(BF16), bd_of(pw[i])) for i in idx]
        x = [x[i] + px[i][CHUNK:] for i in idx]
        pw = [px[i][:CHUNK] for i in idx]
    x = [(x[i] + _dot(x[i].astype(BF16), bd_of(pw[i]))).astype(BF16) for i in idx]
    akv = [_dot(ak[i], v_s[i]) for i in idx]
    p12 = [_dot(x[i], jnp.concatenate([bd_of(akv[i]), a_s[i]], axis=1)) for i in idx]
    p1_s = [bd_of(p12[i][:, :GROUP_W]) for i in idx]
    p2_s = [bd_of(p12[i][:, GROUP_W:]) for i in idx]
    q = [r_t[i] + _dot(rbk[i][:, :n], p2_s[i]) for i in idx]
    y_c = [_dot(rbk[i], jnp.concatenate([p1_s[i], v_s[i]], axis=0)) for i in idx]
    zero = jnp.zeros((CHUNK, GROUP_W), F32)
    gc = [_dot(bk_t[i], jnp.concatenate(
        [jnp.concatenate([p12[i][:, GROUP_W:], p12[i][:, :GROUP_W]], axis=1),
         jnp.concatenate([zero, v_in[i]], axis=1)], axis=0).astype(BF16)) for i in idx]
    gq = [jnp.concatenate([gc[i][:, :GROUP_W] * bm, q[i]], axis=0).astype(BF16) for i in idx]
    c_s = [gc[i][:, GROUP_W:] * bm for i in idx]

    s = [s_ref[k] for k in range(2 * nb)]
    for i, ((bi, step, d), sl) in enumerate(zip(inst, rows)):
        k = 2 * bi + d
        m = _dot(gq[i], s[k].astype(BF16))
        refs[d][6][bi, sl, :] = m[n:] + y_c[i]
        s[k] = decay[i] * s[k] + m[:n] + c_s[i]
    for k in range(2 * nb):
        s_ref[k] = s[k]


def _rw_scan(r, v, ah, lw, kd, bd, masks, bm, T):
    B, S, _ = r.shape
    nlat = T // TQ
    inc, strict = masks
    n = N_HEADS * CHUNK
    nb = RW_SCAN_BATCH if B % RW_SCAN_BATCH == 0 else 1
    fwd = pl.BlockSpec((nb, TQ, GROUP_W), lambda b, p: (b, _fwd_tile(p, nlat), 0))
    bwd = pl.BlockSpec((nb, TQ, GROUP_W), lambda b, p: (b, _bwd_tile(p, nlat), 0))
    fwd2 = pl.BlockSpec((1, nb, TQ, GROUP_W), lambda b, p: (0, b, _fwd_tile(p, nlat), 0))
    bwd2 = pl.BlockSpec((1, nb, TQ, GROUP_W), lambda b, p: (1, b, _bwd_tile(p, nlat), 0))
    out = jax.ShapeDtypeStruct((B, S, GROUP_W), F32)
    return pl.pallas_call(
        _rw_scan_kernel,
        grid=(B // nb, nlat + 1),
        in_specs=[fwd, bwd, fwd, bwd, fwd, bwd, fwd2, bwd2, fwd2, bwd2, fwd2, bwd2,
                  _const_spec((2, CHUNK, n)), _const_spec((2, CHUNK, n)), _const_spec((n, n))],
        out_specs=[fwd, bwd],
        out_shape=[out, out],
        scratch_shapes=[pltpu.VMEM((2 * nb, n, GROUP_W), F32)],
        compiler_params=_cparams(2),
    )(r, r, v, v, ah, ah, lw, lw, kd, kd, bd, bd, inc, strict, bm)


def _gla_scan_kernel(qf_ref, qb_ref, kf_ref, kb_ref, gf_ref, gb_ref, vf_ref, vb_ref, gup_ref, gbias_ref,
                     inc_ref, bm_ref, bmv_ref, of_ref, ob_ref, s_ref):
    p = pl.program_id(1)

    @pl.when(p == 0)
    def _():
        s_ref[...] = jnp.zeros_like(s_ref)

    wk = N_HEADS * GLA_DK
    n = N_HEADS * CHUNK
    bm = bm_ref[...]
    bmv = bmv_ref[...].astype(BF16)
    bmk = (lax.broadcasted_iota(jnp.int32, (n, wk), 0) // CHUNK
           == lax.broadcasted_iota(jnp.int32, (n, wk), 1) // GLA_DK).astype(F32).astype(BF16)
    refs = ((qf_ref, kf_ref, gf_ref, vf_ref, of_ref), (qb_ref, kb_ref, gb_ref, vb_ref, ob_ref))
    orders = [list(_chunk_order(d)) for d in range(2)]
    nb = s_ref.shape[0] // 2
    inst = [(bi, step, d) for step in range(TQ // CHUNK) for bi in range(nb) for d in range(2)]
    idx = range(len(inst))
    rows = [pl.ds(orders[d][step] * CHUNK, CHUNK) for _, step, d in inst]

    qe, ke_s, ks4, v_s, decay = [], [], [], [], []
    for (bi, step, d), sl in zip(inst, rows):
        q_ref, k_ref, g_ref, v_ref, _ = refs[d]
        k = k_ref[bi, sl, :]
        la = -_softplus(-(_dot(g_ref[bi, sl, :].astype(BF16), gup_ref[d]) + gbias_ref[d])) * (1.0 / GLA_TAU)
        b = _cumsum_rows(la, reverse=(d == 1))
        tot = b[CHUNK - 1:CHUNK, :] if d == 0 else b[0:1, :]
        qe.append((q_ref[bi, sl, :] * (GLA_DK ** -0.5) * jnp.exp(b)).astype(BF16))
        ke_s.append(_block_diag(k * jnp.exp(-b), bmk))
        ks4.append(_tile_rows(k * jnp.exp(tot - b)).T.astype(BF16))
        v_s.append(_block_diag(v_ref[bi, sl, :], bmv))
        decay.append(_tile_rows(jnp.broadcast_to(jnp.exp(tot), (CHUNK, wk))).T)
    a_cat = [(_dot_nt(qe[i], ke_s[i]) * inc_ref[inst[i][2]]).astype(BF16) for i in idx]
    o_in = [_dot(a_cat[i], v_s[i]) for i in idx]
    kv = [_dot(ks4[i], v_s[i]) * bm for i in idx]

    s = [s_ref[k] for k in range(2 * nb)]
    for i, ((bi, step, d), sl) in enumerate(zip(inst, rows)):
        k = 2 * bi + d
        refs[d][4][bi, sl, :] = o_in[i] + _dot(qe[i], s[k].astype(BF16))
        s[k] = decay[i] * s[k] + kv[i]
    for k in range(2 * nb):
        s_ref[k] = s[k]


def _gla_scan(z, gup_p, gb, masks, bm, bmv, T, layer):
    B, S, _ = z.shape
    nlat = T // TQ
    inc, _ = masks
    n = N_HEADS * CHUNK
    wk = N_HEADS * GLA_DK
    nb = GLA_SCAN_BATCH if B % GLA_SCAN_BATCH == 0 else 1
    fwd = lambda w, col: pl.BlockSpec((nb, TQ, w), lambda b, p: (b, _fwd_tile(p, nlat), col // w))
    bwd = lambda w, col: pl.BlockSpec((nb, TQ, w), lambda b, p: (b, _bwd_tile(p, nlat), col // w))
    out = jax.ShapeDtypeStruct((B, S, GROUP_W), F32)
    return pl.pallas_call(
        _gla_scan_kernel,
        grid=(B // nb, nlat + 1),
        in_specs=[fwd(wk, COL_GLA_Q), bwd(wk, COL_GLA_Q), fwd(wk, COL_GLA_K), bwd(wk, COL_GLA_K),
                  fwd(wk, COL_GLA_G), bwd(wk, COL_GLA_G), fwd(GROUP_W, COL_GLA_V), bwd(GROUP_W, COL_GLA_V),
                  _const_spec((2, wk, wk), layer), _const_spec((2, 1, wk), layer),
                  _const_spec((2, CHUNK, n)), _const_spec((wk, GROUP_W)), _const_spec((n, n))],
        out_specs=[fwd(GROUP_W, 0), bwd(GROUP_W, 0)],
        out_shape=[out, out],
        scratch_shapes=[pltpu.VMEM((2 * nb, wk, GROUP_W), F32)],
        compiler_params=_cparams(2),
    )(z, z, z, z, z, z, z, z, gup_p, gb, inc, bm, bmv)


def _pack_w_in(w_in):
    L, D, _ = w_in.shape
    na, mla, rw, gla = jnp.split(w_in.astype(BF16), [768, 1120, 2144], axis=-1)
    cq, ckv, kr = jnp.split(mla, [MLA_Q_RANK, MLA_Q_RANK + MLA_KV_RANK], axis=-1)
    gq, gk, gv, gg, go = jnp.split(gla, [128, 256, 512, 528], axis=-1)
    pad = jnp.zeros((L, D, COL_GLA_V - COL_GLA_G - gg.shape[-1]), BF16)
    packed = jnp.concatenate([na, ckv, cq, kr, kr[..., _rope_swap_perm()], gq, gk, gg, pad, gv, go, rw], axis=-1)
    assert packed.shape[-1] == Z_COLS
    return packed


def kernel(x, c, ctx, c_ctx, w_mod, b_mod, g_mix_pre, g_mix_post, g_ffn_pre, g_ffn_post, w_in, w_out, na_rpb, mla_q_norm, mla_w_uq, mla_kv_norm, mla_w_ukv, rw_mu, rw_w0, rw_w_up, rw_a0, rw_a_up, rw_g_up, rw_k_k, rw_k_a, rw_r_k, rw_ln_w, rw_ln_b, gla_gate_up, gla_gate_b, gla_norm, ffn_w_up, ffn_conv_w, ffn_conv_b, ffn_w_down):
    B, T, D = x.shape
    Tc = ctx.shape[1]
    L = w_in.shape[0]
    assert D == D_MODEL and Tc == TQ and T % TMT == 0 and B + 1 <= 8

    w_in_p = _pack_w_in(w_in)
    w_out_b = w_out.astype(BF16)
    wq1, wq2, wk, wv = _mla_weights(mla_w_uq, mla_w_ukv)
    place = _rope_place()
    cos, sin = _rope_tables(T, Tc)
    zero_lo = jnp.zeros((L, 64, 2 * GROUP_W), F32)
    rw_wup_p = jnp.concatenate([jnp.concatenate([rw_w_up[:, 0], rw_w_up[:, 1]], axis=-1), zero_lo], axis=1).astype(BF16)
    rw_aup_p = jnp.concatenate([zero_lo, jnp.concatenate([rw_a_up[:, 0], rw_a_up[:, 1]], axis=-1)], axis=1).astype(BF16)
    rw_gup_b = rw_g_up.astype(BF16)
    wk_gla = N_HEADS * GLA_DK
    gla_gup_p = jnp.concatenate([gla_gate_up, jnp.zeros((L, 2, wk_gla - gla_gate_up.shape[2], wk_gla), F32)], axis=2).astype(BF16)
    ffn_up_b = ffn_w_up.astype(BF16)
    ffn_dn_b = ffn_w_down.astype(BF16)
    ones_b = jnp.asarray(_block_ones(GROUP_W, HEAD_DIM), BF16)
    rw_bm = jnp.asarray(_block_ones(N_HEADS * CHUNK, CHUNK))
    gla_bm = jnp.asarray((np.arange(wk_gla)[:, None] // GLA_DK == np.arange(GROUP_W)[None, :] // HEAD_DIM).astype(np.float32))
    masks = _scan_masks()

    cvecs = jnp.zeros((8, D), F32).at[:B].set(c).at[B].set(c_ctx)
    mods = _modulation(cvecs, w_mod, b_mod).reshape(L, 8, 6, D)
    mods = jnp.pad(mods, ((0, 0), (0, 0), (0, 2), (0, 0)))
    modtabs = jnp.stack([jnp.broadcast_to(mods[:, B:B + 1], (L, B, 8, D)), mods[:, :B]], axis=2)

    xs = jnp.concatenate([x, ctx], axis=1)
    rows = lambda a: a.reshape(L, 1, -1)
    na_bias = _natten_bias(na_rpb, T // GRID_W)
    gla_gb = gla_gate_b[:, :, None, :]
    for i in range(L):
        z = _in_proj(xs, modtabs, rows(g_mix_pre), w_in_p, T, i)
        y_na = _natten(z, na_bias, T, i)
        qt, k, vt = _mla_up(z, cos, sin, rows(mla_q_norm), rows(mla_kv_norm), wq1, wq2, wk, wv, place, T, i)
        y_mla = _mla_attn(qt, k, vt, T)
        r, vv, ah, g, bon, lw, kd, bd = _rw_prep(z, rows(rw_mu), rows(rw_k_k), rows(rw_k_a), rows(rw_r_k),
                                                 rw_w0, rw_a0, rw_wup_p, rw_aup_p, rw_gup_b, ones_b, T, i)
        yf, yb = _rw_scan(r, vv, ah, lw, kd, bd, masks, rw_bm, T)
        of, ob = _gla_scan(z, gla_gup_p, gla_gb, masks, gla_bm, rw_bm, T, i)
        xs = _out_proj(xs, z, y_na, y_mla, yf, yb, bon, g, of, ob, modtabs, rows(g_mix_post), w_out_b,
                       rows(rw_ln_w), rows(rw_ln_b), rows(gla_norm), ones_b, T, i)
        xs = _ffn(xs, modtabs, rows(g_ffn_pre), rows(g_ffn_post), ffn_up_b, ffn_conv_w,
                  rows(ffn_conv_b), ffn_dn_b, T, i, latent_only=(i == L - 1))
    return xs
```

```python
import functools

import numpy as np
import jax
import jax.numpy as jnp
from jax import lax
from jax.experimental import pallas as pl
from jax.experimental.pallas import tpu as pltpu

F32 = jnp.float32
BF16 = jnp.bfloat16

D_MODEL = 1024
GRID_W = 64
EPS = 1e-6
LOG2E = 1.4426950408889634
N_HEADS = 4
HEAD_DIM = 64
GROUP_W = 256
NA_ROWS = 8
NA_COLS = 16
NA_UNION = 12
MLA_Q_RANK = 192
MLA_KV_RANK = 128
MLA_NOPE = 64
MLA_ROPE = 32
MLA_HEAD_PAD = 128
ROPE_THETA = 10000.0
RW_GN_EPS = 64e-5
GLA_DK = 32
GLA_TAU = 16.0
D_FF = 2816
CHUNK = 64
TQ = 256
MLA_TQ = 512
RW_SCAN_BATCH = 2
GLA_SCAN_BATCH = 4
TMT = 512
TM_WIDE = 1024
HALO = 8
FF_CHUNK = 256
FF_GROUP = 4
Z_COLS = 3072

COL_NA = 0
COL_MLA = 768
COL_GLA_Q = 1152
COL_GLA_K = 1280
COL_GLA_G = 1408
COL_GLA_V = 1536
COL_GLA_O = 1792
COL_RW = 2048

VMEM_LIMIT_V7X = 56 * 1024 * 1024


def _cparams(n_axes):
    return pltpu.CompilerParams(dimension_semantics=("arbitrary",) * n_axes,
                                vmem_limit_bytes=VMEM_LIMIT_V7X)


def _const_spec(shape, layer=None):
    nd = len(shape)
    if layer is None:
        return pl.BlockSpec(shape, lambda *_: (0,) * nd, pipeline_mode=pl.Buffered(1))
    return pl.BlockSpec((None,) + tuple(shape), lambda *_: (layer,) + (0,) * nd, pipeline_mode=pl.Buffered(1))


def _dot(a, b):
    return jnp.dot(a, b, preferred_element_type=F32)


def _split3(x):
    hi = x.astype(BF16)
    r1 = x - hi.astype(F32)
    mid = r1.astype(BF16)
    lo = (r1 - mid.astype(F32)).astype(BF16)
    return hi, mid, lo


def _dot_sel_rhs(x, m):
    hi, mid, lo = _split3(x)
    return _dot(hi, m) + (_dot(mid, m) + _dot(lo, m))


def _sigmoid(x):
    return 1.0 / (1.0 + jnp.exp(-x))


def _silu_gain(x):
    return 0.5 + 0.5 * jnp.tanh(0.5 * x)


def _softplus(x):
    return jnp.maximum(x, 0.0) + jnp.log1p(jnp.exp(-jnp.abs(x)))


def _rms(x):
    return x * lax.rsqrt(jnp.mean(x * x, axis=-1, keepdims=True) + EPS)


def _norm_mod(x, g, shift, scale):
    return (_rms(x) * g) * (1.0 + scale) + shift


def _lane_head(width, per_head):
    return lax.broadcasted_iota(jnp.int32, (1, width), 1) // per_head


def _block_ones(n, blk):
    i = np.arange(n) // blk
    return (i[:, None] == i[None, :]).astype(np.float32)


def _mod_kernel(c_ref, w_ref, b_ref, o_ref):
    cv = c_ref[...]
    s = cv * _sigmoid(cv)
    o_ref[0] = _dot(s.astype(BF16), w_ref[0].astype(BF16)) + b_ref[0]


def _modulation(cvecs, w_mod, b_mod):
    L, D, N = w_mod.shape
    tn = 1536
    return pl.pallas_call(
        _mod_kernel,
        grid=(L, N // tn),
        in_specs=[pl.BlockSpec((8, D), lambda l, n: (0, 0)),
                  pl.BlockSpec((1, D, tn), lambda l, n: (l, 0, n)),
                  pl.BlockSpec((1, 1, tn), lambda l, n: (l, 0, n))],
        out_specs=pl.BlockSpec((1, 8, tn), lambda l, n: (l, 0, n)),
        out_shape=jax.ShapeDtypeStruct((L, 8, N), F32),
        compiler_params=_cparams(2),
    )(cvecs, w_mod, b_mod.reshape(L, 1, N))


def _tok_tiles(S, T, tm=TMT):
    assert T % tm == 0 and 0 < S - T <= tm
    return T // tm + 1, T // tm


def _tok_spec(width, col_block=0, tm=TMT):
    return pl.BlockSpec((1, tm, width), lambda b, i: (b, i, col_block))


def _mod_spec(nlat, layer):
    return pl.BlockSpec((None, 1, 1, 8, D_MODEL), lambda b, i: (layer, b, jnp.where(i < nlat, 1, 0), 0, 0))


def _halo_specs(width, col_block, S, tm=TMT):
    per = tm // HALO
    last = S // HALO - 1
    prev = pl.BlockSpec((1, HALO, width), lambda b, i: (b, jnp.maximum(i * per - 1, 0), col_block))
    nxt = pl.BlockSpec((1, HALO, width), lambda b, i: (b, jnp.minimum((i + 1) * per, last), col_block))
    return prev, nxt


def _neighbour_ok(i, nlat):
    prev_ok = jnp.logical_and(i != 0, i != nlat)
    next_ok = i < nlat - 1
    return prev_ok, next_ok


def _for_tile_rows(nlat, tc, body, tm=TMT):
    i = pl.program_id(1)

    @pl.when(i < nlat)
    def _():
        body(tm)

    @pl.when(i >= nlat)
    def _():
        body(tc)


def _in_proj_kernel(x_ref, mod_ref, g_ref, w_ref, z_ref, *, nlat, tc, tm):
    def body(nr):
        m = mod_ref[0, 0]
        h = _norm_mod(x_ref[0, :nr], g_ref[...], m[0:1], m[1:2])
        z_ref[0, :nr] = _dot(h.astype(BF16), w_ref[...])

    _for_tile_rows(nlat, tc, body, tm)


def _in_proj(xs, modtab, g_pre, w_in_p, T, layer):
    B, S, D = xs.shape
    tm = TM_WIDE
    ntiles, nlat = _tok_tiles(S, T, tm)
    return pl.pallas_call(
        functools.partial(_in_proj_kernel, nlat=nlat, tc=S - T, tm=tm),
        grid=(B, ntiles),
        in_specs=[_tok_spec(D, 0, tm), _mod_spec(nlat, layer), _const_spec((1, D), layer),
                  _const_spec((D, Z_COLS), layer)],
        out_specs=_tok_spec(Z_COLS, 0, tm),
        out_shape=jax.ShapeDtypeStruct((B, S, Z_COLS), F32),
        compiler_params=_cparams(2),
    )(xs, modtab, g_pre, w_in_p)


def _out_proj_kernel(x_ref, na_ref, mla_ref, yf_ref, yb_ref, bon_ref, g_ref, of_ref, ob_ref, og_ref,
                     mod_ref, gpost_ref, w_ref, lnw_ref, lnb_ref, gn_ref, ones_ref, o_ref, *, nlat, tc, tm):
    def body(nr):
        m = mod_ref[0, 0]
        ones = ones_ref[...]
        inv_n = 1.0 / HEAD_DIM
        y = yf_ref[0, :nr] + yb_ref[0, :nr]
        yc = y - _dot_sel_rhs(y, ones) * inv_n
        var = _dot_sel_rhs(yc * yc, ones) * inv_n
        y_rw = (yc * lax.rsqrt(var + RW_GN_EPS) * lnw_ref[...] + lnb_ref[...] + bon_ref[0, :nr]) * g_ref[0, :nr]
        o = of_ref[0, :nr] + ob_ref[0, :nr]
        ms = _dot_sel_rhs(o * o, ones) * inv_n
        og = og_ref[0, :nr]
        y_gla = (o * lax.rsqrt(ms + EPS) * gn_ref[...]) * (og * _silu_gain(og))
        y = jnp.concatenate([na_ref[0, :nr], mla_ref[0, :nr], y_rw.astype(BF16), y_gla.astype(BF16)], axis=-1)
        y = _dot(y, w_ref[...])
        o_ref[0, :nr] = x_ref[0, :nr] + m[2:3] * (_rms(y) * gpost_ref[...])

    _for_tile_rows(nlat, tc, body, tm)


def _out_proj(xs, z, y_na, y_mla, yf, yb, bon, g, of, ob, modtab, g_post, w_out_b, ln_w, ln_b, gla_norm,
              ones_b, T, layer):
    B, S, D = xs.shape
    tm = TM_WIDE
    ntiles, nlat = _tok_tiles(S, T, tm)
    grp = _tok_spec(GROUP_W, 0, tm)
    vec = _const_spec((1, GROUP_W), layer)
    return pl.pallas_call(
        functools.partial(_out_proj_kernel, nlat=nlat, tc=S - T, tm=tm),
        grid=(B, ntiles),
        in_specs=[_tok_spec(D, 0, tm), grp, grp, grp, grp, grp, grp, grp, grp,
                  _tok_spec(GROUP_W, COL_GLA_O // GROUP_W, tm),
                  _mod_spec(nlat, layer), _const_spec((1, D), layer), _const_spec((D, D), layer), vec, vec, vec,
                  _const_spec((GROUP_W, GROUP_W))],
        out_specs=_tok_spec(D, 0, tm),
        out_shape=jax.ShapeDtypeStruct((B, S, D), F32),
        compiler_params=_cparams(2),
    )(xs, y_na, y_mla, yf, yb, bon, g, of, ob, z, modtab, g_post, w_out_b, ln_w, ln_b, gla_norm, ones_b)


def _ffn_kernel(xp_ref, x_ref, xn_ref, mod_ref, gpre_ref, gpost_ref, wup_ref, cw_ref, cb_ref,
                wdn_ref, o_ref, *, nlat, tc):
    i = pl.program_id(1)
    prev_ok, next_ok = _neighbour_ok(i, nlat)
    nchunks = D_FF // FF_CHUNK

    def body(nr):
        m = mod_ref[0, 0]
        x = x_ref[0, :nr]
        xe = jnp.concatenate([xp_ref[0], x, xn_ref[0]], axis=0)
        n = nr + 2 * HALO
        h = _norm_mod(xe, gpre_ref[...], m[3:4], m[4:5])
        row = lax.broadcasted_iota(jnp.int32, (n, 1), 0)
        valid = jnp.logical_or(jnp.logical_and(row >= HALO, row < HALO + nr),
                               jnp.logical_or(jnp.logical_and(row < HALO, prev_ok),
                                              jnp.logical_and(row >= HALO + nr, next_ok)))
        hb = jnp.where(valid, h, 0.0).astype(BF16)

        def up(c):
            return [_dot(hb, wup_ref[:, base + c * FF_CHUNK:base + (c + 1) * FF_CHUNK]) for base in (0, D_FF)]

        def conv(z, lo):
            cw = cw_ref[:, lo:lo + FF_CHUNK]
            return (cb_ref[:, lo:lo + FF_CHUNK]
                    + pltpu.roll(z, 1, 0)[HALO:HALO + nr] * cw[0:1]
                    + z[HALO:HALO + nr] * cw[1:2]
                    + pltpu.roll(z, n - 1, 0)[HALO:HALO + nr] * cw[2:3])

        acc = None
        group = []
        z_next = up(0)
        for c in range(nchunks):
            z_val, z_gate = z_next
            if c + 1 < nchunks:
                z_next = up(c + 1)
            val = conv(z_val, c * FF_CHUNK)
            gate = conv(z_gate, D_FF + c * FF_CHUNK)
            group.append(((gate * _silu_gain(gate)) * val).astype(BF16))
            if len(group) == FF_GROUP or c + 1 == nchunks:
                lo = (c + 1 - len(group)) * FF_CHUNK
                part = _dot(jnp.concatenate(group, axis=1), wdn_ref[lo:(c + 1) * FF_CHUNK, :])
                acc = part if acc is None else acc + part
                group = []
        o_ref[0, :nr] = x + m[5:6] * (_rms(acc) * gpost_ref[...])

    _for_tile_rows(nlat, tc, body)


def _ffn(xs, modtab, g_pre, g_post, w_up_b, conv_w, conv_b, w_dn_b, T, layer, latent_only):
    B, S, D = xs.shape
    ntiles, nlat = _tok_tiles(S, T)
    prev, nxt = _halo_specs(D, 0, S)
    return pl.pallas_call(
        functools.partial(_ffn_kernel, nlat=nlat, tc=S - T),
        grid=(B, nlat if latent_only else ntiles),
        in_specs=[prev, _tok_spec(D), nxt, _mod_spec(nlat, layer), _const_spec((1, D), layer),
                  _const_spec((1, D), layer), _const_spec((D, 2 * D_FF), layer),
                  _const_spec((3, 2 * D_FF), layer), _const_spec((1, 2 * D_FF), layer),
                  _const_spec((D_FF, D), layer)],
        out_specs=_tok_spec(D),
        out_shape=jax.ShapeDtypeStruct((B, T if latent_only else S, D), F32),
        compiler_params=_cparams(2),
    )(xs, xs, xs, modtab, g_pre, g_post, w_up_b, conv_w, conv_b, w_dn_b)


def _natten_bias(rpb, rows):
    rt = TQ // GRID_W
    j = np.arange(GRID_W)
    col_start = np.clip(j - NA_COLS // 2, 0, GRID_W - NA_COLS)
    col_in = (j[None, :] >= col_start[:, None]) & (j[None, :] < col_start[:, None] + NA_COLS)
    edge = GRID_W - NA_COLS
    ext = jnp.concatenate([jnp.repeat(rpb[..., :1], edge, axis=-1), rpb,
                           jnp.repeat(rpb[..., -1:], edge, axis=-1)], axis=-1).astype(F32) * LOG2E
    bq = jnp.stack([ext[..., GRID_W - 1 - q:2 * GRID_W - 1 - q] for q in range(GRID_W)], axis=2)
    L = rpb.shape[0]
    bq = jnp.where(col_in[:, None, :], bq, -jnp.inf).reshape(L, N_HEADS, GRID_W, -1)
    cases = []
    for r0 in (0, rt, rows - rt):
        us = min(max(r0 - NA_ROWS // 2, 0), rows - NA_UNION)
        per_row = []
        for r in range(r0, r0 + rt):
            rs = min(max(r - NA_ROWS // 2, 0), rows - NA_ROWS)
            first = rs - r + (NA_ROWS - 1)
            pre, post = rs - us, us + NA_UNION - (rs + NA_ROWS)
            per_row.append(jnp.pad(bq[..., first * GRID_W:(first + NA_ROWS) * GRID_W],
                                   ((0, 0), (0, 0), (0, 0), (pre * GRID_W, post * GRID_W)),
                                   constant_values=-jnp.inf))
        cases.append(jnp.concatenate(per_row, axis=2))
    return jnp.stack(cases, axis=1)


def _natten_kernel(q_ref, k_ref, v_ref, qc_ref, kc_ref, vc_ref, bias_a_ref, bias_b_ref, o_ref, *, nlat, rows):
    j = pl.program_id(1)
    lane_h = _lane_head(GROUP_W, HEAD_DIM)
    kct = kc_ref[0].T.astype(BF16)
    vc = vc_ref[0].astype(BF16)
    scale = HEAD_DIM ** -0.5 * LOG2E
    nwin = NA_UNION * GRID_W

    @pl.when(j < nlat // 2)
    def _():
        wins = []
        for half in range(2):
            us = jnp.clip((2 * j + half) * (TQ // GRID_W) - NA_ROWS // 2, 0, rows - NA_UNION)
            start = pl.multiple_of(us * GRID_W, GRID_W)
            wins.append((k_ref[0, pl.ds(start, nwin), :].T.astype(BF16),
                         v_ref[0, pl.ds(start, nwin), :].astype(BF16)))
        for half, bias_ref in enumerate((bias_a_ref, bias_b_ref)):
            kwt, vw = wins[half]
            q = q_ref[0, half * TQ:(half + 1) * TQ, :] * scale
            acc = jnp.zeros((TQ, GROUP_W), F32)

            def logits(h):
                qh = jnp.where(lane_h == h, q, 0.0).astype(BF16)
                return _dot(qh, kwt), _dot(qh, kct)

            s_next = logits(0)
            for h in range(N_HEADS):
                hm = lane_h == h
                s_w, s_c = s_next
                if h + 1 < N_HEADS:
                    s_next = logits(h + 1)
                s_w = s_w + bias_ref[0, h]
                mx = jnp.maximum(jnp.max(s_w, axis=-1, keepdims=True), jnp.max(s_c, axis=-1, keepdims=True))
                p_w = jnp.exp2(s_w - mx)
                p_c = jnp.exp2(s_c - mx)
                den = jnp.sum(p_w, axis=-1, keepdims=True) + jnp.sum(p_c, axis=-1, keepdims=True)
                o = _dot(p_w.astype(BF16), vw) + _dot(p_c.astype(BF16), vc)
                acc = acc + jnp.where(hm, o * (1.0 / den), 0.0)
            o_ref[0, half * TQ:(half + 1) * TQ, :] = acc.astype(o_ref.dtype)

    @pl.when(j >= nlat // 2)
    def _():
        q = qc_ref[0] * scale
        acc = jnp.zeros((TQ, GROUP_W), F32)
        for h in range(N_HEADS):
            hm = lane_h == h
            s = _dot(jnp.where(hm, q, 0.0).astype(BF16), kct)
            p = jnp.exp2(s - jnp.max(s, axis=-1, keepdims=True))
            den = jnp.sum(p, axis=-1, keepdims=True)
            acc = acc + jnp.where(hm, _dot(p.astype(BF16), vc) * (1.0 / den), 0.0)
        o_ref[0, :TQ, :] = acc.astype(o_ref.dtype)


def _natten(z, bias, T, layer):
    B, S, _ = z.shape
    nlat = T // TQ
    rows = T // GRID_W
    assert rows >= 16 and nlat % 2 == 0 and S - T == TQ
    cb = COL_NA // GROUP_W
    npair = nlat // 2
    lat = lambda c: pl.BlockSpec((1, T, GROUP_W), lambda b, j: (b, 0, cb + c))
    ctx = lambda c: pl.BlockSpec((1, TQ, GROUP_W), lambda b, j: (b, nlat, cb + c))
    case = lambda t: jnp.where(t == 0, 0, jnp.where(t >= nlat - 1, 2, 1))
    tile = lambda j, half: 2 * jnp.minimum(j, npair - 1) + half
    bias_spec = lambda half: pl.BlockSpec((None, 1, N_HEADS, TQ, NA_UNION * GRID_W),
                                          lambda b, j: (layer, case(tile(j, half)), 0, 0, 0))
    return pl.pallas_call(
        functools.partial(_natten_kernel, nlat=nlat, rows=rows),
        grid=(B, npair + 1),
        in_specs=[pl.BlockSpec((1, 2 * TQ, GROUP_W), lambda b, j: (b, jnp.minimum(j, npair - 1), cb)),
                  lat(1), lat(2), ctx(0), ctx(1), ctx(2), bias_spec(0), bias_spec(1)],
        out_specs=pl.BlockSpec((1, 2 * TQ, GROUP_W), lambda b, j: (b, j, 0)),
        out_shape=jax.ShapeDtypeStruct((B, S, GROUP_W), BF16),
        compiler_params=_cparams(2),
    )(z, z, z, z, z, z, bias, bias)


def _rope_tables(T, Tc):
    t = np.arange(T)
    row = (t // GRID_W).astype(np.float32)
    col = (t % GRID_W).astype(np.float32)
    d = MLA_ROPE // 2
    inv = (np.float32(ROPE_THETA) ** (-np.arange(0, d, 2, dtype=np.float32) / np.float32(d))).astype(np.float32)
    cs, sn = [], []
    for pos in (row, col):
        ang = (pos[:, None] * inv[None, :]).astype(np.float32)
        cs += [np.cos(ang), np.cos(ang)]
        sn += [-np.sin(ang), np.sin(ang)]
    pad = MLA_HEAD_PAD - MLA_NOPE - MLA_ROPE
    f32 = np.float32
    cos = np.concatenate([np.ones((T, MLA_NOPE), f32)] + cs + [np.ones((T, pad), f32)], axis=1)
    sin = np.concatenate([np.zeros((T, MLA_NOPE), f32)] + sn + [np.zeros((T, pad), f32)], axis=1)
    cos = np.concatenate([cos, np.ones((Tc, MLA_HEAD_PAD), f32)], axis=0)
    sin = np.concatenate([sin, np.zeros((Tc, MLA_HEAD_PAD), f32)], axis=0)
    return jnp.asarray(cos, F32), jnp.asarray(sin, F32)


def _rope_swap_perm():
    q = MLA_ROPE // 4
    return np.concatenate([np.arange(q, 2 * q), np.arange(0, q), np.arange(3 * q, 4 * q), np.arange(2 * q, 3 * q)])


def _mla_weights(w_uq, w_ukv):
    L = w_uq.shape[0]
    wq = w_uq.reshape(L, MLA_Q_RANK, N_HEADS, MLA_NOPE + MLA_ROPE)
    pad = MLA_HEAD_PAD - MLA_NOPE - MLA_ROPE
    zq = jnp.zeros((L, MLA_Q_RANK, N_HEADS, pad), F32)
    wq1 = jnp.concatenate([wq, zq], axis=-1).reshape(L, MLA_Q_RANK, N_HEADS * MLA_HEAD_PAD)
    rope_sw = wq[..., MLA_NOPE:][..., _rope_swap_perm()]
    wq2 = jnp.concatenate([jnp.zeros((L, MLA_Q_RANK, N_HEADS, MLA_NOPE), F32), rope_sw, zq], axis=-1)
    wq2 = wq2.reshape(L, MLA_Q_RANK, N_HEADS * MLA_HEAD_PAD)
    wkv = w_ukv.reshape(L, MLA_KV_RANK, N_HEADS, 2 * MLA_NOPE)
    wk = jnp.concatenate([wkv[..., :MLA_NOPE], jnp.zeros((L, MLA_KV_RANK, N_HEADS, MLA_HEAD_PAD - MLA_NOPE), F32)], axis=-1)
    wk = wk.reshape(L, MLA_KV_RANK, N_HEADS * MLA_HEAD_PAD)
    wv_t = wkv[..., MLA_NOPE:].reshape(L, MLA_KV_RANK, N_HEADS * MLA_NOPE).transpose(0, 2, 1)
    return wq1.astype(BF16), wq2.astype(BF16), wk.astype(BF16), wv_t.astype(BF16)


def _rope_place():
    e = np.zeros((MLA_ROPE, N_HEADS * MLA_HEAD_PAD), np.float32)
    for h in range(N_HEADS):
        e[np.arange(MLA_ROPE), h * MLA_HEAD_PAD + MLA_NOPE + np.arange(MLA_ROPE)] = 1.0
    return jnp.asarray(e, BF16)


def _mla_up_kernel(z_ref, cos_ref, sin_ref, qn_ref, kvn_ref, wq1_ref, wq2_ref, wk_ref, wv_ref, e_ref,
                   qt_ref, k_ref, vt_ref, *, nlat, tc):
    def body(nr):
        z = z_ref[0, :nr]
        ckv = z[:, :MLA_KV_RANK]
        cq = z[:, MLA_KV_RANK:MLA_KV_RANK + MLA_Q_RANK]
        kr = z[:, MLA_KV_RANK + MLA_Q_RANK:MLA_KV_RANK + MLA_Q_RANK + MLA_ROPE]
        krs = z[:, MLA_KV_RANK + MLA_Q_RANK + MLA_ROPE:]
        cos = jnp.concatenate([cos_ref[:nr]] * N_HEADS, axis=-1)
        sin = jnp.concatenate([sin_ref[:nr]] * N_HEADS, axis=-1)
        nq = (_rms(cq) * qn_ref[...]).astype(BF16)
        nkv_f = _rms(ckv) * kvn_ref[...]
        nkv = nkv_f.astype(BF16)
        q = _dot(nq, wq1_ref[...]) * cos + _dot(nq, wq2_ref[...]) * sin
        scale = (MLA_NOPE + MLA_ROPE) ** -0.5 * LOG2E
        qt_ref[0, :, :nr] = (q * scale).T.astype(BF16)
        k = _dot(nkv, wk_ref[...]) + _dot_sel_rhs(kr, e_ref[...]) * cos + _dot_sel_rhs(krs, e_ref[...]) * sin
        k_ref[0, :nr] = k.astype(BF16)
        vt_ref[0, :, :nr] = _dot(wv_ref[...], nkv_f.T.astype(BF16)).astype(BF16)

    _for_tile_rows(nlat, tc, body)


def _mla_up(z, cos, sin, q_norm, kv_norm, wq1, wq2, wk, wv, place, T, layer):
    B, S, _ = z.shape
    ntiles, nlat = _tok_tiles(S, T)
    HP = N_HEADS * MLA_HEAD_PAD
    zw = MLA_KV_RANK + MLA_Q_RANK + 2 * MLA_ROPE
    tab = pl.BlockSpec((TMT, MLA_HEAD_PAD), lambda b, i: (i, 0))
    return pl.pallas_call(
        functools.partial(_mla_up_kernel, nlat=nlat, tc=S - T),
        grid=(B, ntiles),
        in_specs=[_tok_spec(zw, COL_MLA // zw), tab, tab,
                  _const_spec((1, MLA_Q_RANK), layer), _const_spec((1, MLA_KV_RANK), layer),
                  _const_spec((MLA_Q_RANK, HP), layer), _const_spec((MLA_Q_RANK, HP), layer),
                  _const_spec((MLA_KV_RANK, HP), layer), _const_spec((GROUP_W, MLA_KV_RANK), layer),
                  _const_spec((MLA_ROPE, HP))],
        out_specs=[pl.BlockSpec((1, HP, TMT), lambda b, i: (b, 0, i)), _tok_spec(HP),
                   pl.BlockSpec((1, GROUP_W, TMT), lambda b, i: (b, 0, i))],
        out_shape=[jax.ShapeDtypeStruct((B, HP, S), BF16),
                   jax.ShapeDtypeStruct((B, S, HP), BF16),
                   jax.ShapeDtypeStruct((B, GROUP_W, S), BF16)],
        compiler_params=_cparams(2),
    )(z, cos, sin, q_norm, kv_norm, wq1, wq2, wk, wv, place)


def _mla_attn_kernel(qt_ref, k_ref, vt_ref, o_ref, *, nlat, T):
    j = pl.program_id(1)

    def attend(lo, hi):
        def logits(h):
            hp = slice(h * MLA_HEAD_PAD, (h + 1) * MLA_HEAD_PAD)
            return _dot(k_ref[0, lo:hi, hp], qt_ref[0, hp, :])

        ahead = 2
        pending = [logits(h) for h in range(ahead)]
        pieces = []
        for h in range(N_HEADS):
            s = pending.pop(0)
            if h + ahead < N_HEADS:
                pending.append(logits(h + ahead))
            p = jnp.exp2(s - jnp.max(s, axis=0, keepdims=True))
            den = jnp.sum(p, axis=0, keepdims=True)
            o = _dot(vt_ref[0, h * HEAD_DIM:(h + 1) * HEAD_DIM, lo:hi], p.astype(BF16))
            pieces.append(o * (1.0 / den))
        o_ref[0] = jnp.concatenate(pieces, axis=0).T.astype(o_ref.dtype)

    @pl.when(j < nlat)
    def _():
        attend(0, T + TQ)

    @pl.when(j >= nlat)
    def _():
        attend(T, T + TQ)


def _mla_attn(qt, k, vt, T):
    B, S, HP = k.shape
    assert T % MLA_TQ == 0 and S - T <= MLA_TQ
    nlat = T // MLA_TQ
    return pl.pallas_call(
        functools.partial(_mla_attn_kernel, nlat=nlat, T=T),
        grid=(B, nlat + 1),
        in_specs=[pl.BlockSpec((1, HP, MLA_TQ), lambda b, j: (b, 0, j)),
                  pl.BlockSpec((1, S, HP), lambda b, j: (b, 0, 0)),
                  pl.BlockSpec((1, GROUP_W, S), lambda b, j: (b, 0, 0))],
        out_specs=pl.BlockSpec((1, MLA_TQ, GROUP_W), lambda b, j: (b, j, 0)),
        out_shape=jax.ShapeDtypeStruct((B, S, GROUP_W), BF16),
        compiler_params=_cparams(2),
    )(qt, k, vt)


def _scan_masks():
    t = np.arange(CHUNK)
    inc = np.stack([t[:, None] >= t[None, :], t[:, None] <= t[None, :]]).astype(np.float32)
    strict = np.stack([t[:, None] > t[None, :], t[:, None] < t[None, :]]).astype(np.float32)
    return jnp.asarray(np.tile(inc, (1, 1, N_HEADS))), jnp.asarray(np.tile(strict, (1, 1, N_HEADS)))


def _cumsum_rows(x, reverse):
    n = x.shape[0]
    row = lax.broadcasted_iota(jnp.int32, (n, 1), 0)
    sh = 1
    while sh < n:
        if reverse:
            x = x + jnp.where(row < n - sh, pltpu.roll(x, n - sh, 0), 0.0)
        else:
            x = x + jnp.where(row >= sh, pltpu.roll(x, sh, 0), 0.0)
        sh *= 2
    return x


def _block_diag(x, bm_b):
    return _tile_rows(x.astype(BF16)) * bm_b


def _dot_nt(a, b):
    return lax.dot_general(a, b, (((1,), (1,)), ((), ())), preferred_element_type=F32)


def _fwd_tile(p, nlat):
    return jnp.where(p == 0, nlat, p - 1)


def _bwd_tile(p, nlat):
    return jnp.where(p == 0, nlat, nlat - p)


def _chunk_order(d):
    nch = TQ // CHUNK
    return range(nch) if d == 0 else range(nch - 1, -1, -1)


def _tile_rows(x):
    return jnp.concatenate([x] * N_HEADS, axis=0)


def _rw_prep_kernel(zp_ref, z_ref, zn_ref, mu_ref, kk_ref, ka_ref, rk_ref, w0_ref, a0_ref, wup_ref,
                    aup_ref, gup_ref, ones_ref,
                    r_ref, v_ref, ah_ref, g_ref, bon_ref, lw_ref, kd_ref, bd_ref, *, nlat, tc):
    i = pl.program_id(1)
    prev_ok, next_ok = _neighbour_ok(i, nlat)

    def body(nr):
        z = z_ref[0, :nr]
        row = lax.broadcasted_iota(jnp.int32, (nr, 1), 0)
        before = jnp.where(prev_ok, zp_ref[0, HALO - 1:HALO, :], 0.0)
        after = jnp.where(next_ok, zn_ref[0, 0:1, :], 0.0)
        zprev = jnp.where(row == 0, before, pltpu.roll(z, 1, 0))
        znext = jnp.where(row == nr - 1, after, pltpu.roll(z, nr - 1, 0))
        zs = z + mu_ref[...] * (0.5 * (zprev + znext) - z)
        r = zs[:, 0:GROUP_W]
        k = zs[:, GROUP_W:2 * GROUP_W]
        v = zs[:, 2 * GROUP_W:3 * GROUP_W]
        low = zs[:, 3 * GROUP_W:3 * GROUP_W + 128]
        gd = zs[:, 3 * GROUP_W + 128:]
        ones = ones_ref[...]
        kk = k * kk_ref[...]
        kk = kk * lax.rsqrt(_dot_sel_rhs(kk * kk, ones) + 1e-12)
        wl = _dot(jnp.tanh(low).astype(BF16), wup_ref[...])
        al = _dot(low.astype(BF16), aup_ref[...])
        ksum = jnp.zeros((nr, GROUP_W), F32)
        for d in range(2):
            w_raw = -_softplus(-(w0_ref[d:d + 1, :] + wl[:, d * GROUP_W:(d + 1) * GROUP_W])) - 0.5
            lw_ref[d, 0, :nr] = -jnp.exp(w_raw)
            a = _sigmoid(a0_ref[d:d + 1, :] + al[:, d * GROUP_W:(d + 1) * GROUP_W])
            kd = k * (1.0 + (a - 1.0) * ka_ref[...])
            kd_ref[d, 0, :nr] = kd
            bd_ref[d, 0, :nr] = kk * a
            ksum = ksum + kd
        r_ref[0, :nr] = r
        v_ref[0, :nr] = v
        ah_ref[0, :nr] = -kk
        g_ref[0, :nr] = _dot(_sigmoid(gd).astype(BF16), gup_ref[...])
        bon_ref[0, :nr] = _dot_sel_rhs(r * ksum * rk_ref[...], ones) * v

    _for_tile_rows(nlat, tc, body)


def _rw_prep(z, mu, k_k, k_a, r_k, w0, a0, wup_p, aup_p, gup_b, ones_b, T, layer):
    B, S, _ = z.shape
    ntiles, nlat = _tok_tiles(S, T)
    W = 4 * GROUP_W
    cb = COL_RW // W
    prev, nxt = _halo_specs(W, cb, S)
    one = _tok_spec(GROUP_W)
    two = pl.BlockSpec((2, 1, TMT, GROUP_W), lambda b, i: (0, b, i, 0))
    s1 = jax.ShapeDtypeStruct((B, S, GROUP_W), F32)
    s2 = jax.ShapeDtypeStruct((2, B, S, GROUP_W), F32)
    vec = _const_spec((1, GROUP_W), layer)
    return pl.pallas_call(
        functools.partial(_rw_prep_kernel, nlat=nlat, tc=S - T),
        grid=(B, ntiles),
        in_specs=[prev, _tok_spec(W, cb), nxt,
                  _const_spec((1, W), layer), vec, vec, vec,
                  _const_spec((2, GROUP_W), layer), _const_spec((2, GROUP_W), layer),
                  _const_spec((128, 2 * GROUP_W), layer), _const_spec((128, 2 * GROUP_W), layer),
                  _const_spec((128, GROUP_W), layer), _const_spec((GROUP_W, GROUP_W))],
        out_specs=[one, one, one, one, one, two, two, two],
        out_shape=[s1, s1, s1, s1, s1, s2, s2, s2],
        compiler_params=_cparams(2),
    )(z, z, z, mu, k_k, k_a, r_k, w0, a0, wup_p, aup_p, gup_b, ones_b)


def _rw_scan_kernel(rf_ref, rb_ref, vf_ref, vb_ref, af_ref, ab_ref, lwf_ref, lwb_ref, kdf_ref, kdb_ref,
                    bdf_ref, bdb_ref, inc_ref, strict_ref, bm_ref, yf_ref, yb_ref, s_ref):
    p = pl.program_id(1)

    @pl.when(p == 0)
    def _():
        s_ref[...] = jnp.zeros_like(s_ref)

    n = N_HEADS * CHUNK
    bm = bm_ref[...]
    eye = (lax.broadcasted_iota(jnp.int32, (CHUNK, n), 1) % CHUNK
           == lax.broadcasted_iota(jnp.int32, (CHUNK, n), 0)).astype(F32)
    refs = ((rf_ref, vf_ref, af_ref, lwf_ref, kdf_ref, bdf_ref, yf_ref),
            (rb_ref, vb_ref, ab_ref, lwb_ref, kdb_ref, bdb_ref, yb_ref))
    orders = [list(_chunk_order(d)) for d in range(2)]
    bm_b = bm.astype(BF16)
    bd_of = lambda m: _block_diag(m, bm_b)

    nb = s_ref.shape[0] // 2
    inst = [(bi, step, d) for step in range(TQ // CHUNK) for bi in range(nb) for d in range(2)]
    idx = range(len(inst))
    rows = [pl.ds(orders[d][step] * CHUNK, CHUNK) for _, step, d in inst]

    ar, r_t, v_s, a_s, bk_t, decay, v_in, g_b, g_k = [], [], [], [], [], [], [], [], []
    for (bi, step, d), sl in zip(inst, rows):
        r_ref, v_ref, a_ref, lw_ref, kd_ref, bd_ref, _ = refs[d]
        lw = lw_ref[0, bi, sl, :]
        kd = kd_ref[0, bi, sl, :]
        bd = bd_ref[0, bi, sl, :]
        v = v_ref[bi, sl, :]
        cs = _cumsum_rows(lw, reverse=(d == 1))
        tot = cs[CHUNK - 1:CHUNK, :] if d == 0 else cs[0:1, :]
        e_neg = jnp.exp(-cs)
        e_hat = jnp.exp(tot - cs)
        a_t = a_ref[bi, sl, :] * jnp.exp(cs - lw)
        rt = r_ref[bi, sl, :] * jnp.exp(cs)
        ar.append(jnp.concatenate([a_t, rt], axis=0).astype(BF16))
        r_t.append(rt)
        v_in.append(v)
        g_b.append(_dot_nt(ar[-1], bd_of(bd * e_neg)))
        g_k.append(_dot_nt(ar[-1], bd_of(kd * e_neg)))
        v_s.append(bd_of(v))
        a_s.append(bd_of(a_t))
        t = jnp.concatenate([bd * e_hat, kd * e_hat,
                             jnp.broadcast_to(jnp.exp(tot), (2 * CHUNK, GROUP_W))], axis=0).T
        bk_t.append(t[:, :2 * CHUNK].astype(BF16))
        decay.append(jnp.concatenate([t[:, 2 * CHUNK:]] * 2, axis=1))
    pw =[g_b[i][:CHUNK] * strict_ref[inst[i][2]] for i in idx]
    ak = [(g_k[i][:CHUNK] * strict_ref[inst[i][2]]).astype(BF16) for i in idx]
    rbk = [jnp.concatenate([g_b[i][CHUNK:] * inc_ref[inst[i][2]], g_k[i][CHUNK:] * inc_ref[inst[i][2]]],
                           axis=1).astype(BF16) for i in idx]
    x = [eye + pw[i] for i in idx]
    pw = [_dot(pw[i].astype(BF16), bd_of(pw[i])) for i in idx]
    for _ in range(4):
        px = [_dot(jnp.concatenate([pw[i], x[i]], axis=0).astype(BF16), bd_of(pw[i])) for i in idx]
        x = [x[i] + px[i][CHUNK:] for i in idx]
        pw = [px[i][:CHUNK] for i in idx]
    x = [(x[i] + _dot(x[i].astype(BF16), bd_of(pw[i]))).astype(BF16) for i in idx]
    akv = [_dot(ak[i], v_s[i]) for i in idx]
    p12 = [_dot(x[i], jnp.concatenate([bd_of(akv[i]), a_s[i]], axis=1)) for i in idx]
    p1_s = [bd_of(p12[i][:, :GROUP_W]) for i in idx]
    p2_s = [bd_of(p12[i][:, GROUP_W:]) for i in idx]
    q = [r_t[i] + _dot(rbk[i][:, :n], p2_s[i]) for i in idx]
    y_c = [_dot(rbk[i], jnp.concatenate([p1_s[i], v_s[i]], axis=0)) for i in idx]
    zero = jnp.zeros((CHUNK, GROUP_W), F32)
    gc = [_dot(bk_t[i], jnp.concatenate(
        [jnp.concatenate([p12[i][:, GROUP_W:], p12[i][:, :GROUP_W]], axis=1),
         jnp.concatenate([zero, v_in[i]], axis=1)], axis=0).astype(BF16)) for i in idx]
    gq = [jnp.concatenate([gc[i][:, :GROUP_W] * bm, q[i]], axis=0).astype(BF16) for i in idx]
    c_s = [gc[i][:, GROUP_W:] * bm for i in idx]

    s = [s_ref[k] for k in range(2 * nb)]
    for i, ((bi, step, d), sl) in enumerate(zip(inst, rows)):
        k = 2 * bi + d
        m = _dot(gq[i], s[k].astype(BF16))
        refs[d][6][bi, sl, :] = m[n:] + y_c[i]
        s[k] = decay[i] * s[k] + m[:n] + c_s[i]
    for k in range(2 * nb):
        s_ref[k] = s[k]


def _rw_scan(r, v, ah, lw, kd, bd, masks, bm, T):
    B, S, _ = r.shape
    nlat = T // TQ
    inc, strict = masks
    n = N_HEADS * CHUNK
    nb = RW_SCAN_BATCH if B % RW_SCAN_BATCH == 0 else 1
    fwd = pl.BlockSpec((nb, TQ, GROUP_W), lambda b, p: (b, _fwd_tile(p, nlat), 0))
    bwd = pl.BlockSpec((nb, TQ, GROUP_W), lambda b, p: (b, _bwd_tile(p, nlat), 0))
    fwd2 = pl.BlockSpec((1, nb, TQ, GROUP_W), lambda b, p: (0, b, _fwd_tile(p, nlat), 0))
    bwd2 = pl.BlockSpec((1, nb, TQ, GROUP_W), lambda b, p: (1, b, _bwd_tile(p, nlat), 0))
    out = jax.ShapeDtypeStruct((B, S, GROUP_W), F32)
    return pl.pallas_call(
        _rw_scan_kernel,
        grid=(B // nb, nlat + 1),
        in_specs=[fwd, bwd, fwd, bwd, fwd, bwd, fwd2, bwd2, fwd2, bwd2, fwd2, bwd2,
                  _const_spec((2, CHUNK, n)), _const_spec((2, CHUNK, n)), _const_spec((n, n))],
        out_specs=[fwd, bwd],
        out_shape=[out, out],
        scratch_shapes=[pltpu.VMEM((2 * nb, n, GROUP_W), F32)],
        compiler_params=_cparams(2),
    )(r, r, v, v, ah, ah, lw, lw, kd, kd, bd, bd, inc, strict, bm)


def _gla_scan_kernel(qf_ref, qb_ref, kf_ref, kb_ref, gf_ref, gb_ref, vf_ref, vb_ref, gup_ref, gbias_ref,
                     inc_ref, bm_ref, bmv_ref, of_ref, ob_ref, s_ref):
    p = pl.program_id(1)

    @pl.when(p == 0)
    def _():
        s_ref[...] = jnp.zeros_like(s_ref)

    wk = N_HEADS * GLA_DK
    n = N_HEADS * CHUNK
    bm = bm_ref[...]
    bmv = bmv_ref[...].astype(BF16)
    bmk = (lax.broadcasted_iota(jnp.int32, (n, wk), 0) // CHUNK
           == lax.broadcasted_iota(jnp.int32, (n, wk), 1) // GLA_DK).astype(F32).astype(BF16)
    refs = ((qf_ref, kf_ref, gf_ref, vf_ref, of_ref), (qb_ref, kb_ref, gb_ref, vb_ref, ob_ref))
    orders = [list(_chunk_order(d)) for d in range(2)]
    nb = s_ref.shape[0] // 2
    inst = [(bi, step, d) for step in range(TQ // CHUNK) for bi in range(nb) for d in range(2)]
    idx = range(len(inst))
    rows = [pl.ds(orders[d][step] * CHUNK, CHUNK) for _, step, d in inst]

    qe, ke_s, ks4, v_s, decay = [], [], [], [], []
    for (bi, step, d), sl in zip(inst, rows):
        q_ref, k_ref, g_ref, v_ref, _ = refs[d]
        k = k_ref[bi, sl, :]
        la = -_softplus(-(_dot(g_ref[bi, sl, :].astype(BF16), gup_ref[d]) + gbias_ref[d])) * (1.0 / GLA_TAU)
        b = _cumsum_rows(la, reverse=(d == 1))
        tot = b[CHUNK - 1:CHUNK, :] if d == 0 else b[0:1, :]
        qe.append((q_ref[bi, sl, :] * (GLA_DK ** -0.5) * jnp.exp(b)).astype(BF16))
        ke_s.append(_block_diag(k * jnp.exp(-b), bmk))
        ks4.append(_tile_rows(k * jnp.exp(tot - b)).T.astype(BF16))
        v_s.append(_block_diag(v_ref[bi, sl, :], bmv))
        decay.append(_tile_rows(jnp.broadcast_to(jnp.exp(tot), (CHUNK, wk))).T)
    a_cat = [(_dot_nt(qe[i], ke_s[i]) * inc_ref[inst[i][2]]).astype(BF16) for i in idx]
    o_in = [_dot(a_cat[i], v_s[i]) for i in idx]
    kv = [_dot(ks4[i], v_s[i]) * bm for i in idx]

    s = [s_ref[k] for k in range(2 * nb)]
    for i, ((bi, step, d), sl) in enumerate(zip(inst, rows)):
        k = 2 * bi + d
        refs[d][4][bi, sl, :] = o_in[i] + _dot(qe[i], s[k].astype(BF16))
        s[k] = decay[i] * s[k] + kv[i]
    for k in range(2 * nb):
        s_ref[k] = s[k]


def _gla_scan(z, gup_p, gb, masks, bm, bmv, T, layer):
    B, S, _ = z.shape
    nlat = T // TQ
    inc, _ = masks
    n = N_HEADS * CHUNK
    wk = N_HEADS * GLA_DK
    nb = GLA_SCAN_BATCH if B % GLA_SCAN_BATCH == 0 else 1
    fwd = lambda w, col: pl.BlockSpec((nb, TQ, w), lambda b, p: (b, _fwd_tile(p, nlat), col // w))
    bwd = lambda w, col: pl.BlockSpec((nb, TQ, w), lambda b, p: (b, _bwd_tile(p, nlat), col // w))
    out = jax.ShapeDtypeStruct((B, S, GROUP_W), F32)
    return pl.pallas_call(
        _gla_scan_kernel,
        grid=(B // nb, nlat + 1),
        in_specs=[fwd(wk, COL_GLA_Q), bwd(wk, COL_GLA_Q), fwd(wk, COL_GLA_K), bwd(wk, COL_GLA_K),
                  fwd(wk, COL_GLA_G), bwd(wk, COL_GLA_G), fwd(GROUP_W, COL_GLA_V), bwd(GROUP_W, COL_GLA_V),
                  _const_spec((2, wk, wk), layer), _const_spec((2, 1, wk), layer),
                  _const_spec((2, CHUNK, n)), _const_spec((wk, GROUP_W)), _const_spec((n, n))],
        out_specs=[fwd(GROUP_W, 0), bwd(GROUP_W, 0)],
        out_shape=[out, out],
        scratch_shapes=[pltpu.VMEM((2 * nb, wk, GROUP_W), F32)],
        compiler_params=_cparams(2),
    )(z, z, z, z, z, z, z, z, gup_p, gb, inc, bm, bmv)


def _pack_w_in(w_in):
    L, D, _ = w_in.shape
    na, mla, rw, gla = jnp.split(w_in.astype(BF16), [768, 1120, 2144], axis=-1)
    cq, ckv, kr = jnp.split(mla, [MLA_Q_RANK, MLA_Q_RANK + MLA_KV_RANK], axis=-1)
    gq, gk, gv, gg, go = jnp.split(gla, [128, 256, 512, 528], axis=-1)
    pad = jnp.zeros((L, D, COL_GLA_V - COL_GLA_G - gg.shape[-1]), BF16)
    packed = jnp.concatenate([na, ckv, cq, kr, kr[..., _rope_swap_perm()], gq, gk, gg, pad, gv, go, rw], axis=-1)
    assert packed.shape[-1] == Z_COLS
    return packed


def kernel(x, c, ctx, c_ctx, w_mod, b_mod, g_mix_pre, g_mix_post, g_ffn_pre, g_ffn_post, w_in, w_out, na_rpb, mla_q_norm, mla_w_uq, mla_kv_norm, mla_w_ukv, rw_mu, rw_w0, rw_w_up, rw_a0, rw_a_up, rw_g_up, rw_k_k, rw_k_a, rw_r_k, rw_ln_w, rw_ln_b, gla_gate_up, gla_gate_b, gla_norm, ffn_w_up, ffn_conv_w, ffn_conv_b, ffn_w_down):
    B, T, D = x.shape
    Tc = ctx.shape[1]
    L = w_in.shape[0]
    assert D == D_MODEL and Tc == TQ and T % TMT == 0 and B + 1 <= 8

    w_in_p = _pack_w_in(w_in)
    w_out_b = w_out.astype(BF16)
    wq1, wq2, wk, wv = _mla_weights(mla_w_uq, mla_w_ukv)
    place = _rope_place()
    cos, sin = _rope_tables(T, Tc)
    zero_lo = jnp.zeros((L, 64, 2 * GROUP_W), F32)
    rw_wup_p = jnp.concatenate([jnp.concatenate([rw_w_up[:, 0], rw_w_up[:, 1]], axis=-1), zero_lo], axis=1).astype(BF16)
    rw_aup_p = jnp.concatenate([zero_lo, jnp.concatenate([rw_a_up[:, 0], rw_a_up[:, 1]], axis=-1)], axis=1).astype(BF16)
    rw_gup_b = rw_g_up.astype(BF16)
    wk_gla = N_HEADS * GLA_DK
    gla_gup_p = jnp.concatenate([gla_gate_up, jnp.zeros((L, 2, wk_gla - gla_gate_up.shape[2], wk_gla), F32)], axis=2).astype(BF16)
    ffn_up_b = ffn_w_up.astype(BF16)
    ffn_dn_b = ffn_w_down.astype(BF16)
    ones_b = jnp.asarray(_block_ones(GROUP_W, HEAD_DIM), BF16)
    rw_bm = jnp.asarray(_block_ones(N_HEADS * CHUNK, CHUNK))
    gla_bm = jnp.asarray((np.arange(wk_gla)[:, None] // GLA_DK == np.arange(GROUP_W)[None, :] // HEAD_DIM).astype(np.float32))
    masks = _scan_masks()

    cvecs = jnp.zeros((8, D), F32).at[:B].set(c).at[B].set(c_ctx)
    mods = _modulation(cvecs, w_mod, b_mod).reshape(L, 8, 6, D)
    mods = jnp.pad(mods, ((0, 0), (0, 0), (0, 2), (0, 0)))
    modtabs = jnp.stack([jnp.broadcast_to(mods[:, B:B + 1], (L, B, 8, D)), mods[:, :B]], axis=2)

    xs = jnp.concatenate([x, ctx], axis=1)
    rows = lambda a: a.reshape(L, 1, -1)
    na_bias = _natten_bias(na_rpb, T // GRID_W)
    gla_gb = gla_gate_b[:, :, None, :]
    for i in range(L):
        z = _in_proj(xs, modtabs, rows(g_mix_pre), w_in_p, T, i)
        y_na = _natten(z, na_bias, T, i)
        qt, k, vt = _mla_up(z, cos, sin, rows(mla_q_norm), rows(mla_kv_norm), wq1, wq2, wk, wv, place, T, i)
        y_mla = _mla_attn(qt, k, vt, T)
        r, vv, ah, g, bon, lw, kd, bd = _rw_prep(z, rows(rw_mu), rows(rw_k_k), rows(rw_k_a), rows(rw_r_k),
                                                 rw_w0, rw_a0, rw_wup_p, rw_aup_p, rw_gup_b, ones_b, T, i)
        yf, yb = _rw_scan(r, vv, ah, lw, kd, bd, masks, rw_bm, T)
        of, ob = _gla_scan(z, gla_gup_p, gla_gb, masks, gla_bm, rw_bm, T, i)
        xs = _out_proj(xs, z, y_na, y_mla, yf, yb, bon, g, of, ob, modtabs, rows(g_mix_post), w_out_b,
                       rows(rw_ln_w), rows(rw_ln_b), rows(gla_norm), ones_b, T, i)
        xs = _ffn(xs, modtabs, rows(g_ffn_pre), rows(g_ffn_post), ffn_up_b, ffn_conv_w,
                  rows(ffn_conv_b), ffn_dn_b, T, i, latent_only=(i == L - 1))
    return xs
```

```python
import functools

import numpy as np
import jax
import jax.numpy as jnp
from jax import lax
from jax.experimental import pallas as pl
from jax.experimental.pallas import tpu as pltpu

F32 = jnp.float32
BF16 = jnp.bfloat16

D_MODEL = 1024
GRID_W = 64
EPS = 1e-6
LOG2E = 1.4426950408889634
N_HEADS = 4
HEAD_DIM = 64
GROUP_W = 256
NA_ROWS = 8
NA_COLS = 16
NA_UNION = 12
MLA_Q_RANK = 192
MLA_KV_RANK = 128
MLA_NOPE = 64
MLA_ROPE = 32
MLA_HEAD_PAD = 128
ROPE_THETA = 10000.0
RW_GN_EPS = 64e-5
GLA_DK = 32
GLA_TAU = 16.0
D_FF = 2816
CHUNK = 64
TQ = 256
MLA_TQ = 512
RW_SCAN_BATCH = 2
GLA_SCAN_BATCH = 4
TMT = 512
TM_WIDE = 1024
HALO = 8
FF_CHUNK = 256
FF_GROUP = 4
Z_COLS = 3072

COL_NA = 0
COL_MLA = 768
COL_GLA_Q = 1152
COL_GLA_K = 1280
COL_GLA_G = 1408
COL_GLA_V = 1536
COL_GLA_O = 1792
COL_RW = 2048

VMEM_LIMIT_V7X = 56 * 1024 * 1024


def _cparams(n_axes):
    return pltpu.CompilerParams(dimension_semantics=("arbitrary",) * n_axes,
                                vmem_limit_bytes=VMEM_LIMIT_V7X)


def _const_spec(shape, layer=None):
    nd = len(shape)
    if layer is None:
        return pl.BlockSpec(shape, lambda *_: (0,) * nd, pipeline_mode=pl.Buffered(1))
    return pl.BlockSpec((None,) + tuple(shape), lambda *_: (layer,) + (0,) * nd, pipeline_mode=pl.Buffered(1))


def _dot(a, b):
    return jnp.dot(a, b, preferred_element_type=F32)


def _split3(x):
    hi = x.astype(BF16)
    r1 = x - hi.astype(F32)
    mid = r1.astype(BF16)
    lo = (r1 - mid.astype(F32)).astype(BF16)
    return hi, mid, lo


def _dot_sel_rhs(x, m):
    hi, mid, lo = _split3(x)
    return _dot(hi, m) + (_dot(mid, m) + _dot(lo, m))


def _sigmoid(x):
    return 1.0 / (1.0 + jnp.exp(-x))


def _silu_gain(x):
    return 0.5 + 0.5 * jnp.tanh(0.5 * x)


def _softplus(x):
    return jnp.maximum(x, 0.0) + jnp.log1p(jnp.exp(-jnp.abs(x)))


def _rms(x):
    return x * lax.rsqrt(jnp.mean(x * x, axis=-1, keepdims=True) + EPS)


def _norm_mod(x, g, shift, scale):
    return (_rms(x) * g) * (1.0 + scale) + shift


def _lane_head(width, per_head):
    return lax.broadcasted_iota(jnp.int32, (1, width), 1) // per_head


def _block_ones(n, blk):
    i = np.arange(n) // blk
    return (i[:, None] == i[None, :]).astype(np.float32)


def _mod_kernel(c_ref, w_ref, b_ref, o_ref):
    cv = c_ref[...]
    s = cv * _sigmoid(cv)
    o_ref[0] = _dot(s.astype(BF16), w_ref[0].astype(BF16)) + b_ref[0]


def _modulation(cvecs, w_mod, b_mod):
    L, D, N = w_mod.shape
    tn = 1536
    return pl.pallas_call(
        _mod_kernel,
        grid=(L, N // tn),
        in_specs=[pl.BlockSpec((8, D), lambda l, n: (0, 0)),
                  pl.BlockSpec((1, D, tn), lambda l, n: (l, 0, n)),
                  pl.BlockSpec((1, 1, tn), lambda l, n: (l, 0, n))],
        out_specs=pl.BlockSpec((1, 8, tn), lambda l, n: (l, 0, n)),
        out_shape=jax.ShapeDtypeStruct((L, 8, N), F32),
        compiler_params=_cparams(2),
    )(cvecs, w_mod, b_mod.reshape(L, 1, N))


def _tok_tiles(S, T, tm=TMT):
    assert T % tm == 0 and 0 < S - T <= tm
    return T // tm + 1, T // tm


def _tok_spec(width, col_block=0, tm=TMT):
    return pl.BlockSpec((1, tm, width), lambda b, i: (b, i, col_block))


def _mod_spec(nlat, layer):
    return pl.BlockSpec((None, 1, 1, 8, D_MODEL), lambda b, i: (layer, b, jnp.where(i < nlat, 1, 0), 0, 0))


def _halo_specs(width, col_block, S, tm=TMT):
    per = tm // HALO
    last = S // HALO - 1
    prev = pl.BlockSpec((1, HALO, width), lambda b, i: (b, jnp.maximum(i * per - 1, 0), col_block))
    nxt = pl.BlockSpec((1, HALO, width), lambda b, i: (b, jnp.minimum((i + 1) * per, last), col_block))
    return prev, nxt


def _neighbour_ok(i, nlat):
    prev_ok = jnp.logical_and(i != 0, i != nlat)
    next_ok = i < nlat - 1
    return prev_ok, next_ok


def _for_tile_rows(nlat, tc, body, tm=TMT):
    i = pl.program_id(1)

    @pl.when(i < nlat)
    def _():
        body(tm)

    @pl.when(i >= nlat)
    def _():
        body(tc)


def _stream_source(src, T, Tc, tm):
    nlat = T // tm
    assert Tc < tm
    lat = pl.BlockSpec((1, tm, D_MODEL), lambda b, i: (b, jnp.minimum(i, nlat - 1), 0))
    if isinstance(src, tuple):
        return src, [lat, pl.BlockSpec((1, Tc, D_MODEL), lambda b, i: (b, 0, 0))]
    assert T % Tc == 0
    return (src, src), [lat, pl.BlockSpec((1, Tc, D_MODEL), lambda b, i: (b, T // Tc, 0))]


def _in_proj_kernel(xl_ref, xc_ref, mod_ref, g_ref, w_ref, z_ref, *, nlat, tc, tm):
    def body(nr):
        m = mod_ref[0, 0]
        x = xl_ref[0] if nr == tm else xc_ref[0]
        h = _norm_mod(x, g_ref[...], m[0:1], m[1:2])
        z_ref[0, :nr] = _dot(h.astype(BF16), w_ref[...])

    _for_tile_rows(nlat, tc, body, tm)


def _in_proj(src, modtab, g_pre, w_in_p, T, Tc, layer):
    tm = TM_WIDE
    arrays, specs = _stream_source(src, T, Tc, tm)
    B, S = arrays[0].shape[0], T + Tc
    ntiles, nlat = _tok_tiles(S, T, tm)
    return pl.pallas_call(
        functools.partial(_in_proj_kernel, nlat=nlat, tc=Tc, tm=tm),
        grid=(B, ntiles),
        in_specs=specs + [_mod_spec(nlat, layer), _const_spec((1, D_MODEL), layer),
                          _const_spec((D_MODEL, Z_COLS), layer)],
        out_specs=_tok_spec(Z_COLS, 0, tm),
        out_shape=jax.ShapeDtypeStruct((B, S, Z_COLS), F32),
        compiler_params=_cparams(2),
    )(*arrays, modtab, g_pre, w_in_p)


def _out_proj_kernel(xl_ref, xc_ref, na_ref, mla_ref, yf_ref, yb_ref, bon_ref, g_ref, of_ref, ob_ref, og_ref,
                     mod_ref, gpost_ref, w_ref, lnw_ref, lnb_ref, gn_ref, ones_ref, o_ref, *, nlat, tc, tm):
    def body(nr):
        m = mod_ref[0, 0]
        x = xl_ref[0] if nr == tm else xc_ref[0]
        ones = ones_ref[...]
        inv_n = 1.0 / HEAD_DIM
        y = yf_ref[0, :nr] + yb_ref[0, :nr]
        yc = y - _dot_sel_rhs(y, ones) * inv_n
        var = _dot_sel_rhs(yc * yc, ones) * inv_n
        y_rw = (yc * lax.rsqrt(var + RW_GN_EPS) * lnw_ref[...] + lnb_ref[...] + bon_ref[0, :nr]) * g_ref[0, :nr]
        o = of_ref[0, :nr] + ob_ref[0, :nr]
        ms = _dot_sel_rhs(o * o, ones) * inv_n
        og = og_ref[0, :nr]
        y_gla = (o * lax.rsqrt(ms + EPS) * gn_ref[...]) * (og * _silu_gain(og))
        y = jnp.concatenate([na_ref[0, :nr], mla_ref[0, :nr], y_rw.astype(BF16), y_gla.astype(BF16)], axis=-1)
        y = _dot(y, w_ref[...])
        o_ref[0, :nr] = x + m[2:3] * (_rms(y) * gpost_ref[...])

    _for_tile_rows(nlat, tc, body, tm)


def _out_proj(src, z, y_na, y_mla, yf, yb, bon, g, of, ob, modtab, g_post, w_out_b, ln_w, ln_b, gla_norm,
              ones_b, T, layer):
    B, S, _ = z.shape
    D = D_MODEL
    tm = TM_WIDE
    arrays, specs = _stream_source(src, T, S - T, tm)
    ntiles, nlat = _tok_tiles(S, T, tm)
    grp = _tok_spec(GROUP_W, 0, tm)
    vec = _const_spec((1, GROUP_W), layer)
    return pl.pallas_call(
        functools.partial(_out_proj_kernel, nlat=nlat, tc=S - T, tm=tm),
        grid=(B, ntiles),
        in_specs=specs + [grp, grp, grp, grp, grp, grp, grp, grp,
                          _tok_spec(GROUP_W, COL_GLA_O // GROUP_W, tm),
                          _mod_spec(nlat, layer), _const_spec((1, D), layer), _const_spec((D, D), layer),
                          vec, vec, vec, _const_spec((GROUP_W, GROUP_W))],
        out_specs=_tok_spec(D, 0, tm),
        out_shape=jax.ShapeDtypeStruct((B, S, D), F32),
        compiler_params=_cparams(2),
    )(*arrays, y_na, y_mla, yf, yb, bon, g, of, ob, z, modtab, g_post, w_out_b, ln_w, ln_b, gla_norm, ones_b)


def _ffn_kernel(xp_ref, x_ref, xn_ref, mod_ref, gpre_ref, gpost_ref, wup_ref, cw_ref, cb_ref,
                wdn_ref, o_ref, *, nlat, tc):
    i = pl.program_id(1)
    prev_ok, next_ok = _neighbour_ok(i, nlat)
    nchunks = D_FF // FF_CHUNK

    def body(nr):
        m = mod_ref[0, 0]
        x = x_ref[0, :nr]
        xe = jnp.concatenate([xp_ref[0], x, xn_ref[0]], axis=0)
        n = nr + 2 * HALO
        h = _norm_mod(xe, gpre_ref[...], m[3:4], m[4:5])
        row = lax.broadcasted_iota(jnp.int32, (n, 1), 0)
        valid = jnp.logical_or(jnp.logical_and(row >= HALO, row < HALO + nr),
                               jnp.logical_or(jnp.logical_and(row < HALO, prev_ok),
                                              jnp.logical_and(row >= HALO + nr, next_ok)))
        hb = jnp.where(valid, h, 0.0).astype(BF16)

        def up(c):
            return [_dot(hb, wup_ref[:, base + c * FF_CHUNK:base + (c + 1) * FF_CHUNK]) for base in (0, D_FF)]

        def conv(z, lo):
            cw = cw_ref[:, lo:lo + FF_CHUNK]
            return (cb_ref[:, lo:lo + FF_CHUNK]
                    + pltpu.roll(z, 1, 0)[HALO:HALO + nr] * cw[0:1]
                    + z[HALO:HALO + nr] * cw[1:2]
                    + pltpu.roll(z, n - 1, 0)[HALO:HALO + nr] * cw[2:3])

        acc = None
        group = []
        z_next = up(0)
        for c in range(nchunks):
            z_val, z_gate = z_next
            if c + 1 < nchunks:
                z_next = up(c + 1)
            val = conv(z_val, c * FF_CHUNK)
            gate = conv(z_gate, D_FF + c * FF_CHUNK)
            group.append(((gate * _silu_gain(gate)) * val).astype(BF16))
            if len(group) == FF_GROUP or c + 1 == nchunks:
                lo = (c + 1 - len(group)) * FF_CHUNK
                part = _dot(jnp.concatenate(group, axis=1), wdn_ref[lo:(c + 1) * FF_CHUNK, :])
                acc = part if acc is None else acc + part
                group = []
        o_ref[0, :nr] = x + m[5:6] * (_rms(acc) * gpost_ref[...])

    _for_tile_rows(nlat, tc, body)


def _ffn(xs, modtab, g_pre, g_post, w_up_b, conv_w, conv_b, w_dn_b, T, layer, latent_only):
    B, S, D = xs.shape
    ntiles, nlat = _tok_tiles(S, T)
    prev, nxt = _halo_specs(D, 0, S)
    return pl.pallas_call(
        functools.partial(_ffn_kernel, nlat=nlat, tc=S - T),
        grid=(B, nlat if latent_only else ntiles),
        in_specs=[prev, _tok_spec(D), nxt, _mod_spec(nlat, layer), _const_spec((1, D), layer),
                  _const_spec((1, D), layer), _const_spec((D, 2 * D_FF), layer),
                  _const_spec((3, 2 * D_FF), layer), _const_spec((1, 2 * D_FF), layer),
                  _const_spec((D_FF, D), layer)],
        out_specs=_tok_spec(D),
        out_shape=jax.ShapeDtypeStruct((B, T if latent_only else S, D), F32),
        compiler_params=_cparams(2),
    )(xs, xs, xs, modtab, g_pre, g_post, w_up_b, conv_w, conv_b, w_dn_b)


def _natten_bias(rpb, rows):
    rt = TQ // GRID_W
    j = np.arange(GRID_W)
    col_start = np.clip(j - NA_COLS // 2, 0, GRID_W - NA_COLS)
    col_in = (j[None, :] >= col_start[:, None]) & (j[None, :] < col_start[:, None] + NA_COLS)
    edge = GRID_W - NA_COLS
    ext = jnp.concatenate([jnp.repeat(rpb[..., :1], edge, axis=-1), rpb,
                           jnp.repeat(rpb[..., -1:], edge, axis=-1)], axis=-1).astype(F32) * LOG2E
    bq = jnp.stack([ext[..., GRID_W - 1 - q:2 * GRID_W - 1 - q] for q in range(GRID_W)], axis=2)
    L = rpb.shape[0]
    bq = jnp.where(col_in[:, None, :], bq, -jnp.inf).reshape(L, N_HEADS, GRID_W, -1)
    cases = []
    for r0 in (0, rt, rows - rt):
        us = min(max(r0 - NA_ROWS // 2, 0), rows - NA_UNION)
        per_row = []
        for r in range(r0, r0 + rt):
            rs = min(max(r - NA_ROWS // 2, 0), rows - NA_ROWS)
            first = rs - r + (NA_ROWS - 1)
            pre, post = rs - us, us + NA_UNION - (rs + NA_ROWS)
            per_row.append(jnp.pad(bq[..., first * GRID_W:(first + NA_ROWS) * GRID_W],
                                   ((0, 0), (0, 0), (0, 0), (pre * GRID_W, post * GRID_W)),
                                   constant_values=-jnp.inf))
        cases.append(jnp.concatenate(per_row, axis=2))
    return jnp.stack(cases, axis=1)


def _natten_kernel(q_ref, k_ref, v_ref, qc_ref, kc_ref, vc_ref, bias_a_ref, bias_b_ref, o_ref, *, nlat, rows):
    j = pl.program_id(1)
    lane_h = _lane_head(GROUP_W, HEAD_DIM)
    kct = kc_ref[0].T.astype(BF16)
    vc = vc_ref[0].astype(BF16)
    scale = HEAD_DIM ** -0.5 * LOG2E
    nwin = NA_UNION * GRID_W

    @pl.when(j < nlat // 2)
    def _():
        wins = []
        for half in range(2):
            us = jnp.clip((2 * j + half) * (TQ // GRID_W) - NA_ROWS // 2, 0, rows - NA_UNION)
            start = pl.multiple_of(us * GRID_W, GRID_W)
            wins.append((k_ref[0, pl.ds(start, nwin), :].T.astype(BF16),
                         v_ref[0, pl.ds(start, nwin), :].astype(BF16)))
        for half, bias_ref in enumerate((bias_a_ref, bias_b_ref)):
            kwt, vw = wins[half]
            q = q_ref[0, half * TQ:(half + 1) * TQ, :] * scale
            acc = jnp.zeros((TQ, GROUP_W), F32)

            def logits(h):
                qh = jnp.where(lane_h == h, q, 0.0).astype(BF16)
                return _dot(qh, kwt), _dot(qh, kct)

            s_next = logits(0)
            for h in range(N_HEADS):
                hm = lane_h == h
                s_w, s_c = s_next
                if h + 1 < N_HEADS:
                    s_next = logits(h + 1)
                s_w = s_w + bias_ref[0, h]
                mx = jnp.maximum(jnp.max(s_w, axis=-1, keepdims=True), jnp.max(s_c, axis=-1, keepdims=True))
                p_w = jnp.exp2(s_w - mx)
                p_c = jnp.exp2(s_c - mx)
                den = jnp.sum(p_w, axis=-1, keepdims=True) + jnp.sum(p_c, axis=-1, keepdims=True)
                o = _dot(p_w.astype(BF16), vw) + _dot(p_c.astype(BF16), vc)
                acc = acc + jnp.where(hm, o * (1.0 / den), 0.0)
            o_ref[0, half * TQ:(half + 1) * TQ, :] = acc.astype(o_ref.dtype)

    @pl.when(j >= nlat // 2)
    def _():
        q = qc_ref[0] * scale
        acc = jnp.zeros((TQ, GROUP_W), F32)
        for h in range(N_HEADS):
            hm = lane_h == h
            s = _dot(jnp.where(hm, q, 0.0).astype(BF16), kct)
            p = jnp.exp2(s - jnp.max(s, axis=-1, keepdims=True))
            den = jnp.sum(p, axis=-1, keepdims=True)
            acc = acc + jnp.where(hm, _dot(p.astype(BF16), vc) * (1.0 / den), 0.0)
        o_ref[0, :TQ, :] = acc.astype(o_ref.dtype)


def _natten(z, bias, T, layer):
    B, S, _ = z.shape
    nlat = T // TQ
    rows = T // GRID_W
    assert rows >= 16 and nlat % 2 == 0 and S - T == TQ
    cb = COL_NA // GROUP_W
    npair = nlat // 2
    lat = lambda c: pl.BlockSpec((1, T, GROUP_W), lambda b, j: (b, 0, cb + c))
    ctx = lambda c: pl.BlockSpec((1, TQ, GROUP_W), lambda b, j: (b, nlat, cb + c))
    case = lambda t: jnp.where(t == 0, 0, jnp.where(t >= nlat - 1, 2, 1))
    tile = lambda j, half: 2 * jnp.minimum(j, npair - 1) + half
    bias_spec = lambda half: pl.BlockSpec((None, 1, N_HEADS, TQ, NA_UNION * GRID_W),
                                          lambda b, j: (layer, case(tile(j, half)), 0, 0, 0))
    return pl.pallas_call(
        functools.partial(_natten_kernel, nlat=nlat, rows=rows),
        grid=(B, npair + 1),
        in_specs=[pl.BlockSpec((1, 2 * TQ, GROUP_W), lambda b, j: (b, jnp.minimum(j, npair - 1), cb)),
                  lat(1), lat(2), ctx(0), ctx(1), ctx(2), bias_spec(0), bias_spec(1)],
        out_specs=pl.BlockSpec((1, 2 * TQ, GROUP_W), lambda b, j: (b, j, 0)),
        out_shape=jax.ShapeDtypeStruct((B, S, GROUP_W), BF16),
        compiler_params=_cparams(2),
    )(z, z, z, z, z, z, bias, bias)


def _rope_tables(T, Tc):
    t = np.arange(T)
    row = (t // GRID_W).astype(np.float32)
    col = (t % GRID_W).astype(np.float32)
    d = MLA_ROPE // 2
    inv = (np.float32(ROPE_THETA) ** (-np.arange(0, d, 2, dtype=np.float32) / np.float32(d))).astype(np.float32)
    cs, sn = [], []
    for pos in (row, col):
        ang = (pos[:, None] * inv[None, :]).astype(np.float32)
        cs += [np.cos(ang), np.cos(ang)]
        sn += [-np.sin(ang), np.sin(ang)]
    pad = MLA_HEAD_PAD - MLA_NOPE - MLA_ROPE
    f32 = np.float32
    cos = np.concatenate([np.ones((T, MLA_NOPE), f32)] + cs + [np.ones((T, pad), f32)], axis=1)
    sin = np.concatenate([np.zeros((T, MLA_NOPE), f32)] + sn + [np.zeros((T, pad), f32)], axis=1)
    cos = np.concatenate([cos, np.ones((Tc, MLA_HEAD_PAD), f32)], axis=0)
    sin = np.concatenate([sin, np.zeros((Tc, MLA_HEAD_PAD), f32)], axis=0)
    return jnp.asarray(cos, F32), jnp.asarray(sin, F32)


def _rope_swap_perm():
    q = MLA_ROPE // 4
    return np.concatenate([np.arange(q, 2 * q), np.arange(0, q), np.arange(3 * q, 4 * q), np.arange(2 * q, 3 * q)])


def _mla_weights(w_uq, w_ukv):
    L = w_uq.shape[0]
    wq = w_uq.reshape(L, MLA_Q_RANK, N_HEADS, MLA_NOPE + MLA_ROPE)
    pad = MLA_HEAD_PAD - MLA_NOPE - MLA_ROPE
    zq = jnp.zeros((L, MLA_Q_RANK, N_HEADS, pad), F32)
    wq1 = jnp.concatenate([wq, zq], axis=-1).reshape(L, MLA_Q_RANK, N_HEADS * MLA_HEAD_PAD)
    rope_sw = wq[..., MLA_NOPE:][..., _rope_swap_perm()]
    wq2 = jnp.concatenate([jnp.zeros((L, MLA_Q_RANK, N_HEADS, MLA_NOPE), F32), rope_sw, zq], axis=-1)
    wq2 = wq2.reshape(L, MLA_Q_RANK, N_HEADS * MLA_HEAD_PAD)
    wkv = w_ukv.reshape(L, MLA_KV_RANK, N_HEADS, 2 * MLA_NOPE)
    wk = jnp.concatenate([wkv[..., :MLA_NOPE], jnp.zeros((L, MLA_KV_RANK, N_HEADS, MLA_HEAD_PAD - MLA_NOPE), F32)], axis=-1)
    wk = wk.reshape(L, MLA_KV_RANK, N_HEADS * MLA_HEAD_PAD)
    wv_t = wkv[..., MLA_NOPE:].reshape(L, MLA_KV_RANK, N_HEADS * MLA_NOPE).transpose(0, 2, 1)
    return wq1.astype(BF16), wq2.astype(BF16), wk.astype(BF16), wv_t.astype(BF16)


def _rope_place():
    e = np.zeros((MLA_ROPE, N_HEADS * MLA_HEAD_PAD), np.float32)
    for h in range(N_HEADS):
        e[np.arange(MLA_ROPE), h * MLA_HEAD_PAD + MLA_NOPE + np.arange(MLA_ROPE)] = 1.0
    return jnp.asarray(e, BF16)


def _mla_up_kernel(z_ref, cos_ref, sin_ref, qn_ref, kvn_ref, wq1_ref, wq2_ref, wk_ref, wv_ref, e_ref,
                   qt_ref, k_ref, vt_ref, *, nlat, tc):
    def body(nr):
        z = z_ref[0, :nr]
        ckv = z[:, :MLA_KV_RANK]
        cq = z[:, MLA_KV_RANK:MLA_KV_RANK + MLA_Q_RANK]
        kr = z[:, MLA_KV_RANK + MLA_Q_RANK:MLA_KV_RANK + MLA_Q_RANK + MLA_ROPE]
        krs = z[:, MLA_KV_RANK + MLA_Q_RANK + MLA_ROPE:]
        cos = jnp.concatenate([cos_ref[:nr]] * N_HEADS, axis=-1)
        sin = jnp.concatenate([sin_ref[:nr]] * N_HEADS, axis=-1)
        nq = (_rms(cq) * qn_ref[...]).astype(BF16)
        nkv_f = _rms(ckv) * kvn_ref[...]
        nkv = nkv_f.astype(BF16)
        q = _dot(nq, wq1_ref[...]) * cos + _dot(nq, wq2_ref[...]) * sin
        scale = (MLA_NOPE + MLA_ROPE) ** -0.5 * LOG2E
        qt_ref[0, :, :nr] = (q * scale).T.astype(BF16)
        k = _dot(nkv, wk_ref[...]) + _dot_sel_rhs(kr, e_ref[...]) * cos + _dot_sel_rhs(krs, e_ref[...]) * sin
        k_ref[0, :nr] = k.astype(BF16)
        vt_ref[0, :, :nr] = _dot(wv_ref[...], nkv_f.T.astype(BF16)).astype(BF16)

    _for_tile_rows(nlat, tc, body)


def _mla_up(z, cos, sin, q_norm, kv_norm, wq1, wq2, wk, wv, place, T, layer):
    B, S, _ = z.shape
    ntiles, nlat = _tok_tiles(S, T)
    HP = N_HEADS * MLA_HEAD_PAD
    zw = MLA_KV_RANK + MLA_Q_RANK + 2 * MLA_ROPE
    tab = pl.BlockSpec((TMT, MLA_HEAD_PAD), lambda b, i: (i, 0))
    return pl.pallas_call(
        functools.partial(_mla_up_kernel, nlat=nlat, tc=S - T),
        grid=(B, ntiles),
        in_specs=[_tok_spec(zw, COL_MLA // zw), tab, tab,
                  _const_spec((1, MLA_Q_RANK), layer), _const_spec((1, MLA_KV_RANK), layer),
                  _const_spec((MLA_Q_RANK, HP), layer), _const_spec((MLA_Q_RANK, HP), layer),
                  _const_spec((MLA_KV_RANK, HP), layer), _const_spec((GROUP_W, MLA_KV_RANK), layer),
                  _const_spec((MLA_ROPE, HP))],
        out_specs=[pl.BlockSpec((1, HP, TMT), lambda b, i: (b, 0, i)), _tok_spec(HP),
                   pl.BlockSpec((1, GROUP_W, TMT), lambda b, i: (b, 0, i))],
        out_shape=[jax.ShapeDtypeStruct((B, HP, S), BF16),
                   jax.ShapeDtypeStruct((B, S, HP), BF16),
                   jax.ShapeDtypeStruct((B, GROUP_W, S), BF16)],
        compiler_params=_cparams(2),
    )(z, cos, sin, q_norm, kv_norm, wq1, wq2, wk, wv, place)


def _mla_attn_kernel(qt_ref, k_ref, vt_ref, o_ref, *, nlat, T):
    j = pl.program_id(1)

    def attend(lo, hi):
        def logits(h):
            hp = slice(h * MLA_HEAD_PAD, (h + 1) * MLA_HEAD_PAD)
            return _dot(k_ref[0, lo:hi, hp], qt_ref[0, hp, :])

        ahead = 2
        pending = [logits(h) for h in range(ahead)]
        pieces = []
        for h in range(N_HEADS):
            s = pending.pop(0)
            if h + ahead < N_HEADS:
                pending.append(logits(h + ahead))
            p = jnp.exp2(s - jnp.max(s, axis=0, keepdims=True))
            den = jnp.sum(p, axis=0, keepdims=True)
            o = _dot(vt_ref[0, h * HEAD_DIM:(h + 1) * HEAD_DIM, lo:hi], p.astype(BF16))
            pieces.append(o * (1.0 / den))
        o_ref[0] = jnp.concatenate(pieces, axis=0).T.astype(o_ref.dtype)

    @pl.when(j < nlat)
    def _():
        attend(0, T + TQ)

    @pl.when(j >= nlat)
    def _():
        attend(T, T + TQ)


def _mla_attn(qt, k, vt, T):
    B, S, HP = k.shape
    assert T % MLA_TQ == 0 and S - T <= MLA_TQ
    nlat = T // MLA_TQ
    return pl.pallas_call(
        functools.partial(_mla_attn_kernel, nlat=nlat, T=T),
        grid=(B, nlat + 1),
        in_specs=[pl.BlockSpec((1, HP, MLA_TQ), lambda b, j: (b, 0, j)),
                  pl.BlockSpec((1, S, HP), lambda b, j: (b, 0, 0)),
                  pl.BlockSpec((1, GROUP_W, S), lambda b, j: (b, 0, 0))],
        out_specs=pl.BlockSpec((1, MLA_TQ, GROUP_W), lambda b, j: (b, j, 0)),
        out_shape=jax.ShapeDtypeStruct((B, S, GROUP_W), BF16),
        compiler_params=_cparams(2),
    )(qt, k, vt)


def _scan_masks():
    t = np.arange(CHUNK)
    inc = np.stack([t[:, None] >= t[None, :], t[:, None] <= t[None, :]]).astype(np.float32)
    strict = np.stack([t[:, None] > t[None, :], t[:, None] < t[None, :]]).astype(np.float32)
    return jnp.asarray(np.tile(inc, (1, 1, N_HEADS))), jnp.asarray(np.tile(strict, (1, 1, N_HEADS)))


def _cumsum_rows(x, reverse):
    n = x.shape[0]
    row = lax.broadcasted_iota(jnp.int32, (n, 1), 0)
    sh = 1
    while sh < n:
        if reverse:
            x = x + jnp.where(row < n - sh, pltpu.roll(x, n - sh, 0), 0.0)
        else:
            x = x + jnp.where(row >= sh, pltpu.roll(x, sh, 0), 0.0)
        sh *= 2
    return x


def _block_diag(x, bm_b):
    return _tile_rows(x.astype(BF16)) * bm_b


def _dot_nt(a, b):
    return lax.dot_general(a, b, (((1,), (1,)), ((), ())), preferred_element_type=F32)


def _fwd_tile(p, nlat):
    return jnp.where(p == 0, nlat, p - 1)


def _bwd_tile(p, nlat):
    return jnp.where(p == 0, nlat, nlat - p)


def _chunk_order(d):
    nch = TQ // CHUNK
    return range(nch) if d == 0 else range(nch - 1, -1, -1)


def _tile_rows(x):
    return jnp.concatenate([x] * N_HEADS, axis=0)


def _rw_prep_kernel(zp_ref, z_ref, zn_ref, mu_ref, kk_ref, ka_ref, rk_ref, w0_ref, a0_ref, wup_ref,
                    aup_ref, gup_ref, ones_ref,
                    r_ref, v_ref, ah_ref, g_ref, bon_ref, lw_ref, kd_ref, bd_ref, *, nlat, tc):
    i = pl.program_id(1)
    prev_ok, next_ok = _neighbour_ok(i, nlat)

    def body(nr):
        z = z_ref[0, :nr]
        row = lax.broadcasted_iota(jnp.int32, (nr, 1), 0)
        before = jnp.where(prev_ok, zp_ref[0, HALO - 1:HALO, :], 0.0)
        after = jnp.where(next_ok, zn_ref[0, 0:1, :], 0.0)
        zprev = jnp.where(row == 0, before, pltpu.roll(z, 1, 0))
        znext = jnp.where(row == nr - 1, after, pltpu.roll(z, nr - 1, 0))
        zs = z + mu_ref[...] * (0.5 * (zprev + znext) - z)
        r = zs[:, 0:GROUP_W]
        k = zs[:, GROUP_W:2 * GROUP_W]
        v = zs[:, 2 * GROUP_W:3 * GROUP_W]
        low = zs[:, 3 * GROUP_W:3 * GROUP_W + 128]
        gd = zs[:, 3 * GROUP_W + 128:]
        ones = ones_ref[...]
        kk = k * kk_ref[...]
        kk = kk * lax.rsqrt(_dot_sel_rhs(kk * kk, ones) + 1e-12)
        wl = _dot(jnp.tanh(low).astype(BF16), wup_ref[...])
        al = _dot(low.astype(BF16), aup_ref[...])
        ksum = jnp.zeros((nr, GROUP_W), F32)
        for d in range(2):
            w_raw = -_softplus(-(w0_ref[d:d + 1, :] + wl[:, d * GROUP_W:(d + 1) * GROUP_W])) - 0.5
            lw_ref[d, 0, :nr] = -jnp.exp(w_raw)
            a = _sigmoid(a0_ref[d:d + 1, :] + al[:, d * GROUP_W:(d + 1) * GROUP_W])
            kd = k * (1.0 + (a - 1.0) * ka_ref[...])
            kd_ref[d, 0, :nr] = kd
            bd_ref[d, 0, :nr] = kk * a
            ksum = ksum + kd
        r_ref[0, :nr] = r
        v_ref[0, :nr] = v
        ah_ref[0, :nr] = -kk
        g_ref[0, :nr] = _dot(_sigmoid(gd).astype(BF16), gup_ref[...])
        bon_ref[0, :nr] = _dot_sel_rhs(r * ksum * rk_ref[...], ones) * v

    _for_tile_rows(nlat, tc, body)


def _rw_prep(z, mu, k_k, k_a, r_k, w0, a0, wup_p, aup_p, gup_b, ones_b, T, layer):
    B, S, _ = z.shape
    ntiles, nlat = _tok_tiles(S, T)
    W = 4 * GROUP_W
    cb = COL_RW // W
    prev, nxt = _halo_specs(W, cb, S)
    one = _tok_spec(GROUP_W)
    two = pl.BlockSpec((2, 1, TMT, GROUP_W), lambda b, i: (0, b, i, 0))
    s1 = jax.ShapeDtypeStruct((B, S, GROUP_W), F32)
    s2 = jax.ShapeDtypeStruct((2, B, S, GROUP_W), F32)
    vec = _const_spec((1, GROUP_W), layer)
    return pl.pallas_call(
        functools.partial(_rw_prep_kernel, nlat=nlat, tc=S - T),
        grid=(B, ntiles),
        in_specs=[prev, _tok_spec(W, cb), nxt,
                  _const_spec((1, W), layer), vec, vec, vec,
                  _const_spec((2, GROUP_W), layer), _const_spec((2, GROUP_W), layer),
                  _const_spec((128, 2 * GROUP_W), layer), _const_spec((128, 2 * GROUP_W), layer),
                  _const_spec((128, GROUP_W), layer), _const_spec((GROUP_W, GROUP_W))],
        out_specs=[one, one, one, one, one, two, two, two],
        out_shape=[s1, s1, s1, s1, s1, s2, s2, s2],
        compiler_params=_cparams(2),
    )(z, z, z, mu, k_k, k_a, r_k, w0, a0, wup_p, aup_p, gup_b, ones_b)


def _rw_scan_kernel(rf_ref, rb_ref, vf_ref, vb_ref, af_ref, ab_ref, lwf_ref, lwb_ref, kdf_ref, kdb_ref,
                    bdf_ref, bdb_ref, inc_ref, strict_ref, bm_ref, yf_ref, yb_ref, s_ref):
    p = pl.program_id(1)

    @pl.when(p == 0)
    def _():
        s_ref[...] = jnp.zeros_like(s_ref)

    n = N_HEADS * CHUNK
    bm = bm_ref[...]
    eye = (lax.broadcasted_iota(jnp.int32, (CHUNK, n), 1) % CHUNK
           == lax.broadcasted_iota(jnp.int32, (CHUNK, n), 0)).astype(F32)
    refs = ((rf_ref, vf_ref, af_ref, lwf_ref, kdf_ref, bdf_ref, yf_ref),
            (rb_ref, vb_ref, ab_ref, lwb_ref, kdb_ref, bdb_ref, yb_ref))
    orders = [list(_chunk_order(d)) for d in range(2)]
    bm_b = bm.astype(BF16)
    bd_of = lambda m: _block_diag(m, bm_b)

    nb = s_ref.shape[0] // 2
    inst = [(bi, step, d) for step in range(TQ // CHUNK) for bi in range(nb) for d in range(2)]
    idx = range(len(inst))
    rows = [pl.ds(orders[d][step] * CHUNK, CHUNK) for _, step, d in inst]

    ar, r_t, v_s, a_s, bk_t, decay, v_in, g_b, g_k = [], [], [], [], [], [], [], [], []
    for (bi, step, d), sl in zip(inst, rows):
        r_ref, v_ref, a_ref, lw_ref, kd_ref, bd_ref, _ = refs[d]
        lw = lw_ref[0, bi, sl, :]
        kd = kd_ref[0, bi, sl, :]
        bd = bd_ref[0, bi, sl, :]
        v = v_ref[bi, sl, :]
        cs = _cumsum_rows(lw, reverse=(d == 1))
        tot = cs[CHUNK - 1:CHUNK, :] if d == 0 else cs[0:1, :]
        e_neg = jnp.exp(-cs)
        e_hat = jnp.exp(tot - cs)
        a_t = a_ref[bi, sl, :] * jnp.exp(cs - lw)
        rt = r_ref[bi, sl, :] * jnp.exp(cs)
        ar.append(jnp.concatenate([a_t, rt], axis=0).astype(BF16))
        r_t.append(rt)
        v_in.append(v)
        g_b.append(_dot_nt(ar[-1], bd_of(bd * e_neg)))
        g_k.append(_dot_nt(ar[-1], bd_of(kd * e_neg)))
        v_s.append(bd_of(v))
        a_s.append(bd_of(a_t))
        t = jnp.concatenate([bd * e_hat, kd * e_hat,
                             jnp.broadcast_to(jnp.exp(tot), (2 * CHUNK, GROUP_W))], axis=0).T
        bk_t.append(t[:, :2 * CHUNK].astype(BF16))
        decay.append(jnp.concatenate([t[:, 2 * CHUNK:]] * 2, axis=1))
    pw =[g_b[i][:CHUNK] * strict_ref[inst[i][2]] for i in idx]
    ak = [(g_k[i][:CHUNK] * strict_ref[inst[i][2]]).astype(BF16) for i in idx]
    rbk = [jnp.concatenate([g_b[i][CHUNK:] * inc_ref[inst[i][2]], g_k[i][CHUNK:] * inc_ref[inst[i][2]]],
                           axis=1).astype(BF16) for i in idx]
    x = [eye + pw[i] for i in idx]
    pw = [_dot(pw[i].astype(BF16), bd_of(pw[i])) for i in idx]
    for _ in range(4):
        px = [_dot(jnp.concatenate([pw[i], x[i]], axis=0).astype(BF16), bd_of(pw[i])) for i in idx]
        x = [x[i] + px[i][CHUNK:] for i in idx]
        pw = [px[i][:CHUNK] for i in idx]
    x = [(x[i] + _dot(x[i].astype(BF16), bd_of(pw[i]))).astype(BF16) for i in idx]
    akv = [_dot(ak[i], v_s[i]) for i in idx]
    p12 = [_dot(x[i], jnp.concatenate([bd_of(akv[i]), a_s[i]], axis=1)) for i in idx]
    p1_s = [bd_of(p12[i][:, :GROUP_W]) for i in idx]
    p2_s = [bd_of(p12[i][:, GROUP_W:]) for i in idx]
    q = [r_t[i] + _dot(rbk[i][:, :n], p2_s[i]) for i in idx]
    y_c = [_dot(rbk[i], jnp.concatenate([p1_s[i], v_s[i]], axis=0)) for i in idx]
    zero = jnp.zeros((CHUNK, GROUP_W), F32)
    gc = [_dot(bk_t[i], jnp.concatenate(
        [jnp.concatenate([p12[i][:, GROUP_W:], p12[i][:, :GROUP_W]], axis=1),
         jnp.concatenate([zero, v_in[i]], axis=1)], axis=0).astype(BF16)) for i in idx]
    gq = [jnp.concatenate([gc[i][:, :GROUP_W] * bm, q[i]], axis=0).astype(BF16) for i in idx]
    c_s = [gc[i][:, GROUP_W:] * bm for i in idx]

    s = [s_ref[k] for k in range(2 * nb)]
    for i, ((bi, step, d), sl) in enumerate(zip(inst, rows)):
        k = 2 * bi + d
        m = _dot(gq[i], s[k].astype(BF16))
        refs[d][6][bi, sl, :] = m[n:] + y_c[i]
        s[k] = decay[i] * s[k] + m[:n] + c_s[i]
    for k in range(2 * nb):
        s_ref[k] = s[k]


def _rw_scan(r, v, ah, lw, kd, bd, masks, bm, T):
    B, S, _ = r.shape
    nlat = T // TQ
    inc, strict = masks
    n = N_HEADS * CHUNK
    nb = RW_SCAN_BATCH if B % RW_SCAN_BATCH == 0 else 1
    fwd = pl.BlockSpec((nb, TQ, GROUP_W), lambda b, p: (b, _fwd_tile(p, nlat), 0))
    bwd = pl.BlockSpec((nb, TQ, GROUP_W), lambda b, p: (b, _bwd_tile(p, nlat), 0))
    fwd2 = pl.BlockSpec((1, nb, TQ, GROUP_W), lambda b, p: (0, b, _fwd_tile(p, nlat), 0))
    bwd2 = pl.BlockSpec((1, nb, TQ, GROUP_W), lambda b, p: (1, b, _bwd_tile(p, nlat), 0))
    out = jax.ShapeDtypeStruct((B, S, GROUP_W), F32)
    return pl.pallas_call(
        _rw_scan_kernel,
        grid=(B // nb, nlat + 1),
        in_specs=[fwd, bwd, fwd, bwd, fwd, bwd, fwd2, bwd2, fwd2, bwd2, fwd2, bwd2,
                  _const_spec((2, CHUNK, n)), _const_spec((2, CHUNK, n)), _const_spec((n, n))],
        out_specs=[fwd, bwd],
        out_shape=[out, out],
        scratch_shapes=[pltpu.VMEM((2 * nb, n, GROUP_W), F32)],
        compiler_params=_cparams(2),
    )(r, r, v, v, ah, ah, lw, lw, kd, kd, bd, bd, inc, strict, bm)


def _gla_scan_kernel(qf_ref, qb_ref, kf_ref, kb_ref, gf_ref, gb_ref, vf_ref, vb_ref, gup_ref, gbias_ref,
                     inc_ref, bm_ref, bmv_ref, of_ref, ob_ref, s_ref):
    p = pl.program_id(1)

    @pl.when(p == 0)
    def _():
        s_ref[...] = jnp.zeros_like(s_ref)

    wk = N_HEADS * GLA_DK
    n = N_HEADS * CHUNK
    bm = bm_ref[...]
    bmv = bmv_ref[...].astype(BF16)
    bmk = (lax.broadcasted_iota(jnp.int32, (n, wk), 0) // CHUNK
           == lax.broadcasted_iota(jnp.int32, (n, wk), 1) // GLA_DK).astype(F32).astype(BF16)
    refs = ((qf_ref, kf_ref, gf_ref, vf_ref, of_ref), (qb_ref, kb_ref, gb_ref, vb_ref, ob_ref))
    orders = [list(_chunk_order(d)) for d in range(2)]
    nb = s_ref.shape[0] // 2
    inst = [(bi, step, d) for step in range(TQ // CHUNK) for bi in range(nb) for d in range(2)]
    idx = range(len(inst))
    rows = [pl.ds(orders[d][step] * CHUNK, CHUNK) for _, step, d in inst]

    qe, ke_s, ks4, v_s, decay = [], [], [], [], []
    for (bi, step, d), sl in zip(inst, rows):
        q_ref, k_ref, g_ref, v_ref, _ = refs[d]
        k = k_ref[bi, sl, :]
        la = -_softplus(-(_dot(g_ref[bi, sl, :].astype(BF16), gup_ref[d]) + gbias_ref[d])) * (1.0 / GLA_TAU)
        b = _cumsum_rows(la, reverse=(d == 1))
        tot = b[CHUNK - 1:CHUNK, :] if d == 0 else b[0:1, :]
        qe.append((q_ref[bi, sl, :] * (GLA_DK ** -0.5) * jnp.exp(b)).astype(BF16))
        ke_s.append(_block_diag(k * jnp.exp(-b), bmk))
        ks4.append(_tile_rows(k * jnp.exp(tot - b)).T.astype(BF16))
        v_s.append(_block_diag(v_ref[bi, sl, :], bmv))
        decay.append(_tile_rows(jnp.broadcast_to(jnp.exp(tot), (CHUNK, wk))).T)
    a_cat = [(_dot_nt(qe[i], ke_s[i]) * inc_ref[inst[i][2]]).astype(BF16) for i in idx]
    o_in = [_dot(a_cat[i], v_s[i]) for i in idx]
    kv = [_dot(ks4[i], v_s[i]) * bm for i in idx]

    s = [s_ref[k] for k in range(2 * nb)]
    for i, ((bi, step, d), sl) in enumerate(zip(inst, rows)):
        k = 2 * bi + d
        refs[d][4][bi, sl, :] = o_in[i] + _dot(qe[i], s[k].astype(BF16))
        s[k] = decay[i] * s[k] + kv[i]
    for k in range(2 * nb):
        s_ref[k] = s[k]


def _gla_scan(z, gup_p, gb, masks, bm, bmv, T, layer):
    B, S, _ = z.shape
    nlat = T // TQ
    inc, _ = masks
    n = N_HEADS * CHUNK
    wk = N_HEADS * GLA_DK
    nb = GLA_SCAN_BATCH if B % GLA_SCAN_BATCH == 0 else 1
    fwd = lambda w, col: pl.BlockSpec((nb, TQ, w), lambda b, p: (b, _fwd_tile(p, nlat), col // w))
    bwd = lambda w, col: pl.BlockSpec((nb, TQ, w), lambda b, p: (b, _bwd_tile(p, nlat), col // w))
    out = jax.ShapeDtypeStruct((B, S, GROUP_W), F32)
    return pl.pallas_call(
        _gla_scan_kernel,
        grid=(B // nb, nlat + 1),
        in_specs=[fwd(wk, COL_GLA_Q), bwd(wk, COL_GLA_Q), fwd(wk, COL_GLA_K), bwd(wk, COL_GLA_K),
                  fwd(wk, COL_GLA_G), bwd(wk, COL_GLA_G), fwd(GROUP_W, COL_GLA_V), bwd(GROUP_W, COL_GLA_V),
                  _const_spec((2, wk, wk), layer), _const_spec((2, 1, wk), layer),
                  _const_spec((2, CHUNK, n)), _const_spec((wk, GROUP_W)), _const_spec((n, n))],
        out_specs=[fwd(GROUP_W, 0), bwd(GROUP_W, 0)],
        out_shape=[out, out],
        scratch_shapes=[pltpu.VMEM((2 * nb, wk, GROUP_W), F32)],
        compiler_params=_cparams(2),
    )(z, z, z, z, z, z, z, z, gup_p, gb, inc, bm, bmv)


def _pack_w_in(w_in):
    L, D, _ = w_in.shape
    na, mla, rw, gla = jnp.split(w_in.astype(BF16), [768, 1120, 2144], axis=-1)
    cq, ckv, kr = jnp.split(mla, [MLA_Q_RANK, MLA_Q_RANK + MLA_KV_RANK], axis=-1)
    gq, gk, gv, gg, go = jnp.split(gla, [128, 256, 512, 528], axis=-1)
    pad = jnp.zeros((L, D, COL_GLA_V - COL_GLA_G - gg.shape[-1]), BF16)
    packed = jnp.concatenate([na, ckv, cq, kr, kr[..., _rope_swap_perm()], gq, gk, gg, pad, gv, go, rw], axis=-1)
    assert packed.shape[-1] == Z_COLS
    return packed


def kernel(x, c, ctx, c_ctx, w_mod, b_mod, g_mix_pre, g_mix_post, g_ffn_pre, g_ffn_post, w_in, w_out, na_rpb, mla_q_norm, mla_w_uq, mla_kv_norm, mla_w_ukv, rw_mu, rw_w0, rw_w_up, rw_a0, rw_a_up, rw_g_up, rw_k_k, rw_k_a, rw_r_k, rw_ln_w, rw_ln_b, gla_gate_up, gla_gate_b, gla_norm, ffn_w_up, ffn_conv_w, ffn_conv_b, ffn_w_down):
    B, T, D = x.shape
    Tc = ctx.shape[1]
    L = w_in.shape[0]
    assert D == D_MODEL and Tc == TQ and T % TMT == 0 and B + 1 <= 8

    w_in_p = _pack_w_in(w_in)
    w_out_b = w_out.astype(BF16)
    wq1, wq2, wk, wv = _mla_weights(mla_w_uq, mla_w_ukv)
    place = _rope_place()
    cos, sin = _rope_tables(T, Tc)
    zero_lo = jnp.zeros((L, 64, 2 * GROUP_W), F32)
    rw_wup_p = jnp.concatenate([jnp.concatenate([rw_w_up[:, 0], rw_w_up[:, 1]], axis=-1), zero_lo], axis=1).astype(BF16)
    rw_aup_p = jnp.concatenate([zero_lo, jnp.concatenate([rw_a_up[:, 0], rw_a_up[:, 1]], axis=-1)], axis=1).astype(BF16)
    rw_gup_b = rw_g_up.astype(BF16)
    wk_gla = N_HEADS * GLA_DK
    gla_gup_p = jnp.concatenate([gla_gate_up, jnp.zeros((L, 2, wk_gla - gla_gate_up.shape[2], wk_gla), F32)], axis=2).astype(BF16)
    ffn_up_b = ffn_w_up.astype(BF16)
    ffn_dn_b = ffn_w_down.astype(BF16)
    ones_b = jnp.asarray(_block_ones(GROUP_W, HEAD_DIM), BF16)
    rw_bm = jnp.asarray(_block_ones(N_HEADS * CHUNK, CHUNK))
    gla_bm = jnp.asarray((np.arange(wk_gla)[:, None] // GLA_DK == np.arange(GROUP_W)[None, :] // HEAD_DIM).astype(np.float32))
    masks = _scan_masks()

    cvecs = jnp.zeros((8, D), F32).at[:B].set(c).at[B].set(c_ctx)
    mods = _modulation(cvecs, w_mod, b_mod).reshape(L, 8, 6, D)
    mods = jnp.pad(mods, ((0, 0), (0, 0), (0, 2), (0, 0)))
    modtabs = jnp.stack([jnp.broadcast_to(mods[:, B:B + 1], (L, B, 8, D)), mods[:, :B]], axis=2)

    xs = (x, ctx)
    rows = lambda a: a.reshape(L, 1, -1)
    na_bias = _natten_bias(na_rpb, T // GRID_W)
    gla_gb = gla_gate_b[:, :, None, :]
    for i in range(L):
        z = _in_proj(xs, modtabs, rows(g_mix_pre), w_in_p, T, Tc, i)
        y_na = _natten(z, na_bias, T, i)
        qt, k, vt = _mla_up(z, cos, sin, rows(mla_q_norm), rows(mla_kv_norm), wq1, wq2, wk, wv, place, T, i)
        y_mla = _mla_attn(qt, k, vt, T)
        r, vv, ah, g, bon, lw, kd, bd = _rw_prep(z, rows(rw_mu), rows(rw_k_k), rows(rw_k_a), rows(rw_r_k),
                                                 rw_w0, rw_a0, rw_wup_p, rw_aup_p, rw_gup_b, ones_b, T, i)
        yf, yb = _rw_scan(r, vv, ah, lw, kd, bd, masks, rw_bm, T)
        of, ob = _gla_scan(z, gla_gup_p, gla_gb, masks, gla_bm, rw_bm, T, i)
        xs = _out_proj(xs, z, y_na, y_mla, yf, yb, bon, g, of, ob, modtabs, rows(g_mix_post), w_out_b,
                       rows(rw_ln_w), rows(rw_ln_b), rows(gla_norm), ones_b, T, i)
        xs = _ffn(xs, modtabs, rows(g_ffn_pre), rows(g_ffn_post), ffn_up_b, ffn_conv_w,
                  rows(ffn_conv_b), ffn_dn_b, T, i, latent_only=(i == L - 1))
    return xs
```

```python
import functools

import numpy as np
import jax
import jax.numpy as jnp
from jax import lax
from jax.experimental import pallas as pl
from jax.experimental.pallas import tpu as pltpu

F32 = jnp.float32
BF16 = jnp.bfloat16

D_MODEL = 1024
GRID_W = 64
EPS = 1e-6
LOG2E = 1.4426950408889634
N_HEADS = 4
HEAD_DIM = 64
GROUP_W = 256
NA_ROWS = 8
NA_COLS = 16
NA_UNION = 12
MLA_Q_RANK = 192
MLA_KV_RANK = 128
MLA_NOPE = 64
MLA_ROPE = 32
MLA_HEAD_PAD = 128
ROPE_THETA = 10000.0
RW_GN_EPS = 64e-5
GLA_DK = 32
GLA_TAU = 16.0
D_FF = 2816
CHUNK = 64
TQ = 256
MLA_TQ = 512
RW_SCAN_BATCH = 2
GLA_SCAN_BATCH = 4
TMT = 512
TM_WIDE = 1024
HALO = 8
FF_CHUNK = 256
FF_GROUP = 4
Z_COLS = 3072

COL_NA = 0
COL_MLA = 768
COL_GLA_Q = 1152
COL_GLA_K = 1280
COL_GLA_G = 1408
COL_GLA_V = 1536
COL_GLA_O = 1792
COL_RW = 2048

VMEM_LIMIT_V7X = 56 * 1024 * 1024


def _cparams(n_axes):
    return pltpu.CompilerParams(dimension_semantics=("arbitrary",) * n_axes,
                                vmem_limit_bytes=VMEM_LIMIT_V7X)


def _const_spec(shape, layer=None):
    nd = len(shape)
    if layer is None:
        return pl.BlockSpec(shape, lambda *_: (0,) * nd, pipeline_mode=pl.Buffered(1))
    return pl.BlockSpec((None,) + tuple(shape), lambda *_: (layer,) + (0,) * nd, pipeline_mode=pl.Buffered(1))


def _dot(a, b):
    return jnp.dot(a, b, preferred_element_type=F32)


def _split3(x):
    hi = x.astype(BF16)
    r1 = x - hi.astype(F32)
    mid = r1.astype(BF16)
    lo = (r1 - mid.astype(F32)).astype(BF16)
    return hi, mid, lo


def _dot_sel_rhs(x, m):
    hi, mid, lo = _split3(x)
    return _dot(hi, m) + (_dot(mid, m) + _dot(lo, m))


def _sigmoid(x):
    return 1.0 / (1.0 + jnp.exp(-x))


def _silu_gain(x):
    return 0.5 + 0.5 * jnp.tanh(0.5 * x)


def _softplus(x):
    return jnp.maximum(x, 0.0) + jnp.log1p(jnp.exp(-jnp.abs(x)))


def _rms(x):
    return x * lax.rsqrt(jnp.mean(x * x, axis=-1, keepdims=True) + EPS)


def _norm_mod(x, g, shift, scale):
    return (_rms(x) * g) * (1.0 + scale) + shift


def _lane_head(width, per_head):
    return lax.broadcasted_iota(jnp.int32, (1, width), 1) // per_head


def _block_ones(n, blk):
    i = np.arange(n) // blk
    return (i[:, None] == i[None, :]).astype(np.float32)


def _mod_kernel(c_ref, w_ref, b_ref, o_ref):
    cv = c_ref[...]
    s = cv * _sigmoid(cv)
    o_ref[0] = _dot(s.astype(BF16), w_ref[0].astype(BF16)) + b_ref[0]


def _modulation(cvecs, w_mod, b_mod):
    L, D, N = w_mod.shape
    tn = 1536
    return pl.pallas_call(
        _mod_kernel,
        grid=(L, N // tn),
        in_specs=[pl.BlockSpec((8, D), lambda l, n: (0, 0)),
                  pl.BlockSpec((1, D, tn), lambda l, n: (l, 0, n)),
                  pl.BlockSpec((1, 1, tn), lambda l, n: (l, 0, n))],
        out_specs=pl.BlockSpec((1, 8, tn), lambda l, n: (l, 0, n)),
        out_shape=jax.ShapeDtypeStruct((L, 8, N), F32),
        compiler_params=_cparams(2),
    )(cvecs, w_mod, b_mod.reshape(L, 1, N))


def _tok_tiles(S, T, tm=TMT):
    assert T % tm == 0 and 0 < S - T <= tm
    return T // tm + 1, T // tm


def _tok_spec(width, col_block=0, tm=TMT):
    return pl.BlockSpec((1, tm, width), lambda b, i: (b, i, col_block))


def _mod_spec(nlat, layer):
    return pl.BlockSpec((None, 1, 1, 8, D_MODEL), lambda b, i: (layer, b, jnp.where(i < nlat, 1, 0), 0, 0))


def _halo_specs(width, col_block, S, tm=TMT):
    per = tm // HALO
    last = S // HALO - 1
    prev = pl.BlockSpec((1, HALO, width), lambda b, i: (b, jnp.maximum(i * per - 1, 0), col_block))
    nxt = pl.BlockSpec((1, HALO, width), lambda b, i: (b, jnp.minimum((i + 1) * per, last), col_block))
    return prev, nxt


def _neighbour_ok(i, nlat):
    prev_ok = jnp.logical_and(i != 0, i != nlat)
    next_ok = i < nlat - 1
    return prev_ok, next_ok


def _for_tile_rows(nlat, tc, body, tm=TMT):
    i = pl.program_id(1)

    @pl.when(i < nlat)
    def _():
        body(tm)

    @pl.when(i >= nlat)
    def _():
        body(tc)


def _stream_source(src, T, Tc, tm):
    nlat = T // tm
    assert Tc < tm
    lat = pl.BlockSpec((1, tm, D_MODEL), lambda b, i: (b, jnp.minimum(i, nlat - 1), 0))
    if isinstance(src, tuple):
        return src, [lat, pl.BlockSpec((1, Tc, D_MODEL), lambda b, i: (b, 0, 0))]
    assert T % Tc == 0
    return (src, src), [lat, pl.BlockSpec((1, Tc, D_MODEL), lambda b, i: (b, T // Tc, 0))]


def _in_proj_kernel(xl_ref, xc_ref, mod_ref, g_ref, w_ref, z_ref, *, nlat, tc, tm):
    def body(nr):
        m = mod_ref[0, 0]
        x = xl_ref[0] if nr == tm else xc_ref[0]
        h = _norm_mod(x, g_ref[...], m[0:1], m[1:2])
        z_ref[0, :nr] = _dot(h.astype(BF16), w_ref[...])

    _for_tile_rows(nlat, tc, body, tm)


def _in_proj(src, modtab, g_pre, w_in_p, T, Tc, layer):
    tm = TM_WIDE
    arrays, specs = _stream_source(src, T, Tc, tm)
    B, S = arrays[0].shape[0], T + Tc
    ntiles, nlat = _tok_tiles(S, T, tm)
    return pl.pallas_call(
        functools.partial(_in_proj_kernel, nlat=nlat, tc=Tc, tm=tm),
        grid=(B, ntiles),
        in_specs=specs + [_mod_spec(nlat, layer), _const_spec((1, D_MODEL), layer),
                          _const_spec((D_MODEL, Z_COLS), layer)],
        out_specs=_tok_spec(Z_COLS, 0, tm),
        out_shape=jax.ShapeDtypeStruct((B, S, Z_COLS), F32),
        compiler_params=_cparams(2),
    )(*arrays, modtab, g_pre, w_in_p)


def _out_proj_kernel(xl_ref, xc_ref, na_ref, mla_ref, yf_ref, yb_ref, bon_ref, g_ref, of_ref, ob_ref, og_ref,
                     mod_ref, gpost_ref, w_ref, lnw_ref, lnb_ref, gn_ref, ones_ref, o_ref, *, nlat, tc, tm):
    def body(nr):
        m = mod_ref[0, 0]
        x = xl_ref[0] if nr == tm else xc_ref[0]
        ones = ones_ref[...]
        inv_n = 1.0 / HEAD_DIM
        y = yf_ref[0, :nr] + yb_ref[0, :nr]
        yc = y - _dot_sel_rhs(y, ones) * inv_n
        var = _dot_sel_rhs(yc * yc, ones) * inv_n
        y_rw = (yc * lax.rsqrt(var + RW_GN_EPS) * lnw_ref[...] + lnb_ref[...] + bon_ref[0, :nr]) * g_ref[0, :nr]
        o = of_ref[0, :nr] + ob_ref[0, :nr]
        ms = _dot_sel_rhs(o * o, ones) * inv_n
        og = og_ref[0, :nr]
        y_gla = (o * lax.rsqrt(ms + EPS) * gn_ref[...]) * (og * _silu_gain(og))
        y = jnp.concatenate([na_ref[0, :nr], mla_ref[0, :nr], y_rw.astype(BF16), y_gla.astype(BF16)], axis=-1)
        y = _dot(y, w_ref[...])
        o_ref[0, :nr] = x + m[2:3] * (_rms(y) * gpost_ref[...])

    _for_tile_rows(nlat, tc, body, tm)


def _out_proj(src, z, y_na, y_mla, yf, yb, bon, g, of, ob, modtab, g_post, w_out_b, ln_w, ln_b, gla_norm,
              ones_b, T, layer):
    B, S, _ = z.shape
    D = D_MODEL
    tm = TM_WIDE
    arrays, specs = _stream_source(src, T, S - T, tm)
    ntiles, nlat = _tok_tiles(S, T, tm)
    grp = _tok_spec(GROUP_W, 0, tm)
    vec = _const_spec((1, GROUP_W), layer)
    return pl.pallas_call(
        functools.partial(_out_proj_kernel, nlat=nlat, tc=S - T, tm=tm),
        grid=(B, ntiles),
        in_specs=specs + [grp, grp, grp, grp, grp, grp, grp, grp,
                          _tok_spec(GROUP_W, COL_GLA_O // GROUP_W, tm),
                          _mod_spec(nlat, layer), _const_spec((1, D), layer), _const_spec((D, D), layer),
                          vec, vec, vec, _const_spec((GROUP_W, GROUP_W))],
        out_specs=_tok_spec(D, 0, tm),
        out_shape=jax.ShapeDtypeStruct((B, S, D), F32),
        compiler_params=_cparams(2),
    )(*arrays, y_na, y_mla, yf, yb, bon, g, of, ob, z, modtab, g_post, w_out_b, ln_w, ln_b, gla_norm, ones_b)


def _ffn_kernel(xp_ref, x_ref, xn_ref, mod_ref, gpre_ref, gpost_ref, wup_ref, cw_ref, cb_ref,
                wdn_ref, o_ref, *, nlat, tc):
    i = pl.program_id(1)
    prev_ok, next_ok = _neighbour_ok(i, nlat)
    nchunks = D_FF // FF_CHUNK

    def body(nr):
        m = mod_ref[0, 0]
        x = x_ref[0, :nr]
        xe = jnp.concatenate([xp_ref[0], x, xn_ref[0]], axis=0)
        n = nr + 2 * HALO
        h = _norm_mod(xe, gpre_ref[...], m[3:4], m[4:5])
        row = lax.broadcasted_iota(jnp.int32, (n, 1), 0)
        valid = jnp.logical_or(jnp.logical_and(row >= HALO, row < HALO + nr),
                               jnp.logical_or(jnp.logical_and(row < HALO, prev_ok),
                                              jnp.logical_and(row >= HALO + nr, next_ok)))
        hb = jnp.where(valid, h, 0.0).astype(BF16)

        def up(c):
            return [_dot(hb, wup_ref[:, base + c * FF_CHUNK:base + (c + 1) * FF_CHUNK]) for base in (0, D_FF)]

        def conv(z, lo):
            cw = cw_ref[:, lo:lo + FF_CHUNK]
            return (cb_ref[:, lo:lo + FF_CHUNK]
                    + pltpu.roll(z, 1, 0)[HALO:HALO + nr] * cw[0:1]
                    + z[HALO:HALO + nr] * cw[1:2]
                    + pltpu.roll(z, n - 1, 0)[HALO:HALO + nr] * cw[2:3])

        acc = None
        group = []
        z_next = up(0)
        for c in range(nchunks):
            z_val, z_gate = z_next
            if c + 1 < nchunks:
                z_next = up(c + 1)
            val = conv(z_val, c * FF_CHUNK)
            gate = conv(z_gate, D_FF + c * FF_CHUNK)
            group.append(((gate * _silu_gain(gate)) * val).astype(BF16))
            if len(group) == FF_GROUP or c + 1 == nchunks:
                lo = (c + 1 - len(group)) * FF_CHUNK
                part = _dot(jnp.concatenate(group, axis=1), wdn_ref[lo:(c + 1) * FF_CHUNK, :])
                acc = part if acc is None else acc + part
                group = []
        o_ref[0, :nr] = x + m[5:6] * (_rms(acc) * gpost_ref[...])

    _for_tile_rows(nlat, tc, body)


def _ffn(xs, modtab, g_pre, g_post, w_up_b, conv_w, conv_b, w_dn_b, T, layer, latent_only):
    B, S, D = xs.shape
    ntiles, nlat = _tok_tiles(S, T)
    prev, nxt = _halo_specs(D, 0, S)
    return pl.pallas_call(
        functools.partial(_ffn_kernel, nlat=nlat, tc=S - T),
        grid=(B, nlat if latent_only else ntiles),
        in_specs=[prev, _tok_spec(D), nxt, _mod_spec(nlat, layer), _const_spec((1, D), layer),
                  _const_spec((1, D), layer), _const_spec((D, 2 * D_FF), layer),
                  _const_spec((3, 2 * D_FF), layer), _const_spec((1, 2 * D_FF), layer),
                  _const_spec((D_FF, D), layer)],
        out_specs=_tok_spec(D),
        out_shape=jax.ShapeDtypeStruct((B, T if latent_only else S, D), F32),
        compiler_params=_cparams(2),
    )(xs, xs, xs, modtab, g_pre, g_post, w_up_b, conv_w, conv_b, w_dn_b)


def _natten_bias(rpb, rows):
    rt = TQ // GRID_W
    j = np.arange(GRID_W)
    col_start = np.clip(j - NA_COLS // 2, 0, GRID_W - NA_COLS)
    col_in = (j[None, :] >= col_start[:, None]) & (j[None, :] < col_start[:, None] + NA_COLS)
    edge = GRID_W - NA_COLS
    ext = jnp.concatenate([jnp.repeat(rpb[..., :1], edge, axis=-1), rpb,
                           jnp.repeat(rpb[..., -1:], edge, axis=-1)], axis=-1).astype(F32) * LOG2E
    bq = jnp.stack([ext[..., GRID_W - 1 - q:2 * GRID_W - 1 - q] for q in range(GRID_W)], axis=2)
    L = rpb.shape[0]
    bq = jnp.where(col_in[:, None, :], bq, -jnp.inf).reshape(L, N_HEADS, GRID_W, -1)
    cases = []
    for r0 in (0, rt, rows - rt):
        us = min(max(r0 - NA_ROWS // 2, 0), rows - NA_UNION)
        per_row = []
        for r in range(r0, r0 + rt):
            rs = min(max(r - NA_ROWS // 2, 0), rows - NA_ROWS)
            first = rs - r + (NA_ROWS - 1)
            pre, post = rs - us, us + NA_UNION - (rs + NA_ROWS)
            per_row.append(jnp.pad(bq[..., first * GRID_W:(first + NA_ROWS) * GRID_W],
                                   ((0, 0), (0, 0), (0, 0), (pre * GRID_W, post * GRID_W)),
                                   constant_values=-jnp.inf))
        cases.append(jnp.concatenate(per_row, axis=2))
    return jnp.stack(cases, axis=1)


def _natten_kernel(q_ref, k_ref, v_ref, qc_ref, kc_ref, vc_ref, bias_a_ref, bias_b_ref, o_ref, *, nlat, rows):
    j = pl.program_id(1)
    lane_h = _lane_head(GROUP_W, HEAD_DIM)
    kct = kc_ref[0].T.astype(BF16)
    vc = vc_ref[0].astype(BF16)
    scale = HEAD_DIM ** -0.5 * LOG2E
    nwin = NA_UNION * GRID_W

    @pl.when(j < nlat // 2)
    def _():
        wins = []
        for half in range(2):
            us = jnp.clip((2 * j + half) * (TQ // GRID_W) - NA_ROWS // 2, 0, rows - NA_UNION)
            start = pl.multiple_of(us * GRID_W, GRID_W)
            wins.append((k_ref[0, pl.ds(start, nwin), :].T.astype(BF16),
                         v_ref[0, pl.ds(start, nwin), :].astype(BF16)))
        for half, bias_ref in enumerate((bias_a_ref, bias_b_ref)):
            kwt, vw = wins[half]
            q = q_ref[0, half * TQ:(half + 1) * TQ, :] * scale
            acc = jnp.zeros((TQ, GROUP_W), F32)

            def logits(h):
                qh = jnp.where(lane_h == h, q, 0.0).astype(BF16)
                return _dot(qh, kwt), _dot(qh, kct)

            s_next = logits(0)
            for h in range(N_HEADS):
                hm = lane_h == h
                s_w, s_c = s_next
                if h + 1 < N_HEADS:
                    s_next = logits(h + 1)
                s_w = s_w + bias_ref[0, h]
                mx = jnp.maximum(jnp.max(s_w, axis=-1, keepdims=True), jnp.max(s_c, axis=-1, keepdims=True))
                p_w = jnp.exp2(s_w - mx)
                p_c = jnp.exp2(s_c - mx)
                den = jnp.sum(p_w, axis=-1, keepdims=True) + jnp.sum(p_c, axis=-1, keepdims=True)
                o = _dot(p_w.astype(BF16), vw) + _dot(p_c.astype(BF16), vc)
                acc = acc + jnp.where(hm, o * (1.0 / den), 0.0)
            o_ref[0, half * TQ:(half + 1) * TQ, :] = acc.astype(o_ref.dtype)

    @pl.when(j >= nlat // 2)
    def _():
        q = qc_ref[0] * scale
        acc = jnp.zeros((TQ, GROUP_W), F32)
        for h in range(N_HEADS):
            hm = lane_h == h
            s = _dot(jnp.where(hm, q, 0.0).astype(BF16), kct)
            p = jnp.exp2(s - jnp.max(s, axis=-1, keepdims=True))
            den = jnp.sum(p, axis=-1, keepdims=True)
            acc = acc + jnp.where(hm, _dot(p.astype(BF16), vc) * (1.0 / den), 0.0)
        o_ref[0, :TQ, :] = acc.astype(o_ref.dtype)


def _natten(z, bias, T, layer):
    B, S, _ = z.shape
    nlat = T // TQ
    rows = T // GRID_W
    assert rows >= 16 and nlat % 2 == 0 and S - T == TQ
    cb = COL_NA // GROUP_W
    npair = nlat // 2
    lat = lambda c: pl.BlockSpec((1, T, GROUP_W), lambda b, j: (b, 0, cb + c))
    ctx = lambda c: pl.BlockSpec((1, TQ, GROUP_W), lambda b, j: (b, nlat, cb + c))
    case = lambda t: jnp.where(t == 0, 0, jnp.where(t >= nlat - 1, 2, 1))
    tile = lambda j, half: 2 * jnp.minimum(j, npair - 1) + half
    bias_spec = lambda half: pl.BlockSpec((None, 1, N_HEADS, TQ, NA_UNION * GRID_W),
                                          lambda b, j: (layer, case(tile(j, half)), 0, 0, 0))
    return pl.pallas_call(
        functools.partial(_natten_kernel, nlat=nlat, rows=rows),
        grid=(B, npair + 1),
        in_specs=[pl.BlockSpec((1, 2 * TQ, GROUP_W), lambda b, j: (b, jnp.minimum(j, npair - 1), cb)),
                  lat(1), lat(2), ctx(0), ctx(1), ctx(2), bias_spec(0), bias_spec(1)],
        out_specs=pl.BlockSpec((1, 2 * TQ, GROUP_W), lambda b, j: (b, j, 0)),
        out_shape=jax.ShapeDtypeStruct((B, S, GROUP_W), BF16),
        compiler_params=_cparams(2),
    )(z, z, z, z, z, z, bias, bias)


def _rope_tables(T, Tc):
    t = np.arange(T)
    row = (t // GRID_W).astype(np.float32)
    col = (t % GRID_W).astype(np.float32)
    d = MLA_ROPE // 2
    inv = (np.float32(ROPE_THETA) ** (-np.arange(0, d, 2, dtype=np.float32) / np.float32(d))).astype(np.float32)
    cs, sn = [], []
    for pos in (row, col):
        ang = (pos[:, None] * inv[None, :]).astype(np.float32)
        cs += [np.cos(ang), np.cos(ang)]
        sn += [-np.sin(ang), np.sin(ang)]
    pad = MLA_HEAD_PAD - MLA_NOPE - MLA_ROPE
    f32 = np.float32
    cos = np.concatenate([np.ones((T, MLA_NOPE), f32)] + cs + [np.ones((T, pad), f32)], axis=1)
    sin = np.concatenate([np.zeros((T, MLA_NOPE), f32)] + sn + [np.zeros((T, pad), f32)], axis=1)
    cos = np.concatenate([cos, np.ones((Tc, MLA_HEAD_PAD), f32)], axis=0)
    sin = np.concatenate([sin, np.zeros((Tc, MLA_HEAD_PAD), f32)], axis=0)
    return jnp.asarray(cos, F32), jnp.asarray(sin, F32)


def _rope_swap_perm():
    q = MLA_ROPE // 4
    return np.concatenate([np.arange(q, 2 * q), np.arange(0, q), np.arange(3 * q, 4 * q), np.arange(2 * q, 3 * q)])


def _mla_weights(w_uq, w_ukv):
    L = w_uq.shape[0]
    wq = w_uq.reshape(L, MLA_Q_RANK, N_HEADS, MLA_NOPE + MLA_ROPE)
    pad = MLA_HEAD_PAD - MLA_NOPE - MLA_ROPE
    zq = jnp.zeros((L, MLA_Q_RANK, N_HEADS, pad), F32)
    wq1 = jnp.concatenate([wq, zq], axis=-1).reshape(L, MLA_Q_RANK, N_HEADS * MLA_HEAD_PAD)
    rope_sw = wq[..., MLA_NOPE:][..., _rope_swap_perm()]
    wq2 = jnp.concatenate([jnp.zeros((L, MLA_Q_RANK, N_HEADS, MLA_NOPE), F32), rope_sw, zq], axis=-1)
    wq2 = wq2.reshape(L, MLA_Q_RANK, N_HEADS * MLA_HEAD_PAD)
    wkv = w_ukv.reshape(L, MLA_KV_RANK, N_HEADS, 2 * MLA_NOPE)
    wk = jnp.concatenate([wkv[..., :MLA_NOPE], jnp.zeros((L, MLA_KV_RANK, N_HEADS, MLA_HEAD_PAD - MLA_NOPE), F32)], axis=-1)
    wk = wk.reshape(L, MLA_KV_RANK, N_HEADS * MLA_HEAD_PAD)
    wv_t = wkv[..., MLA_NOPE:].reshape(L, MLA_KV_RANK, N_HEADS * MLA_NOPE).transpose(0, 2, 1)
    return wq1.astype(BF16), wq2.astype(BF16), wk.astype(BF16), wv_t.astype(BF16)


def _rope_place():
    e = np.zeros((MLA_ROPE, N_HEADS * MLA_HEAD_PAD), np.float32)
    for h in range(N_HEADS):
        e[np.arange(MLA_ROPE), h * MLA_HEAD_PAD + MLA_NOPE + np.arange(MLA_ROPE)] = 1.0
    return jnp.asarray(e, BF16)


def _mla_up_kernel(z_ref, cos_ref, sin_ref, qn_ref, kvn_ref, wq1_ref, wq2_ref, wk_ref, wv_ref, e_ref,
                   qt_ref, k_ref, vt_ref, *, nlat, tc):
    def body(nr):
        z = z_ref[0, :nr]
        ckv = z[:, :MLA_KV_RANK]
        cq = z[:, MLA_KV_RANK:MLA_KV_RANK + MLA_Q_RANK]
        kr = z[:, MLA_KV_RANK + MLA_Q_RANK:MLA_KV_RANK + MLA_Q_RANK + MLA_ROPE]
        krs = z[:, MLA_KV_RANK + MLA_Q_RANK + MLA_ROPE:]
        cos = jnp.concatenate([cos_ref[:nr]] * N_HEADS, axis=-1)
        sin = jnp.concatenate([sin_ref[:nr]] * N_HEADS, axis=-1)
        nq = (_rms(cq) * qn_ref[...]).astype(BF16)
        nkv_f = _rms(ckv) * kvn_ref[...]
        nkv = nkv_f.astype(BF16)
        q = _dot(nq, wq1_ref[...]) * cos + _dot(nq, wq2_ref[...]) * sin
        scale = (MLA_NOPE + MLA_ROPE) ** -0.5 * LOG2E
        qt_ref[0, :, :nr] = (q * scale).T.astype(BF16)
        k = _dot(nkv, wk_ref[...]) + _dot_sel_rhs(kr, e_ref[...]) * cos + _dot_sel_rhs(krs, e_ref[...]) * sin
        k_ref[0, :nr] = k.astype(BF16)
        vt_ref[0, :, :nr] = _dot(wv_ref[...], nkv_f.T.astype(BF16)).astype(BF16)

    _for_tile_rows(nlat, tc, body)


def _mla_up(z, cos, sin, q_norm, kv_norm, wq1, wq2, wk, wv, place, T, layer):
    B, S, _ = z.shape
    ntiles, nlat = _tok_tiles(S, T)
    HP = N_HEADS * MLA_HEAD_PAD
    zw = MLA_KV_RANK + MLA_Q_RANK + 2 * MLA_ROPE
    tab = pl.BlockSpec((TMT, MLA_HEAD_PAD), lambda b, i: (i, 0))
    return pl.pallas_call(
        functools.partial(_mla_up_kernel, nlat=nlat, tc=S - T),
        grid=(B, ntiles),
        in_specs=[_tok_spec(zw, COL_MLA // zw), tab, tab,
                  _const_spec((1, MLA_Q_RANK), layer), _const_spec((1, MLA_KV_RANK), layer),
                  _const_spec((MLA_Q_RANK, HP), layer), _const_spec((MLA_Q_RANK, HP), layer),
                  _const_spec((MLA_KV_RANK, HP), layer), _const_spec((GROUP_W, MLA_KV_RANK), layer),
                  _const_spec((MLA_ROPE, HP))],
        out_specs=[pl.BlockSpec((1, HP, TMT), lambda b, i: (b, 0, i)), _tok_spec(HP),
                   pl.BlockSpec((1, GROUP_W, TMT), lambda b, i: (b, 0, i))],
        out_shape=[jax.ShapeDtypeStruct((B, HP, S), BF16),
                   jax.ShapeDtypeStruct((B, S, HP), BF16),
                   jax.ShapeDtypeStruct((B, GROUP_W, S), BF16)],
        compiler_params=_cparams(2),
    )(z, cos, sin, q_norm, kv_norm, wq1, wq2, wk, wv, place)


def _mla_attn_kernel(qt_ref, k_ref, vt_ref, o_ref, *, nlat, T):
    j = pl.program_id(1)

    def attend(lo, hi):
        def logits(h):
            hp = slice(h * MLA_HEAD_PAD, (h + 1) * MLA_HEAD_PAD)
            return _dot(k_ref[0, lo:hi, hp], qt_ref[0, hp, :])

        ahead = 3
        pending = [logits(h) for h in range(ahead)]
        pieces = []
        for h in range(N_HEADS):
            s = pending.pop(0)
            if h + ahead < N_HEADS:
                pending.append(logits(h + ahead))
            p = jnp.exp2(s - jnp.max(s, axis=0, keepdims=True))
            den = jnp.sum(p, axis=0, keepdims=True)
            o = _dot(vt_ref[0, h * HEAD_DIM:(h + 1) * HEAD_DIM, lo:hi], p.astype(BF16))
            pieces.append(o * (1.0 / den))
        o_ref[0] = jnp.concatenate(pieces, axis=0).T.astype(o_ref.dtype)

    @pl.when(j < nlat)
    def _():
        attend(0, T + TQ)

    @pl.when(j >= nlat)
    def _():
        attend(T, T + TQ)


def _mla_attn(qt, k, vt, T):
    B, S, HP = k.shape
    assert T % MLA_TQ == 0 and S - T <= MLA_TQ
    nlat = T // MLA_TQ
    return pl.pallas_call(
        functools.partial(_mla_attn_kernel, nlat=nlat, T=T),
        grid=(B, nlat + 1),
        in_specs=[pl.BlockSpec((1, HP, MLA_TQ), lambda b, j: (b, 0, j)),
                  pl.BlockSpec((1, S, HP), lambda b, j: (b, 0, 0)),
                  pl.BlockSpec((1, GROUP_W, S), lambda b, j: (b, 0, 0))],
        out_specs=pl.BlockSpec((1, MLA_TQ, GROUP_W), lambda b, j: (b, j, 0)),
        out_shape=jax.ShapeDtypeStruct((B, S, GROUP_W), BF16),
        compiler_params=_cparams(2),
    )(qt, k, vt)


def _scan_masks():
    t = np.arange(CHUNK)
    inc = np.stack([t[:, None] >= t[None, :], t[:, None] <= t[None, :]]).astype(np.float32)
    strict = np.stack([t[:, None] > t[None, :], t[:, None] < t[None, :]]).astype(np.float32)
    return jnp.asarray(np.tile(inc, (1, 1, N_HEADS))), jnp.asarray(np.tile(strict, (1, 1, N_HEADS)))


def _cumsum_rows(x, reverse):
    n = x.shape[0]
    row = lax.broadcasted_iota(jnp.int32, (n, 1), 0)
    sh = 1
    while sh < n:
        if reverse:
            x = x + jnp.where(row < n - sh, pltpu.roll(x, n - sh, 0), 0.0)
        else:
            x = x + jnp.where(row >= sh, pltpu.roll(x, sh, 0), 0.0)
        sh *= 2
    return x


def _block_diag(x, bm_b):
    return _tile_rows(x.astype(BF16)) * bm_b


def _dot_nt(a, b):
    return lax.dot_general(a, b, (((1,), (1,)), ((), ())), preferred_element_type=F32)


def _fwd_tile(p, nlat):
    return jnp.where(p == 0, nlat, p - 1)


def _bwd_tile(p, nlat):
    return jnp.where(p == 0, nlat, nlat - p)


def _chunk_order(d):
    nch = TQ // CHUNK
    return range(nch) if d == 0 else range(nch - 1, -1, -1)


def _tile_rows(x):
    return jnp.concatenate([x] * N_HEADS, axis=0)


def _rw_prep_kernel(zp_ref, z_ref, zn_ref, mu_ref, kk_ref, ka_ref, rk_ref, w0_ref, a0_ref, wup_ref,
                    aup_ref, gup_ref, ones_ref,
                    r_ref, v_ref, ah_ref, g_ref, bon_ref, lw_ref, kd_ref, bd_ref, *, nlat, tc):
    i = pl.program_id(1)
    prev_ok, next_ok = _neighbour_ok(i, nlat)

    def body(nr):
        z = z_ref[0, :nr]
        row = lax.broadcasted_iota(jnp.int32, (nr, 1), 0)
        before = jnp.where(prev_ok, zp_ref[0, HALO - 1:HALO, :], 0.0)
        after = jnp.where(next_ok, zn_ref[0, 0:1, :], 0.0)
        zprev = jnp.where(row == 0, before, pltpu.roll(z, 1, 0))
        znext = jnp.where(row == nr - 1, after, pltpu.roll(z, nr - 1, 0))
        zs = z + mu_ref[...] * (0.5 * (zprev + znext) - z)
        r = zs[:, 0:GROUP_W]
        k = zs[:, GROUP_W:2 * GROUP_W]
        v = zs[:, 2 * GROUP_W:3 * GROUP_W]
        low = zs[:, 3 * GROUP_W:3 * GROUP_W + 128]
        gd = zs[:, 3 * GROUP_W + 128:]
        ones = ones_ref[...]
        kk = k * kk_ref[...]
        kk = kk * lax.rsqrt(_dot_sel_rhs(kk * kk, ones) + 1e-12)
        wl = _dot(jnp.tanh(low).astype(BF16), wup_ref[...])
        al = _dot(low.astype(BF16), aup_ref[...])
        ksum = jnp.zeros((nr, GROUP_W), F32)
        for d in range(2):
            w_raw = -_softplus(-(w0_ref[d:d + 1, :] + wl[:, d * GROUP_W:(d + 1) * GROUP_W])) - 0.5
            lw_ref[d, 0, :nr] = -jnp.exp(w_raw)
            a = _sigmoid(a0_ref[d:d + 1, :] + al[:, d * GROUP_W:(d + 1) * GROUP_W])
            kd = k * (1.0 + (a - 1.0) * ka_ref[...])
            kd_ref[d, 0, :nr] = kd
            bd_ref[d, 0, :nr] = kk * a
            ksum = ksum + kd
        r_ref[0, :nr] = r
        v_ref[0, :nr] = v
        ah_ref[0, :nr] = -kk
        g_ref[0, :nr] = _dot(_sigmoid(gd).astype(BF16), gup_ref[...])
        bon_ref[0, :nr] = _dot_sel_rhs(r * ksum * rk_ref[...], ones) * v

    _for_tile_rows(nlat, tc, body)


def _rw_prep(z, mu, k_k, k_a, r_k, w0, a0, wup_p, aup_p, gup_b, ones_b, T, layer):
    B, S, _ = z.shape
    ntiles, nlat = _tok_tiles(S, T)
    W = 4 * GROUP_W
    cb = COL_RW // W
    prev, nxt = _halo_specs(W, cb, S)
    one = _tok_spec(GROUP_W)
    two = pl.BlockSpec((2, 1, TMT, GROUP_W), lambda b, i: (0, b, i, 0))
    s1 = jax.ShapeDtypeStruct((B, S, GROUP_W), F32)
    s2 = jax.ShapeDtypeStruct((2, B, S, GROUP_W), F32)
    vec = _const_spec((1, GROUP_W), layer)
    return pl.pallas_call(
        functools.partial(_rw_prep_kernel, nlat=nlat, tc=S - T),
        grid=(B, ntiles),
        in_specs=[prev, _tok_spec(W, cb), nxt,
                  _const_spec((1, W), layer), vec, vec, vec,
                  _const_spec((2, GROUP_W), layer), _const_spec((2, GROUP_W), layer),
                  _const_spec((128, 2 * GROUP_W), layer), _const_spec((128, 2 * GROUP_W), layer),
                  _const_spec((128, GROUP_W), layer), _const_spec((GROUP_W, GROUP_W))],
        out_specs=[one, one, one, one, one, two, two, two],
        out_shape=[s1, s1, s1, s1, s1, s2, s2, s2],
        compiler_params=_cparams(2),
    )(z, z, z, mu, k_k, k_a, r_k, w0, a0, wup_p, aup_p, gup_b, ones_b)


def _rw_scan_kernel(rf_ref, rb_ref, vf_ref, vb_ref, af_ref, ab_ref, lwf_ref, lwb_ref, kdf_ref, kdb_ref,
                    bdf_ref, bdb_ref, inc_ref, strict_ref, bm_ref, yf_ref, yb_ref, s_ref):
    p = pl.program_id(1)

    @pl.when(p == 0)
    def _():
        s_ref[...] = jnp.zeros_like(s_ref)

    n = N_HEADS * CHUNK
    bm = bm_ref[...]
    eye = (lax.broadcasted_iota(jnp.int32, (CHUNK, n), 1) % CHUNK
           == lax.broadcasted_iota(jnp.int32, (CHUNK, n), 0)).astype(F32)
    refs = ((rf_ref, vf_ref, af_ref, lwf_ref, kdf_ref, bdf_ref, yf_ref),
            (rb_ref, vb_ref, ab_ref, lwb_ref, kdb_ref, bdb_ref, yb_ref))
    orders = [list(_chunk_order(d)) for d in range(2)]
    bm_b = bm.astype(BF16)
    bd_of = lambda m: _block_diag(m, bm_b)

    nb = s_ref.shape[0] // 2
    inst = [(bi, step, d) for step in range(TQ // CHUNK) for bi in range(nb) for d in range(2)]
    idx = range(len(inst))
    rows = [pl.ds(orders[d][step] * CHUNK, CHUNK) for _, step, d in inst]

    ar, r_t, v_s, a_s, bk_t, decay, v_in, g_b, g_k = [], [], [], [], [], [], [], [], []
    for (bi, step, d), sl in zip(inst, rows):
        r_ref, v_ref, a_ref, lw_ref, kd_ref, bd_ref, _ = refs[d]
        lw = lw_ref[0, bi, sl, :]
        kd = kd_ref[0, bi, sl, :]
        bd = bd_ref[0, bi, sl, :]
        v = v_ref[bi, sl, :]
        cs = _cumsum_rows(lw, reverse=(d == 1))
        tot = cs[CHUNK - 1:CHUNK, :] if d == 0 else cs[0:1, :]
        e_neg = jnp.exp(-cs)
        e_hat = jnp.exp(tot - cs)
        a_t = a_ref[bi, sl, :] * jnp.exp(cs - lw)
        rt = r_ref[bi, sl, :] * jnp.exp(cs)
        ar.append(jnp.concatenate([a_t, rt], axis=0).astype(BF16))
        r_t.append(rt)
        v_in.append(v)
        g_b.append(_dot_nt(ar[-1], bd_of(bd * e_neg)))
        g_k.append(_dot_nt(ar[-1], bd_of(kd * e_neg)))
        v_s.append(bd_of(v))
        a_s.append(bd_of(a_t))
        t = jnp.concatenate([bd * e_hat, kd * e_hat,
                             jnp.broadcast_to(jnp.exp(tot), (2 * CHUNK, GROUP_W))], axis=0).T
        bk_t.append(t[:, :2 * CHUNK].astype(BF16))
        decay.append(jnp.concatenate([t[:, 2 * CHUNK:]] * 2, axis=1))
    pw =[g_b[i][:CHUNK] * strict_ref[inst[i][2]] for i in idx]
    ak = [(g_k[i][:CHUNK] * strict_ref[inst[i][2]]).astype(BF16) for i in idx]
    rbk = [jnp.concatenate([g_b[i][CHUNK:] * inc_ref[inst[i][2]], g_k[i][CHUNK:] * inc_ref[inst[i][2]]],
                           axis=1).astype(BF16) for i in idx]
    x = [eye + pw[i] for i in idx]
    pw = [_dot(pw[i].astype(BF16), bd_of(pw[i])) for i in idx]
    for _ in range(4):
        px = [_dot(jnp.concatenate([pw[i], x[i]], axis=0).astype(BF16), bd_of(pw[i])) for i in idx]
        x = [x[i] + px[i][CHUNK:] for i in idx]
        pw = [px[i][:CHUNK] for i in idx]
    x = [(x[i] + _dot(x[i].astype(BF16), bd_of(pw[i]))).astype(BF16) for i in idx]
    akv = [_dot(ak[i], v_s[i]) for i in idx]
    p12 = [_dot(x[i], jnp.concatenate([bd_of(akv[i]), a_s[i]], axis=1)) for i in idx]
    p1_s = [bd_of(p12[i][:, :GROUP_W]) for i in idx]
    p2_s = [bd_of(p12[i][:, GROUP_W:]) for i in idx]
    q = [r_t[i] + _dot(rbk[i][:, :n], p2_s[i]) for i in idx]
    y_c = [_dot(rbk[i], jnp.concatenate([p1_s[i], v_s[i]], axis=0)) for i in idx]
    zero = jnp.zeros((CHUNK, GROUP_W), F32)
    gc = [_dot(bk_t[i], jnp.concatenate(
        [jnp.concatenate([p12[i][:, GROUP_W:], p12[i][:, :GROUP_W]], axis=1),
         jnp.concatenate([zero, v_in[i]], axis=1)], axis=0).astype(BF16)) for i in idx]
    gq = [jnp.concatenate([gc[i][:, :GROUP_W] * bm, q[i]], axis=0).astype(BF16) for i in idx]
    c_s = [gc[i][:, GROUP_W:] * bm for i in idx]

    s = [s_ref[k] for k in range(2 * nb)]
    for i, ((bi, step, d), sl) in enumerate(zip(inst, rows)):
        k = 2 * bi + d
        m = _dot(gq[i], s[k].astype(BF16))
        refs[d][6][bi, sl, :] = m[n:] + y_c[i]
        s[k] = decay[i] * s[k] + m[:n] + c_s[i]
    for k in range(2 * nb):
        s_ref[k] = s[k]


def _rw_scan(r, v, ah, lw, kd, bd, masks, bm, T):
    B, S, _ = r.shape
    nlat = T // TQ
    inc, strict = masks
    n = N_HEADS * CHUNK
    nb = RW_SCAN_BATCH if B % RW_SCAN_BATCH == 0 else 1
    fwd = pl.BlockSpec((nb, TQ, GROUP_W), lambda b, p: (b, _fwd_tile(p, nlat), 0))
    bwd = pl.BlockSpec((nb, TQ, GROUP_W), lambda b, p: (b, _bwd_tile(p, nlat), 0))
    fwd2 = pl.BlockSpec((1, nb, TQ, GROUP_W), lambda b, p: (0, b, _fwd_tile(p, nlat), 0))
    bwd2 = pl.BlockSpec((1, nb, TQ, GROUP_W), lambda b, p: (1, b, _bwd_tile(p, nlat), 0))
    out = jax.ShapeDtypeStruct((B, S, GROUP_W), F32)
    return pl.pallas_call(
        _rw_scan_kernel,
        grid=(B // nb, nlat + 1),
        in_specs=[fwd, bwd, fwd, bwd, fwd, bwd, fwd2, bwd2, fwd2, bwd2, fwd2, bwd2,
                  _const_spec((2, CHUNK, n)), _const_spec((2, CHUNK, n)), _const_spec((n, n))],
        out_specs=[fwd, bwd],
        out_shape=[out, out],
        scratch_shapes=[pltpu.VMEM((2 * nb, n, GROUP_W), F32)],
        compiler_params=_cparams(2),
    )(r, r, v, v, ah, ah, lw, lw, kd, kd, bd, bd, inc, strict, bm)


def _gla_scan_kernel(qf_ref, qb_ref, kf_ref, kb_ref, gf_ref, gb_ref, vf_ref, vb_ref, gup_ref, gbias_ref,
                     inc_ref, bm_ref, bmv_ref, of_ref, ob_ref, s_ref):
    p = pl.program_id(1)

    @pl.when(p == 0)
    def _():
        s_ref[...] = jnp.zeros_like(s_ref)

    wk = N_HEADS * GLA_DK
    n = N_HEADS * CHUNK
    bm = bm_ref[...]
    bmv = bmv_ref[...].astype(BF16)
    bmk = (lax.broadcasted_iota(jnp.int32, (n, wk), 0) // CHUNK
           == lax.broadcasted_iota(jnp.int32, (n, wk), 1) // GLA_DK).astype(F32).astype(BF16)
    refs = ((qf_ref, kf_ref, gf_ref, vf_ref, of_ref), (qb_ref, kb_ref, gb_ref, vb_ref, ob_ref))
    orders = [list(_chunk_order(d)) for d in range(2)]
    nb = s_ref.shape[0] // 2
    inst = [(bi, step, d) for step in range(TQ // CHUNK) for bi in range(nb) for d in range(2)]
    idx = range(len(inst))
    rows = [pl.ds(orders[d][step] * CHUNK, CHUNK) for _, step, d in inst]

    qe, ke_s, ks4, v_s, decay = [], [], [], [], []
    for (bi, step, d), sl in zip(inst, rows):
        q_ref, k_ref, g_ref, v_ref, _ = refs[d]
        k = k_ref[bi, sl, :]
        la = -_softplus(-(_dot(g_ref[bi, sl, :].astype(BF16), gup_ref[d]) + gbias_ref[d])) * (1.0 / GLA_TAU)
        b = _cumsum_rows(la, reverse=(d == 1))
        tot = b[CHUNK - 1:CHUNK, :] if d == 0 else b[0:1, :]
        qe.append((q_ref[bi, sl, :] * (GLA_DK ** -0.5) * jnp.exp(b)).astype(BF16))
        ke_s.append(_block_diag(k * jnp.exp(-b), bmk))
        ks4.append(_tile_rows(k * jnp.exp(tot - b)).T.astype(BF16))
        v_s.append(_block_diag(v_ref[bi, sl, :], bmv))
        decay.append(_tile_rows(jnp.broadcast_to(jnp.exp(tot), (CHUNK, wk))).T)
    a_cat = [(_dot_nt(qe[i], ke_s[i]) * inc_ref[inst[i][2]]).astype(BF16) for i in idx]
    o_in = [_dot(a_cat[i], v_s[i]) for i in idx]
    kv = [_dot(ks4[i], v_s[i]) * bm for i in idx]

    s = [s_ref[k] for k in range(2 * nb)]
    for i, ((bi, step, d), sl) in enumerate(zip(inst, rows)):
        k = 2 * bi + d
        refs[d][4][bi, sl, :] = o_in[i] + _dot(qe[i], s[k].astype(BF16))
        s[k] = decay[i] * s[k] + kv[i]
    for k in range(2 * nb):
        s_ref[k] = s[k]


def _gla_scan(z, gup_p, gb, masks, bm, bmv, T, layer):
    B, S, _ = z.shape
    nlat = T // TQ
    inc, _ = masks
    n = N_HEADS * CHUNK
    wk = N_HEADS * GLA_DK
    nb = GLA_SCAN_BATCH if B % GLA_SCAN_BATCH == 0 else 1
    fwd = lambda w, col: pl.BlockSpec((nb, TQ, w), lambda b, p: (b, _fwd_tile(p, nlat), col // w))
    bwd = lambda w, col: pl.BlockSpec((nb, TQ, w), lambda b, p: (b, _bwd_tile(p, nlat), col // w))
    out = jax.ShapeDtypeStruct((B, S, GROUP_W), F32)
    return pl.pallas_call(
        _gla_scan_kernel,
        grid=(B // nb, nlat + 1),
        in_specs=[fwd(wk, COL_GLA_Q), bwd(wk, COL_GLA_Q), fwd(wk, COL_GLA_K), bwd(wk, COL_GLA_K),
                  fwd(wk, COL_GLA_G), bwd(wk, COL_GLA_G), fwd(GROUP_W, COL_GLA_V), bwd(GROUP_W, COL_GLA_V),
                  _const_spec((2, wk, wk), layer), _const_spec((2, 1, wk), layer),
                  _const_spec((2, CHUNK, n)), _const_spec((wk, GROUP_W)), _const_spec((n, n))],
        out_specs=[fwd(GROUP_W, 0), bwd(GROUP_W, 0)],
        out_shape=[out, out],
        scratch_shapes=[pltpu.VMEM((2 * nb, wk, GROUP_W), F32)],
        compiler_params=_cparams(2),
    )(z, z, z, z, z, z, z, z, gup_p, gb, inc, bm, bmv)


def _pack_w_in(w_in):
    L, D, _ = w_in.shape
    na, mla, rw, gla = jnp.split(w_in.astype(BF16), [768, 1120, 2144], axis=-1)
    cq, ckv, kr = jnp.split(mla, [MLA_Q_RANK, MLA_Q_RANK + MLA_KV_RANK], axis=-1)
    gq, gk, gv, gg, go = jnp.split(gla, [128, 256, 512, 528], axis=-1)
    pad = jnp.zeros((L, D, COL_GLA_V - COL_GLA_G - gg.shape[-1]), BF16)
    packed = jnp.concatenate([na, ckv, cq, kr, kr[..., _rope_swap_perm()], gq, gk, gg, pad, gv, go, rw], axis=-1)
    assert packed.shape[-1] == Z_COLS
    return packed


def kernel(x, c, ctx, c_ctx, w_mod, b_mod, g_mix_pre, g_mix_post, g_ffn_pre, g_ffn_post, w_in, w_out, na_rpb, mla_q_norm, mla_w_uq, mla_kv_norm, mla_w_ukv, rw_mu, rw_w0, rw_w_up, rw_a0, rw_a_up, rw_g_up, rw_k_k, rw_k_a, rw_r_k, rw_ln_w, rw_ln_b, gla_gate_up, gla_gate_b, gla_norm, ffn_w_up, ffn_conv_w, ffn_conv_b, ffn_w_down):
    B, T, D = x.shape
    Tc = ctx.shape[1]
    L = w_in.shape[0]
    assert D == D_MODEL and Tc == TQ and T % TMT == 0 and B + 1 <= 8

    w_in_p = _pack_w_in(w_in)
    w_out_b = w_out.astype(BF16)
    wq1, wq2, wk, wv = _mla_weights(mla_w_uq, mla_w_ukv)
    place = _rope_place()
    cos, sin = _rope_tables(T, Tc)
    zero_lo = jnp.zeros((L, 64, 2 * GROUP_W), F32)
    rw_wup_p = jnp.concatenate([jnp.concatenate([rw_w_up[:, 0], rw_w_up[:, 1]], axis=-1), zero_lo], axis=1).astype(BF16)
    rw_aup_p = jnp.concatenate([zero_lo, jnp.concatenate([rw_a_up[:, 0], rw_a_up[:, 1]], axis=-1)], axis=1).astype(BF16)
    rw_gup_b = rw_g_up.astype(BF16)
    wk_gla = N_HEADS * GLA_DK
    gla_gup_p = jnp.concatenate([gla_gate_up, jnp.zeros((L, 2, wk_gla - gla_gate_up.shape[2], wk_gla), F32)], axis=2).astype(BF16)
    ffn_up_b = ffn_w_up.astype(BF16)
    ffn_dn_b = ffn_w_down.astype(BF16)
    ones_b = jnp.asarray(_block_ones(GROUP_W, HEAD_DIM), BF16)
    rw_bm = jnp.asarray(_block_ones(N_HEADS * CHUNK, CHUNK))
    gla_bm = jnp.asarray((np.arange(wk_gla)[:, None] // GLA_DK == np.arange(GROUP_W)[None, :] // HEAD_DIM).astype(np.float32))
    masks = _scan_masks()

    cvecs = jnp.zeros((8, D), F32).at[:B].set(c).at[B].set(c_ctx)
    mods = _modulation(cvecs, w_mod, b_mod).reshape(L, 8, 6, D)
    mods = jnp.pad(mods, ((0, 0), (0, 0), (0, 2), (0, 0)))
    modtabs = jnp.stack([jnp.broadcast_to(mods[:, B:B + 1], (L, B, 8, D)), mods[:, :B]], axis=2)

    xs = (x, ctx)
    rows = lambda a: a.reshape(L, 1, -1)
    na_bias = _natten_bias(na_rpb, T // GRID_W)
    gla_gb = gla_gate_b[:, :, None, :]
    for i in range(L):
        z = _in_proj(xs, modtabs, rows(g_mix_pre), w_in_p, T, Tc, i)
        y_na = _natten(z, na_bias, T, i)
        qt, k, vt = _mla_up(z, cos, sin, rows(mla_q_norm), rows(mla_kv_norm), wq1, wq2, wk, wv, place, T, i)
        y_mla = _mla_attn(qt, k, vt, T)
        r, vv, ah, g, bon, lw, kd, bd = _rw_prep(z, rows(rw_mu), rows(rw_k_k), rows(rw_k_a), rows(rw_r_k),
                                                 rw_w0, rw_a0, rw_wup_p, rw_aup_p, rw_gup_b, ones_b, T, i)
        yf, yb = _rw_scan(r, vv, ah, lw, kd, bd, masks, rw_bm, T)
        of, ob = _gla_scan(z, gla_gup_p, gla_gb, masks, gla_bm, rw_bm, T, i)
        xs = _out_proj(xs, z, y_na, y_mla, yf, yb, bon, g, of, ob, modtabs, rows(g_mix_post), w_out_b,
                       rows(rw_ln_w), rows(rw_ln_b), rows(gla_norm), ones_b, T, i)
        xs = _ffn(xs, modtabs, rows(g_ffn_pre), rows(g_ffn_post), ffn_up_b, ffn_conv_w,
                  rows(ffn_conv_b), ffn_dn_b, T, i, latent_only=(i == L - 1))
    return xs
```

```python
import functools

import numpy as np
import jax
import jax.numpy as jnp
from jax import lax
from jax.experimental import pallas as pl
from jax.experimental.pallas import tpu as pltpu

F32 = jnp.float32
BF16 = jnp.bfloat16

D_MODEL = 1024
GRID_W = 64
EPS = 1e-6
LOG2E = 1.4426950408889634
N_HEADS = 4
HEAD_DIM = 64
GROUP_W = 256
NA_ROWS = 8
NA_COLS = 16
NA_UNION = 12
MLA_Q_RANK = 192
MLA_KV_RANK = 128
MLA_NOPE = 64
MLA_ROPE = 32
MLA_HEAD_PAD = 128
ROPE_THETA = 10000.0
RW_GN_EPS = 64e-5
GLA_DK = 32
GLA_TAU = 16.0
D_FF = 2816
CHUNK = 64
TQ = 256
MLA_TQ = 512
RW_SCAN_BATCH = 2
GLA_SCAN_BATCH = 4
TMT = 512
TM_WIDE = 1024
HALO = 8
FF_CHUNK = 256
FF_GROUP = 4
Z_COLS = 3072

COL_NA = 0
COL_MLA = 768
COL_GLA_Q = 1152
COL_GLA_K = 1280
COL_GLA_G = 1408
COL_GLA_V = 1536
COL_GLA_O = 1792
COL_RW = 2048

VMEM_LIMIT_V7X = 56 * 1024 * 1024


def _cparams(n_axes):
    return pltpu.CompilerParams(dimension_semantics=("arbitrary",) * n_axes,
                                vmem_limit_bytes=VMEM_LIMIT_V7X)


def _const_spec(shape, layer=None):
    nd = len(shape)
    if layer is None:
        return pl.BlockSpec(shape, lambda *_: (0,) * nd, pipeline_mode=pl.Buffered(1))
    return pl.BlockSpec((None,) + tuple(shape), lambda *_: (layer,) + (0,) * nd, pipeline_mode=pl.Buffered(1))


def _dot(a, b):
    return jnp.dot(a, b, preferred_element_type=F32)


def _split3(x):
    hi = x.astype(BF16)
    r1 = x - hi.astype(F32)
    mid = r1.astype(BF16)
    lo = (r1 - mid.astype(F32)).astype(BF16)
    return hi, mid, lo


def _dot_sel_rhs(x, m):
    hi, mid, lo = _split3(x)
    return _dot(hi, m) + (_dot(mid, m) + _dot(lo, m))


def _sigmoid(x):
    return 1.0 / (1.0 + jnp.exp(-x))


def _silu_gain(x):
    return 0.5 + 0.5 * jnp.tanh(0.5 * x)


def _softplus(x):
    return jnp.maximum(x, 0.0) + jnp.log1p(jnp.exp(-jnp.abs(x)))


def _rms(x):
    return x * lax.rsqrt(jnp.mean(x * x, axis=-1, keepdims=True) + EPS)


def _norm_mod(x, g, shift, scale):
    return (_rms(x) * g) * (1.0 + scale) + shift


def _lane_head(width, per_head):
    return lax.broadcasted_iota(jnp.int32, (1, width), 1) // per_head


def _block_ones(n, blk):
    i = np.arange(n) // blk
    return (i[:, None] == i[None, :]).astype(np.float32)


def _mod_kernel(c_ref, w_ref, b_ref, o_ref):
    cv = c_ref[...]
    s = cv * _sigmoid(cv)
    o_ref[0] = _dot(s.astype(BF16), w_ref[0].astype(BF16)) + b_ref[0]


def _modulation(cvecs, w_mod, b_mod):
    L, D, N = w_mod.shape
    tn = 1536
    return pl.pallas_call(
        _mod_kernel,
        grid=(L, N // tn),
        in_specs=[pl.BlockSpec((8, D), lambda l, n: (0, 0)),
                  pl.BlockSpec((1, D, tn), lambda l, n: (l, 0, n)),
                  pl.BlockSpec((1, 1, tn), lambda l, n: (l, 0, n))],
        out_specs=pl.BlockSpec((1, 8, tn), lambda l, n: (l, 0, n)),
        out_shape=jax.ShapeDtypeStruct((L, 8, N), F32),
        compiler_params=_cparams(2),
    )(cvecs, w_mod, b_mod.reshape(L, 1, N))


def _tok_tiles(S, T, tm=TMT):
    assert T % tm == 0 and 0 < S - T <= tm
    return T // tm + 1, T // tm


def _tok_spec(width, col_block=0, tm=TMT):
    return pl.BlockSpec((1, tm, width), lambda b, i: (b, i, col_block))


def _mod_spec(nlat, layer):
    return pl.BlockSpec((None, 1, 1, 8, D_MODEL), lambda b, i: (layer, b, jnp.where(i < nlat, 1, 0), 0, 0))


def _halo_specs(width, col_block, S, tm=TMT):
    per = tm // HALO
    last = S // HALO - 1
    prev = pl.BlockSpec((1, HALO, width), lambda b, i: (b, jnp.maximum(i * per - 1, 0), col_block))
    nxt = pl.BlockSpec((1, HALO, width), lambda b, i: (b, jnp.minimum((i + 1) * per, last), col_block))
    return prev, nxt


def _neighbour_ok(i, nlat):
    prev_ok = jnp.logical_and(i != 0, i != nlat)
    next_ok = i < nlat - 1
    return prev_ok, next_ok


def _for_tile_rows(nlat, tc, body, tm=TMT):
    i = pl.program_id(1)

    @pl.when(i < nlat)
    def _():
        body(tm)

    @pl.when(i >= nlat)
    def _():
        body(tc)


def _stream_source(src, T, Tc, tm):
    nlat = T // tm
    assert Tc < tm
    lat = pl.BlockSpec((1, tm, D_MODEL), lambda b, i: (b, jnp.minimum(i, nlat - 1), 0))
    if isinstance(src, tuple):
        return src, [lat, pl.BlockSpec((1, Tc, D_MODEL), lambda b, i: (b, 0, 0))]
    assert T % Tc == 0
    return (src, src), [lat, pl.BlockSpec((1, Tc, D_MODEL), lambda b, i: (b, T // Tc, 0))]


def _in_proj_kernel(xl_ref, xc_ref, mod_ref, g_ref, w_ref, z_ref, *, nlat, tc, tm):
    def body(nr):
        m = mod_ref[0, 0]
        x = xl_ref[0] if nr == tm else xc_ref[0]
        h = _norm_mod(x, g_ref[...], m[0:1], m[1:2])
        z_ref[0, :nr] = _dot(h.astype(BF16), w_ref[...])

    _for_tile_rows(nlat, tc, body, tm)


def _in_proj(src, modtab, g_pre, w_in_p, T, Tc, layer):
    tm = TM_WIDE
    arrays, specs = _stream_source(src, T, Tc, tm)
    B, S = arrays[0].shape[0], T + Tc
    ntiles, nlat = _tok_tiles(S, T, tm)
    return pl.pallas_call(
        functools.partial(_in_proj_kernel, nlat=nlat, tc=Tc, tm=tm),
        grid=(B, ntiles),
        in_specs=specs + [_mod_spec(nlat, layer), _const_spec((1, D_MODEL), layer),
                          _const_spec((D_MODEL, Z_COLS), layer)],
        out_specs=_tok_spec(Z_COLS, 0, tm),
        out_shape=jax.ShapeDtypeStruct((B, S, Z_COLS), F32),
        compiler_params=_cparams(2),
    )(*arrays, modtab, g_pre, w_in_p)


def _out_proj_kernel(xl_ref, xc_ref, na_ref, mla_ref, yf_ref, yb_ref, bon_ref, g_ref, of_ref, ob_ref, og_ref,
                     mod_ref, gpost_ref, w_ref, lnw_ref, lnb_ref, gn_ref, ones_ref, o_ref, *, nlat, tc, tm):
    def body(nr):
        m = mod_ref[0, 0]
        x = xl_ref[0] if nr == tm else xc_ref[0]
        ones = ones_ref[...]
        inv_n = 1.0 / HEAD_DIM
        y = yf_ref[0, :nr] + yb_ref[0, :nr]
        yc = y - _dot_sel_rhs(y, ones) * inv_n
        var = _dot_sel_rhs(yc * yc, ones) * inv_n
        y_rw = (yc * lax.rsqrt(var + RW_GN_EPS) * lnw_ref[...] + lnb_ref[...] + bon_ref[0, :nr]) * g_ref[0, :nr]
        o = of_ref[0, :nr] + ob_ref[0, :nr]
        ms = _dot_sel_rhs(o * o, ones) * inv_n
        og = og_ref[0, :nr]
        y_gla = (o * lax.rsqrt(ms + EPS) * gn_ref[...]) * (og * _silu_gain(og))
        y = jnp.concatenate([na_ref[0, :nr], mla_ref[0, :nr], y_rw.astype(BF16), y_gla.astype(BF16)], axis=-1)
        y = _dot(y, w_ref[...])
        o_ref[0, :nr] = x + m[2:3] * (_rms(y) * gpost_ref[...])

    _for_tile_rows(nlat, tc, body, tm)


def _out_proj(src, z, y_na, y_mla, yf, yb, bon, g, of, ob, modtab, g_post, w_out_b, ln_w, ln_b, gla_norm,
              ones_b, T, layer):
    B, S, _ = z.shape
    D = D_MODEL
    tm = TM_WIDE
    arrays, specs = _stream_source(src, T, S - T, tm)
    ntiles, nlat = _tok_tiles(S, T, tm)
    grp = _tok_spec(GROUP_W, 0, tm)
    vec = _const_spec((1, GROUP_W), layer)
    return pl.pallas_call(
        functools.partial(_out_proj_kernel, nlat=nlat, tc=S - T, tm=tm),
        grid=(B, ntiles),
        in_specs=specs + [grp, grp, grp, grp, grp, grp, grp, grp,
                          _tok_spec(GROUP_W, COL_GLA_O // GROUP_W, tm),
                          _mod_spec(nlat, layer), _const_spec((1, D), layer), _const_spec((D, D), layer),
                          vec, vec, vec, _const_spec((GROUP_W, GROUP_W))],
        out_specs=_tok_spec(D, 0, tm),
        out_shape=jax.ShapeDtypeStruct((B, S, D), F32),
        compiler_params=_cparams(2),
    )(*arrays, y_na, y_mla, yf, yb, bon, g, of, ob, z, modtab, g_post, w_out_b, ln_w, ln_b, gla_norm, ones_b)


def _ffn_kernel(xp_ref, x_ref, xn_ref, mod_ref, gpre_ref, gpost_ref, wup_ref, cw_ref, cb_ref,
                wdn_ref, o_ref, *, nlat, tc):
    i = pl.program_id(1)
    prev_ok, next_ok = _neighbour_ok(i, nlat)
    nchunks = D_FF // FF_CHUNK

    def body(nr):
        m = mod_ref[0, 0]
        x = x_ref[0, :nr]
        xe = jnp.concatenate([xp_ref[0], x, xn_ref[0]], axis=0)
        n = nr + 2 * HALO
        h = _norm_mod(xe, gpre_ref[...], m[3:4], m[4:5])
        row = lax.broadcasted_iota(jnp.int32, (n, 1), 0)
        valid = jnp.logical_or(jnp.logical_and(row >= HALO, row < HALO + nr),
                               jnp.logical_or(jnp.logical_and(row < HALO, prev_ok),
                                              jnp.logical_and(row >= HALO + nr, next_ok)))
        hb = jnp.where(valid, h, 0.0).astype(BF16)

        def up(c):
            return [_dot(hb, wup_ref[:, base + c * FF_CHUNK:base + (c + 1) * FF_CHUNK]) for base in (0, D_FF)]

        def conv(z, lo):
            cw = cw_ref[:, lo:lo + FF_CHUNK]
            return (cb_ref[:, lo:lo + FF_CHUNK]
                    + pltpu.roll(z, 1, 0)[HALO:HALO + nr] * cw[0:1]
                    + z[HALO:HALO + nr] * cw[1:2]
                    + pltpu.roll(z, n - 1, 0)[HALO:HALO + nr] * cw[2:3])

        acc = None
        group = []
        ahead = 2
        pending = [up(c) for c in range(ahead)]
        for c in range(nchunks):
            z_val, z_gate = pending.pop(0)
            if c + ahead < nchunks:
                pending.append(up(c + ahead))
            val = conv(z_val, c * FF_CHUNK)
            gate = conv(z_gate, D_FF + c * FF_CHUNK)
            group.append(((gate * _silu_gain(gate)) * val).astype(BF16))
            if len(group) == FF_GROUP or c + 1 == nchunks:
                lo = (c + 1 - len(group)) * FF_CHUNK
                part = _dot(jnp.concatenate(group, axis=1), wdn_ref[lo:(c + 1) * FF_CHUNK, :])
                acc = part if acc is None else acc + part
                group = []
        o_ref[0, :nr] = x + m[5:6] * (_rms(acc) * gpost_ref[...])

    _for_tile_rows(nlat, tc, body)


def _ffn(xs, modtab, g_pre, g_post, w_up_b, conv_w, conv_b, w_dn_b, T, layer, latent_only):
    B, S, D = xs.shape
    ntiles, nlat = _tok_tiles(S, T)
    prev, nxt = _halo_specs(D, 0, S)
    return pl.pallas_call(
        functools.partial(_ffn_kernel, nlat=nlat, tc=S - T),
        grid=(B, nlat if latent_only else ntiles),
        in_specs=[prev, _tok_spec(D), nxt, _mod_spec(nlat, layer), _const_spec((1, D), layer),
                  _const_spec((1, D), layer), _const_spec((D, 2 * D_FF), layer),
                  _const_spec((3, 2 * D_FF), layer), _const_spec((1, 2 * D_FF), layer),
                  _const_spec((D_FF, D), layer)],
        out_specs=_tok_spec(D),
        out_shape=jax.ShapeDtypeStruct((B, T if latent_only else S, D), F32),
        compiler_params=_cparams(2),
    )(xs, xs, xs, modtab, g_pre, g_post, w_up_b, conv_w, conv_b, w_dn_b)


def _natten_bias(rpb, rows):
    rt = TQ // GRID_W
    j = np.arange(GRID_W)
    col_start = np.clip(j - NA_COLS // 2, 0, GRID_W - NA_COLS)
    col_in = (j[None, :] >= col_start[:, None]) & (j[None, :] < col_start[:, None] + NA_COLS)
    edge = GRID_W - NA_COLS
    ext = jnp.concatenate([jnp.repeat(rpb[..., :1], edge, axis=-1), rpb,
                           jnp.repeat(rpb[..., -1:], edge, axis=-1)], axis=-1).astype(F32) * LOG2E
    bq = jnp.stack([ext[..., GRID_W - 1 - q:2 * GRID_W - 1 - q] for q in range(GRID_W)], axis=2)
    L = rpb.shape[0]
    bq = jnp.where(col_in[:, None, :], bq, -jnp.inf).reshape(L, N_HEADS, GRID_W, -1)
    cases = []
    for r0 in (0, rt, rows - rt):
        us = min(max(r0 - NA_ROWS // 2, 0), rows - NA_UNION)
        per_row = []
        for r in range(r0, r0 + rt):
            rs = min(max(r - NA_ROWS // 2, 0), rows - NA_ROWS)
            first = rs - r + (NA_ROWS - 1)
            pre, post = rs - us, us + NA_UNION - (rs + NA_ROWS)
            per_row.append(jnp.pad(bq[..., first * GRID_W:(first + NA_ROWS) * GRID_W],
                                   ((0, 0), (0, 0), (0, 0), (pre * GRID_W, post * GRID_W)),
                                   constant_values=-jnp.inf))
        cases.append(jnp.concatenate(per_row, axis=2))
    return jnp.stack(cases, axis=1)


def _natten_kernel(q_ref, k_ref, v_ref, qc_ref, kc_ref, vc_ref, bias_a_ref, bias_b_ref, o_ref, *, nlat, rows):
    j = pl.program_id(1)
    lane_h = _lane_head(GROUP_W, HEAD_DIM)
    kct = kc_ref[0].T.astype(BF16)
    vc = vc_ref[0].astype(BF16)
    scale = HEAD_DIM ** -0.5 * LOG2E
    nwin = NA_UNION * GRID_W

    @pl.when(j < nlat // 2)
    def _():
        wins = []
        for half in range(2):
            us = jnp.clip((2 * j + half) * (TQ // GRID_W) - NA_ROWS // 2, 0, rows - NA_UNION)
            start = pl.multiple_of(us * GRID_W, GRID_W)
            wins.append((k_ref[0, pl.ds(start, nwin), :].T.astype(BF16),
                         v_ref[0, pl.ds(start, nwin), :].astype(BF16)))
        for half, bias_ref in enumerate((bias_a_ref, bias_b_ref)):
            kwt, vw = wins[half]
            q = q_ref[0, half * TQ:(half + 1) * TQ, :] * scale
            acc = jnp.zeros((TQ, GROUP_W), F32)

            def logits(h):
                qh = jnp.where(lane_h == h, q, 0.0).astype(BF16)
                return _dot(qh, kwt), _dot(qh, kct)

            s_next = logits(0)
            for h in range(N_HEADS):
                hm = lane_h == h
                s_w, s_c = s_next
                if h + 1 < N_HEADS:
                    s_next = logits(h + 1)
                s_w = s_w + bias_ref[0, h]
                mx = jnp.maximum(jnp.max(s_w, axis=-1, keepdims=True), jnp.max(s_c, axis=-1, keepdims=True))
                p_w = jnp.exp2(s_w - mx)
                p_c = jnp.exp2(s_c - mx)
                den = jnp.sum(p_w, axis=-1, keepdims=True) + jnp.sum(p_c, axis=-1, keepdims=True)
                o = _dot(p_w.astype(BF16), vw) + _dot(p_c.astype(BF16), vc)
                acc = acc + jnp.where(hm, o * (1.0 / den), 0.0)
            o_ref[0, half * TQ:(half + 1) * TQ, :] = acc.astype(o_ref.dtype)

    @pl.when(j >= nlat // 2)
    def _():
        q = qc_ref[0] * scale
        acc = jnp.zeros((TQ, GROUP_W), F32)
        for h in range(N_HEADS):
            hm = lane_h == h
            s = _dot(jnp.where(hm, q, 0.0).astype(BF16), kct)
            p = jnp.exp2(s - jnp.max(s, axis=-1, keepdims=True))
            den = jnp.sum(p, axis=-1, keepdims=True)
            acc = acc + jnp.where(hm, _dot(p.astype(BF16), vc) * (1.0 / den), 0.0)
        o_ref[0, :TQ, :] = acc.astype(o_ref.dtype)


def _natten(z, bias, T, layer):
    B, S, _ = z.shape
    nlat = T // TQ
    rows = T // GRID_W
    assert rows >= 16 and nlat % 2 == 0 and S - T == TQ
    cb = COL_NA // GROUP_W
    npair = nlat // 2
    lat = lambda c: pl.BlockSpec((1, T, GROUP_W), lambda b, j: (b, 0, cb + c))
    ctx = lambda c: pl.BlockSpec((1, TQ, GROUP_W), lambda b, j: (b, nlat, cb + c))
    case = lambda t: jnp.where(t == 0, 0, jnp.where(t >= nlat - 1, 2, 1))
    tile = lambda j, half: 2 * jnp.minimum(j, npair - 1) + half
    bias_spec = lambda half: pl.BlockSpec((None, 1, N_HEADS, TQ, NA_UNION * GRID_W),
                                          lambda b, j: (layer, case(tile(j, half)), 0, 0, 0))
    return pl.pallas_call(
        functools.partial(_natten_kernel, nlat=nlat, rows=rows),
        grid=(B, npair + 1),
        in_specs=[pl.BlockSpec((1, 2 * TQ, GROUP_W), lambda b, j: (b, jnp.minimum(j, npair - 1), cb)),
                  lat(1), lat(2), ctx(0), ctx(1), ctx(2), bias_spec(0), bias_spec(1)],
        out_specs=pl.BlockSpec((1, 2 * TQ, GROUP_W), lambda b, j: (b, j, 0)),
        out_shape=jax.ShapeDtypeStruct((B, S, GROUP_W), BF16),
        compiler_params=_cparams(2),
    )(z, z, z, z, z, z, bias, bias)


def _rope_tables(T, Tc):
    t = np.arange(T)
    row = (t // GRID_W).astype(np.float32)
    col = (t % GRID_W).astype(np.float32)
    d = MLA_ROPE // 2
    inv = (np.float32(ROPE_THETA) ** (-np.arange(0, d, 2, dtype=np.float32) / np.float32(d))).astype(np.float32)
    cs, sn = [], []
    for pos in (row, col):
        ang = (pos[:, None] * inv[None, :]).astype(np.float32)
        cs += [np.cos(ang), np.cos(ang)]
        sn += [-np.sin(ang), np.sin(ang)]
    pad = MLA_HEAD_PAD - MLA_NOPE - MLA_ROPE
    f32 = np.float32
    cos = np.concatenate([np.ones((T, MLA_NOPE), f32)] + cs + [np.ones((T, pad), f32)], axis=1)
    sin = np.concatenate([np.zeros((T, MLA_NOPE), f32)] + sn + [np.zeros((T, pad), f32)], axis=1)
    cos = np.concatenate([cos, np.ones((Tc, MLA_HEAD_PAD), f32)], axis=0)
    sin = np.concatenate([sin, np.zeros((Tc, MLA_HEAD_PAD), f32)], axis=0)
    return jnp.asarray(cos, F32), jnp.asarray(sin, F32)


def _rope_swap_perm():
    q = MLA_ROPE // 4
    return np.concatenate([np.arange(q, 2 * q), np.arange(0, q), np.arange(3 * q, 4 * q), np.arange(2 * q, 3 * q)])


def _mla_weights(w_uq, w_ukv):
    L = w_uq.shape[0]
    wq = w_uq.reshape(L, MLA_Q_RANK, N_HEADS, MLA_NOPE + MLA_ROPE)
    pad = MLA_HEAD_PAD - MLA_NOPE - MLA_ROPE
    zq = jnp.zeros((L, MLA_Q_RANK, N_HEADS, pad), F32)
    wq1 = jnp.concatenate([wq, zq], axis=-1).reshape(L, MLA_Q_RANK, N_HEADS * MLA_HEAD_PAD)
    rope_sw = wq[..., MLA_NOPE:][..., _rope_swap_perm()]
    wq2 = jnp.concatenate([jnp.zeros((L, MLA_Q_RANK, N_HEADS, MLA_NOPE), F32), rope_sw, zq], axis=-1)
    wq2 = wq2.reshape(L, MLA_Q_RANK, N_HEADS * MLA_HEAD_PAD)
    wkv = w_ukv.reshape(L, MLA_KV_RANK, N_HEADS, 2 * MLA_NOPE)
    wk = jnp.concatenate([wkv[..., :MLA_NOPE], jnp.zeros((L, MLA_KV_RANK, N_HEADS, MLA_HEAD_PAD - MLA_NOPE), F32)], axis=-1)
    wk = wk.reshape(L, MLA_KV_RANK, N_HEADS * MLA_HEAD_PAD)
    wv_t = wkv[..., MLA_NOPE:].reshape(L, MLA_KV_RANK, N_HEADS * MLA_NOPE).transpose(0, 2, 1)
    return wq1.astype(BF16), wq2.astype(BF16), wk.astype(BF16), wv_t.astype(BF16)


def _rope_place():
    e = np.zeros((MLA_ROPE, N_HEADS * MLA_HEAD_PAD), np.float32)
    for h in range(N_HEADS):
        e[np.arange(MLA_ROPE), h * MLA_HEAD_PAD + MLA_NOPE + np.arange(MLA_ROPE)] = 1.0
    return jnp.asarray(e, BF16)


def _mla_up_kernel(z_ref, cos_ref, sin_ref, qn_ref, kvn_ref, wq1_ref, wq2_ref, wk_ref, wv_ref, e_ref,
                   qt_ref, k_ref, vt_ref, *, nlat, tc):
    def body(nr):
        z = z_ref[0, :nr]
        ckv = z[:, :MLA_KV_RANK]
        cq = z[:, MLA_KV_RANK:MLA_KV_RANK + MLA_Q_RANK]
        kr = z[:, MLA_KV_RANK + MLA_Q_RANK:MLA_KV_RANK + MLA_Q_RANK + MLA_ROPE]
        krs = z[:, MLA_KV_RANK + MLA_Q_RANK + MLA_ROPE:]
        cos = jnp.concatenate([cos_ref[:nr]] * N_HEADS, axis=-1)
        sin = jnp.concatenate([sin_ref[:nr]] * N_HEADS, axis=-1)
        nq = (_rms(cq) * qn_ref[...]).astype(BF16)
        nkv_f = _rms(ckv) * kvn_ref[...]
        nkv = nkv_f.astype(BF16)
        q = _dot(nq, wq1_ref[...]) * cos + _dot(nq, wq2_ref[...]) * sin
        scale = (MLA_NOPE + MLA_ROPE) ** -0.5 * LOG2E
        qt_ref[0, :, :nr] = (q * scale).T.astype(BF16)
        k = _dot(nkv, wk_ref[...]) + _dot_sel_rhs(kr, e_ref[...]) * cos + _dot_sel_rhs(krs, e_ref[...]) * sin
        k_ref[0, :nr] = k.astype(BF16)
        vt_ref[0, :, :nr] = _dot(wv_ref[...], nkv_f.T.astype(BF16)).astype(BF16)

    _for_tile_rows(nlat, tc, body)


def _mla_up(z, cos, sin, q_norm, kv_norm, wq1, wq2, wk, wv, place, T, layer):
    B, S, _ = z.shape
    ntiles, nlat = _tok_tiles(S, T)
    HP = N_HEADS * MLA_HEAD_PAD
    zw = MLA_KV_RANK + MLA_Q_RANK + 2 * MLA_ROPE
    tab = pl.BlockSpec((TMT, MLA_HEAD_PAD), lambda b, i: (i, 0))
    return pl.pallas_call(
        functools.partial(_mla_up_kernel, nlat=nlat, tc=S - T),
        grid=(B, ntiles),
        in_specs=[_tok_spec(zw, COL_MLA // zw), tab, tab,
                  _const_spec((1, MLA_Q_RANK), layer), _const_spec((1, MLA_KV_RANK), layer),
                  _const_spec((MLA_Q_RANK, HP), layer), _const_spec((MLA_Q_RANK, HP), layer),
                  _const_spec((MLA_KV_RANK, HP), layer), _const_spec((GROUP_W, MLA_KV_RANK), layer),
                  _const_spec((MLA_ROPE, HP))],
        out_specs=[pl.BlockSpec((1, HP, TMT), lambda b, i: (b, 0, i)), _tok_spec(HP),
                   pl.BlockSpec((1, GROUP_W, TMT), lambda b, i: (b, 0, i))],
        out_shape=[jax.ShapeDtypeStruct((B, HP, S), BF16),
                   jax.ShapeDtypeStruct((B, S, HP), BF16),
                   jax.ShapeDtypeStruct((B, GROUP_W, S), BF16)],
        compiler_params=_cparams(2),
    )(z, cos, sin, q_norm, kv_norm, wq1, wq2, wk, wv, place)


def _mla_attn_kernel(qt_ref, k_ref, vt_ref, o_ref, *, nlat, T):
    j = pl.program_id(1)

    def attend(lo, hi):
        def logits(h):
            hp = slice(h * MLA_HEAD_PAD, (h + 1) * MLA_HEAD_PAD)
            return _dot(k_ref[0, lo:hi, hp], qt_ref[0, hp, :])

        ahead = 3
        pending = [logits(h) for h in range(ahead)]
        pieces = []
        for h in range(N_HEADS):
            s = pending.pop(0)
            if h + ahead < N_HEADS:
                pending.append(logits(h + ahead))
            p = jnp.exp2(s - jnp.max(s, axis=0, keepdims=True))
            den = jnp.sum(p, axis=0, keepdims=True)
            o = _dot(vt_ref[0, h * HEAD_DIM:(h + 1) * HEAD_DIM, lo:hi], p.astype(BF16))
            pieces.append(o * (1.0 / den))
        o_ref[0] = jnp.concatenate(pieces, axis=0).T.astype(o_ref.dtype)

    @pl.when(j < nlat)
    def _():
        attend(0, T + TQ)

    @pl.when(j >= nlat)
    def _():
        attend(T, T + TQ)


def _mla_attn(qt, k, vt, T):
    B, S, HP = k.shape
    assert T % MLA_TQ == 0 and S - T <= MLA_TQ
    nlat = T // MLA_TQ
    return pl.pallas_call(
        functools.partial(_mla_attn_kernel, nlat=nlat, T=T),
        grid=(B, nlat + 1),
        in_specs=[pl.BlockSpec((1, HP, MLA_TQ), lambda b, j: (b, 0, j)),
                  pl.BlockSpec((1, S, HP), lambda b, j: (b, 0, 0)),
                  pl.BlockSpec((1, GROUP_W, S), lambda b, j: (b, 0, 0))],
        out_specs=pl.BlockSpec((1, MLA_TQ, GROUP_W), lambda b, j: (b, j, 0)),
        out_shape=jax.ShapeDtypeStruct((B, S, GROUP_W), BF16),
        compiler_params=_cparams(2),
    )(qt, k, vt)


def _scan_masks():
    t = np.arange(CHUNK)
    inc = np.stack([t[:, None] >= t[None, :], t[:, None] <= t[None, :]]).astype(np.float32)
    strict = np.stack([t[:, None] > t[None, :], t[:, None] < t[None, :]]).astype(np.float32)
    return jnp.asarray(np.tile(inc, (1, 1, N_HEADS))), jnp.asarray(np.tile(strict, (1, 1, N_HEADS)))


def _cumsum_rows(x, reverse):
    n = x.shape[0]
    row = lax.broadcasted_iota(jnp.int32, (n, 1), 0)
    sh = 1
    while sh < n:
        if reverse:
            x = x + jnp.where(row < n - sh, pltpu.roll(x, n - sh, 0), 0.0)
        else:
            x = x + jnp.where(row >= sh, pltpu.roll(x, sh, 0), 0.0)
        sh *= 2
    return x


def _block_diag(x, bm_b):
    return _tile_rows(x.astype(BF16)) * bm_b


def _dot_nt(a, b):
    return lax.dot_general(a, b, (((1,), (1,)), ((), ())), preferred_element_type=F32)


def _fwd_tile(p, nlat):
    return jnp.where(p == 0, nlat, p - 1)


def _bwd_tile(p, nlat):
    return jnp.where(p == 0, nlat, nlat - p)


def _chunk_order(d):
    nch = TQ // CHUNK
    return range(nch) if d == 0 else range(nch - 1, -1, -1)


def _tile_rows(x):
    return jnp.concatenate([x] * N_HEADS, axis=0)


def _rw_prep_kernel(zp_ref, z_ref, zn_ref, mu_ref, kk_ref, ka_ref, rk_ref, w0_ref, a0_ref, wup_ref,
                    aup_ref, gup_ref, ones_ref,
                    r_ref, v_ref, ah_ref, g_ref, bon_ref, lw_ref, kd_ref, bd_ref, *, nlat, tc):
    i = pl.program_id(1)
    prev_ok, next_ok = _neighbour_ok(i, nlat)

    def body(nr):
        z = z_ref[0, :nr]
        row = lax.broadcasted_iota(jnp.int32, (nr, 1), 0)
        before = jnp.where(prev_ok, zp_ref[0, HALO - 1:HALO, :], 0.0)
        after = jnp.where(next_ok, zn_ref[0, 0:1, :], 0.0)
        zprev = jnp.where(row == 0, before, pltpu.roll(z, 1, 0))
        znext = jnp.where(row == nr - 1, after, pltpu.roll(z, nr - 1, 0))
        zs = z + mu_ref[...] * (0.5 * (zprev + znext) - z)
        r = zs[:, 0:GROUP_W]
        k = zs[:, GROUP_W:2 * GROUP_W]
        v = zs[:, 2 * GROUP_W:3 * GROUP_W]
        low = zs[:, 3 * GROUP_W:3 * GROUP_W + 128]
        gd = zs[:, 3 * GROUP_W + 128:]
        ones = ones_ref[...]
        kk = k * kk_ref[...]
        kk = kk * lax.rsqrt(_dot_sel_rhs(kk * kk, ones) + 1e-12)
        wl = _dot(jnp.tanh(low).astype(BF16), wup_ref[...])
        al = _dot(low.astype(BF16), aup_ref[...])
        ksum = jnp.zeros((nr, GROUP_W), F32)
        for d in range(2):
            w_raw = -_softplus(-(w0_ref[d:d + 1, :] + wl[:, d * GROUP_W:(d + 1) * GROUP_W])) - 0.5
            lw_ref[d, 0, :nr] = -jnp.exp(w_raw)
            a = _sigmoid(a0_ref[d:d + 1, :] + al[:, d * GROUP_W:(d + 1) * GROUP_W])
            kd = k * (1.0 + (a - 1.0) * ka_ref[...])
            kd_ref[d, 0, :nr] = kd
            bd_ref[d, 0, :nr] = kk * a
            ksum = ksum + kd
        r_ref[0, :nr] = r
        v_ref[0, :nr] = v
        ah_ref[0, :nr] = -kk
        g_ref[0, :nr] = _dot(_sigmoid(gd).astype(BF16), gup_ref[...])
        bon_ref[0, :nr] = _dot_sel_rhs(r * ksum * rk_ref[...], ones) * v

    _for_tile_rows(nlat, tc, body)


def _rw_prep(z, mu, k_k, k_a, r_k, w0, a0, wup_p, aup_p, gup_b, ones_b, T, layer):
    B, S, _ = z.shape
    ntiles, nlat = _tok_tiles(S, T)
    W = 4 * GROUP_W
    cb = COL_RW // W
    prev, nxt = _halo_specs(W, cb, S)
    one = _tok_spec(GROUP_W)
    two = pl.BlockSpec((2, 1, TMT, GROUP_W), lambda b, i: (0, b, i, 0))
    s1 = jax.ShapeDtypeStruct((B, S, GROUP_W), F32)
    s2 = jax.ShapeDtypeStruct((2, B, S, GROUP_W), F32)
    vec = _const_spec((1, GROUP_W), layer)
    return pl.pallas_call(
        functools.partial(_rw_prep_kernel, nlat=nlat, tc=S - T),
        grid=(B, ntiles),
        in_specs=[prev, _tok_spec(W, cb), nxt,
                  _const_spec((1, W), layer), vec, vec, vec,
                  _const_spec((2, GROUP_W), layer), _const_spec((2, GROUP_W), layer),
                  _const_spec((128, 2 * GROUP_W), layer), _const_spec((128, 2 * GROUP_W), layer),
                  _const_spec((128, GROUP_W), layer), _const_spec((GROUP_W, GROUP_W))],
        out_specs=[one, one, one, one, one, two, two, two],
        out_shape=[s1, s1, s1, s1, s1, s2, s2, s2],
        compiler_params=_cparams(2),
    )(z, z, z, mu, k_k, k_a, r_k, w0, a0, wup_p, aup_p, gup_b, ones_b)


def _rw_scan_kernel(rf_ref, rb_ref, vf_ref, vb_ref, af_ref, ab_ref, lwf_ref, lwb_ref, kdf_ref, kdb_ref,
                    bdf_ref, bdb_ref, inc_ref, strict_ref, bm_ref, yf_ref, yb_ref, s_ref):
    p = pl.program_id(1)

    @pl.when(p == 0)
    def _():
        s_ref[...] = jnp.zeros_like(s_ref)

    n = N_HEADS * CHUNK
    bm = bm_ref[...]
    eye = (lax.broadcasted_iota(jnp.int32, (CHUNK, n), 1) % CHUNK
           == lax.broadcasted_iota(jnp.int32, (CHUNK, n), 0)).astype(F32)
    refs = ((rf_ref, vf_ref, af_ref, lwf_ref, kdf_ref, bdf_ref, yf_ref),
            (rb_ref, vb_ref, ab_ref, lwb_ref, kdb_ref, bdb_ref, yb_ref))
    orders = [list(_chunk_order(d)) for d in range(2)]
    bm_b = bm.astype(BF16)
    bd_of = lambda m: _block_diag(m, bm_b)

    nb = s_ref.shape[0] // 2
    inst = [(bi, step, d) for step in range(TQ // CHUNK) for bi in range(nb) for d in range(2)]
    idx = range(len(inst))
    rows = [pl.ds(orders[d][step] * CHUNK, CHUNK) for _, step, d in inst]

    ar, r_t, v_s, a_s, bk_t, decay, v_in, g_b, g_k = [], [], [], [], [], [], [], [], []
    for (bi, step, d), sl in zip(inst, rows):
        r_ref, v_ref, a_ref, lw_ref, kd_ref, bd_ref, _ = refs[d]
        lw = lw_ref[0, bi, sl, :]
        kd = kd_ref[0, bi, sl, :]
        bd = bd_ref[0, bi, sl, :]
        v = v_ref[bi, sl, :]
        cs = _cumsum_rows(lw, reverse=(d == 1))
        tot = cs[CHUNK - 1:CHUNK, :] if d == 0 else cs[0:1, :]
        e_neg = jnp.exp(-cs)
        e_hat = jnp.exp(tot - cs)
        a_t = a_ref[bi, sl, :] * jnp.exp(cs - lw)
        rt = r_ref[bi, sl, :] * jnp.exp(cs)
        ar.append(jnp.concatenate([a_t, rt], axis=0).astype(BF16))
        r_t.append(rt)
        v_in.append(v)
        g_b.append(_dot_nt(ar[-1], bd_of(bd * e_neg)))
        g_k.append(_dot_nt(ar[-1], bd_of(kd * e_neg)))
        v_s.append(bd_of(v))
        a_s.append(bd_of(a_t))
        t = jnp.concatenate([bd * e_hat, kd * e_hat,
                             jnp.broadcast_to(jnp.exp(tot), (2 * CHUNK, GROUP_W))], axis=0).T
        bk_t.append(t[:, :2 * CHUNK].astype(BF16))
        decay.append(jnp.concatenate([t[:, 2 * CHUNK:]] * 2, axis=1))
    pw =[g_b[i][:CHUNK] * strict_ref[inst[i][2]] for i in idx]
    ak = [(g_k[i][:CHUNK] * strict_ref[inst[i][2]]).astype(BF16) for i in idx]
    rbk = [jnp.concatenate([g_b[i][CHUNK:] * inc_ref[inst[i][2]], g_k[i][CHUNK:] * inc_ref[inst[i][2]]],
                           axis=1).astype(BF16) for i in idx]
    x = [eye + pw[i] for i in idx]
    pw = [_dot(pw[i].astype(BF16), bd_of(pw[i])) for i in idx]
    for _ in range(4):
        px = [_dot(jnp.concatenate([pw[i], x[i]], axis=0).astype(BF16), bd_of(pw[i])) for i in idx]
        x = [x[i] + px[i][CHUNK:] for i in idx]
        pw = [px[i][:CHUNK] for i in idx]
    x = [(x[i] + _dot(x[i].astype(BF16), bd_of(pw[i]))).astype(BF16) for i in idx]
    akv = [_dot(ak[i], v_s[i]) for i in idx]
    p12 = [_dot(x[i], jnp.concatenate([bd_of(akv[i]), a_s[i]], axis=1)) for i in idx]
    p1_s = [bd_of(p12[i][:, :GROUP_W]) for i in idx]
    p2_s = [bd_of(p12[i][:, GROUP_W:]) for i in idx]
    q = [r_t[i] + _dot(rbk[i][:, :n], p2_s[i]) for i in idx]
    y_c = [_dot(rbk[i], jnp.concatenate([p1_s[i], v_s[i]], axis=0)) for i in idx]
    zero = jnp.zeros((CHUNK, GROUP_W), F32)
    gc = [_dot(bk_t[i], jnp.concatenate(
        [jnp.concatenate([p12[i][:, GROUP_W:], p12[i][:, :GROUP_W]], axis=1),
         jnp.concatenate([zero, v_in[i]], axis=1)], axis=0).astype(BF16)) for i in idx]
    gq = [jnp.concatenate([gc[i][:, :GROUP_W] * bm, q[i]], axis=0).astype(BF16) for i in idx]
    c_s = [gc[i][:, GROUP_W:] * bm for i in idx]

    s = [s_ref[k] for k in range(2 * nb)]
    for i, ((bi, step, d), sl) in enumerate(zip(inst, rows)):
        k = 2 * bi + d
        m = _dot(gq[i], s[k].astype(BF16))
        refs[d][6][bi, sl, :] = m[n:] + y_c[i]
        s[k] = decay[i] * s[k] + m[:n] + c_s[i]
    for k in range(2 * nb):
        s_ref[k] = s[k]


def _rw_scan(r, v, ah, lw, kd, bd, masks, bm, T):
    B, S, _ = r.shape
    nlat = T // TQ
    inc, strict = masks
    n = N_HEADS * CHUNK
    nb = RW_SCAN_BATCH if B % RW_SCAN_BATCH == 0 else 1
    fwd = pl.BlockSpec((nb, TQ, GROUP_W), lambda b, p: (b, _fwd_tile(p, nlat), 0))
    bwd = pl.BlockSpec((nb, TQ, GROUP_W), lambda b, p: (b, _bwd_tile(p, nlat), 0))
    fwd2 = pl.BlockSpec((1, nb, TQ, GROUP_W), lambda b, p: (0, b, _fwd_tile(p, nlat), 0))
    bwd2 = pl.BlockSpec((1, nb, TQ, GROUP_W), lambda b, p: (1, b, _bwd_tile(p, nlat), 0))
    out = jax.ShapeDtypeStruct((B, S, GROUP_W), F32)
    return pl.pallas_call(
        _rw_scan_kernel,
        grid=(B // nb, nlat + 1),
        in_specs=[fwd, bwd, fwd, bwd, fwd, bwd, fwd2, bwd2, fwd2, bwd2, fwd2, bwd2,
                  _const_spec((2, CHUNK, n)), _const_spec((2, CHUNK, n)), _const_spec((n, n))],
        out_specs=[fwd, bwd],
        out_shape=[out, out],
        scratch_shapes=[pltpu.VMEM((2 * nb, n, GROUP_W), F32)],
        compiler_params=_cparams(2),
    )(r, r, v, v, ah, ah, lw, lw, kd, kd, bd, bd, inc, strict, bm)


def _gla_scan_kernel(qf_ref, qb_ref, kf_ref, kb_ref, gf_ref, gb_ref, vf_ref, vb_ref, gup_ref, gbias_ref,
                     inc_ref, bm_ref, bmv_ref, of_ref, ob_ref, s_ref):
    p = pl.program_id(1)

    @pl.when(p == 0)
    def _():
        s_ref[...] = jnp.zeros_like(s_ref)

    wk = N_HEADS * GLA_DK
    n = N_HEADS * CHUNK
    bm = bm_ref[...]
    bmv = bmv_ref[...].astype(BF16)
    bmk = (lax.broadcasted_iota(jnp.int32, (n, wk), 0) // CHUNK
           == lax.broadcasted_iota(jnp.int32, (n, wk), 1) // GLA_DK).astype(F32).astype(BF16)
    refs = ((qf_ref, kf_ref, gf_ref, vf_ref, of_ref), (qb_ref, kb_ref, gb_ref, vb_ref, ob_ref))
    orders = [list(_chunk_order(d)) for d in range(2)]
    nb = s_ref.shape[0] // 2
    inst = [(bi, step, d) for step in range(TQ // CHUNK) for bi in range(nb) for d in range(2)]
    idx = range(len(inst))
    rows = [pl.ds(orders[d][step] * CHUNK, CHUNK) for _, step, d in inst]

    qe, ke_s, ks4, v_s, decay = [], [], [], [], []
    for (bi, step, d), sl in zip(inst, rows):
        q_ref, k_ref, g_ref, v_ref, _ = refs[d]
        k = k_ref[bi, sl, :]
        la = -_softplus(-(_dot(g_ref[bi, sl, :].astype(BF16), gup_ref[d]) + gbias_ref[d])) * (1.0 / GLA_TAU)
        b = _cumsum_rows(la, reverse=(d == 1))
        tot = b[CHUNK - 1:CHUNK, :] if d == 0 else b[0:1, :]
        qe.append((q_ref[bi, sl, :] * (GLA_DK ** -0.5) * jnp.exp(b)).astype(BF16))
        ke_s.append(_block_diag(k * jnp.exp(-b), bmk))
        ks4.append(_tile_rows(k * jnp.exp(tot - b)).T.astype(BF16))
        v_s.append(_block_diag(v_ref[bi, sl, :], bmv))
        decay.append(_tile_rows(jnp.broadcast_to(jnp.exp(tot), (CHUNK, wk))).T)
    a_cat = [(_dot_nt(qe[i], ke_s[i]) * inc_ref[inst[i][2]]).astype(BF16) for i in idx]
    o_in = [_dot(a_cat[i], v_s[i]) for i in idx]
    kv = [_dot(ks4[i], v_s[i]) * bm for i in idx]

    s = [s_ref[k] for k in range(2 * nb)]
    for i, ((bi, step, d), sl) in enumerate(zip(inst, rows)):
        k = 2 * bi + d
        refs[d][4][bi, sl, :] = o_in[i] + _dot(qe[i], s[k].astype(BF16))
        s[k] = decay[i] * s[k] + kv[i]
    for k in range(2 * nb):
        s_ref[k] = s[k]


def _gla_scan(z, gup_p, gb, masks, bm, bmv, T, layer):
    B, S, _ = z.shape
    nlat = T // TQ
    inc, _ = masks
    n = N_HEADS * CHUNK
    wk = N_HEADS * GLA_DK
    nb = GLA_SCAN_BATCH if B % GLA_SCAN_BATCH == 0 else 1
    fwd = lambda w, col: pl.BlockSpec((nb, TQ, w), lambda b, p: (b, _fwd_tile(p, nlat), col // w))
    bwd = lambda w, col: pl.BlockSpec((nb, TQ, w), lambda b, p: (b, _bwd_tile(p, nlat), col // w))
    out = jax.ShapeDtypeStruct((B, S, GROUP_W), F32)
    return pl.pallas_call(
        _gla_scan_kernel,
        grid=(B // nb, nlat + 1),
        in_specs=[fwd(wk, COL_GLA_Q), bwd(wk, COL_GLA_Q), fwd(wk, COL_GLA_K), bwd(wk, COL_GLA_K),
                  fwd(wk, COL_GLA_G), bwd(wk, COL_GLA_G), fwd(GROUP_W, COL_GLA_V), bwd(GROUP_W, COL_GLA_V),
                  _const_spec((2, wk, wk), layer), _const_spec((2, 1, wk), layer),
                  _const_spec((2, CHUNK, n)), _const_spec((wk, GROUP_W)), _const_spec((n, n))],
        out_specs=[fwd(GROUP_W, 0), bwd(GROUP_W, 0)],
        out_shape=[out, out],
        scratch_shapes=[pltpu.VMEM((2 * nb, wk, GROUP_W), F32)],
        compiler_params=_cparams(2),
    )(z, z, z, z, z, z, z, z, gup_p, gb, inc, bm, bmv)


def _pack_w_in(w_in):
    L, D, _ = w_in.shape
    na, mla, rw, gla = jnp.split(w_in.astype(BF16), [768, 1120, 2144], axis=-1)
    cq, ckv, kr = jnp.split(mla, [MLA_Q_RANK, MLA_Q_RANK + MLA_KV_RANK], axis=-1)
    gq, gk, gv, gg, go = jnp.split(gla, [128, 256, 512, 528], axis=-1)
    pad = jnp.zeros((L, D, COL_GLA_V - COL_GLA_G - gg.shape[-1]), BF16)
    packed = jnp.concatenate([na, ckv, cq, kr, kr[..., _rope_swap_perm()], gq, gk, gg, pad, gv, go, rw], axis=-1)
    assert packed.shape[-1] == Z_COLS
    return packed


def kernel(x, c, ctx, c_ctx, w_mod, b_mod, g_mix_pre, g_mix_post, g_ffn_pre, g_ffn_post, w_in, w_out, na_rpb, mla_q_norm, mla_w_uq, mla_kv_norm, mla_w_ukv, rw_mu, rw_w0, rw_w_up, rw_a0, rw_a_up, rw_g_up, rw_k_k, rw_k_a, rw_r_k, rw_ln_w, rw_ln_b, gla_gate_up, gla_gate_b, gla_norm, ffn_w_up, ffn_conv_w, ffn_conv_b, ffn_w_down):
    B, T, D = x.shape
    Tc = ctx.shape[1]
    L = w_in.shape[0]
    assert D == D_MODEL and Tc == TQ and T % TMT == 0 and B + 1 <= 8

    w_in_p = _pack_w_in(w_in)
    w_out_b = w_out.astype(BF16)
    wq1, wq2, wk, wv = _mla_weights(mla_w_uq, mla_w_ukv)
    place = _rope_place()
    cos, sin = _rope_tables(T, Tc)
    zero_lo = jnp.zeros((L, 64, 2 * GROUP_W), F32)
    rw_wup_p = jnp.concatenate([jnp.concatenate([rw_w_up[:, 0], rw_w_up[:, 1]], axis=-1), zero_lo], axis=1).astype(BF16)
    rw_aup_p = jnp.concatenate([zero_lo, jnp.concatenate([rw_a_up[:, 0], rw_a_up[:, 1]], axis=-1)], axis=1).astype(BF16)
    rw_gup_b = rw_g_up.astype(BF16)
    wk_gla = N_HEADS * GLA_DK
    gla_gup_p = jnp.concatenate([gla_gate_up, jnp.zeros((L, 2, wk_gla - gla_gate_up.shape[2], wk_gla), F32)], axis=2).astype(BF16)
    ffn_up_b = ffn_w_up.astype(BF16)
    ffn_dn_b = ffn_w_down.astype(BF16)
    ones_b = jnp.asarray(_block_ones(GROUP_W, HEAD_DIM), BF16)
    rw_bm = jnp.asarray(_block_ones(N_HEADS * CHUNK, CHUNK))
    gla_bm = jnp.asarray((np.arange(wk_gla)[:, None] // GLA_DK == np.arange(GROUP_W)[None, :] // HEAD_DIM).astype(np.float32))
    masks = _scan_masks()

    cvecs = jnp.zeros((8, D), F32).at[:B].set(c).at[B].set(c_ctx)
    mods = _modulation(cvecs, w_mod, b_mod).reshape(L, 8, 6, D)
    mods = jnp.pad(mods, ((0, 0), (0, 0), (0, 2), (0, 0)))
    modtabs = jnp.stack([jnp.broadcast_to(mods[:, B:B + 1], (L, B, 8, D)), mods[:, :B]], axis=2)

    xs = (x, ctx)
    rows = lambda a: a.reshape(L, 1, -1)
    na_bias = _natten_bias(na_rpb, T // GRID_W)
    gla_gb = gla_gate_b[:, :, None, :]
    for i in range(L):
        z = _in_proj(xs, modtabs, rows(g_mix_pre), w_in_p, T, Tc, i)
        y_na = _natten(z, na_bias, T, i)
        qt, k, vt = _mla_up(z, cos, sin, rows(mla_q_norm), rows(mla_kv_norm), wq1, wq2, wk, wv, place, T, i)
        y_mla = _mla_attn(qt, k, vt, T)
        r, vv, ah, g, bon, lw, kd, bd = _rw_prep(z, rows(rw_mu), rows(rw_k_k), rows(rw_k_a), rows(rw_r_k),
                                                 rw_w0, rw_a0, rw_wup_p, rw_aup_p, rw_gup_b, ones_b, T, i)
        yf, yb = _rw_scan(r, vv, ah, lw, kd, bd, masks, rw_bm, T)
        of, ob = _gla_scan(z, gla_gup_p, gla_gb, masks, gla_bm, rw_bm, T, i)
        xs = _out_proj(xs, z, y_na, y_mla, yf, yb, bon, g, of, ob, modtabs, rows(g_mix_post), w_out_b,
                       rows(rw_ln_w), rows(rw_ln_b), rows(gla_norm), ones_b, T, i)
        xs = _ffn(xs, modtabs, rows(g_ffn_pre), rows(g_ffn_post), ffn_up_b, ffn_conv_w,
                  rows(ffn_conv_b), ffn_dn_b, T, i, latent_only=(i == L - 1))
    return xs
```
